```python
import math
import jax, jax.numpy as jnp
from jax import lax
import numpy as np

D_MODEL = 1024
BATCH = 2
SEQ = 8192
DEPTH = 4

GRID_W = 64
CTX_LEN = 256
HEAD_DIM = 64
BLOCK = 128
EPS = 1e-6
ROPE_BASE = 10000.0
DIFF_HEADS = 4
DIFF_V = 2 * HEAD_DIM
SWA_Q_HEADS = 8
SWA_KV_HEADS = 2
SWA_GROUP = SWA_Q_HEADS // SWA_KV_HEADS
WINDOW = 128
MLA_HEADS = 8
MLA_Q_LORA = 512
MLA_KV_LORA = 256
MLA_NOPE = 64
MLA_ROPE = 32
MLA_QK = MLA_NOPE + MLA_ROPE
MLA_V = 64
NA_HEADS = 8
NA_ROWS_MAX = 8
NA_COLS = 16
D_FF = 4 * D_MODEL
N_EVEN = (DEPTH + 1) // 2
N_ODD = DEPTH // 2
EVEN_SPLITS = (DIFF_HEADS * 2 * HEAD_DIM, DIFF_HEADS * 2 * HEAD_DIM, DIFF_HEADS * DIFF_V,
               SWA_Q_HEADS * HEAD_DIM, SWA_KV_HEADS * HEAD_DIM, SWA_KV_HEADS * HEAD_DIM)
ODD_SPLITS = (MLA_Q_LORA, MLA_KV_LORA, MLA_ROPE,
              NA_HEADS * HEAD_DIM, NA_HEADS * HEAD_DIM, NA_HEADS * HEAD_DIM)
EVEN_IN = sum(EVEN_SPLITS)
ODD_IN = sum(ODD_SPLITS)
MIX_OUT = DIFF_HEADS * DIFF_V + SWA_Q_HEADS * HEAD_DIM

kernel_name = 'hybrid_diffusion_prefix_trunk'


def rmsnorm(x, g):
    xf = x.astype(jnp.float32)
    y = xf * lax.rsqrt(jnp.mean(xf * xf, axis=-1, keepdims=True) + EPS)
    return (y * g.astype(jnp.float32)).astype(x.dtype)


def modulate(x, g, shift, scale):
    return rmsnorm(x, g) * (1.0 + scale) + shift


def split_cols(p, sizes):
    idx = [int(i) for i in np.cumsum(sizes)[:-1]]
    return jnp.split(p, idx, axis=-1)


def axial_rope_tables(n_tokens, rot_dim):
    t = jnp.arange(n_tokens, dtype=jnp.int32)
    row = (t // GRID_W).astype(jnp.float32)
    col = (t % GRID_W).astype(jnp.float32)
    n_freq = rot_dim // 4
    freqs = jnp.power(ROPE_BASE, -jnp.arange(n_freq, dtype=jnp.float32) / n_freq)
    ar = row[:, None] * freqs[None, :]
    ac = col[:, None] * freqs[None, :]
    ang = jnp.concatenate([ar, ar, ac, ac], axis=-1)
    return jnp.cos(ang), jnp.sin(ang)


def rope_2d(x, cos, sin):
    shape = (1, x.shape[1]) + (1,) * (x.ndim - 3) + (x.shape[-1],)
    x1, x2, x3, x4 = jnp.split(x, 4, axis=-1)
    rot = jnp.concatenate([-x2, x1, -x4, x3], axis=-1)
    return (x * cos.reshape(shape) + rot * sin.reshape(shape)).astype(x.dtype)


def sweep_query_blocks(fn, *qs):
    B, S = qs[0].shape[:2]
    nb = S // BLOCK
    blocks = tuple(jnp.moveaxis(q.reshape((B, nb, BLOCK) + q.shape[2:]), 1, 0) for q in qs)
    out = lax.map(lambda a: fn(*a), blocks)
    return jnp.moveaxis(out, 0, 1).reshape((B, S) + out.shape[3:])


def mha_attend(q, k, v, bias=None, mask=None):
    s = jnp.einsum('bqhd,bkhd->bhqk', q, k).astype(jnp.float32) * (q.shape[-1] ** -0.5)
    if bias is not None:
        s = s + bias[None]
    if mask is not None:
        s = jnp.where(mask, s, -jnp.inf)
    p = jax.nn.softmax(s, axis=-1)
    return jnp.einsum('bhqk,bkhd->bqhd', p.astype(v.dtype), v)


def diff_attend(q, k, v, lam):
    s = jnp.einsum('bqhtd,bkhtd->bthqk', q, k).astype(jnp.float32) * (q.shape[-1] ** -0.5)
    a = jax.nn.softmax(s, axis=-1)
    p = a[:, 0] - lam * a[:, 1]
    return jnp.einsum('bhqk,bkhe->bqhe', p.astype(v.dtype), v)


def gqa_sink_attend(q, k, v, sink, mask=None):
    s = jnp.einsum('bqhgd,bkhd->bhgqk', q, k).astype(jnp.float32) * (q.shape[-1] ** -0.5)
    if mask is not None:
        s = jnp.where(mask, s, -jnp.inf)
    sk = sink.astype(jnp.float32)[None, :, :, None, None]
    m = jnp.maximum(jnp.max(s, axis=-1, keepdims=True), sk)
    e = jnp.exp(s - m)
    p = e / (jnp.sum(e, axis=-1, keepdims=True) + jnp.exp(sk - m))
    return jnp.einsum('bhgqk,bkhd->bqhgd', p.astype(v.dtype), v)


def window_gqa(q, k, v, kc, vc, sink):
    B, S = q.shape[:2]
    nb = S // BLOCK
    L = kc.shape[1]

    def band(t):
        tp = jnp.pad(t, ((0, 0), (BLOCK, BLOCK), (0, 0), (0, 0))).reshape((B, nb + 2, BLOCK) + t.shape[2:])
        strip = jnp.concatenate([tp[:, :-2], tp[:, 1:-1], tp[:, 2:]], axis=2)
        return jnp.moveaxis(strip, 1, 0)

    qb = jnp.moveaxis(q.reshape((B, nb, BLOCK) + q.shape[2:]), 1, 0)
    qi = jnp.arange(BLOCK)[:, None]
    kj = jnp.arange(3 * BLOCK)[None, :] - BLOCK
    ctx_mask = jnp.ones((BLOCK, L), dtype=bool)

    def blk(args):
        n, qn, kn, vn = args
        pos_k = n * BLOCK + kj
        m = (jnp.abs(kj - qi) <= WINDOW) & (pos_k >= 0) & (pos_k < S)
        m = jnp.concatenate([m, ctx_mask], axis=1)
        kk = jnp.concatenate([kn, kc], axis=1)
        vv = jnp.concatenate([vn, vc], axis=1)
        return gqa_sink_attend(qn, kk, vv, sink, m)

    out = lax.map(blk, (jnp.arange(nb), qb, band(k), band(v)))
    return jnp.moveaxis(out, 0, 1).reshape(B, S, -1)


def neighbourhood_attn(q, k, v, kc, vc, rpb):
    B, S, H, d = q.shape
    rows = S // GRID_W
    kr = min(NA_ROWS_MAX, rows)
    L = kc.shape[1]
    qg = jnp.moveaxis(q.reshape(B, rows, GRID_W, H, d), 1, 0)
    kg = k.reshape(B, rows, GRID_W, H, d)
    vg = v.reshape(B, rows, GRID_W, H, d)
    cq = jnp.arange(GRID_W)
    cs = jnp.clip(cq - NA_COLS // 2, 0, GRID_W - NA_COLS)
    ck = jnp.arange(GRID_W)
    col_mask = (ck[None, :] >= cs[:, None]) & (ck[None, :] < cs[:, None] + NA_COLS)
    nb_mask = jnp.broadcast_to(col_mask[:, None, :], (GRID_W, kr, GRID_W)).reshape(GRID_W, kr * GRID_W)
    mask = jnp.concatenate([nb_mask, jnp.ones((GRID_W, L), dtype=bool)], axis=1)
    dc_idx = jnp.clip(ck[None, :] - cq[:, None], -(NA_COLS - 1), NA_COLS - 1) + NA_COLS - 1
    rpb32 = rpb.astype(jnp.float32)

    def row_block(args):
        r, qr = args
        rs = jnp.clip(r - kr // 2, 0, rows - kr)
        krows = lax.dynamic_slice_in_dim(kg, rs, kr, axis=1).reshape(B, kr * GRID_W, H, d)
        vrows = lax.dynamic_slice_in_dim(vg, rs, kr, axis=1).reshape(B, kr * GRID_W, H, d)
        dr_idx = rs + jnp.arange(kr) - r + NA_ROWS_MAX - 1
        bias = rpb32[:, dr_idx[None, :, None], dc_idx[:, None, :]].reshape(H, GRID_W, kr * GRID_W)
        bias = jnp.pad(bias, ((0, 0), (0, 0), (0, L)))
        kk = jnp.concatenate([krows, kc], axis=1)
        vv = jnp.concatenate([vrows, vc], axis=1)
        return mha_attend(qr, kk, vv, bias, mask)

    out = lax.map(row_block, (jnp.arange(rows), qg))
    return jnp.moveaxis(out, 0, 1).reshape(B, S, H * d)


def even_mixer(p_lat, p_ctx, q_g, k_g, lam_v, subln_g, sq_g, sk_g, sink, cos, sin, lam_init, need_ctx):
    def project(p, rope):
        B, N, _ = p.shape
        qa, ka, va, qb, kb, vb = split_cols(p, EVEN_SPLITS)
        qa = rmsnorm(qa.reshape(B, N, DIFF_HEADS, 2, HEAD_DIM), q_g)
        ka = rmsnorm(ka.reshape(B, N, DIFF_HEADS, 2, HEAD_DIM), k_g)
        va = va.reshape(B, N, DIFF_HEADS, DIFF_V)
        qb = rmsnorm(qb.reshape(B, N, SWA_KV_HEADS, SWA_GROUP, HEAD_DIM), sq_g)
        kb = rmsnorm(kb.reshape(B, N, SWA_KV_HEADS, HEAD_DIM), sk_g)
        vb = vb.reshape(B, N, SWA_KV_HEADS, HEAD_DIM)
        if rope:
            qa, ka, qb, kb = [rope_2d(t, cos, sin) for t in (qa, ka, qb, kb)]
        return qa, ka, va, qb, kb, vb

    qa_l, ka_l, va_l, qb_l, kb_l, vb_l = project(p_lat, True)
    qa_c, ka_c, va_c, qb_c, kb_c, vb_c = project(p_ctx, False)
    lv = lam_v.astype(jnp.float32)
    lam = jnp.exp(jnp.sum(lv[0] * lv[1])) - jnp.exp(jnp.sum(lv[2] * lv[3])) + lam_init
    sink_hg = sink.reshape(SWA_KV_HEADS, SWA_GROUP)

    def diff_out(o):
        B, N = o.shape[:2]
        return (rmsnorm(o, subln_g) * (1.0 - lam_init)).reshape(B, N, DIFF_HEADS * DIFF_V)

    ka_all = jnp.concatenate([ka_l, ka_c], axis=1)
    va_all = jnp.concatenate([va_l, va_c], axis=1)
    o_a = sweep_query_blocks(lambda qblk: diff_attend(qblk, ka_all, va_all, lam), qa_l)
    o_b = window_gqa(qb_l, kb_l, vb_l, kb_c, vb_c, sink_hg)
    o_lat = jnp.concatenate([diff_out(o_a), o_b], axis=-1)
    if not need_ctx:
        return o_lat, None
    B, L = p_ctx.shape[:2]
    o_ca = diff_out(diff_attend(qa_c, ka_c, va_c, lam))
    o_cb = gqa_sink_attend(qb_c, kb_c, vb_c, sink_hg).reshape(B, L, -1)
    return o_lat, jnp.concatenate([o_ca, o_cb], axis=-1)


def odd_mixer(p_lat, p_ctx, qa_g, kva_g, wq_up, wkv_up, mq_g, mk_g, nq_g, nk_g, rpb, cos, sin, need_ctx):
    def project(p, rope):
        B, N, _ = p.shape
        q_a, kv_a, k_r, nq, nk, nv = split_cols(p, ODD_SPLITS)
        q = (rmsnorm(q_a, qa_g) @ wq_up).reshape(B, N, MLA_HEADS, MLA_QK)
        kv = (rmsnorm(kv_a, kva_g) @ wkv_up).reshape(B, N, MLA_HEADS, MLA_NOPE + MLA_V)
        k = jnp.concatenate([kv[..., :MLA_NOPE],
                             jnp.broadcast_to(k_r[:, :, None, :], (B, N, MLA_HEADS, MLA_ROPE))], axis=-1)
        v = kv[..., MLA_NOPE:]
        q = rmsnorm(q, mq_g)
        k = rmsnorm(k, mk_g)
        if rope:
            q = jnp.concatenate([q[..., :MLA_NOPE], rope_2d(q[..., MLA_NOPE:], cos, sin)], axis=-1)
            k = jnp.concatenate([k[..., :MLA_NOPE], rope_2d(k[..., MLA_NOPE:], cos, sin)], axis=-1)
        nq = rmsnorm(nq.reshape(B, N, NA_HEADS, HEAD_DIM), nq_g)
        nk = rmsnorm(nk.reshape(B, N, NA_HEADS, HEAD_DIM), nk_g)
        nv = nv.reshape(B, N, NA_HEADS, HEAD_DIM)
        return q, k, v, nq, nk, nv

    q_l, k_l, v_l, nq_l, nk_l, nv_l = project(p_lat, True)
    q_c, k_c, v_c, nq_c, nk_c, nv_c = project(p_ctx, False)
    B, S = p_lat.shape[:2]
    k_all = jnp.concatenate([k_l, k_c], axis=1)
    v_all = jnp.concatenate([v_l, v_c], axis=1)
    o_c = sweep_query_blocks(lambda qblk: mha_attend(qblk, k_all, v_all), q_l).reshape(B, S, -1)
    o_d = neighbourhood_attn(nq_l, nk_l, nv_l, nk_c, nv_c, rpb)
    o_lat = jnp.concatenate([o_c, o_d], axis=-1)
    if not need_ctx:
        return o_lat, None
    L = p_ctx.shape[1]
    o_cc = mha_attend(q_c, k_c, v_c).reshape(B, L, -1)
    o_cd = mha_attend(nq_c, nk_c, nv_c).reshape(B, L, -1)
    return o_lat, jnp.concatenate([o_cc, o_cd], axis=-1)


def sq_relu_mlp(h, w1, w2):
    return jnp.square(jax.nn.relu(h @ w1)) @ w2


def setup_inputs(seed: int = 0) -> dict:
    key = jax.random.key(seed)
    ks = jax.random.split(key, 32)
    D = D_MODEL

    def nrm(k, shape, scale):
        return jax.random.normal(k, shape, jnp.float32) * scale

    def gain(k, shape):
        return 1.0 + 0.02 * jax.random.normal(k, shape, jnp.float32)

    return {
        'x': nrm(ks[0], (BATCH, SEQ, D), 1.0),
        'c': nrm(ks[1], (BATCH, D), 1.0),
        'ctx': nrm(ks[2], (BATCH, CTX_LEN, D), 1.0),
        'c_ctx': nrm(ks[3], (D,), 1.0),
        'ada_w': nrm(ks[4], (DEPTH, D, 6 * D), D ** -0.5),
        'ada_b': nrm(ks[5], (DEPTH, 6 * D), 0.02),
        'norm1_g': gain(ks[6], (DEPTH, D)),
        'norm2_g': gain(ks[7], (DEPTH, D)),
        'w_out': nrm(ks[8], (DEPTH, MIX_OUT, D), MIX_OUT ** -0.5),
        'mlp_w1': nrm(ks[9], (DEPTH, D, D_FF), D ** -0.5),
        'mlp_w2': nrm(ks[10], (DEPTH, D_FF, D), D_FF ** -0.5),
        'ev_w_in': nrm(ks[11], (N_EVEN, D, EVEN_IN), D ** -0.5),
        'diff_q_g': gain(ks[12], (N_EVEN, HEAD_DIM)),
        'diff_k_g': gain(ks[13], (N_EVEN, HEAD_DIM)),
        'diff_lam': nrm(ks[14], (N_EVEN, 4, HEAD_DIM), 0.1),
        'diff_subln_g': gain(ks[15], (N_EVEN, DIFF_V)),
        'swa_q_g': gain(ks[16], (N_EVEN, HEAD_DIM)),
        'swa_k_g': gain(ks[17], (N_EVEN, HEAD_DIM)),
        'swa_sink': nrm(ks[18], (N_EVEN, SWA_Q_HEADS), 0.5),
        'od_w_in': nrm(ks[19], (N_ODD, D, ODD_IN), D ** -0.5),
        'mla_qa_g': gain(ks[20], (N_ODD, MLA_Q_LORA)),
        'mla_kva_g': gain(ks[21], (N_ODD, MLA_KV_LORA)),
        'mla_wq_up': nrm(ks[22], (N_ODD, MLA_Q_LORA, MLA_HEADS * MLA_QK), MLA_Q_LORA ** -0.5),
        'mla_wkv_up': nrm(ks[23], (N_ODD, MLA_KV_LORA, MLA_HEADS * (MLA_NOPE + MLA_V)), MLA_KV_LORA ** -0.5),
        'mla_q_g': gain(ks[24], (N_ODD, MLA_QK)),
        'mla_k_g': gain(ks[25], (N_ODD, MLA_QK)),
        'na_q_g': gain(ks[26], (N_ODD, HEAD_DIM)),
        'na_k_g': gain(ks[27], (N_ODD, HEAD_DIM)),
        'na_rpb': nrm(ks[28], (N_ODD, NA_HEADS, 2 * NA_ROWS_MAX - 1, 2 * NA_COLS - 1), 0.2),
    }


def reference(x, c, ctx, c_ctx, ada_w, ada_b, norm1_g, norm2_g, w_out, mlp_w1, mlp_w2,
              ev_w_in, diff_q_g, diff_k_g, diff_lam, diff_subln_g, swa_q_g, swa_k_g, swa_sink,
              od_w_in, mla_qa_g, mla_kva_g, mla_wq_up, mla_wkv_up, mla_q_g, mla_k_g,
              na_q_g, na_k_g, na_rpb):
    S = x.shape[1]
    cos64, sin64 = axial_rope_tables(S, HEAD_DIM)
    cos32, sin32 = axial_rope_tables(S, MLA_ROPE)
    xc = ctx
    for layer in range(DEPTH):
        need_ctx = layer < DEPTH - 1
        mod_l = (jax.nn.silu(c) @ ada_w[layer] + ada_b[layer])[:, None, :]
        mod_c = (jax.nn.silu(c_ctx) @ ada_w[layer] + ada_b[layer])[None, None, :]
        sh1, sc1, g1, sh2, sc2, g2 = jnp.split(mod_l, 6, axis=-1)
        csh1, csc1, cg1, csh2, csc2, cg2 = jnp.split(mod_c, 6, axis=-1)
        h_l = modulate(x, norm1_g[layer], sh1, sc1)
        h_c = modulate(xc, norm1_g[layer], csh1, csc1)
        i = layer // 2
        if layer % 2 == 0:
            lam_init = 0.8 - 0.6 * math.exp(-0.3 * layer)
            o_l, o_c = even_mixer(h_l @ ev_w_in[i], h_c @ ev_w_in[i], diff_q_g[i], diff_k_g[i], diff_lam[i],
                                  diff_subln_g[i], swa_q_g[i], swa_k_g[i], swa_sink[i],
                                  cos64, sin64, lam_init, need_ctx)
        else:
            o_l, o_c = odd_mixer(h_l @ od_w_in[i], h_c @ od_w_in[i], mla_qa_g[i], mla_kva_g[i], mla_wq_up[i],
                                 mla_wkv_up[i], mla_q_g[i], mla_k_g[i], na_q_g[i], na_k_g[i], na_rpb[i],
                                 cos32, sin32, need_ctx)
        x = x + g1 * (o_l @ w_out[layer])
        x = x + g2 * sq_relu_mlp(modulate(x, norm2_g[layer], sh2, sc2), mlp_w1[layer], mlp_w2[layer])
        if need_ctx:
            xc = xc + cg1 * (o_c @ w_out[layer])
            xc = xc + cg2 * sq_relu_mlp(modulate(xc, norm2_g[layer], csh2, csc2), mlp_w1[layer], mlp_w2[layer])
    return x
```

```python
import functools
import math

import numpy as np
import jax
import jax.numpy as jnp
from jax import lax
from jax.experimental import pallas as pl
from jax.experimental.pallas import tpu as pltpu

F32 = jnp.float32
BF16 = jnp.bfloat16

LANES = 128
HALF = LANES // 2
GRID_W = 64
HEAD_DIM = 64
EPS = 1e-6
ROPE_BASE = 10000.0
DIFF_HEADS = 4
SWA_Q_HEADS = 8
SWA_KV_HEADS = 2
WINDOW = 128
MLA_HEADS = 8
MLA_Q_LORA = 512
MLA_KV_LORA = 256
MLA_NOPE = 64
MLA_ROPE = 32
MLA_QK = MLA_NOPE + MLA_ROPE
MLA_V = 64
NA_HEADS = 8
NA_ROWS = 8
NA_COLS = 16
NEG = -1e30

TM = 512
TQ = 512
TK_FULL = 512
MOD_TN = 1536
MLP_TF = 1024
VMEM_LIMIT = 56 * 1024 * 1024


def _params(*sem):
    return pltpu.CompilerParams(dimension_semantics=sem, vmem_limit_bytes=VMEM_LIMIT)


def _mod_kernel(c_ref, w_ref, b_ref, o_ref):
    c = c_ref[...]
    a = (c * jax.nn.sigmoid(c)).astype(BF16)
    w = w_ref[0].astype(BF16)
    o_ref[0] = jnp.dot(a, w, preferred_element_type=F32) + b_ref[0]


def _modulation(cvec, ada_w, ada_b):
    depth, d, n = ada_w.shape
    rows = cvec.shape[0]
    return pl.pallas_call(
        _mod_kernel,
        grid=(depth, n // MOD_TN),
        in_specs=[
            pl.BlockSpec((rows, d), lambda l, j: (0, 0)),
            pl.BlockSpec((1, d, MOD_TN), lambda l, j: (l, 0, j)),
            pl.BlockSpec((1, 1, MOD_TN), lambda l, j: (l, 0, j)),
        ],
        out_specs=pl.BlockSpec((1, rows, MOD_TN), lambda l, j: (l, 0, j)),
        out_shape=jax.ShapeDtypeStruct((depth, rows, n), F32),
        compiler_params=_params("parallel", "parallel"),
        name="modulation",
    )(cvec, ada_w, ada_b.reshape(depth, 1, n))


def _modulated(x_ref, g_ref, sh_ref, sc_ref):
    x = x_ref[...]
    ms = jnp.mean(x * x, axis=-1, keepdims=True)
    y = x * lax.rsqrt(ms + EPS) * g_ref[...]
    return y * (1.0 + sc_ref[0]) + sh_ref[0]


def _group_meansq(x, g_ref, inv_n):
    sq = x * x
    hi = sq.astype(BF16)
    lo = (sq - hi.astype(F32)).astype(BF16)
    g = g_ref[...]
    ss = jnp.dot(hi, g, preferred_element_type=F32) + jnp.dot(lo, g, preferred_element_type=F32)
    return ss * inv_n


def _rope(x, cos, sin_signed, quarter):
    lane = lax.broadcasted_iota(jnp.int32, x.shape, 1)
    odd = (lane & quarter) != 0
    from_lower = pltpu.roll(x, quarter, 1)
    from_upper = pltpu.roll(x, LANES - quarter, 1)
    return x * cos + jnp.where(odd, from_lower, from_upper) * sin_signed


def _norm_rope_chunk(acc, g_ref, inv_n, gain, cos, sin, quarter):
    y = acc * lax.rsqrt(_group_meansq(acc, g_ref, inv_n) + EPS) * gain
    if quarter is None:
        return y
    return jnp.concatenate([_rope(y[:, :LANES], cos, sin, quarter),
                            _rope(y[:, LANES:], cos, sin, quarter)], axis=1)


def _token_specs(d, bpb, nbatch):
    row = lambda i, *_: (jnp.minimum(i // bpb, nbatch), 0, 0)
    return [
        pl.BlockSpec((TM, d), lambda i, *_: (i, 0)),
        pl.BlockSpec((1, d), lambda i, *_: (0, 0)),
        pl.BlockSpec((1, 1, d), row),
        pl.BlockSpec((1, 1, d), row),
    ], row


def _rope_spec(bpb, nbatch):
    return pl.BlockSpec((TM, LANES), lambda i: (jnp.where(i < bpb * nbatch, i % bpb, bpb), 0))


def _whole(shape):
    return pl.BlockSpec(shape, lambda *_: (0,) * len(shape))


EVEN_COLS = 2560
EVEN_NORM_CHUNKS = (0, 1, 2, 3, 6, 7, 8)


def _proj_even_kernel(x_ref, g_ref, sh_ref, sc_ref, w_ref, gain_ref, g64_ref, cos_ref, sin_ref, o_ref):
    h = _modulated(x_ref, g_ref, sh_ref, sc_ref).astype(BF16)
    cos = cos_ref[...]
    sin = sin_ref[...]
    for c in range(EVEN_COLS // 256):
        cols = slice(c * 256, (c + 1) * 256)
        acc = jnp.dot(h, w_ref[:, cols], preferred_element_type=F32)
        if c in EVEN_NORM_CHUNKS:
            acc = _norm_rope_chunk(acc, g64_ref, 1.0 / HEAD_DIM, gain_ref[:, cols], cos, sin, HEAD_DIM // 4)
        o_ref[:, cols] = acc.astype(BF16)


def _proj_even(xa, g, sh, sc, w, gain, g64, cos, sin, bpb, nbatch):
    na, d = xa.shape
    tok_specs, _ = _token_specs(d, bpb, nbatch)
    return pl.pallas_call(
        _proj_even_kernel,
        grid=(na // TM,),
        in_specs=tok_specs + [
            _whole(w.shape), _whole(gain.shape), _whole(g64.shape),
            _rope_spec(bpb, nbatch), _rope_spec(bpb, nbatch),
        ],
        out_specs=pl.BlockSpec((TM, EVEN_COLS), lambda i: (i, 0)),
        out_shape=jax.ShapeDtypeStruct((na, EVEN_COLS), BF16),
        compiler_params=_params("parallel"),
        name="proj_even",
    )(xa, g, sh, sc, w, gain, g64, cos, sin)


ODD_W_COLS = 2432
ODD_COLS = 4096
MLA_PAD = LANES


def _proj_odd_kernel(x_ref, g_ref, sh_ref, sc_ref, w_ref, wq_ref, wkv_ref, qa_g_ref, kva_g_ref,
                     mq_g_ref, mk_g_ref, nq_g_ref, nk_g_ref, g64_ref, g128_ref, cos_ref, sin_ref, o_ref):
    h = _modulated(x_ref, g_ref, sh_ref, sc_ref).astype(BF16)
    cos = cos_ref[...]
    sin = sin_ref[...]
    quarter = MLA_ROPE // 4

    def low_rank(cols, gain_ref):
        a = jnp.dot(h, w_ref[:, cols], preferred_element_type=F32)
        ms = jnp.mean(a * a, axis=-1, keepdims=True)
        return (a * lax.rsqrt(ms + EPS) * gain_ref[...]).astype(BF16)

    qa = low_rank(slice(0, MLA_Q_LORA), qa_g_ref)
    for c in range(MLA_HEADS * MLA_PAD // 256):
        cols = slice(c * 256, (c + 1) * 256)
        acc = jnp.dot(qa, wq_ref[:, cols], preferred_element_type=F32)
        y = _norm_rope_chunk(acc, g128_ref, 1.0 / MLA_QK, mq_g_ref[...], cos, sin, quarter)
        o_ref[:, cols] = y.astype(BF16)

    kva = low_rank(slice(MLA_Q_LORA, MLA_Q_LORA + MLA_KV_LORA), kva_g_ref)
    kslot = jnp.dot(h, w_ref[:, 2304:2432], preferred_element_type=F32)
    kslot2 = jnp.concatenate([kslot, kslot], axis=1)
    for c in range(MLA_HEADS * MLA_PAD // 256):
        cols = slice(c * 256, (c + 1) * 256)
        acc = jnp.dot(kva, wkv_ref[:, cols], preferred_element_type=F32) + kslot2
        y = _norm_rope_chunk(acc, g128_ref, 1.0 / MLA_QK, mk_g_ref[...], cos, sin, quarter)
        o_ref[:, 1024 + c * 256:1024 + (c + 1) * 256] = y.astype(BF16)
    for c in range(MLA_HEADS * MLA_V // 256):
        acc = jnp.dot(kva, wkv_ref[:, 1024 + c * 256:1024 + (c + 1) * 256], preferred_element_type=F32)
        o_ref[:, 2048 + c * 256:2048 + (c + 1) * 256] = acc.astype(BF16)

    for c in range(6):
        acc = jnp.dot(h, w_ref[:, 768 + c * 256:768 + (c + 1) * 256], preferred_element_type=F32)
        if c < 2:
            acc = _norm_rope_chunk(acc, g64_ref, 1.0 / HEAD_DIM, nq_g_ref[...], None, None, None)
        elif c < 4:
            acc = _norm_rope_chunk(acc, g64_ref, 1.0 / HEAD_DIM, nk_g_ref[...], None, None, None)
        o_ref[:, 2560 + c * 256:2560 + (c + 1) * 256] = acc.astype(BF16)


def _proj_odd(xa, g, sh, sc, w, wq, wkv, gains, g64, g128, cos, sin, bpb, nbatch):
    na, d = xa.shape
    tok_specs, _ = _token_specs(d, bpb, nbatch)
    consts = (w, wq, wkv) + tuple(gains) + (g64, g128)
    return pl.pallas_call(
        _proj_odd_kernel,
        grid=(na // TM,),
        in_specs=tok_specs + [_whole(a.shape) for a in consts] + [_rope_spec(bpb, nbatch)] * 2,
        out_specs=pl.BlockSpec((TM, ODD_COLS), lambda i: (i, 0)),
        out_shape=jax.ShapeDtypeStruct((na, ODD_COLS), BF16),
        compiler_params=_params("parallel"),
        name="proj_odd",
    )(xa, g, sh, sc, *consts, cos, sin)


def _attn_kernel(*refs, mode, split, main, seq, has_sink, lam_init, tq):
    it = iter(refs)
    q_ref = next(it)
    k_ref = v_ref = bias_ref = sink_ref = lam_ref = subln_ref = None
    if main is not None:
        k_ref, v_ref = next(it), next(it)
    kc_ref, vc_ref = next(it), next(it)
    if main == "na":
        bias_ref = next(it)
    if has_sink:
        sink_ref = next(it)
    if mode == "diff":
        lam_ref, subln_ref = next(it), next(it)
    o_ref = next(it)

    pair = pl.program_id(1)
    qi = pl.program_id(2)
    q = q_ref[...]
    low = lax.broadcasted_iota(jnp.int32, (1, LANES), 1) < HALF
    if split == "mask":
        zero = jnp.zeros_like(q)
        qs = (jnp.where(low, q, zero), jnp.where(low, zero, q))
        k_of = lambda k, s: k
    else:
        qs = (q[:, :LANES], q[:, LANES:])
        k_of = lambda k, s: k[:, s * LANES:(s + 1) * LANES]

    def init(s):
        if has_sink:
            m0 = jnp.full((tq, 1), sink_ref[2 * pair + s], F32)
            l0 = jnp.ones((tq, 1), F32)
        else:
            m0 = jnp.full((tq, 1), NEG, F32)
            l0 = jnp.zeros((tq, 1), F32)
        return m0, l0, jnp.zeros((tq, LANES), F32)

    def segment(state, k, v, logit_fn):
        out = []
        for s in range(2):
            m, l, acc = state[s]
            logits = lax.dot_general(qs[s], k_of(k, s), (((1,), (1,)), ((), ())),
                                     preferred_element_type=F32)
            if logit_fn is not None:
                logits = logit_fn(logits, s)
            m_new = jnp.maximum(m, jnp.max(logits, axis=-1, keepdims=True))
            alpha = jnp.exp(m - m_new)
            p = jnp.exp(logits - m_new)
            l = alpha * l + jnp.sum(p, axis=-1, keepdims=True)
            acc = alpha * acc + jnp.dot(p.astype(BF16), v, preferred_element_type=F32)
            out.append((m_new, l, acc))
        return tuple(out)

    state = (init(0), init(1))

    if main == "full":
        def body(c, carry):
            st = ((carry[0], carry[1], carry[2]), (carry[3], carry[4], carry[5]))
            off = pl.multiple_of(c * TK_FULL, TK_FULL)
            st = segment(st, k_ref[pl.ds(off, TK_FULL), :], v_ref[pl.ds(off, TK_FULL), :], None)
            return st[0] + st[1]
        carry = lax.fori_loop(0, seq // TK_FULL, body, state[0] + state[1])
        state = (tuple(carry[:3]), tuple(carry[3:]))
    elif main == "window":
        span = tq + 2 * WINDOW
        q0 = qi * tq
        w0 = jnp.clip(q0 - WINDOW, 0, seq - span)
        off = pl.multiple_of(w0, WINDOW)
        delta = w0 - q0
        row = lax.broadcasted_iota(jnp.int32, (tq, span), 0)
        col = lax.broadcasted_iota(jnp.int32, (tq, span), 1)
        inside = jnp.abs(col - row + delta) <= WINDOW
        state = segment(state, k_ref[pl.ds(off, span), :], v_ref[pl.ds(off, span), :],
                        lambda lg, s: jnp.where(inside, lg, NEG))
    elif main == "na":
        rows_q = tq // GRID_W
        span_rows = 2 * NA_ROWS
        w0 = jnp.clip(qi * rows_q - NA_ROWS // 2, 0, seq // GRID_W - span_rows)
        off = pl.multiple_of(w0 * GRID_W, GRID_W)
        span = span_rows * GRID_W
        state = segment(state, k_ref[pl.ds(off, span), :], v_ref[pl.ds(off, span), :],
                        lambda lg, s: lg + bias_ref[0, s])

    state = segment(state, kc_ref[...], vc_ref[...], None)

    (_, l_lo, acc_lo), (_, l_hi, acc_hi) = state
    o_lo = acc_lo / l_lo
    o_hi = acc_hi / l_hi
    if mode == "diff":
        lv = lam_ref[...]
        lam = (jnp.exp(jnp.sum(lv[0:1] * lv[1:2], axis=-1, keepdims=True))
               - jnp.exp(jnp.sum(lv[2:3] * lv[3:4], axis=-1, keepdims=True)) + lam_init)
        o = o_lo - lam * o_hi
        ms = jnp.mean(o * o, axis=-1, keepdims=True)
        o = o * lax.rsqrt(ms + EPS) * subln_ref[...] * (1.0 - lam_init)
    else:
        o = jnp.where(low, o_lo, o_hi)
    o_ref[...] = o.astype(BF16)


def _attention(p, *, name, mode, split, main, n_pairs, q_blk, k_blk, v_blk, out_cols, ctx_queries,
               nbatch, seq, ctx_len, bias=None, sink=None, lam=None, subln=None, lam_init=0.0):
    n_lat = nbatch * seq
    qw = LANES if split == "mask" else 2 * LANES
    if ctx_queries:
        tq, nq = ctx_len, 1
        q_row = lambda b, j, i: n_lat // ctx_len + b
        out_rows = nbatch * ctx_len
        out_row = lambda b, j, i: b
    else:
        tq, nq = TQ, seq // TQ
        q_row = lambda b, j, i: b * nq + i
        out_rows = n_lat
        out_row = q_row
    ctx_row = lambda b, j, i: n_lat // ctx_len + b

    in_specs = [pl.BlockSpec((tq, qw), lambda b, j, i: (q_row(b, j, i), q_blk(j)))]
    args = [p]
    if main is not None:
        in_specs += [pl.BlockSpec((seq, qw), lambda b, j, i: (b, k_blk(j))),
                     pl.BlockSpec((seq, LANES), lambda b, j, i: (b, v_blk(j)))]
        args += [p, p]
    in_specs += [pl.BlockSpec((ctx_len, qw), lambda b, j, i: (ctx_row(b, j, i), k_blk(j))),
                 pl.BlockSpec((ctx_len, LANES), lambda b, j, i: (ctx_row(b, j, i), v_blk(j)))]
    args += [p, p]
    if main == "na":
        span = 2 * NA_ROWS * GRID_W
        cls = lambda i: jnp.where(i == 0, 0, jnp.where(i == nq - 1, 2, 1))
        in_specs.append(pl.BlockSpec((1, 2, tq, span), lambda b, j, i: (cls(i), j, 0, 0)))
        args.append(bias)
    if sink is not None:
        in_specs.append(pl.BlockSpec(memory_space=pltpu.SMEM))
        args.append(sink)
    if mode == "diff":
        in_specs += [_whole(lam.shape), _whole(subln.shape)]
        args += [lam, subln]

    kern = functools.partial(_attn_kernel, mode=mode, split=split, main=main, seq=seq,
                             has_sink=sink is not None, lam_init=lam_init, tq=tq)
    return pl.pallas_call(
        kern,
        grid=(nbatch, n_pairs, nq),
        in_specs=in_specs,
        out_specs=pl.BlockSpec((tq, LANES), lambda b, j, i: (out_row(b, j, i), j)),
        out_shape=jax.ShapeDtypeStruct((out_rows, out_cols), BF16),
        compiler_params=_params("parallel", "parallel", "arbitrary"),
        name=name,
    )(*args)


def _na_bias_kernel(rpb_ref, o_ref, *, n_rows, rows_q):
    h = pl.program_id(0)
    n_dr = 2 * NA_ROWS - 1
    n_dc = 2 * NA_COLS - 1
    cq = lax.broadcasted_iota(jnp.int32, (GRID_W, LANES), 0)
    kc = lax.broadcasted_iota(jnp.int32, (GRID_W, LANES), 1) & (GRID_W - 1)
    dc_idx = jnp.clip(kc - cq, -(NA_COLS - 1), NA_COLS - 1) + NA_COLS - 1
    cs = jnp.clip(cq - NA_COLS // 2, 0, GRID_W - NA_COLS)
    col_ok = (kc >= cs) & (kc < cs + NA_COLS)
    tiles = [jnp.zeros((GRID_W, LANES), F32) for _ in range(n_dr)]
    for dc in range(n_dc):
        hit = dc_idx == dc
        for dr in range(n_dr):
            tiles[dr] = jnp.where(hit, rpb_ref[(h * n_dr + dr) * n_dc + dc], tiles[dr])
    tiles = [jnp.where(col_ok, t, NEG) for t in tiles]
    masked = jnp.full((GRID_W, LANES), NEG, F32)
    low = lax.broadcasted_iota(jnp.int32, (GRID_W, LANES), 1) < HALF

    span_rows = 2 * NA_ROWS
    first_q_row = (0, span_rows, n_rows - rows_q)
    for c in range(3):
        r0 = first_q_row[c]
        w0 = min(max(r0 - NA_ROWS // 2, 0), n_rows - span_rows)
        for qr in range(rows_q):
            r = r0 + qr
            rs = min(max(r - NA_ROWS // 2, 0), n_rows - NA_ROWS)
            for m in range(span_rows // 2):
                halves = []
                for kr in (2 * m, 2 * m + 1):
                    k_abs = w0 + kr
                    halves.append(tiles[k_abs - r + NA_ROWS - 1] if rs <= k_abs < rs + NA_ROWS else masked)
                o_ref[c, 0, qr * GRID_W:(qr + 1) * GRID_W, m * LANES:(m + 1) * LANES] = (
                    jnp.where(low, halves[0], halves[1]))


def _na_bias(rpb, seq):
    n_rows = seq // GRID_W
    rows_q = TQ // GRID_W
    span = 2 * NA_ROWS * GRID_W
    kern = functools.partial(_na_bias_kernel, n_rows=n_rows, rows_q=rows_q)
    return pl.pallas_call(
        kern,
        grid=(NA_HEADS,),
        in_specs=[pl.BlockSpec(memory_space=pltpu.SMEM)],
        out_specs=pl.BlockSpec((3, 1, TQ, span), lambda h: (0, h, 0, 0)),
        out_shape=jax.ShapeDtypeStruct((3, NA_HEADS, TQ, span), F32),
        compiler_params=_params("parallel"),
        name="na_bias",
    )(rpb.reshape(-1))


def _outproj_kernel(x_ref, oa_ref, ob_ref, w_ref, gate_ref, o_ref):
    half = oa_ref.shape[1]
    y = (jnp.dot(oa_ref[...], w_ref[:half, :], preferred_element_type=F32)
         + jnp.dot(ob_ref[...], w_ref[half:, :], preferred_element_type=F32))
    o_ref[...] = x_ref[...] + gate_ref[0] * y


def _outproj(xa, oa, ob, w, gate, n_tokens, bpb, nbatch):
    d = xa.shape[1]
    half = oa.shape[1]
    row = lambda i: (jnp.minimum(i // bpb, nbatch), 0, 0)
    return pl.pallas_call(
        _outproj_kernel,
        grid=(n_tokens // TM,),
        in_specs=[
            pl.BlockSpec((TM, d), lambda i: (i, 0)),
            pl.BlockSpec((TM, half), lambda i: (i, 0)),
            pl.BlockSpec((TM, half), lambda i: (i, 0)),
            _whole(w.shape),
            pl.BlockSpec((1, 1, d), row),
        ],
        out_specs=pl.BlockSpec((TM, d), lambda i: (i, 0)),
        out_shape=jax.ShapeDtypeStruct((n_tokens, d), F32),
        compiler_params=_params("parallel"),
        name="outproj",
    )(xa, oa, ob, w, gate)


def _mlp_kernel(x_ref, g_ref, sh_ref, sc_ref, gate_ref, w1_ref, w2_ref, o_ref, h_sc, acc_sc):
    f = pl.program_id(1)

    @pl.when(f == 0)
    def _():
        h_sc[...] = _modulated(x_ref, g_ref, sh_ref, sc_ref).astype(BF16)
        acc_sc[...] = jnp.zeros_like(acc_sc)

    a = jnp.dot(h_sc[...], w1_ref[...], preferred_element_type=F32)
    a = jnp.square(jnp.maximum(a, 0.0)).astype(BF16)
    acc_sc[...] += jnp.dot(a, w2_ref[...], preferred_element_type=F32)

    @pl.when(f == pl.num_programs(1) - 1)
    def _():
        o_ref[...] = x_ref[...] + gate_ref[0] * acc_sc[...]


def _mlp(xa, g, sh, sc, gate, w1, w2, n_tokens, bpb, nbatch):
    d = xa.shape[1]
    dff = w1.shape[1]
    tok_specs, row = _token_specs(d, bpb, nbatch)
    return pl.pallas_call(
        _mlp_kernel,
        grid=(n_tokens // TM, dff // MLP_TF),
        in_specs=tok_specs + [
            pl.BlockSpec((1, 1, d), row),
            pl.BlockSpec((d, MLP_TF), lambda i, f: (0, f)),
            pl.BlockSpec((MLP_TF, d), lambda i, f: (f, 0)),
        ],
        out_specs=pl.BlockSpec((TM, d), lambda i, f: (i, 0)),
        out_shape=jax.ShapeDtypeStruct((n_tokens, d), F32),
        scratch_shapes=[pltpu.VMEM((TM, d), BF16), pltpu.VMEM((TM, d), F32)],
        compiler_params=_params("parallel", "arbitrary"),
        name="mlp",
    )(xa, g, sh, sc, gate, w1, w2)


def _block_diag_ones(group):
    idx = np.arange(256) // group
    return jnp.asarray(idx[:, None] == idx[None, :], dtype=BF16)


def _rope_tables(seq, rot_dim, lane0, pad_rows):
    t = jnp.arange(seq, dtype=jnp.int32)
    row = (t // GRID_W).astype(F32)
    col = (t % GRID_W).astype(F32)
    n_freq = rot_dim // 4
    freqs = jnp.power(ROPE_BASE, -jnp.arange(n_freq, dtype=F32) / n_freq)
    ar = row[:, None] * freqs[None, :]
    ac = col[:, None] * freqs[None, :]
    ang = jnp.concatenate([ar, ar, ac, ac], axis=-1)
    sign = jnp.asarray(np.tile(np.repeat([-1.0, 1.0], n_freq), 2), F32)
    cos, sin = jnp.cos(ang), jnp.sin(ang) * sign[None, :]
    if lane0 is None:
        reps = LANES // rot_dim
        cos, sin = jnp.tile(cos, (1, reps)), jnp.tile(sin, (1, reps))
    else:
        pad = ((0, 0), (lane0, LANES - lane0 - rot_dim))
        cos = jnp.pad(cos, pad, constant_values=1.0)
        sin = jnp.pad(sin, pad)
    cos = jnp.concatenate([cos, jnp.ones((pad_rows, LANES), F32)], axis=0)
    sin = jnp.concatenate([sin, jnp.zeros((pad_rows, LANES), F32)], axis=0)
    return cos, sin


def _tile_gain(g, reps, scale=1.0):
    return (jnp.tile(g.astype(F32), reps) * scale)[None, :]


def _even_weights(w, q_g, k_g, sq_g, sk_g):
    nq = DIFF_HEADS * 2 * HEAD_DIM
    qa, ka, va = w[:, :nq], w[:, nq:2 * nq], w[:, 2 * nq:3 * nq]
    o = 3 * nq
    qb = w[:, o:o + SWA_Q_HEADS * HEAD_DIM]
    o += SWA_Q_HEADS * HEAD_DIM
    kb = [w[:, o + i * HEAD_DIM:o + (i + 1) * HEAD_DIM] for i in range(SWA_KV_HEADS)]
    o += SWA_KV_HEADS * HEAD_DIM
    vb = [w[:, o + i * HEAD_DIM:o + (i + 1) * HEAD_DIM] for i in range(SWA_KV_HEADS)]
    dup = lambda parts: [p for p in parts for _ in range(2)]
    w_new = jnp.concatenate([qa, ka, va, qb] + dup(kb) + dup(vb), axis=1).astype(BF16)
    scale = HEAD_DIM ** -0.5
    ones = lambda n: jnp.ones((1, n), F32)
    gain = jnp.concatenate([
        _tile_gain(q_g, 2 * DIFF_HEADS, scale), _tile_gain(k_g, 2 * DIFF_HEADS), ones(nq),
        _tile_gain(sq_g, SWA_Q_HEADS, scale), _tile_gain(sk_g, 2 * SWA_KV_HEADS),
        ones(2 * SWA_KV_HEADS * HEAD_DIM)], axis=1)
    return w_new, gain


def _odd_weights(w, wq_up, wkv_up, qa_g, kva_g, mq_g, mk_g, nq_g, nk_g):
    d = w.shape[0]
    o = 0
    parts = []
    for n in (MLA_Q_LORA, MLA_KV_LORA, MLA_ROPE, NA_HEADS * HEAD_DIM, NA_HEADS * HEAD_DIM, NA_HEADS * HEAD_DIM):
        parts.append(w[:, o:o + n])
        o += n
    q_a, kv_a, k_r, nq, nk, nv = parts
    slot = jnp.concatenate([jnp.zeros((d, MLA_NOPE), w.dtype), k_r,
                            jnp.zeros((d, LANES - MLA_QK), w.dtype)], axis=1)
    w_new = jnp.concatenate([q_a, kv_a, nq, nk, nv, slot], axis=1).astype(BF16)
    wq = jnp.pad(wq_up.reshape(MLA_Q_LORA, MLA_HEADS, MLA_QK), ((0, 0), (0, 0), (0, MLA_PAD - MLA_QK)))
    wq = wq.reshape(MLA_Q_LORA, MLA_HEADS * MLA_PAD).astype(BF16)
    wkv = wkv_up.reshape(MLA_KV_LORA, MLA_HEADS, MLA_NOPE + MLA_V)
    wk = jnp.pad(wkv[..., :MLA_NOPE], ((0, 0), (0, 0), (0, MLA_PAD - MLA_NOPE))).reshape(MLA_KV_LORA, -1)
    wv = wkv[..., MLA_NOPE:].reshape(MLA_KV_LORA, MLA_HEADS * MLA_V)
    wkv_new = jnp.concatenate([wk, wv], axis=1).astype(BF16)
    pad_gain = lambda g, scale: _tile_gain(jnp.pad(g.astype(F32), (0, MLA_PAD - MLA_QK)), 2, scale)
    gains = (qa_g.astype(F32)[None, :], kva_g.astype(F32)[None, :],
             pad_gain(mq_g, MLA_QK ** -0.5), pad_gain(mk_g, 1.0),
             _tile_gain(nq_g, 4, HEAD_DIM ** -0.5), _tile_gain(nk_g, 4))
    return w_new, wq, wkv_new, gains


def kernel(x, c, ctx, c_ctx, ada_w, ada_b, norm1_g, norm2_g, w_out, mlp_w1, mlp_w2, ev_w_in, diff_q_g,
           diff_k_g, diff_lam, diff_subln_g, swa_q_g, swa_k_g, swa_sink, od_w_in, mla_qa_g, mla_kva_g,
           mla_wq_up, mla_wkv_up, mla_q_g, mla_k_g, na_q_g, na_k_g, na_rpb):
    nbatch, seq, d = x.shape
    ctx_len = ctx.shape[1]
    depth = ada_w.shape[0]
    n_lat = nbatch * seq
    n_ctx = nbatch * ctx_len
    assert seq % TM == 0 and n_ctx == TM and seq % TQ == 0 and seq % GRID_W == 0
    assert seq // GRID_W >= 2 * NA_ROWS + TQ // GRID_W and seq >= TQ + 2 * WINDOW
    bpb = seq // TM

    mod_rows = 8
    cvec = jnp.concatenate([c, c_ctx[None, :], jnp.zeros((mod_rows - nbatch - 1, d), F32)], axis=0)
    mod = _modulation(cvec, ada_w, ada_b)

    cos64, sin64 = _rope_tables(seq, HEAD_DIM, None, n_ctx)
    cos32, sin32 = _rope_tables(seq, MLA_ROPE, MLA_NOPE, n_ctx)
    g64 = _block_diag_ones(HEAD_DIM)
    g128 = _block_diag_ones(LANES)

    xa = jnp.concatenate([x.reshape(n_lat, d), ctx.reshape(n_ctx, d)], axis=0)
    common = dict(nbatch=nbatch, seq=seq, ctx_len=ctx_len)

    for layer in range(depth):
        need_ctx = layer < depth - 1
        n_tok = n_lat + n_ctx if need_ctx else n_lat
        m = mod[layer, :nbatch + 1]
        sh1, sc1, g1, sh2, sc2, g2 = [m[:, k * d:(k + 1) * d].reshape(nbatch + 1, 1, d) for k in range(6)]
        n1 = norm1_g[layer][None, :]
        n2 = norm2_g[layer][None, :]
        i = layer // 2
        if layer % 2 == 0:
            lam_init = 0.8 - 0.6 * math.exp(-0.3 * layer)
            w_in, gain = _even_weights(ev_w_in[i], diff_q_g[i], diff_k_g[i], swa_q_g[i], swa_k_g[i])
            p = _proj_even(xa, n1, sh1, sc1, w_in, gain, g64, cos64, sin64, bpb, nbatch)
            lam = diff_lam[i].astype(F32)
            subln = diff_subln_g[i].astype(F32)[None, :]
            sink = swa_sink[i].astype(F32)
            diff_kw = dict(mode="diff", split="mask", n_pairs=DIFF_HEADS, q_blk=lambda j: j,
                           k_blk=lambda j: 4 + j, v_blk=lambda j: 8 + j, out_cols=DIFF_HEADS * LANES,
                           lam=lam, subln=subln, lam_init=lam_init, **common)
            swa_kw = dict(mode="pair", split="mask", n_pairs=SWA_Q_HEADS // 2, q_blk=lambda j: 12 + j,
                          k_blk=lambda j: 16 + j // 2, v_blk=lambda j: 18 + j // 2,
                          out_cols=SWA_Q_HEADS * HEAD_DIM, sink=sink, **common)
            oa = _attention(p, name="diff_attn", main="full", ctx_queries=False, **diff_kw)
            ob = _attention(p, name="window_attn", main="window", ctx_queries=False, **swa_kw)
            if need_ctx:
                oa_c = _attention(p, name="diff_attn_ctx", main=None, ctx_queries=True, **diff_kw)
                ob_c = _attention(p, name="window_attn_ctx", main=None, ctx_queries=True, **swa_kw)
        else:
            w_in, wq, wkv, gains = _odd_weights(od_w_in[i], mla_wq_up[i], mla_wkv_up[i], mla_qa_g[i],
                                                mla_kva_g[i], mla_q_g[i], mla_k_g[i], na_q_g[i], na_k_g[i])
            p = _proj_odd(xa, n1, sh1, sc1, w_in, wq, wkv, gains, g64, g128, cos32, sin32, bpb, nbatch)
            bias = _na_bias(na_rpb[i].astype(F32), seq)
            mla_kw = dict(mode="pair", split="slice", n_pairs=MLA_HEADS // 2, q_blk=lambda j: j,
                          k_blk=lambda j: 4 + j, v_blk=lambda j: 16 + j, out_cols=MLA_HEADS * MLA_V, **common)
            na_kw = dict(mode="pair", split="mask", n_pairs=NA_HEADS // 2, q_blk=lambda j: 20 + j,
                         k_blk=lambda j: 24 + j, v_blk=lambda j: 28 + j, out_cols=NA_HEADS * HEAD_DIM,
                         **common)
            oa = _attention(p, name="mla_attn", main="full", ctx_queries=False, **mla_kw)
            ob = _attention(p, name="na_attn", main="na", ctx_queries=False, bias=bias, **na_kw)
            if need_ctx:
                oa_c = _attention(p, name="mla_attn_ctx", main=None, ctx_queries=True, **mla_kw)
                ob_c = _attention(p, name="na_attn_ctx", main=None, ctx_queries=True, **na_kw)
        if need_ctx:
            oa = jnp.concatenate([oa, oa_c], axis=0)
            ob = jnp.concatenate([ob, ob_c], axis=0)
        xa = _outproj(xa, oa, ob, w_out[layer].astype(BF16), g1, n_tok, bpb, nbatch)
        xa = _mlp(xa, n2, sh2, sc2, g2, mlp_w1[layer].astype(BF16), mlp_w2[layer].astype(BF16),
                  n_tok, bpb, nbatch)
    return xa.reshape(nbatch, seq, d)
```

```python
import functools
import math

import numpy as np
import jax
import jax.numpy as jnp
from jax import lax
from jax.experimental import pallas as pl
from jax.experimental.pallas import tpu as pltpu

F32 = jnp.float32
BF16 = jnp.bfloat16

LANES = 128
HALF = LANES // 2
GRID_W = 64
HEAD_DIM = 64
EPS = 1e-6
ROPE_BASE = 10000.0
DIFF_HEADS = 4
SWA_Q_HEADS = 8
SWA_KV_HEADS = 2
WINDOW = 128
MLA_HEADS = 8
MLA_Q_LORA = 512
MLA_KV_LORA = 256
MLA_NOPE = 64
MLA_ROPE = 32
MLA_QK = MLA_NOPE + MLA_ROPE
MLA_V = 64
NA_HEADS = 8
NA_ROWS = 8
NA_COLS = 16
NEG = -1e30
LOG2E = math.log2(math.e)
MAX_FREE_LOGIT_BOUND = 50.0

TM = 512
TQ = 512
TK_FULL = 512
TK_MAX_FREE = 2048
MOD_TN = 1536
MLP_TF = 1024
VMEM_LIMIT = 56 * 1024 * 1024


def _params(*sem):
    return pltpu.CompilerParams(dimension_semantics=sem, vmem_limit_bytes=VMEM_LIMIT)


def _mod_kernel(c_ref, w_ref, b_ref, o_ref):
    c = c_ref[...]
    a = (c * jax.nn.sigmoid(c)).astype(BF16)
    w = w_ref[0].astype(BF16)
    o_ref[0] = jnp.dot(a, w, preferred_element_type=F32) + b_ref[0]


def _modulation(cvec, ada_w, ada_b):
    depth, d, n = ada_w.shape
    rows = cvec.shape[0]
    return pl.pallas_call(
        _mod_kernel,
        grid=(depth, n // MOD_TN),
        in_specs=[
            pl.BlockSpec((rows, d), lambda l, j: (0, 0)),
            pl.BlockSpec((1, d, MOD_TN), lambda l, j: (l, 0, j)),
            pl.BlockSpec((1, 1, MOD_TN), lambda l, j: (l, 0, j)),
        ],
        out_specs=pl.BlockSpec((1, rows, MOD_TN), lambda l, j: (l, 0, j)),
        out_shape=jax.ShapeDtypeStruct((depth, rows, n), F32),
        compiler_params=_params("parallel", "parallel"),
        name="modulation",
    )(cvec, ada_w, ada_b.reshape(depth, 1, n))


def _modulated(x_ref, g_ref, sh_ref, sc_ref):
    x = x_ref[...]
    ms = jnp.mean(x * x, axis=-1, keepdims=True)
    y = x * lax.rsqrt(ms + EPS) * g_ref[...]
    return y * (1.0 + sc_ref[0]) + sh_ref[0]


def _group_meansq(x, g_ref, inv_n):
    sq = x * x
    hi = sq.astype(BF16)
    lo = (sq - hi.astype(F32)).astype(BF16)
    g = g_ref[...]
    ss = jnp.dot(hi, g, preferred_element_type=F32) + jnp.dot(lo, g, preferred_element_type=F32)
    return ss * inv_n


def _rope(x, cos, sin_signed, quarter):
    lane = lax.broadcasted_iota(jnp.int32, x.shape, 1)
    odd = (lane & quarter) != 0
    from_lower = pltpu.roll(x, quarter, 1)
    from_upper = pltpu.roll(x, LANES - quarter, 1)
    return x * cos + jnp.where(odd, from_lower, from_upper) * sin_signed


def _norm_rope_chunk(acc, g_ref, inv_n, gain, cos, sin, quarter):
    y = acc * lax.rsqrt(_group_meansq(acc, g_ref, inv_n) + EPS) * gain
    if quarter is None:
        return y
    return jnp.concatenate([_rope(y[:, :LANES], cos, sin, quarter),
                            _rope(y[:, LANES:], cos, sin, quarter)], axis=1)


def _token_specs(d, bpb, nbatch):
    row = lambda i, *_: (jnp.minimum(i // bpb, nbatch), 0, 0)
    return [
        pl.BlockSpec((TM, d), lambda i, *_: (i, 0)),
        pl.BlockSpec((1, d), lambda i, *_: (0, 0)),
        pl.BlockSpec((1, 1, d), row),
        pl.BlockSpec((1, 1, d), row),
    ], row


def _rope_spec(bpb, nbatch):
    return pl.BlockSpec((TM, LANES), lambda i: (jnp.where(i < bpb * nbatch, i % bpb, bpb), 0))


def _whole(shape):
    return pl.BlockSpec(shape, lambda *_: (0,) * len(shape))


EVEN_COLS = 2560
EVEN_NORM_CHUNKS = (0, 1, 2, 3, 6, 7, 8)


def _proj_even_kernel(x_ref, g_ref, sh_ref, sc_ref, w_ref, gain_ref, g64_ref, cos_ref, sin_ref, o_ref):
    h = _modulated(x_ref, g_ref, sh_ref, sc_ref).astype(BF16)
    cos = cos_ref[...]
    sin = sin_ref[...]
    for c in range(EVEN_COLS // 256):
        cols = slice(c * 256, (c + 1) * 256)
        acc = jnp.dot(h, w_ref[:, cols], preferred_element_type=F32)
        if c in EVEN_NORM_CHUNKS:
            acc = _norm_rope_chunk(acc, g64_ref, 1.0 / HEAD_DIM, gain_ref[:, cols], cos, sin, HEAD_DIM // 4)
        o_ref[:, cols] = acc.astype(BF16)


def _proj_even(xa, g, sh, sc, w, gain, g64, cos, sin, bpb, nbatch):
    na, d = xa.shape
    tok_specs, _ = _token_specs(d, bpb, nbatch)
    return pl.pallas_call(
        _proj_even_kernel,
        grid=(na // TM,),
        in_specs=tok_specs + [
            _whole(w.shape), _whole(gain.shape), _whole(g64.shape),
            _rope_spec(bpb, nbatch), _rope_spec(bpb, nbatch),
        ],
        out_specs=pl.BlockSpec((TM, EVEN_COLS), lambda i: (i, 0)),
        out_shape=jax.ShapeDtypeStruct((na, EVEN_COLS), BF16),
        compiler_params=_params("parallel"),
        name="proj_even",
    )(xa, g, sh, sc, w, gain, g64, cos, sin)


ODD_W_COLS = 2432
ODD_COLS = 4096
MLA_PAD = LANES


def _proj_odd_kernel(x_ref, g_ref, sh_ref, sc_ref, w_ref, wq_ref, wkv_ref, qa_g_ref, kva_g_ref,
                     mq_g_ref, mk_g_ref, nq_g_ref, nk_g_ref, g64_ref, g128_ref, cos_ref, sin_ref, o_ref):
    h = _modulated(x_ref, g_ref, sh_ref, sc_ref).astype(BF16)
    cos = cos_ref[...]
    sin = sin_ref[...]
    quarter = MLA_ROPE // 4

    def low_rank(cols, gain_ref):
        a = jnp.dot(h, w_ref[:, cols], preferred_element_type=F32)
        ms = jnp.mean(a * a, axis=-1, keepdims=True)
        return (a * lax.rsqrt(ms + EPS) * gain_ref[...]).astype(BF16)

    qa = low_rank(slice(0, MLA_Q_LORA), qa_g_ref)
    for c in range(MLA_HEADS * MLA_PAD // 256):
        cols = slice(c * 256, (c + 1) * 256)
        acc = jnp.dot(qa, wq_ref[:, cols], preferred_element_type=F32)
        y = _norm_rope_chunk(acc, g128_ref, 1.0 / MLA_QK, mq_g_ref[...], cos, sin, quarter)
        o_ref[:, cols] = y.astype(BF16)

    kva = low_rank(slice(MLA_Q_LORA, MLA_Q_LORA + MLA_KV_LORA), kva_g_ref)
    kslot = jnp.dot(h, w_ref[:, 2304:2432], preferred_element_type=F32)
    kslot2 = jnp.concatenate([kslot, kslot], axis=1)
    for c in range(MLA_HEADS * MLA_PAD // 256):
        cols = slice(c * 256, (c + 1) * 256)
        acc = jnp.dot(kva, wkv_ref[:, cols], preferred_element_type=F32) + kslot2
        y = _norm_rope_chunk(acc, g128_ref, 1.0 / MLA_QK, mk_g_ref[...], cos, sin, quarter)
        o_ref[:, 1024 + c * 256:1024 + (c + 1) * 256] = y.astype(BF16)
    for c in range(MLA_HEADS * MLA_V // 256):
        acc = jnp.dot(kva, wkv_ref[:, 1024 + c * 256:1024 + (c + 1) * 256], preferred_element_type=F32)
        o_ref[:, 2048 + c * 256:2048 + (c + 1) * 256] = acc.astype(BF16)

    for c in range(6):
        acc = jnp.dot(h, w_ref[:, 768 + c * 256:768 + (c + 1) * 256], preferred_element_type=F32)
        if c < 2:
            acc = _norm_rope_chunk(acc, g64_ref, 1.0 / HEAD_DIM, nq_g_ref[...], None, None, None)
        elif c < 4:
            acc = _norm_rope_chunk(acc, g64_ref, 1.0 / HEAD_DIM, nk_g_ref[...], None, None, None)
        o_ref[:, 2560 + c * 256:2560 + (c + 1) * 256] = acc.astype(BF16)


def _proj_odd(xa, g, sh, sc, w, wq, wkv, gains, g64, g128, cos, sin, bpb, nbatch):
    na, d = xa.shape
    tok_specs, _ = _token_specs(d, bpb, nbatch)
    consts = (w, wq, wkv) + tuple(gains) + (g64, g128)
    return pl.pallas_call(
        _proj_odd_kernel,
        grid=(na // TM,),
        in_specs=tok_specs + [_whole(a.shape) for a in consts] + [_rope_spec(bpb, nbatch)] * 2,
        out_specs=pl.BlockSpec((TM, ODD_COLS), lambda i: (i, 0)),
        out_shape=jax.ShapeDtypeStruct((na, ODD_COLS), BF16),
        compiler_params=_params("parallel"),
        name="proj_odd",
    )(xa, g, sh, sc, *consts, cos, sin)


def _attn_kernel(*refs, mode, split, main, seq, has_sink, lam_init, tq, base2, guarded):
    it = iter(refs)
    q_ref = next(it)
    k_ref = v_ref = bias_ref = sink_ref = lam_ref = subln_ref = safe_ref = None
    if main is not None:
        k_ref, v_ref = next(it), next(it)
    kc_ref, vc_ref = next(it), next(it)
    if main == "na":
        bias_ref = next(it)
    if has_sink:
        sink_ref = next(it)
    if guarded:
        safe_ref = next(it)
    if mode == "diff":
        lam_ref, subln_ref = next(it), next(it)
    o_ref = next(it)

    pair = pl.program_id(1)
    qi = pl.program_id(2)
    q = q_ref[...]
    low = lax.broadcasted_iota(jnp.int32, (1, LANES), 1) < HALF
    if split == "mask":
        zero = jnp.zeros_like(q)
        qs = (jnp.where(low, q, zero), jnp.where(low, zero, q))
        k_of = lambda k, s: k
    else:
        qs = (q[:, :LANES], q[:, LANES:])
        k_of = lambda k, s: k[:, s * LANES:(s + 1) * LANES]
    exp = jnp.exp2 if base2 else jnp.exp

    def logits_of(s, k):
        return lax.dot_general(qs[s], k_of(k, s), (((1,), (1,)), ((), ())), preferred_element_type=F32)

    def finish(o_lo, o_hi):
        if mode == "diff":
            lv = lam_ref[...]
            lam = (jnp.exp(jnp.sum(lv[0:1] * lv[1:2], axis=-1, keepdims=True))
                   - jnp.exp(jnp.sum(lv[2:3] * lv[3:4], axis=-1, keepdims=True)) + lam_init)
            o = o_lo - lam * o_hi
            ms = jnp.mean(o * o, axis=-1, keepdims=True)
            o = o * lax.rsqrt(ms + EPS) * subln_ref[...] * (1.0 - lam_init)
        else:
            o = jnp.where(low, o_lo, o_hi)
        o_ref[...] = o.astype(BF16)

    def init(s):
        if has_sink:
            m0 = jnp.full((tq, 1), sink_ref[2 * pair + s], F32)
            l0 = jnp.ones((tq, 1), F32)
        else:
            m0 = jnp.full((tq, 1), NEG, F32)
            l0 = jnp.zeros((tq, 1), F32)
        return m0, l0, jnp.zeros((tq, LANES), F32)

    def segment(state, k, v, logit_fn):
        out = []
        for s in range(2):
            m, l, acc = state[s]
            logits = logits_of(s, k)
            if logit_fn is not None:
                logits = logit_fn(logits, s)
            m_new = jnp.maximum(m, jnp.max(logits, axis=-1, keepdims=True))
            alpha = exp(m - m_new)
            p = exp(logits - m_new)
            l = alpha * l + jnp.sum(p, axis=-1, keepdims=True)
            acc = alpha * acc + jnp.dot(p.astype(BF16), v, preferred_element_type=F32)
            out.append((m_new, l, acc))
        return tuple(out)

    def online():
        state = (init(0), init(1))
        if main == "full":
            def body(c, carry):
                st = ((carry[0], carry[1], carry[2]), (carry[3], carry[4], carry[5]))
                off = pl.multiple_of(c * TK_FULL, TK_FULL)
                st = segment(st, k_ref[pl.ds(off, TK_FULL), :], v_ref[pl.ds(off, TK_FULL), :], None)
                return st[0] + st[1]
            carry = lax.fori_loop(0, seq // TK_FULL, body, state[0] + state[1])
            state = (tuple(carry[:3]), tuple(carry[3:]))
        elif main == "window":
            span = tq + 2 * WINDOW
            q0 = qi * tq
            w0 = jnp.clip(q0 - WINDOW, 0, seq - span)
            off = pl.multiple_of(w0, WINDOW)
            delta = w0 - q0
            row = lax.broadcasted_iota(jnp.int32, (tq, span), 0)
            col = lax.broadcasted_iota(jnp.int32, (tq, span), 1)
            inside = jnp.abs(col - row + delta) <= WINDOW
            state = segment(state, k_ref[pl.ds(off, span), :], v_ref[pl.ds(off, span), :],
                            lambda lg, s: jnp.where(inside, lg, NEG))
        elif main == "na":
            rows_q = tq // GRID_W
            span_rows = 2 * NA_ROWS
            w0 = jnp.clip(qi * rows_q - NA_ROWS // 2, 0, seq // GRID_W - span_rows)
            off = pl.multiple_of(w0 * GRID_W, GRID_W)
            span = span_rows * GRID_W
            state = segment(state, k_ref[pl.ds(off, span), :], v_ref[pl.ds(off, span), :],
                            lambda lg, s: lg + bias_ref[0, s])
        state = segment(state, kc_ref[...], vc_ref[...], None)
        (_, l_lo, acc_lo), (_, l_hi, acc_hi) = state
        finish(acc_lo / l_lo, acc_hi / l_hi)

    def max_free():
        ones = jnp.ones((TK_MAX_FREE, LANES), BF16)

        def accumulate(acc, k, v_ext):
            out = []
            for s in range(2):
                p = exp(logits_of(s, k)).astype(BF16)
                out.append(acc[s] + jnp.dot(p, v_ext, preferred_element_type=F32))
            return tuple(out)

        def body(c, acc):
            off = pl.multiple_of(c * TK_MAX_FREE, TK_MAX_FREE)
            v_ext = jnp.concatenate([v_ref[pl.ds(off, TK_MAX_FREE), :], ones], axis=1)
            return accumulate(acc, k_ref[pl.ds(off, TK_MAX_FREE), :], v_ext)

        zero_acc = jnp.zeros((tq, 2 * LANES), F32)
        acc = lax.fori_loop(0, seq // TK_MAX_FREE, body, (zero_acc, zero_acc))
        vc_ext = jnp.concatenate([vc_ref[...], ones[:vc_ref.shape[0]]], axis=1)
        acc = accumulate(acc, kc_ref[...], vc_ext)
        finish(acc[0][:, :LANES] / acc[0][:, LANES:], acc[1][:, :LANES] / acc[1][:, LANES:])

    if guarded:
        safe = safe_ref[0] == 1
        pl.when(safe)(max_free)
        pl.when(jnp.logical_not(safe))(online)
    else:
        online()


def _attention(p, *, name, mode, split, main, n_pairs, q_blk, k_blk, v_blk, out_cols, ctx_queries,
               nbatch, seq, ctx_len, bias=None, sink=None, lam=None, subln=None, lam_init=0.0,
               base2=False, safe=None):
    n_lat = nbatch * seq
    qw = LANES if split == "mask" else 2 * LANES
    if ctx_queries:
        tq, nq = ctx_len, 1
        q_row = lambda b, j, i: n_lat // ctx_len + b
        out_rows = nbatch * ctx_len
        out_row = lambda b, j, i: b
    else:
        tq, nq = TQ, seq // TQ
        q_row = lambda b, j, i: b * nq + i
        out_rows = n_lat
        out_row = q_row
    ctx_row = lambda b, j, i: n_lat // ctx_len + b

    in_specs = [pl.BlockSpec((tq, qw), lambda b, j, i: (q_row(b, j, i), q_blk(j)))]
    args = [p]
    if main is not None:
        in_specs += [pl.BlockSpec((seq, qw), lambda b, j, i: (b, k_blk(j))),
                     pl.BlockSpec((seq, LANES), lambda b, j, i: (b, v_blk(j)))]
        args += [p, p]
    in_specs += [pl.BlockSpec((ctx_len, qw), lambda b, j, i: (ctx_row(b, j, i), k_blk(j))),
                 pl.BlockSpec((ctx_len, LANES), lambda b, j, i: (ctx_row(b, j, i), v_blk(j)))]
    args += [p, p]
    if main == "na":
        span = 2 * NA_ROWS * GRID_W
        cls = lambda i: jnp.where(i == 0, 0, jnp.where(i == nq - 1, 2, 1))
        in_specs.append(pl.BlockSpec((1, 2, tq, span), lambda b, j, i: (cls(i), j, 0, 0)))
        args.append(bias)
    if sink is not None:
        in_specs.append(pl.BlockSpec(memory_space=pltpu.SMEM))
        args.append(sink)
    guarded = safe is not None and main == "full"
    if guarded:
        in_specs.append(pl.BlockSpec(memory_space=pltpu.SMEM))
        args.append(safe)
    if mode == "diff":
        in_specs += [_whole(lam.shape), _whole(subln.shape)]
        args += [lam, subln]

    kern = functools.partial(_attn_kernel, mode=mode, split=split, main=main, seq=seq,
                             has_sink=sink is not None, lam_init=lam_init, tq=tq, base2=base2,
                             guarded=guarded)
    return pl.pallas_call(
        kern,
        grid=(nbatch, n_pairs, nq),
        in_specs=in_specs,
        out_specs=pl.BlockSpec((tq, LANES), lambda b, j, i: (out_row(b, j, i), j)),
        out_shape=jax.ShapeDtypeStruct((out_rows, out_cols), BF16),
        compiler_params=_params("parallel", "parallel", "arbitrary"),
        name=name,
    )(*args)


def _na_bias_kernel(rpb_ref, o_ref, *, n_rows, rows_q):
    h = pl.program_id(0)
    n_dr = 2 * NA_ROWS - 1
    n_dc = 2 * NA_COLS - 1
    cq = lax.broadcasted_iota(jnp.int32, (GRID_W, LANES), 0)
    kc = lax.broadcasted_iota(jnp.int32, (GRID_W, LANES), 1) & (GRID_W - 1)
    dc_idx = jnp.clip(kc - cq, -(NA_COLS - 1), NA_COLS - 1) + NA_COLS - 1
    cs = jnp.clip(cq - NA_COLS // 2, 0, GRID_W - NA_COLS)
    col_ok = (kc >= cs) & (kc < cs + NA_COLS)
    tiles = [jnp.zeros((GRID_W, LANES), F32) for _ in range(n_dr)]
    for dc in range(n_dc):
        hit = dc_idx == dc
        for dr in range(n_dr):
            tiles[dr] = jnp.where(hit, rpb_ref[(h * n_dr + dr) * n_dc + dc], tiles[dr])
    tiles = [jnp.where(col_ok, t, NEG) for t in tiles]
    masked = jnp.full((GRID_W, LANES), NEG, F32)
    low = lax.broadcasted_iota(jnp.int32, (GRID_W, LANES), 1) < HALF

    span_rows = 2 * NA_ROWS
    first_q_row = (0, span_rows, n_rows - rows_q)
    for c in range(3):
        r0 = first_q_row[c]
        w0 = min(max(r0 - NA_ROWS // 2, 0), n_rows - span_rows)
        for qr in range(rows_q):
            r = r0 + qr
            rs = min(max(r - NA_ROWS // 2, 0), n_rows - NA_ROWS)
            for m in range(span_rows // 2):
                halves = []
                for kr in (2 * m, 2 * m + 1):
                    k_abs = w0 + kr
                    halves.append(tiles[k_abs - r + NA_ROWS - 1] if rs <= k_abs < rs + NA_ROWS else masked)
                o_ref[c, 0, qr * GRID_W:(qr + 1) * GRID_W, m * LANES:(m + 1) * LANES] = (
                    jnp.where(low, halves[0], halves[1]))


def _na_bias(rpb, seq):
    n_rows = seq // GRID_W
    rows_q = TQ // GRID_W
    span = 2 * NA_ROWS * GRID_W
    kern = functools.partial(_na_bias_kernel, n_rows=n_rows, rows_q=rows_q)
    return pl.pallas_call(
        kern,
        grid=(NA_HEADS,),
        in_specs=[pl.BlockSpec(memory_space=pltpu.SMEM)],
        out_specs=pl.BlockSpec((3, 1, TQ, span), lambda h: (0, h, 0, 0)),
        out_shape=jax.ShapeDtypeStruct((3, NA_HEADS, TQ, span), F32),
        compiler_params=_params("parallel"),
        name="na_bias",
    )(rpb.reshape(-1))


def _outproj_kernel(x_ref, oa_ref, ob_ref, w_ref, gate_ref, o_ref):
    half = oa_ref.shape[1]
    y = (jnp.dot(oa_ref[...], w_ref[:half, :], preferred_element_type=F32)
         + jnp.dot(ob_ref[...], w_ref[half:, :], preferred_element_type=F32))
    o_ref[...] = x_ref[...] + gate_ref[0] * y


def _outproj(xa, oa, ob, w, gate, n_tokens, bpb, nbatch):
    d = xa.shape[1]
    half = oa.shape[1]
    row = lambda i: (jnp.minimum(i // bpb, nbatch), 0, 0)
    return pl.pallas_call(
        _outproj_kernel,
        grid=(n_tokens // TM,),
        in_specs=[
            pl.BlockSpec((TM, d), lambda i: (i, 0)),
            pl.BlockSpec((TM, half), lambda i: (i, 0)),
            pl.BlockSpec((TM, half), lambda i: (i, 0)),
            _whole(w.shape),
            pl.BlockSpec((1, 1, d), row),
        ],
        out_specs=pl.BlockSpec((TM, d), lambda i: (i, 0)),
        out_shape=jax.ShapeDtypeStruct((n_tokens, d), F32),
        compiler_params=_params("parallel"),
        name="outproj",
    )(xa, oa, ob, w, gate)


def _mlp_kernel(x_ref, g_ref, sh_ref, sc_ref, gate_ref, w1_ref, w2_ref, o_ref, h_sc, acc_sc):
    f = pl.program_id(1)

    @pl.when(f == 0)
    def _():
        h_sc[...] = _modulated(x_ref, g_ref, sh_ref, sc_ref).astype(BF16)
        acc_sc[...] = jnp.zeros_like(acc_sc)

    a = jnp.dot(h_sc[...], w1_ref[...], preferred_element_type=F32)
    a = jnp.square(jnp.maximum(a, 0.0)).astype(BF16)
    acc_sc[...] += jnp.dot(a, w2_ref[...], preferred_element_type=F32)

    @pl.when(f == pl.num_programs(1) - 1)
    def _():
        o_ref[...] = x_ref[...] + gate_ref[0] * acc_sc[...]


def _mlp(xa, g, sh, sc, gate, w1, w2, n_tokens, bpb, nbatch):
    d = xa.shape[1]
    dff = w1.shape[1]
    tok_specs, row = _token_specs(d, bpb, nbatch)
    return pl.pallas_call(
        _mlp_kernel,
        grid=(n_tokens // TM, dff // MLP_TF),
        in_specs=tok_specs + [
            pl.BlockSpec((1, 1, d), row),
            pl.BlockSpec((d, MLP_TF), lambda i, f: (0, f)),
            pl.BlockSpec((MLP_TF, d), lambda i, f: (f, 0)),
        ],
        out_specs=pl.BlockSpec((TM, d), lambda i, f: (i, 0)),
        out_shape=jax.ShapeDtypeStruct((n_tokens, d), F32),
        scratch_shapes=[pltpu.VMEM((TM, d), BF16), pltpu.VMEM((TM, d), F32)],
        compiler_params=_params("parallel", "arbitrary"),
        name="mlp",
    )(xa, g, sh, sc, gate, w1, w2)


def _block_diag_ones(group):
    idx = np.arange(256) // group
    return jnp.asarray(idx[:, None] == idx[None, :], dtype=BF16)


def _rope_tables(seq, rot_dim, lane0, pad_rows):
    t = jnp.arange(seq, dtype=jnp.int32)
    row = (t // GRID_W).astype(F32)
    col = (t % GRID_W).astype(F32)
    n_freq = rot_dim // 4
    freqs = jnp.power(ROPE_BASE, -jnp.arange(n_freq, dtype=F32) / n_freq)
    ar = row[:, None] * freqs[None, :]
    ac = col[:, None] * freqs[None, :]
    ang = jnp.concatenate([ar, ar, ac, ac], axis=-1)
    sign = jnp.asarray(np.tile(np.repeat([-1.0, 1.0], n_freq), 2), F32)
    cos, sin = jnp.cos(ang), jnp.sin(ang) * sign[None, :]
    if lane0 is None:
        reps = LANES // rot_dim
        cos, sin = jnp.tile(cos, (1, reps)), jnp.tile(sin, (1, reps))
    else:
        pad = ((0, 0), (lane0, LANES - lane0 - rot_dim))
        cos = jnp.pad(cos, pad, constant_values=1.0)
        sin = jnp.pad(sin, pad)
    cos = jnp.concatenate([cos, jnp.ones((pad_rows, LANES), F32)], axis=0)
    sin = jnp.concatenate([sin, jnp.zeros((pad_rows, LANES), F32)], axis=0)
    return cos, sin


def _logits_bounded(q_gain, k_gain, dim):
    bound = jnp.max(jnp.abs(q_gain)) * jnp.max(jnp.abs(k_gain)) * (dim ** 0.5) * 1.02
    return (bound <= MAX_FREE_LOGIT_BOUND).astype(jnp.int32).reshape(1)


def _tile_gain(g, reps, scale=1.0):
    return (jnp.tile(g.astype(F32), reps) * scale)[None, :]


def _even_weights(w, q_g, k_g, sq_g, sk_g):
    nq = DIFF_HEADS * 2 * HEAD_DIM
    qa, ka, va = w[:, :nq], w[:, nq:2 * nq], w[:, 2 * nq:3 * nq]
    o = 3 * nq
    qb = w[:, o:o + SWA_Q_HEADS * HEAD_DIM]
    o += SWA_Q_HEADS * HEAD_DIM
    kb = [w[:, o + i * HEAD_DIM:o + (i + 1) * HEAD_DIM] for i in range(SWA_KV_HEADS)]
    o += SWA_KV_HEADS * HEAD_DIM
    vb = [w[:, o + i * HEAD_DIM:o + (i + 1) * HEAD_DIM] for i in range(SWA_KV_HEADS)]
    dup = lambda parts: [p for p in parts for _ in range(2)]
    w_new = jnp.concatenate([qa, ka, va, qb] + dup(kb) + dup(vb), axis=1).astype(BF16)
    scale = HEAD_DIM ** -0.5
    ones = lambda n: jnp.ones((1, n), F32)
    gain = jnp.concatenate([
        _tile_gain(q_g, 2 * DIFF_HEADS, scale * LOG2E), _tile_gain(k_g, 2 * DIFF_HEADS), ones(nq),
        _tile_gain(sq_g, SWA_Q_HEADS, scale), _tile_gain(sk_g, 2 * SWA_KV_HEADS),
        ones(2 * SWA_KV_HEADS * HEAD_DIM)], axis=1)
    return w_new, gain


def _odd_weights(w, wq_up, wkv_up, qa_g, kva_g, mq_g, mk_g, nq_g, nk_g):
    d = w.shape[0]
    o = 0
    parts = []
    for n in (MLA_Q_LORA, MLA_KV_LORA, MLA_ROPE, NA_HEADS * HEAD_DIM, NA_HEADS * HEAD_DIM, NA_HEADS * HEAD_DIM):
        parts.append(w[:, o:o + n])
        o += n
    q_a, kv_a, k_r, nq, nk, nv = parts
    slot = jnp.concatenate([jnp.zeros((d, MLA_NOPE), w.dtype), k_r,
                            jnp.zeros((d, LANES - MLA_QK), w.dtype)], axis=1)
    w_new = jnp.concatenate([q_a, kv_a, nq, nk, nv, slot], axis=1).astype(BF16)
    wq = jnp.pad(wq_up.reshape(MLA_Q_LORA, MLA_HEADS, MLA_QK), ((0, 0), (0, 0), (0, MLA_PAD - MLA_QK)))
    wq = wq.reshape(MLA_Q_LORA, MLA_HEADS * MLA_PAD).astype(BF16)
    wkv = wkv_up.reshape(MLA_KV_LORA, MLA_HEADS, MLA_NOPE + MLA_V)
    wk = jnp.pad(wkv[..., :MLA_NOPE], ((0, 0), (0, 0), (0, MLA_PAD - MLA_NOPE))).reshape(MLA_KV_LORA, -1)
    wv = wkv[..., MLA_NOPE:].reshape(MLA_KV_LORA, MLA_HEADS * MLA_V)
    wkv_new = jnp.concatenate([wk, wv], axis=1).astype(BF16)
    pad_gain = lambda g, scale: _tile_gain(jnp.pad(g.astype(F32), (0, MLA_PAD - MLA_QK)), 2, scale)
    gains = (qa_g.astype(F32)[None, :], kva_g.astype(F32)[None, :],
             pad_gain(mq_g, MLA_QK ** -0.5 * LOG2E), pad_gain(mk_g, 1.0),
             _tile_gain(nq_g, 4, HEAD_DIM ** -0.5), _tile_gain(nk_g, 4))
    return w_new, wq, wkv_new, gains


def kernel(x, c, ctx, c_ctx, ada_w, ada_b, norm1_g, norm2_g, w_out, mlp_w1, mlp_w2, ev_w_in, diff_q_g,
           diff_k_g, diff_lam, diff_subln_g, swa_q_g, swa_k_g, swa_sink, od_w_in, mla_qa_g, mla_kva_g,
           mla_wq_up, mla_wkv_up, mla_q_g, mla_k_g, na_q_g, na_k_g, na_rpb):
    nbatch, seq, d = x.shape
    ctx_len = ctx.shape[1]
    depth = ada_w.shape[0]
    n_lat = nbatch * seq
    n_ctx = nbatch * ctx_len
    assert seq % TM == 0 and n_ctx == TM and seq % TQ == 0 and seq % GRID_W == 0
    assert seq // GRID_W >= 2 * NA_ROWS + TQ // GRID_W and seq >= TQ + 2 * WINDOW
    bpb = seq // TM

    mod_rows = 8
    cvec = jnp.concatenate([c, c_ctx[None, :], jnp.zeros((mod_rows - nbatch - 1, d), F32)], axis=0)
    mod = _modulation(cvec, ada_w, ada_b)

    cos64, sin64 = _rope_tables(seq, HEAD_DIM, None, n_ctx)
    cos32, sin32 = _rope_tables(seq, MLA_ROPE, MLA_NOPE, n_ctx)
    g64 = _block_diag_ones(HEAD_DIM)
    g128 = _block_diag_ones(LANES)

    xa = jnp.concatenate([x.reshape(n_lat, d), ctx.reshape(n_ctx, d)], axis=0)
    common = dict(nbatch=nbatch, seq=seq, ctx_len=ctx_len)

    for layer in range(depth):
        need_ctx = layer < depth - 1
        n_tok = n_lat + n_ctx if need_ctx else n_lat
        m = mod[layer, :nbatch + 1]
        sh1, sc1, g1, sh2, sc2, g2 = [m[:, k * d:(k + 1) * d].reshape(nbatch + 1, 1, d) for k in range(6)]
        n1 = norm1_g[layer][None, :]
        n2 = norm2_g[layer][None, :]
        i = layer // 2
        if layer % 2 == 0:
            lam_init = 0.8 - 0.6 * math.exp(-0.3 * layer)
            w_in, gain = _even_weights(ev_w_in[i], diff_q_g[i], diff_k_g[i], swa_q_g[i], swa_k_g[i])
            p = _proj_even(xa, n1, sh1, sc1, w_in, gain, g64, cos64, sin64, bpb, nbatch)
            lam = diff_lam[i].astype(F32)
            subln = diff_subln_g[i].astype(F32)[None, :]
            sink = swa_sink[i].astype(F32)
            diff_kw = dict(mode="diff", split="mask", n_pairs=DIFF_HEADS, q_blk=lambda j: j,
                           k_blk=lambda j: 4 + j, v_blk=lambda j: 8 + j, out_cols=DIFF_HEADS * LANES,
                           lam=lam, subln=subln, lam_init=lam_init, base2=True,
                           safe=_logits_bounded(diff_q_g[i], diff_k_g[i], HEAD_DIM), **common)
            swa_kw = dict(mode="pair", split="mask", n_pairs=SWA_Q_HEADS // 2, q_blk=lambda j: 12 + j,
                          k_blk=lambda j: 16 + j // 2, v_blk=lambda j: 18 + j // 2,
                          out_cols=SWA_Q_HEADS * HEAD_DIM, sink=sink, **common)
            oa = _attention(p, name="diff_attn", main="full", ctx_queries=False, **diff_kw)
            ob = _attention(p, name="window_attn", main="window", ctx_queries=False, **swa_kw)
            if need_ctx:
                oa_c = _attention(p, name="diff_attn_ctx", main=None, ctx_queries=True, **diff_kw)
                ob_c = _attention(p, name="window_attn_ctx", main=None, ctx_queries=True, **swa_kw)
        else:
            w_in, wq, wkv, gains = _odd_weights(od_w_in[i], mla_wq_up[i], mla_wkv_up[i], mla_qa_g[i],
                                                mla_kva_g[i], mla_q_g[i], mla_k_g[i], na_q_g[i], na_k_g[i])
            p = _proj_odd(xa, n1, sh1, sc1, w_in, wq, wkv, gains, g64, g128, cos32, sin32, bpb, nbatch)
            bias = _na_bias(na_rpb[i].astype(F32), seq)
            mla_kw = dict(mode="pair", split="slice", n_pairs=MLA_HEADS // 2, q_blk=lambda j: j,
                          k_blk=lambda j: 4 + j, v_blk=lambda j: 16 + j, out_cols=MLA_HEADS * MLA_V,
                          base2=True, safe=_logits_bounded(mla_q_g[i], mla_k_g[i], MLA_QK), **common)
            na_kw = dict(mode="pair", split="mask", n_pairs=NA_HEADS // 2, q_blk=lambda j: 20 + j,
                         k_blk=lambda j: 24 + j, v_blk=lambda j: 28 + j, out_cols=NA_HEADS * HEAD_DIM,
                         **common)
            oa = _attention(p, name="mla_attn", main="full", ctx_queries=False, **mla_kw)
            ob = _attention(p, name="na_attn", main="na", ctx_queries=False, bias=bias, **na_kw)
            if need_ctx:
                oa_c = _attention(p, name="mla_attn_ctx", main=None, ctx_queries=True, **mla_kw)
                ob_c = _attention(p, name="na_attn_ctx", main=None, ctx_queries=True, **na_kw)
        if need_ctx:
            oa = jnp.concatenate([oa, oa_c], axis=0)
            ob = jnp.concatenate([ob, ob_c], axis=0)
        xa = _outproj(xa, oa, ob, w_out[layer].astype(BF16), g1, n_tok, bpb, nbatch)
        xa = _mlp(xa, n2, sh2, sc2, g2, mlp_w1[layer].astype(BF16), mlp_w2[layer].astype(BF16),
                  n_tok, bpb, nbatch)
    return xa.reshape(nbatch, seq, d)
```

```python
import functools
import math

import numpy as np
import jax
import jax.numpy as jnp
from jax import lax
from jax.experimental import pallas as pl
from jax.experimental.pallas import tpu as pltpu

F32 = jnp.float32
BF16 = jnp.bfloat16

LANES = 128
HALF = LANES // 2
GRID_W = 64
HEAD_DIM = 64
EPS = 1e-6
ROPE_BASE = 10000.0
DIFF_HEADS = 4
SWA_Q_HEADS = 8
SWA_KV_HEADS = 2
WINDOW = 128
MLA_HEADS = 8
MLA_Q_LORA = 512
MLA_KV_LORA = 256
MLA_NOPE = 64
MLA_ROPE = 32
MLA_QK = MLA_NOPE + MLA_ROPE
MLA_V = 64
NA_HEADS = 8
NA_ROWS = 8
NA_COLS = 16
NEG = -1e30
LOG2E = math.log2(math.e)
MAX_FREE_LOGIT_BOUND = 50.0

TM = 512
TQ = 512
TQ_FULL = 1024
TK_FULL = 512
TK_MAX_FREE = 2048
MOD_TN = 1536
MLP_TF = 1024
VMEM_LIMIT = 56 * 1024 * 1024


def _params(*sem):
    return pltpu.CompilerParams(dimension_semantics=sem, vmem_limit_bytes=VMEM_LIMIT)


def _mod_kernel(c_ref, w_ref, b_ref, o_ref):
    c = c_ref[...]
    a = (c * jax.nn.sigmoid(c)).astype(BF16)
    w = w_ref[0].astype(BF16)
    o_ref[0] = jnp.dot(a, w, preferred_element_type=F32) + b_ref[0]


def _modulation(cvec, ada_w, ada_b):
    depth, d, n = ada_w.shape
    rows = cvec.shape[0]
    return pl.pallas_call(
        _mod_kernel,
        grid=(depth, n // MOD_TN),
        in_specs=[
            pl.BlockSpec((rows, d), lambda l, j: (0, 0)),
            pl.BlockSpec((1, d, MOD_TN), lambda l, j: (l, 0, j)),
            pl.BlockSpec((1, 1, MOD_TN), lambda l, j: (l, 0, j)),
        ],
        out_specs=pl.BlockSpec((1, rows, MOD_TN), lambda l, j: (l, 0, j)),
        out_shape=jax.ShapeDtypeStruct((depth, rows, n), F32),
        compiler_params=_params("parallel", "parallel"),
        name="modulation",
    )(cvec, ada_w, ada_b.reshape(depth, 1, n))


def _modulated(x_ref, g_ref, sh_ref, sc_ref):
    x = x_ref[...]
    ms = jnp.mean(x * x, axis=-1, keepdims=True)
    y = x * lax.rsqrt(ms + EPS) * g_ref[...]
    return y * (1.0 + sc_ref[0]) + sh_ref[0]


def _group_meansq(chunks, g_ref, inv_n):
    rows = chunks[0].shape[0]
    sq = jnp.concatenate([(c * c).astype(BF16) for c in chunks], axis=0)
    ss = jnp.dot(sq, g_ref[...], preferred_element_type=F32) * inv_n
    return [ss[i * rows:(i + 1) * rows] for i in range(len(chunks))]


def _rope(x, cos, sin_signed, quarter):
    lane = lax.broadcasted_iota(jnp.int32, x.shape, 1)
    odd = (lane & quarter) != 0
    from_lower = pltpu.roll(x, quarter, 1)
    from_upper = pltpu.roll(x, LANES - quarter, 1)
    return x * cos + jnp.where(odd, from_lower, from_upper) * sin_signed


def _norm_rope_chunk(acc, meansq, gain, cos, sin, quarter):
    y = acc * lax.rsqrt(meansq + EPS) * gain
    if quarter is None:
        return y
    return jnp.concatenate([_rope(y[:, :LANES], cos, sin, quarter),
                            _rope(y[:, LANES:], cos, sin, quarter)], axis=1)


def _token_specs(d, bpb, nbatch):
    row = lambda i, *_: (jnp.minimum(i // bpb, nbatch), 0, 0)
    return [
        pl.BlockSpec((TM, d), lambda i, *_: (i, 0)),
        pl.BlockSpec((1, d), lambda i, *_: (0, 0)),
        pl.BlockSpec((1, 1, d), row),
        pl.BlockSpec((1, 1, d), row),
    ], row


def _rope_spec(bpb, nbatch):
    return pl.BlockSpec((TM, LANES), lambda i: (jnp.where(i < bpb * nbatch, i % bpb, bpb), 0))


def _whole(shape):
    return pl.BlockSpec(shape, lambda *_: (0,) * len(shape))


EVEN_COLS = 2560
EVEN_NORM_CHUNKS = (0, 1, 2, 3, 6, 7, 8)


def _proj_even_kernel(x_ref, g_ref, sh_ref, sc_ref, w_ref, gain_ref, g64_ref, cos_ref, sin_ref, o_ref):
    h = _modulated(x_ref, g_ref, sh_ref, sc_ref).astype(BF16)
    cos = cos_ref[...]
    sin = sin_ref[...]
    full = jnp.dot(h, w_ref[...], preferred_element_type=F32)
    chunk = lambda c: full[:, c * 256:(c + 1) * 256]
    meansq = dict(zip(EVEN_NORM_CHUNKS,
                      _group_meansq([chunk(c) for c in EVEN_NORM_CHUNKS], g64_ref, 1.0 / HEAD_DIM)))
    for c in range(EVEN_COLS // 256):
        cols = slice(c * 256, (c + 1) * 256)
        acc = chunk(c)
        if c in EVEN_NORM_CHUNKS:
            acc = _norm_rope_chunk(acc, meansq[c], gain_ref[:, cols], cos, sin, HEAD_DIM // 4)
        o_ref[:, cols] = acc.astype(BF16)


def _proj_even(xa, g, sh, sc, w, gain, g64, cos, sin, bpb, nbatch):
    na, d = xa.shape
    tok_specs, _ = _token_specs(d, bpb, nbatch)
    return pl.pallas_call(
        _proj_even_kernel,
        grid=(na // TM,),
        in_specs=tok_specs + [
            _whole(w.shape), _whole(gain.shape), _whole(g64.shape),
            _rope_spec(bpb, nbatch), _rope_spec(bpb, nbatch),
        ],
        out_specs=pl.BlockSpec((TM, EVEN_COLS), lambda i: (i, 0)),
        out_shape=jax.ShapeDtypeStruct((na, EVEN_COLS), BF16),
        compiler_params=_params("parallel"),
        name="proj_even",
    )(xa, g, sh, sc, w, gain, g64, cos, sin)


ODD_W_COLS = 2432
ODD_COLS = 4096
MLA_PAD = LANES


def _proj_odd_kernel(x_ref, g_ref, sh_ref, sc_ref, w_ref, wq_ref, wkv_ref, qa_g_ref, kva_g_ref,
                     mq_g_ref, mk_g_ref, nq_g_ref, nk_g_ref, g64_ref, g128_ref, cos_ref, sin_ref, o_ref):
    h = _modulated(x_ref, g_ref, sh_ref, sc_ref).astype(BF16)
    cos = cos_ref[...]
    sin = sin_ref[...]
    quarter = MLA_ROPE // 4

    full = jnp.dot(h, w_ref[...], preferred_element_type=F32)

    def low_rank(a, gain_ref):
        ms = jnp.mean(a * a, axis=-1, keepdims=True)
        return (a * lax.rsqrt(ms + EPS) * gain_ref[...]).astype(BF16)

    qa = low_rank(full[:, :MLA_Q_LORA], qa_g_ref)
    kva = low_rank(full[:, MLA_Q_LORA:MLA_Q_LORA + MLA_KV_LORA], kva_g_ref)
    qf = jnp.dot(qa, wq_ref[...], preferred_element_type=F32)
    kvf = jnp.dot(kva, wkv_ref[...], preferred_element_type=F32)
    kslot = full[:, 2304:2432]
    kslot2 = jnp.concatenate([kslot, kslot], axis=1)
    n_mla = MLA_HEADS * MLA_PAD // 256
    q_chunks = [qf[:, c * 256:(c + 1) * 256] for c in range(n_mla)]
    k_chunks = [kvf[:, c * 256:(c + 1) * 256] + kslot2 for c in range(n_mla)]
    ms128 = _group_meansq(q_chunks + k_chunks, g128_ref, 1.0 / MLA_QK)
    for c in range(n_mla):
        y = _norm_rope_chunk(q_chunks[c], ms128[c], mq_g_ref[...], cos, sin, quarter)
        o_ref[:, c * 256:(c + 1) * 256] = y.astype(BF16)
        y = _norm_rope_chunk(k_chunks[c], ms128[n_mla + c], mk_g_ref[...], cos, sin, quarter)
        o_ref[:, 1024 + c * 256:1024 + (c + 1) * 256] = y.astype(BF16)
    o_ref[:, 2048:2560] = kvf[:, 1024:1536].astype(BF16)

    n_chunks = [full[:, 768 + c * 256:768 + (c + 1) * 256] for c in range(4)]
    ms64 = _group_meansq(n_chunks, g64_ref, 1.0 / HEAD_DIM)
    for c in range(4):
        gain = nq_g_ref[...] if c < 2 else nk_g_ref[...]
        y = _norm_rope_chunk(n_chunks[c], ms64[c], gain, None, None, None)
        o_ref[:, 2560 + c * 256:2560 + (c + 1) * 256] = y.astype(BF16)
    o_ref[:, 3584:4096] = full[:, 1792:2304].astype(BF16)


def _proj_odd(xa, g, sh, sc, w, wq, wkv, gains, g64, g128, cos, sin, bpb, nbatch):
    na, d = xa.shape
    tok_specs, _ = _token_specs(d, bpb, nbatch)
    consts = (w, wq, wkv) + tuple(gains) + (g64, g128)
    return pl.pallas_call(
        _proj_odd_kernel,
        grid=(na // TM,),
        in_specs=tok_specs + [_whole(a.shape) for a in consts] + [_rope_spec(bpb, nbatch)] * 2,
        out_specs=pl.BlockSpec((TM, ODD_COLS), lambda i: (i, 0)),
        out_shape=jax.ShapeDtypeStruct((na, ODD_COLS), BF16),
        compiler_params=_params("parallel"),
        name="proj_odd",
    )(xa, g, sh, sc, *consts, cos, sin)


def _attn_kernel(*refs, mode, split, main, seq, has_sink, lam_init, tq):
    it = iter(refs)
    q_ref = next(it)
    k_ref = v_ref = bias_ref = sink_ref = lam_ref = subln_ref = None
    if main is not None:
        k_ref, v_ref = next(it), next(it)
    kc_ref, vc_ref = next(it), next(it)
    if main == "na":
        bias_ref = next(it)
    if has_sink:
        sink_ref = next(it)
    safe_ref = next(it)
    if mode == "diff":
        lam_ref, subln_ref = next(it), next(it)
    o_ref = next(it)

    pair = pl.program_id(1)
    qi = pl.program_id(2)
    q = q_ref[...]
    low = lax.broadcasted_iota(jnp.int32, (1, LANES), 1) < HALF
    if split == "mask":
        zero = jnp.zeros_like(q)
        qs = (jnp.where(low, q, zero), jnp.where(low, zero, q))
        k_of = lambda k, s: k
    else:
        qs = (q[:, :LANES], q[:, LANES:])
        k_of = lambda k, s: k[:, s * LANES:(s + 1) * LANES]
    sinks = [sink_ref[2 * pair + s] * LOG2E for s in range(2)] if has_sink else None

    def logits_of(s, k):
        return lax.dot_general(qs[s], k_of(k, s), (((1,), (1,)), ((), ())), preferred_element_type=F32)

    def finish(o_lo, o_hi):
        if mode == "diff":
            lv = lam_ref[...]
            lam = (jnp.exp(jnp.sum(lv[0:1] * lv[1:2], axis=-1, keepdims=True))
                   - jnp.exp(jnp.sum(lv[2:3] * lv[3:4], axis=-1, keepdims=True)) + lam_init)
            o = o_lo - lam * o_hi
            ms = jnp.mean(o * o, axis=-1, keepdims=True)
            o = o * lax.rsqrt(ms + EPS) * subln_ref[...] * (1.0 - lam_init)
        else:
            o = jnp.where(low, o_lo, o_hi)
        o_ref[...] = o.astype(BF16)

    def init(s):
        if has_sink:
            m0 = jnp.full((tq, 1), sinks[s], F32)
            l0 = jnp.ones((tq, 1), F32)
        else:
            m0 = jnp.full((tq, 1), NEG, F32)
            l0 = jnp.zeros((tq, 1), F32)
        return m0, l0, jnp.zeros((tq, LANES), F32)

    def segment(state, k, v, logit_fn):
        out = []
        for s in range(2):
            m, l, acc = state[s]
            logits = logits_of(s, k)
            if logit_fn is not None:
                logits = logit_fn(logits, s)
            m_new = jnp.maximum(m, jnp.max(logits, axis=-1, keepdims=True))
            alpha = jnp.exp2(m - m_new)
            p = jnp.exp2(logits - m_new)
            l = alpha * l + jnp.sum(p, axis=-1, keepdims=True)
            acc = alpha * acc + jnp.dot(p.astype(BF16), v, preferred_element_type=F32)
            out.append((m_new, l, acc))
        return tuple(out)

    def all_keys(step, state, tk):
        if main == "full":
            def body(c, st):
                off = pl.multiple_of(c * tk, tk)
                return step(st, k_ref[pl.ds(off, tk), :], v_ref[pl.ds(off, tk), :], None)
            state = lax.fori_loop(0, seq // tk, body, state)
        elif main == "window":
            span = tq + 2 * WINDOW
            q0 = qi * tq
            w0 = jnp.clip(q0 - WINDOW, 0, seq - span)
            off = pl.multiple_of(w0, WINDOW)
            delta = w0 - q0
            row = lax.broadcasted_iota(jnp.int32, (tq, span), 0)
            col = lax.broadcasted_iota(jnp.int32, (tq, span), 1)
            inside = jnp.abs(col - row + delta) <= WINDOW
            state = step(state, k_ref[pl.ds(off, span), :], v_ref[pl.ds(off, span), :],
                         lambda lg, s: jnp.where(inside, lg, NEG))
        elif main == "na":
            rows_q = tq // GRID_W
            span_rows = 2 * NA_ROWS
            w0 = jnp.clip(qi * rows_q - NA_ROWS // 2, 0, seq // GRID_W - span_rows)
            off = pl.multiple_of(w0 * GRID_W, GRID_W)
            span = span_rows * GRID_W
            state = step(state, k_ref[pl.ds(off, span), :], v_ref[pl.ds(off, span), :],
                         lambda lg, s: lg + bias_ref[0, s])
        return step(state, kc_ref[...], vc_ref[...], None)

    def online():
        (_, l_lo, acc_lo), (_, l_hi, acc_hi) = all_keys(segment, (init(0), init(1)), TK_FULL)
        finish(acc_lo / l_lo, acc_hi / l_hi)

    def accumulate(acc, k, v, logit_fn):
        v_ext = jnp.concatenate([v, jnp.ones_like(v)], axis=1)
        out = []
        for s in range(2):
            logits = logits_of(s, k)
            if logit_fn is not None:
                logits = logit_fn(logits, s)
            p = jnp.exp2(logits).astype(BF16)
            out.append(acc[s] + jnp.dot(p, v_ext, preferred_element_type=F32))
        return tuple(out)

    def max_free():
        zero_acc = jnp.zeros((tq, 2 * LANES), F32)
        acc = all_keys(accumulate, (zero_acc, zero_acc), TK_MAX_FREE)
        outs = []
        for s in range(2):
            den = acc[s][:, LANES:]
            if has_sink:
                den = den + jnp.exp2(jnp.full((1, LANES), sinks[s], F32))
            outs.append(acc[s][:, :LANES] / den)
        finish(*outs)

    safe = safe_ref[0] == 1
    pl.when(safe)(max_free)
    pl.when(jnp.logical_not(safe))(online)


def _attention(p, *, name, mode, split, main, n_pairs, q_blk, k_blk, v_blk, out_cols, ctx_queries,
               nbatch, seq, ctx_len, safe, bias=None, sink=None, lam=None, subln=None, lam_init=0.0):
    n_lat = nbatch * seq
    qw = LANES if split == "mask" else 2 * LANES
    if ctx_queries:
        tq, nq = ctx_len, 1
        q_row = lambda b, j, i: n_lat // ctx_len + b
        out_rows = nbatch * ctx_len
        out_row = lambda b, j, i: b
    else:
        tq = TQ_FULL if main == "full" else TQ
        nq = seq // tq
        q_row = lambda b, j, i: b * nq + i
        out_rows = n_lat
        out_row = q_row
    ctx_row = lambda b, j, i: n_lat // ctx_len + b

    in_specs = [pl.BlockSpec((tq, qw), lambda b, j, i: (q_row(b, j, i), q_blk(j)))]
    args = [p]
    if main is not None:
        in_specs += [pl.BlockSpec((seq, qw), lambda b, j, i: (b, k_blk(j))),
                     pl.BlockSpec((seq, LANES), lambda b, j, i: (b, v_blk(j)))]
        args += [p, p]
    in_specs += [pl.BlockSpec((ctx_len, qw), lambda b, j, i: (ctx_row(b, j, i), k_blk(j))),
                 pl.BlockSpec((ctx_len, LANES), lambda b, j, i: (ctx_row(b, j, i), v_blk(j)))]
    args += [p, p]
    if main == "na":
        span = 2 * NA_ROWS * GRID_W
        cls = lambda i: jnp.where(i == 0, 0, jnp.where(i == nq - 1, 2, 1))
        in_specs.append(pl.BlockSpec((1, 2, tq, span), lambda b, j, i: (cls(i), j, 0, 0)))
        args.append(bias)
    if sink is not None:
        in_specs.append(pl.BlockSpec(memory_space=pltpu.SMEM))
        args.append(sink)
    in_specs.append(pl.BlockSpec(memory_space=pltpu.SMEM))
    args.append(safe)
    if mode == "diff":
        in_specs += [_whole(lam.shape), _whole(subln.shape)]
        args += [lam, subln]

    kern = functools.partial(_attn_kernel, mode=mode, split=split, main=main, seq=seq,
                             has_sink=sink is not None, lam_init=lam_init, tq=tq)
    return pl.pallas_call(
        kern,
        grid=(nbatch, n_pairs, nq),
        in_specs=in_specs,
        out_specs=pl.BlockSpec((tq, LANES), lambda b, j, i: (out_row(b, j, i), j)),
        out_shape=jax.ShapeDtypeStruct((out_rows, out_cols), BF16),
        compiler_params=_params("parallel", "parallel", "arbitrary"),
        name=name,
    )(*args)


def _na_bias_kernel(rpb_ref, o_ref, *, n_rows, rows_q):
    h = pl.program_id(0)
    n_dr = 2 * NA_ROWS - 1
    n_dc = 2 * NA_COLS - 1
    cq = lax.broadcasted_iota(jnp.int32, (GRID_W, LANES), 0)
    kc = lax.broadcasted_iota(jnp.int32, (GRID_W, LANES), 1) & (GRID_W - 1)
    dc_idx = jnp.clip(kc - cq, -(NA_COLS - 1), NA_COLS - 1) + NA_COLS - 1
    cs = jnp.clip(cq - NA_COLS // 2, 0, GRID_W - NA_COLS)
    col_ok = (kc >= cs) & (kc < cs + NA_COLS)
    tiles = [jnp.zeros((GRID_W, LANES), F32) for _ in range(n_dr)]
    for dc in range(n_dc):
        hit = dc_idx == dc
        for dr in range(n_dr):
            tiles[dr] = jnp.where(hit, rpb_ref[(h * n_dr + dr) * n_dc + dc] * LOG2E, tiles[dr])
    tiles = [jnp.where(col_ok, t, NEG) for t in tiles]
    masked = jnp.full((GRID_W, LANES), NEG, F32)
    low = lax.broadcasted_iota(jnp.int32, (GRID_W, LANES), 1) < HALF

    span_rows = 2 * NA_ROWS
    first_q_row = (0, span_rows, n_rows - rows_q)
    for c in range(3):
        r0 = first_q_row[c]
        w0 = min(max(r0 - NA_ROWS // 2, 0), n_rows - span_rows)
        for qr in range(rows_q):
            r = r0 + qr
            rs = min(max(r - NA_ROWS // 2, 0), n_rows - NA_ROWS)
            for m in range(span_rows // 2):
                halves = []
                for kr in (2 * m, 2 * m + 1):
                    k_abs = w0 + kr
                    halves.append(tiles[k_abs - r + NA_ROWS - 1] if rs <= k_abs < rs + NA_ROWS else masked)
                o_ref[c, 0, qr * GRID_W:(qr + 1) * GRID_W, m * LANES:(m + 1) * LANES] = (
                    jnp.where(low, halves[0], halves[1]))


def _na_bias(rpb, seq):
    n_rows = seq // GRID_W
    rows_q = TQ // GRID_W
    span = 2 * NA_ROWS * GRID_W
    kern = functools.partial(_na_bias_kernel, n_rows=n_rows, rows_q=rows_q)
    return pl.pallas_call(
        kern,
        grid=(NA_HEADS,),
        in_specs=[pl.BlockSpec(memory_space=pltpu.SMEM)],
        out_specs=pl.BlockSpec((3, 1, TQ, span), lambda h: (0, h, 0, 0)),
        out_shape=jax.ShapeDtypeStruct((3, NA_HEADS, TQ, span), F32),
        compiler_params=_params("parallel"),
        name="na_bias",
    )(rpb.reshape(-1))


def _outproj_kernel(x_ref, oa_ref, ob_ref, w_ref, gate_ref, o_ref):
    half = oa_ref.shape[1]
    y = (jnp.dot(oa_ref[...], w_ref[:half, :], preferred_element_type=F32)
         + jnp.dot(ob_ref[...], w_ref[half:, :], preferred_element_type=F32))
    o_ref[...] = x_ref[...] + gate_ref[0] * y


def _outproj(xa, oa, ob, w, gate, n_tokens, bpb, nbatch):
    d = xa.shape[1]
    half = oa.shape[1]
    row = lambda i: (jnp.minimum(i // bpb, nbatch), 0, 0)
    return pl.pallas_call(
        _outproj_kernel,
        grid=(n_tokens // TM,),
        in_specs=[
            pl.BlockSpec((TM, d), lambda i: (i, 0)),
            pl.BlockSpec((TM, half), lambda i: (i, 0)),
            pl.BlockSpec((TM, half), lambda i: (i, 0)),
            _whole(w.shape),
            pl.BlockSpec((1, 1, d), row),
        ],
        out_specs=pl.BlockSpec((TM, d), lambda i: (i, 0)),
        out_shape=jax.ShapeDtypeStruct((n_tokens, d), F32),
        compiler_params=_params("parallel"),
        name="outproj",
    )(xa, oa, ob, w, gate)


def _mlp_kernel(x_ref, g_ref, sh_ref, sc_ref, gate_ref, w1_ref, w2_ref, o_ref, h_sc, acc_sc):
    f = pl.program_id(1)

    @pl.when(f == 0)
    def _():
        h_sc[...] = _modulated(x_ref, g_ref, sh_ref, sc_ref).astype(BF16)
        acc_sc[...] = jnp.zeros_like(acc_sc)

    a = jnp.dot(h_sc[...], w1_ref[...], preferred_element_type=F32)
    a = jnp.square(jnp.maximum(a, 0.0)).astype(BF16)
    acc_sc[...] += jnp.dot(a, w2_ref[...], preferred_element_type=F32)

    @pl.when(f == pl.num_programs(1) - 1)
    def _():
        o_ref[...] = x_ref[...] + gate_ref[0] * acc_sc[...]


def _mlp(xa, g, sh, sc, gate, w1, w2, n_tokens, bpb, nbatch):
    d = xa.shape[1]
    dff = w1.shape[1]
    tok_specs, row = _token_specs(d, bpb, nbatch)
    return pl.pallas_call(
        _mlp_kernel,
        grid=(n_tokens // TM, dff // MLP_TF),
        in_specs=tok_specs + [
            pl.BlockSpec((1, 1, d), row),
            pl.BlockSpec((d, MLP_TF), lambda i, f: (0, f)),
            pl.BlockSpec((MLP_TF, d), lambda i, f: (f, 0)),
        ],
        out_specs=pl.BlockSpec((TM, d), lambda i, f: (i, 0)),
        out_shape=jax.ShapeDtypeStruct((n_tokens, d), F32),
        scratch_shapes=[pltpu.VMEM((TM, d), BF16), pltpu.VMEM((TM, d), F32)],
        compiler_params=_params("parallel", "arbitrary"),
        name="mlp",
    )(xa, g, sh, sc, gate, w1, w2)


def _block_diag_ones(group):
    idx = np.arange(256) // group
    return jnp.asarray(idx[:, None] == idx[None, :], dtype=BF16)


def _rope_tables(seq, rot_dim, lane0, pad_rows):
    t = jnp.arange(seq, dtype=jnp.int32)
    row = (t // GRID_W).astype(F32)
    col = (t % GRID_W).astype(F32)
    n_freq = rot_dim // 4
    freqs = jnp.power(ROPE_BASE, -jnp.arange(n_freq, dtype=F32) / n_freq)
    ar = row[:, None] * freqs[None, :]
    ac = col[:, None] * freqs[None, :]
    ang = jnp.concatenate([ar, ar, ac, ac], axis=-1)
    sign = jnp.asarray(np.tile(np.repeat([-1.0, 1.0], n_freq), 2), F32)
    cos, sin = jnp.cos(ang), jnp.sin(ang) * sign[None, :]
    if lane0 is None:
        reps = LANES // rot_dim
        cos, sin = jnp.tile(cos, (1, reps)), jnp.tile(sin, (1, reps))
    else:
        pad = ((0, 0), (lane0, LANES - lane0 - rot_dim))
        cos = jnp.pad(cos, pad, constant_values=1.0)
        sin = jnp.pad(sin, pad)
    cos = jnp.concatenate([cos, jnp.ones((pad_rows, LANES), F32)], axis=0)
    sin = jnp.concatenate([sin, jnp.zeros((pad_rows, LANES), F32)], axis=0)
    return cos, sin


def _logits_bounded(q_gain, k_gain, dim, bias=None, sink=None):
    bound = jnp.max(jnp.abs(q_gain)) * jnp.max(jnp.abs(k_gain)) * (dim ** 0.5) * 1.02
    if bias is not None:
        bound = bound + jnp.max(jnp.abs(bias))
    if sink is not None:
        bound = jnp.maximum(bound, jnp.max(jnp.abs(sink)))
    return (bound <= MAX_FREE_LOGIT_BOUND).astype(jnp.int32).reshape(1)


def _tile_gain(g, reps, scale=1.0):
    return (jnp.tile(g.astype(F32), reps) * scale)[None, :]


def _even_weights(w, q_g, k_g, sq_g, sk_g):
    nq = DIFF_HEADS * 2 * HEAD_DIM
    qa, ka, va = w[:, :nq], w[:, nq:2 * nq], w[:, 2 * nq:3 * nq]
    o = 3 * nq
    qb = w[:, o:o + SWA_Q_HEADS * HEAD_DIM]
    o += SWA_Q_HEADS * HEAD_DIM
    kb = [w[:, o + i * HEAD_DIM:o + (i + 1) * HEAD_DIM] for i in range(SWA_KV_HEADS)]
    o += SWA_KV_HEADS * HEAD_DIM
    vb = [w[:, o + i * HEAD_DIM:o + (i + 1) * HEAD_DIM] for i in range(SWA_KV_HEADS)]
    dup = lambda parts: [p for p in parts for _ in range(2)]
    w_new = jnp.concatenate([qa, ka, va, qb] + dup(kb) + dup(vb), axis=1).astype(BF16)
    scale = HEAD_DIM ** -0.5 * LOG2E
    ones = lambda n: jnp.ones((1, n), F32)
    gain = jnp.concatenate([
        _tile_gain(q_g, 2 * DIFF_HEADS, scale), _tile_gain(k_g, 2 * DIFF_HEADS), ones(nq),
        _tile_gain(sq_g, SWA_Q_HEADS, scale), _tile_gain(sk_g, 2 * SWA_KV_HEADS),
        ones(2 * SWA_KV_HEADS * HEAD_DIM)], axis=1)
    return w_new, gain


def _odd_weights(w, wq_up, wkv_up, qa_g, kva_g, mq_g, mk_g, nq_g, nk_g):
    d = w.shape[0]
    o = 0
    parts = []
    for n in (MLA_Q_LORA, MLA_KV_LORA, MLA_ROPE, NA_HEADS * HEAD_DIM, NA_HEADS * HEAD_DIM, NA_HEADS * HEAD_DIM):
        parts.append(w[:, o:o + n])
        o += n
    q_a, kv_a, k_r, nq, nk, nv = parts
    slot = jnp.concatenate([jnp.zeros((d, MLA_NOPE), w.dtype), k_r,
                            jnp.zeros((d, LANES - MLA_QK), w.dtype)], axis=1)
    w_new = jnp.concatenate([q_a, kv_a, nq, nk, nv, slot], axis=1).astype(BF16)
    wq = jnp.pad(wq_up.reshape(MLA_Q_LORA, MLA_HEADS, MLA_QK), ((0, 0), (0, 0), (0, MLA_PAD - MLA_QK)))
    wq = wq.reshape(MLA_Q_LORA, MLA_HEADS * MLA_PAD).astype(BF16)
    wkv = wkv_up.reshape(MLA_KV_LORA, MLA_HEADS, MLA_NOPE + MLA_V)
    wk = jnp.pad(wkv[..., :MLA_NOPE], ((0, 0), (0, 0), (0, MLA_PAD - MLA_NOPE))).reshape(MLA_KV_LORA, -1)
    wv = wkv[..., MLA_NOPE:].reshape(MLA_KV_LORA, MLA_HEADS * MLA_V)
    wkv_new = jnp.concatenate([wk, wv], axis=1).astype(BF16)
    pad_gain = lambda g, scale: _tile_gain(jnp.pad(g.astype(F32), (0, MLA_PAD - MLA_QK)), 2, scale)
    gains = (qa_g.astype(F32)[None, :], kva_g.astype(F32)[None, :],
             pad_gain(mq_g, MLA_QK ** -0.5 * LOG2E), pad_gain(mk_g, 1.0),
             _tile_gain(nq_g, 4, HEAD_DIM ** -0.5 * LOG2E), _tile_gain(nk_g, 4))
    return w_new, wq, wkv_new, gains


def kernel(x, c, ctx, c_ctx, ada_w, ada_b, norm1_g, norm2_g, w_out, mlp_w1, mlp_w2, ev_w_in, diff_q_g,
           diff_k_g, diff_lam, diff_subln_g, swa_q_g, swa_k_g, swa_sink, od_w_in, mla_qa_g, mla_kva_g,
           mla_wq_up, mla_wkv_up, mla_q_g, mla_k_g, na_q_g, na_k_g, na_rpb):
    nbatch, seq, d = x.shape
    ctx_len = ctx.shape[1]
    depth = ada_w.shape[0]
    n_lat = nbatch * seq
    n_ctx = nbatch * ctx_len
    assert seq % TM == 0 and n_ctx == TM and seq % TQ == 0 and seq % GRID_W == 0
    assert seq // GRID_W >= 2 * NA_ROWS + TQ // GRID_W and seq >= TQ + 2 * WINDOW
    bpb = seq // TM

    mod_rows = 8
    cvec = jnp.concatenate([c, c_ctx[None, :], jnp.zeros((mod_rows - nbatch - 1, d), F32)], axis=0)
    mod = _modulation(cvec, ada_w, ada_b)

    cos64, sin64 = _rope_tables(seq, HEAD_DIM, None, n_ctx)
    cos32, sin32 = _rope_tables(seq, MLA_ROPE, MLA_NOPE, n_ctx)
    g64 = _block_diag_ones(HEAD_DIM)
    g128 = _block_diag_ones(LANES)

    xa = jnp.concatenate([x.reshape(n_lat, d), ctx.reshape(n_ctx, d)], axis=0)
    common = dict(nbatch=nbatch, seq=seq, ctx_len=ctx_len)

    for layer in range(depth):
        need_ctx = layer < depth - 1
        n_tok = n_lat + n_ctx if need_ctx else n_lat
        m = mod[layer, :nbatch + 1]
        sh1, sc1, g1, sh2, sc2, g2 = [m[:, k * d:(k + 1) * d].reshape(nbatch + 1, 1, d) for k in range(6)]
        n1 = norm1_g[layer][None, :]
        n2 = norm2_g[layer][None, :]
        i = layer // 2
        if layer % 2 == 0:
            lam_init = 0.8 - 0.6 * math.exp(-0.3 * layer)
            w_in, gain = _even_weights(ev_w_in[i], diff_q_g[i], diff_k_g[i], swa_q_g[i], swa_k_g[i])
            p = _proj_even(xa, n1, sh1, sc1, w_in, gain, g64, cos64, sin64, bpb, nbatch)
            lam = diff_lam[i].astype(F32)
            subln = diff_subln_g[i].astype(F32)[None, :]
            sink = swa_sink[i].astype(F32)
            diff_kw = dict(mode="diff", split="mask", n_pairs=DIFF_HEADS, q_blk=lambda j: j,
                           k_blk=lambda j: 4 + j, v_blk=lambda j: 8 + j, out_cols=DIFF_HEADS * LANES,
                           lam=lam, subln=subln, lam_init=lam_init,
                           safe=_logits_bounded(diff_q_g[i], diff_k_g[i], HEAD_DIM), **common)
            swa_kw = dict(mode="pair", split="mask", n_pairs=SWA_Q_HEADS // 2, q_blk=lambda j: 12 + j,
                          k_blk=lambda j: 16 + j // 2, v_blk=lambda j: 18 + j // 2,
                          out_cols=SWA_Q_HEADS * HEAD_DIM, sink=sink,
                          safe=_logits_bounded(swa_q_g[i], swa_k_g[i], HEAD_DIM, sink=sink), **common)
            oa = _attention(p, name="diff_attn", main="full", ctx_queries=False, **diff_kw)
            ob = _attention(p, name="window_attn", main="window", ctx_queries=False, **swa_kw)
            if need_ctx:
                oa_c = _attention(p, name="diff_attn_ctx", main=None, ctx_queries=True, **diff_kw)
                ob_c = _attention(p, name="window_attn_ctx", main=None, ctx_queries=True, **swa_kw)
        else:
            w_in, wq, wkv, gains = _odd_weights(od_w_in[i], mla_wq_up[i], mla_wkv_up[i], mla_qa_g[i],
                                                mla_kva_g[i], mla_q_g[i], mla_k_g[i], na_q_g[i], na_k_g[i])
            p = _proj_odd(xa, n1, sh1, sc1, w_in, wq, wkv, gains, g64, g128, cos32, sin32, bpb, nbatch)
            bias = _na_bias(na_rpb[i].astype(F32), seq)
            mla_kw = dict(mode="pair", split="slice", n_pairs=MLA_HEADS // 2, q_blk=lambda j: j,
                          k_blk=lambda j: 4 + j, v_blk=lambda j: 16 + j, out_cols=MLA_HEADS * MLA_V,
                          safe=_logits_bounded(mla_q_g[i], mla_k_g[i], MLA_QK), **common)
            na_kw = dict(mode="pair", split="mask", n_pairs=NA_HEADS // 2, q_blk=lambda j: 20 + j,
                         k_blk=lambda j: 24 + j, v_blk=lambda j: 28 + j, out_cols=NA_HEADS * HEAD_DIM,
                         safe=_logits_bounded(na_q_g[i], na_k_g[i], HEAD_DIM, bias=na_rpb[i]), **common)
            oa = _attention(p, name="mla_attn", main="full", ctx_queries=False, **mla_kw)
            ob = _attention(p, name="na_attn", main="na", ctx_queries=False, bias=bias, **na_kw)
            if need_ctx:
                oa_c = _attention(p, name="mla_attn_ctx", main=None, ctx_queries=True, **mla_kw)
                ob_c = _attention(p, name="na_attn_ctx", main=None, ctx_queries=True, **na_kw)
        if need_ctx:
            oa = jnp.concatenate([oa, oa_c], axis=0)
            ob = jnp.concatenate([ob, ob_c], axis=0)
        xa = _outproj(xa, oa, ob, w_out[layer].astype(BF16), g1, n_tok, bpb, nbatch)
        xa = _mlp(xa, n2, sh2, sc2, g2, mlp_w1[layer].astype(BF16), mlp_w2[layer].astype(BF16),
                  n_tok, bpb, nbatch)
    return xa.reshape(nbatch, seq, d)
```

```python
import functools
import math

import numpy as np
import jax
import jax.numpy as jnp
from jax import lax
from jax.experimental import pallas as pl
from jax.experimental.pallas import tpu as pltpu

F32 = jnp.float32
BF16 = jnp.bfloat16

LANES = 128
HALF = LANES // 2
GRID_W = 64
HEAD_DIM = 64
EPS = 1e-6
ROPE_BASE = 10000.0
DIFF_HEADS = 4
SWA_Q_HEADS = 8
SWA_KV_HEADS = 2
WINDOW = 128
MLA_HEADS = 8
MLA_Q_LORA = 512
MLA_KV_LORA = 256
MLA_NOPE = 64
MLA_ROPE = 32
MLA_QK = MLA_NOPE + MLA_ROPE
MLA_V = 64
NA_HEADS = 8
NA_ROWS = 8
NA_COLS = 16
NEG = -1e30
LOG2E = math.log2(math.e)
MAX_FREE_LOGIT_BOUND = 50.0

TM = 512
TQ = 512
TQ_FULL = 1024
TK_FULL = 512
TK_MAX_FREE = 4096
MOD_TN = 1536
VMEM_LIMIT = 56 * 1024 * 1024


def _params(*sem):
    return pltpu.CompilerParams(dimension_semantics=sem, vmem_limit_bytes=VMEM_LIMIT)


def _mod_kernel(c_ref, w_ref, b_ref, o_ref):
    c = c_ref[...]
    a = (c * jax.nn.sigmoid(c)).astype(BF16)
    w = w_ref[0].astype(BF16)
    o_ref[0] = jnp.dot(a, w, preferred_element_type=F32) + b_ref[0]


def _modulation(cvec, ada_w, ada_b):
    depth, d, n = ada_w.shape
    rows = cvec.shape[0]
    return pl.pallas_call(
        _mod_kernel,
        grid=(depth, n // MOD_TN),
        in_specs=[
            pl.BlockSpec((rows, d), lambda l, j: (0, 0)),
            pl.BlockSpec((1, d, MOD_TN), lambda l, j: (l, 0, j)),
            pl.BlockSpec((1, 1, MOD_TN), lambda l, j: (l, 0, j)),
        ],
        out_specs=pl.BlockSpec((1, rows, MOD_TN), lambda l, j: (l, 0, j)),
        out_shape=jax.ShapeDtypeStruct((depth, rows, n), F32),
        compiler_params=_params("parallel", "parallel"),
        name="modulation",
    )(cvec, ada_w, ada_b.reshape(depth, 1, n))


def _modulated(x_ref, g_ref, sh_ref, sc_ref):
    x = x_ref[...]
    ms = jnp.mean(x * x, axis=-1, keepdims=True)
    y = x * lax.rsqrt(ms + EPS) * g_ref[...]
    return y * (1.0 + sc_ref[0]) + sh_ref[0]


def _group_meansq(chunks, g_ref, inv_n):
    rows = chunks[0].shape[0]
    sq = jnp.concatenate([(c * c).astype(BF16) for c in chunks], axis=0)
    ss = jnp.dot(sq, g_ref[...], preferred_element_type=F32) * inv_n
    return [ss[i * rows:(i + 1) * rows] for i in range(len(chunks))]


def _swap_quarters(x, quarter):
    lane = lax.broadcasted_iota(jnp.int32, x.shape, 1)
    odd = (lane & quarter) != 0
    from_lower = pltpu.roll(x, quarter, 1)
    from_upper = pltpu.roll(x, LANES - quarter, 1)
    return jnp.where(odd, from_lower, from_upper)


def _rope(x, cos, sin_signed, quarter):
    return x * cos + _swap_quarters(x, quarter) * sin_signed


def _norm_rope_chunk(acc, meansq, gain, cos, sin, quarter):
    y = acc * lax.rsqrt(meansq + EPS) * gain
    if quarter is None:
        return y
    return jnp.concatenate([_rope(y[:, :LANES], cos, sin, quarter),
                            _rope(y[:, LANES:], cos, sin, quarter)], axis=1)


def _token_specs(d, bpb, nbatch):
    row = lambda i, *_: (jnp.minimum(i // bpb, nbatch), 0, 0)
    return [
        pl.BlockSpec((TM, d), lambda i, *_: (i, 0)),
        pl.BlockSpec((1, d), lambda i, *_: (0, 0)),
        pl.BlockSpec((1, 1, d), row),
        pl.BlockSpec((1, 1, d), row),
    ], row


def _rope_spec(bpb, nbatch):
    return pl.BlockSpec((TM, LANES), lambda i: (jnp.where(i < bpb * nbatch, i % bpb, bpb), 0))


def _whole(shape):
    return pl.BlockSpec(shape, lambda *_: (0,) * len(shape))


EVEN_COLS = 2560
EVEN_NORM_CHUNKS = (0, 1, 2, 3, 6, 7, 8)
EVEN_DOT_RANGES = ((0, 4), (6, 9), (4, 6), (9, 10))


def _proj_even_kernel(x_ref, g_ref, sh_ref, sc_ref, w_ref, gain_ref, g64_ref, cos_ref, sin_ref, o_ref):
    h = _modulated(x_ref, g_ref, sh_ref, sc_ref).astype(BF16)
    cos = cos_ref[...]
    sin = sin_ref[...]
    for first, last in EVEN_DOT_RANGES:
        full = jnp.dot(h, w_ref[:, first * 256:last * 256], preferred_element_type=F32)
        chunk = lambda c: full[:, (c - first) * 256:(c - first + 1) * 256]
        normed = [c for c in range(first, last) if c in EVEN_NORM_CHUNKS]
        meansq = {}
        if normed:
            meansq = dict(zip(normed, _group_meansq([chunk(c) for c in normed], g64_ref, 1.0 / HEAD_DIM)))
        for c in range(first, last):
            cols = slice(c * 256, (c + 1) * 256)
            acc = chunk(c)
            if c in EVEN_NORM_CHUNKS:
                acc = _norm_rope_chunk(acc, meansq[c], gain_ref[:, cols], cos, sin, HEAD_DIM // 4)
            o_ref[:, cols] = acc.astype(BF16)


def _proj_even(xa, g, sh, sc, w, gain, g64, cos, sin, bpb, nbatch):
    na, d = xa.shape
    tok_specs, _ = _token_specs(d, bpb, nbatch)
    return pl.pallas_call(
        _proj_even_kernel,
        grid=(na // TM,),
        in_specs=tok_specs + [
            _whole(w.shape), _whole(gain.shape), _whole(g64.shape),
            _rope_spec(bpb, nbatch), _rope_spec(bpb, nbatch),
        ],
        out_specs=pl.BlockSpec((TM, EVEN_COLS), lambda i: (i, 0)),
        out_shape=jax.ShapeDtypeStruct((na, EVEN_COLS), BF16),
        compiler_params=_params("parallel"),
        name="proj_even",
    )(xa, g, sh, sc, w, gain, g64, cos, sin)


ODD_W_COLS = 2432
ODD_COLS = 4096
MLA_PAD = LANES


def _proj_odd_kernel(x_ref, g_ref, sh_ref, sc_ref, w_ref, wq_ref, wkv_ref, qa_g_ref, kva_g_ref,
                     mq_g_ref, mk_g_ref, nq_g_ref, nk_g_ref, g64_ref, g128_ref, cos_ref, sin_ref, o_ref):
    h = _modulated(x_ref, g_ref, sh_ref, sc_ref).astype(BF16)
    cos = cos_ref[...]
    sin = sin_ref[...]
    quarter = MLA_ROPE // 4

    full = jnp.dot(h, w_ref[...], preferred_element_type=F32)

    def low_rank(a, gain_ref):
        ms = jnp.mean(a * a, axis=-1, keepdims=True)
        return (a * lax.rsqrt(ms + EPS) * gain_ref[...]).astype(BF16)

    n_mla = MLA_HEADS * MLA_PAD // 256
    qa = low_rank(full[:, :MLA_Q_LORA], qa_g_ref)
    qf = jnp.dot(qa, wq_ref[...], preferred_element_type=F32)
    q_chunks = [qf[:, c * 256:(c + 1) * 256] for c in range(n_mla)]
    ms_q = _group_meansq(q_chunks, g128_ref, 1.0 / MLA_QK)
    kva = low_rank(full[:, MLA_Q_LORA:MLA_Q_LORA + MLA_KV_LORA], kva_g_ref)
    kvf = jnp.dot(kva, wkv_ref[...], preferred_element_type=F32)
    kslot = full[:, 2304:2432]
    kslot2 = jnp.concatenate([kslot, kslot], axis=1)
    k_rot = _swap_quarters(kslot * mk_g_ref[:, :LANES], quarter)
    k_rot2 = jnp.concatenate([k_rot, k_rot], axis=1)
    cos2 = jnp.concatenate([cos, cos], axis=1)
    sin2 = jnp.concatenate([sin, sin], axis=1)
    k_chunks = [kvf[:, c * 256:(c + 1) * 256] + kslot2 for c in range(n_mla)]
    ms_k = _group_meansq(k_chunks, g128_ref, 1.0 / MLA_QK)
    for c in range(n_mla):
        y = _norm_rope_chunk(q_chunks[c], ms_q[c], mq_g_ref[...], cos, sin, quarter)
        o_ref[:, c * 256:(c + 1) * 256] = y.astype(BF16)
    for c in range(n_mla):
        inv_rms = lax.rsqrt(ms_k[c] + EPS)
        y = (k_chunks[c] * inv_rms * mk_g_ref[...]) * cos2 + (inv_rms * k_rot2) * sin2
        o_ref[:, 1024 + c * 256:1024 + (c + 1) * 256] = y.astype(BF16)
    o_ref[:, 2048:2560] = kvf[:, 1024:1536].astype(BF16)

    n_chunks = [full[:, 768 + c * 256:768 + (c + 1) * 256] for c in range(4)]
    ms64 = _group_meansq(n_chunks, g64_ref, 1.0 / HEAD_DIM)
    for c in range(4):
        gain = nq_g_ref[...] if c < 2 else nk_g_ref[...]
        y = _norm_rope_chunk(n_chunks[c], ms64[c], gain, None, None, None)
        o_ref[:, 2560 + c * 256:2560 + (c + 1) * 256] = y.astype(BF16)
    o_ref[:, 3584:4096] = full[:, 1792:2304].astype(BF16)


def _proj_odd(xa, g, sh, sc, w, wq, wkv, gains, g64, g128, cos, sin, bpb, nbatch):
    na, d = xa.shape
    tok_specs, _ = _token_specs(d, bpb, nbatch)
    consts = (w, wq, wkv) + tuple(gains) + (g64, g128)
    return pl.pallas_call(
        _proj_odd_kernel,
        grid=(na // TM,),
        in_specs=tok_specs + [_whole(a.shape) for a in consts] + [_rope_spec(bpb, nbatch)] * 2,
        out_specs=pl.BlockSpec((TM, ODD_COLS), lambda i: (i, 0)),
        out_shape=jax.ShapeDtypeStruct((na, ODD_COLS), BF16),
        compiler_params=_params("parallel"),
        name="proj_odd",
    )(xa, g, sh, sc, *consts, cos, sin)


def _attn_kernel(*refs, mode, split, main, seq, has_sink, lam_init, tq):
    it = iter(refs)
    q_ref = next(it)
    k_ref = v_ref = bias_ref = sink_ref = lam_ref = subln_ref = None
    if main is not None:
        k_ref, v_ref = next(it), next(it)
    kc_ref, vc_ref = next(it), next(it)
    if main == "na":
        bias_ref = next(it)
    if has_sink:
        sink_ref = next(it)
    safe_ref = next(it)
    if mode == "diff":
        lam_ref, subln_ref = next(it), next(it)
    o_ref = next(it)

    pair = pl.program_id(1)
    qi = pl.program_id(2)
    q = q_ref[...]
    low = lax.broadcasted_iota(jnp.int32, (1, LANES), 1) < HALF
    if split == "mask":
        zero = jnp.zeros_like(q)
        qs = (jnp.where(low, q, zero), jnp.where(low, zero, q))
        k_of = lambda k, s: k
    else:
        qs = (q[:, :LANES], q[:, LANES:])
        k_of = lambda k, s: k[:, s * LANES:(s + 1) * LANES]
    sinks = [sink_ref[2 * pair + s] * LOG2E for s in range(2)] if has_sink else None

    def logits_of(s, k):
        return lax.dot_general(qs[s], k_of(k, s), (((1,), (1,)), ((), ())), preferred_element_type=F32)

    def finish(o_lo, o_hi):
        if mode == "diff":
            lv = lam_ref[...]
            lam = (jnp.exp(jnp.sum(lv[0:1] * lv[1:2], axis=-1, keepdims=True))
                   - jnp.exp(jnp.sum(lv[2:3] * lv[3:4], axis=-1, keepdims=True)) + lam_init)
            o = o_lo - lam * o_hi
            ms = jnp.mean(o * o, axis=-1, keepdims=True)
            o = o * lax.rsqrt(ms + EPS) * subln_ref[...] * (1.0 - lam_init)
        else:
            o = jnp.where(low, o_lo, o_hi)
        o_ref[...] = o.astype(BF16)

    def init(s):
        if has_sink:
            m0 = jnp.full((tq, 1), sinks[s], F32)
            l0 = jnp.ones((tq, 1), F32)
        else:
            m0 = jnp.full((tq, 1), NEG, F32)
            l0 = jnp.zeros((tq, 1), F32)
        return m0, l0, jnp.zeros((tq, LANES), F32)

    def segment(state, k, v, logit_fn):
        out = []
        for s in range(2):
            m, l, acc = state[s]
            logits = logits_of(s, k)
            if logit_fn is not None:
                logits = logit_fn(logits, s)
            m_new = jnp.maximum(m, jnp.max(logits, axis=-1, keepdims=True))
            alpha = jnp.exp2(m - m_new)
            p = jnp.exp2(logits - m_new)
            l = alpha * l + jnp.sum(p, axis=-1, keepdims=True)
            acc = alpha * acc + jnp.dot(p.astype(BF16), v, preferred_element_type=F32)
            out.append((m_new, l, acc))
        return tuple(out)

    def all_keys(step, state, tk):
        if main == "full":
            def body(c, st):
                off = pl.multiple_of(c * tk, tk)
                return step(st, k_ref[pl.ds(off, tk), :], v_ref[pl.ds(off, tk), :], None)
            state = lax.fori_loop(0, seq // tk, body, state)
        elif main == "window":
            span = tq + 2 * WINDOW
            q0 = qi * tq
            w0 = jnp.clip(q0 - WINDOW, 0, seq - span)
            off = pl.multiple_of(w0, WINDOW)
            delta = w0 - q0
            row = lax.broadcasted_iota(jnp.int32, (tq, span), 0)
            col = lax.broadcasted_iota(jnp.int32, (tq, span), 1)
            inside = jnp.abs(col - row + delta) <= WINDOW
            state = step(state, k_ref[pl.ds(off, span), :], v_ref[pl.ds(off, span), :],
                         lambda lg, s: jnp.where(inside, lg, NEG))
        elif main == "na":
            rows_q = tq // GRID_W
            span_rows = 2 * NA_ROWS
            w0 = jnp.clip(qi * rows_q - NA_ROWS // 2, 0, seq // GRID_W - span_rows)
            off = pl.multiple_of(w0 * GRID_W, GRID_W)
            span = span_rows * GRID_W
            state = step(state, k_ref[pl.ds(off, span), :], v_ref[pl.ds(off, span), :],
                         lambda lg, s: lg + bias_ref[0, s])
        return step(state, kc_ref[...], vc_ref[...], None)

    def online():
        (_, l_lo, acc_lo), (_, l_hi, acc_hi) = all_keys(segment, (init(0), init(1)), TK_FULL)
        finish(acc_lo / l_lo, acc_hi / l_hi)

    def accumulate(acc, k, v, logit_fn):
        v_ext = jnp.concatenate([v, jnp.ones_like(v)], axis=1)
        out = []
        for s in range(2):
            logits = logits_of(s, k)
            if logit_fn is not None:
                logits = logit_fn(logits, s)
            p = jnp.exp2(logits).astype(BF16)
            out.append(acc[s] + jnp.dot(p, v_ext, preferred_element_type=F32))
        return tuple(out)

    def max_free():
        zero_acc = jnp.zeros((tq, 2 * LANES), F32)
        acc = all_keys(accumulate, (zero_acc, zero_acc), TK_MAX_FREE)
        outs = []
        for s in range(2):
            den = acc[s][:, LANES:]
            if has_sink:
                den = den + jnp.exp2(jnp.full((1, LANES), sinks[s], F32))
            outs.append(acc[s][:, :LANES] / den)
        finish(*outs)

    safe = safe_ref[0] == 1
    pl.when(safe)(max_free)
    pl.when(jnp.logical_not(safe))(online)


def _attention(p, *, name, mode, split, main, n_pairs, q_blk, k_blk, v_blk, out_cols, ctx_queries,
               nbatch, seq, ctx_len, safe, bias=None, sink=None, lam=None, subln=None, lam_init=0.0):
    n_lat = nbatch * seq
    qw = LANES if split == "mask" else 2 * LANES
    if ctx_queries:
        tq, nq = ctx_len, 1
        q_row = lambda b, j, i: n_lat // ctx_len + b
        out_rows = nbatch * ctx_len
        out_row = lambda b, j, i: b
    else:
        tq = TQ_FULL if main == "full" else TQ
        nq = seq // tq
        q_row = lambda b, j, i: b * nq + i
        out_rows = n_lat
        out_row = q_row
    ctx_row = lambda b, j, i: n_lat // ctx_len + b

    in_specs = [pl.BlockSpec((tq, qw), lambda b, j, i: (q_row(b, j, i), q_blk(j)))]
    args = [p]
    if main is not None:
        in_specs += [pl.BlockSpec((seq, qw), lambda b, j, i: (b, k_blk(j))),
                     pl.BlockSpec((seq, LANES), lambda b, j, i: (b, v_blk(j)))]
        args += [p, p]
    in_specs += [pl.BlockSpec((ctx_len, qw), lambda b, j, i: (ctx_row(b, j, i), k_blk(j))),
                 pl.BlockSpec((ctx_len, LANES), lambda b, j, i: (ctx_row(b, j, i), v_blk(j)))]
    args += [p, p]
    if main == "na":
        span = 2 * NA_ROWS * GRID_W
        cls = lambda i: jnp.where(i == 0, 0, jnp.where(i == nq - 1, 2, 1))
        in_specs.append(pl.BlockSpec((1, 2, tq, span), lambda b, j, i: (cls(i), j, 0, 0)))
        args.append(bias)
    if sink is not None:
        in_specs.append(pl.BlockSpec(memory_space=pltpu.SMEM))
        args.append(sink)
    in_specs.append(pl.BlockSpec(memory_space=pltpu.SMEM))
    args.append(safe)
    if mode == "diff":
        in_specs += [_whole(lam.shape), _whole(subln.shape)]
        args += [lam, subln]

    kern = functools.partial(_attn_kernel, mode=mode, split=split, main=main, seq=seq,
                             has_sink=sink is not None, lam_init=lam_init, tq=tq)
    return pl.pallas_call(
        kern,
        grid=(nbatch, n_pairs, nq),
        in_specs=in_specs,
        out_specs=pl.BlockSpec((tq, LANES), lambda b, j, i: (out_row(b, j, i), j)),
        out_shape=jax.ShapeDtypeStruct((out_rows, out_cols), BF16),
        compiler_params=_params("parallel", "parallel", "arbitrary"),
        name=name,
    )(*args)


def _na_bias_kernel(rpb_ref, o_ref, *, n_rows, rows_q):
    h = pl.program_id(0)
    n_dr = 2 * NA_ROWS - 1
    n_dc = 2 * NA_COLS - 1
    cq = lax.broadcasted_iota(jnp.int32, (GRID_W, LANES), 0)
    kc = lax.broadcasted_iota(jnp.int32, (GRID_W, LANES), 1) & (GRID_W - 1)
    dc_idx = jnp.clip(kc - cq, -(NA_COLS - 1), NA_COLS - 1) + NA_COLS - 1
    cs = jnp.clip(cq - NA_COLS // 2, 0, GRID_W - NA_COLS)
    col_ok = (kc >= cs) & (kc < cs + NA_COLS)
    tiles = [jnp.zeros((GRID_W, LANES), F32) for _ in range(n_dr)]
    for dc in range(n_dc):
        hit = dc_idx == dc
        for dr in range(n_dr):
            tiles[dr] = jnp.where(hit, rpb_ref[(h * n_dr + dr) * n_dc + dc] * LOG2E, tiles[dr])
    tiles = [jnp.where(col_ok, t, NEG) for t in tiles]
    masked = jnp.full((GRID_W, LANES), NEG, F32)
    low = lax.broadcasted_iota(jnp.int32, (GRID_W, LANES), 1) < HALF

    span_rows = 2 * NA_ROWS
    first_q_row = (0, span_rows, n_rows - rows_q)
    for c in range(3):
        r0 = first_q_row[c]
        w0 = min(max(r0 - NA_ROWS // 2, 0), n_rows - span_rows)
        for qr in range(rows_q):
            r = r0 + qr
            rs = min(max(r - NA_ROWS // 2, 0), n_rows - NA_ROWS)
            for m in range(span_rows // 2):
                halves = []
                for kr in (2 * m, 2 * m + 1):
                    k_abs = w0 + kr
                    halves.append(tiles[k_abs - r + NA_ROWS - 1] if rs <= k_abs < rs + NA_ROWS else masked)
                o_ref[c, 0, qr * GRID_W:(qr + 1) * GRID_W, m * LANES:(m + 1) * LANES] = (
                    jnp.where(low, halves[0], halves[1]))


def _na_bias(rpb, seq):
    n_rows = seq // GRID_W
    rows_q = TQ // GRID_W
    span = 2 * NA_ROWS * GRID_W
    kern = functools.partial(_na_bias_kernel, n_rows=n_rows, rows_q=rows_q)
    return pl.pallas_call(
        kern,
        grid=(NA_HEADS,),
        in_specs=[pl.BlockSpec(memory_space=pltpu.SMEM)],
        out_specs=pl.BlockSpec((3, 1, TQ, span), lambda h: (0, h, 0, 0)),
        out_shape=jax.ShapeDtypeStruct((3, NA_HEADS, TQ, span), F32),
        compiler_params=_params("parallel"),
        name="na_bias",
    )(rpb.reshape(-1))


def _post_kernel(x_ref, oa_ref, ob_ref, wo_ref, g1_ref, n2_ref, sh_ref, sc_ref, g2_ref, w1_ref, w2_ref, o_ref):
    half = oa_ref.shape[1]
    y = (jnp.dot(oa_ref[...], wo_ref[:half, :], preferred_element_type=F32)
         + jnp.dot(ob_ref[...], wo_ref[half:, :], preferred_element_type=F32))
    x1 = x_ref[...] + g1_ref[0] * y
    ms = jnp.mean(x1 * x1, axis=-1, keepdims=True)
    h = (x1 * lax.rsqrt(ms + EPS) * n2_ref[...]) * (1.0 + sc_ref[0]) + sh_ref[0]
    a = jnp.dot(h.astype(BF16), w1_ref[...], preferred_element_type=F32)
    a = jnp.square(jnp.maximum(a, 0.0)).astype(BF16)
    o_ref[...] = x1 + g2_ref[0] * jnp.dot(a, w2_ref[...], preferred_element_type=F32)


def _post(xa, oa, ob, wo, g1, n2, sh, sc, g2, w1, w2, n_tokens, bpb, nbatch):
    d = xa.shape[1]
    half = oa.shape[1]
    row = lambda i: (jnp.minimum(i // bpb, nbatch), 0, 0)
    mod_spec = pl.BlockSpec((1, 1, d), row)
    resident = lambda a: pl.BlockSpec(a.shape, lambda i: (0, 0), pipeline_mode=pl.Buffered(1))
    return pl.pallas_call(
        _post_kernel,
        grid=(n_tokens // TM,),
        in_specs=[
            pl.BlockSpec((TM, d), lambda i: (i, 0)),
            pl.BlockSpec((TM, half), lambda i: (i, 0)),
            pl.BlockSpec((TM, half), lambda i: (i, 0)),
            resident(wo), mod_spec, pl.BlockSpec((1, d), lambda i: (0, 0)), mod_spec, mod_spec, mod_spec,
            resident(w1), resident(w2),
        ],
        out_specs=pl.BlockSpec((TM, d), lambda i: (i, 0)),
        out_shape=jax.ShapeDtypeStruct((n_tokens, d), F32),
        compiler_params=_params("parallel"),
        name="post",
    )(xa, oa, ob, wo, g1, n2, sh, sc, g2, w1, w2)


def _block_diag_ones(group):
    idx = np.arange(256) // group
    return jnp.asarray(idx[:, None] == idx[None, :], dtype=BF16)


def _rope_tables(seq, rot_dim, lane0, pad_rows):
    t = jnp.arange(seq, dtype=jnp.int32)
    row = (t // GRID_W).astype(F32)
    col = (t % GRID_W).astype(F32)
    n_freq = rot_dim // 4
    freqs = jnp.power(ROPE_BASE, -jnp.arange(n_freq, dtype=F32) / n_freq)
    ar = row[:, None] * freqs[None, :]
    ac = col[:, None] * freqs[None, :]
    ang = jnp.concatenate([ar, ar, ac, ac], axis=-1)
    sign = jnp.asarray(np.tile(np.repeat([-1.0, 1.0], n_freq), 2), F32)
    cos, sin = jnp.cos(ang), jnp.sin(ang) * sign[None, :]
    if lane0 is None:
        reps = LANES // rot_dim
        cos, sin = jnp.tile(cos, (1, reps)), jnp.tile(sin, (1, reps))
    else:
        pad = ((0, 0), (lane0, LANES - lane0 - rot_dim))
        cos = jnp.pad(cos, pad, constant_values=1.0)
        sin = jnp.pad(sin, pad)
    cos = jnp.concatenate([cos, jnp.ones((pad_rows, LANES), F32)], axis=0)
    sin = jnp.concatenate([sin, jnp.zeros((pad_rows, LANES), F32)], axis=0)
    return cos, sin


def _logits_bounded(q_gain, k_gain, dim, bias=None, sink=None):
    bound = jnp.max(jnp.abs(q_gain)) * jnp.max(jnp.abs(k_gain)) * (dim ** 0.5) * 1.02
    if bias is not None:
        bound = bound + jnp.max(jnp.abs(bias))
    if sink is not None:
        bound = jnp.maximum(bound, jnp.max(jnp.abs(sink)))
    return (bound <= MAX_FREE_LOGIT_BOUND).astype(jnp.int32).reshape(1)


def _tile_gain(g, reps, scale=1.0):
    return (jnp.tile(g.astype(F32), reps) * scale)[None, :]


def _even_weights(w, q_g, k_g, sq_g, sk_g):
    nq = DIFF_HEADS * 2 * HEAD_DIM
    qa, ka, va = w[:, :nq], w[:, nq:2 * nq], w[:, 2 * nq:3 * nq]
    o = 3 * nq
    qb = w[:, o:o + SWA_Q_HEADS * HEAD_DIM]
    o += SWA_Q_HEADS * HEAD_DIM
    kb = [w[:, o + i * HEAD_DIM:o + (i + 1) * HEAD_DIM] for i in range(SWA_KV_HEADS)]
    o += SWA_KV_HEADS * HEAD_DIM
    vb = [w[:, o + i * HEAD_DIM:o + (i + 1) * HEAD_DIM] for i in range(SWA_KV_HEADS)]
    dup = lambda parts: [p for p in parts for _ in range(2)]
    w_new = jnp.concatenate([qa, ka, va, qb] + dup(kb) + dup(vb), axis=1).astype(BF16)
    scale = HEAD_DIM ** -0.5 * LOG2E
    ones = lambda n: jnp.ones((1, n), F32)
    gain = jnp.concatenate([
        _tile_gain(q_g, 2 * DIFF_HEADS, scale), _tile_gain(k_g, 2 * DIFF_HEADS), ones(nq),
        _tile_gain(sq_g, SWA_Q_HEADS, scale), _tile_gain(sk_g, 2 * SWA_KV_HEADS),
        ones(2 * SWA_KV_HEADS * HEAD_DIM)], axis=1)
    return w_new, gain


def _odd_weights(w, wq_up, wkv_up, qa_g, kva_g, mq_g, mk_g, nq_g, nk_g):
    d = w.shape[0]
    o = 0
    parts = []
    for n in (MLA_Q_LORA, MLA_KV_LORA, MLA_ROPE, NA_HEADS * HEAD_DIM, NA_HEADS * HEAD_DIM, NA_HEADS * HEAD_DIM):
        parts.append(w[:, o:o + n])
        o += n
    q_a, kv_a, k_r, nq, nk, nv = parts
    slot = jnp.concatenate([jnp.zeros((d, MLA_NOPE), w.dtype), k_r,
                            jnp.zeros((d, LANES - MLA_QK), w.dtype)], axis=1)
    w_new = jnp.concatenate([q_a, kv_a, nq, nk, nv, slot], axis=1).astype(BF16)
    wq = jnp.pad(wq_up.reshape(MLA_Q_LORA, MLA_HEADS, MLA_QK), ((0, 0), (0, 0), (0, MLA_PAD - MLA_QK)))
    wq = wq.reshape(MLA_Q_LORA, MLA_HEADS * MLA_PAD).astype(BF16)
    wkv = wkv_up.reshape(MLA_KV_LORA, MLA_HEADS, MLA_NOPE + MLA_V)
    wk = jnp.pad(wkv[..., :MLA_NOPE], ((0, 0), (0, 0), (0, MLA_PAD - MLA_NOPE))).reshape(MLA_KV_LORA, -1)
    wv = wkv[..., MLA_NOPE:].reshape(MLA_KV_LORA, MLA_HEADS * MLA_V)
    wkv_new = jnp.concatenate([wk, wv], axis=1).astype(BF16)
    pad_gain = lambda g, scale: _tile_gain(jnp.pad(g.astype(F32), (0, MLA_PAD - MLA_QK)), 2, scale)
    gains = (qa_g.astype(F32)[None, :], kva_g.astype(F32)[None, :],
             pad_gain(mq_g, MLA_QK ** -0.5 * LOG2E), pad_gain(mk_g, 1.0),
             _tile_gain(nq_g, 4, HEAD_DIM ** -0.5 * LOG2E), _tile_gain(nk_g, 4))
    return w_new, wq, wkv_new, gains


def kernel(x, c, ctx, c_ctx, ada_w, ada_b, norm1_g, norm2_g, w_out, mlp_w1, mlp_w2, ev_w_in, diff_q_g,
           diff_k_g, diff_lam, diff_subln_g, swa_q_g, swa_k_g, swa_sink, od_w_in, mla_qa_g, mla_kva_g,
           mla_wq_up, mla_wkv_up, mla_q_g, mla_k_g, na_q_g, na_k_g, na_rpb):
    nbatch, seq, d = x.shape
    ctx_len = ctx.shape[1]
    depth = ada_w.shape[0]
    n_lat = nbatch * seq
    n_ctx = nbatch * ctx_len
    assert seq % TM == 0 and n_ctx == TM and seq % TQ == 0 and seq % GRID_W == 0
    assert seq // GRID_W >= 2 * NA_ROWS + TQ // GRID_W and seq >= TQ + 2 * WINDOW
    bpb = seq // TM

    mod_rows = 8
    cvec = jnp.concatenate([c, c_ctx[None, :], jnp.zeros((mod_rows - nbatch - 1, d), F32)], axis=0)
    mod = _modulation(cvec, ada_w, ada_b)

    cos64, sin64 = _rope_tables(seq, HEAD_DIM, None, n_ctx)
    cos32, sin32 = _rope_tables(seq, MLA_ROPE, MLA_NOPE, n_ctx)
    g64 = _block_diag_ones(HEAD_DIM)
    g128 = _block_diag_ones(LANES)

    xa = jnp.concatenate([x.reshape(n_lat, d), ctx.reshape(n_ctx, d)], axis=0)
    common = dict(nbatch=nbatch, seq=seq, ctx_len=ctx_len)

    for layer in range(depth):
        need_ctx = layer < depth - 1
        n_tok = n_lat + n_ctx if need_ctx else n_lat
        m = mod[layer, :nbatch + 1]
        sh1, sc1, g1, sh2, sc2, g2 = [m[:, k * d:(k + 1) * d].reshape(nbatch + 1, 1, d) for k in range(6)]
        n1 = norm1_g[layer][None, :]
        n2 = norm2_g[layer][None, :]
        i = layer // 2
        if layer % 2 == 0:
            lam_init = 0.8 - 0.6 * math.exp(-0.3 * layer)
            w_in, gain = _even_weights(ev_w_in[i], diff_q_g[i], diff_k_g[i], swa_q_g[i], swa_k_g[i])
            p = _proj_even(xa, n1, sh1, sc1, w_in, gain, g64, cos64, sin64, bpb, nbatch)
            lam = diff_lam[i].astype(F32)
            subln = diff_subln_g[i].astype(F32)[None, :]
            sink = swa_sink[i].astype(F32)
            diff_kw = dict(mode="diff", split="mask", n_pairs=DIFF_HEADS, q_blk=lambda j: j,
                           k_blk=lambda j: 4 + j, v_blk=lambda j: 8 + j, out_cols=DIFF_HEADS * LANES,
                           lam=lam, subln=subln, lam_init=lam_init,
                           safe=_logits_bounded(diff_q_g[i], diff_k_g[i], HEAD_DIM), **common)
            swa_kw = dict(mode="pair", split="mask", n_pairs=SWA_Q_HEADS // 2, q_blk=lambda j: 12 + j,
                          k_blk=lambda j: 16 + j // 2, v_blk=lambda j: 18 + j // 2,
                          out_cols=SWA_Q_HEADS * HEAD_DIM, sink=sink,
                          safe=_logits_bounded(swa_q_g[i], swa_k_g[i], HEAD_DIM, sink=sink), **common)
            oa = _attention(p, name="diff_attn", main="full", ctx_queries=False, **diff_kw)
            ob = _attention(p, name="window_attn", main="window", ctx_queries=False, **swa_kw)
            if need_ctx:
                oa_c = _attention(p, name="diff_attn_ctx", main=None, ctx_queries=True, **diff_kw)
                ob_c = _attention(p, name="window_attn_ctx", main=None, ctx_queries=True, **swa_kw)
        else:
            w_in, wq, wkv, gains = _odd_weights(od_w_in[i], mla_wq_up[i], mla_wkv_up[i], mla_qa_g[i],
                                                mla_kva_g[i], mla_q_g[i], mla_k_g[i], na_q_g[i], na_k_g[i])
            p = _proj_odd(xa, n1, sh1, sc1, w_in, wq, wkv, gains, g64, g128, cos32, sin32, bpb, nbatch)
            bias = _na_bias(na_rpb[i].astype(F32), seq)
            mla_kw = dict(mode="pair", split="slice", n_pairs=MLA_HEADS // 2, q_blk=lambda j: j,
                          k_blk=lambda j: 4 + j, v_blk=lambda j: 16 + j, out_cols=MLA_HEADS * MLA_V,
                          safe=_logits_bounded(mla_q_g[i], mla_k_g[i], MLA_QK), **common)
            na_kw = dict(mode="pair", split="mask", n_pairs=NA_HEADS // 2, q_blk=lambda j: 20 + j,
                         k_blk=lambda j: 24 + j, v_blk=lambda j: 28 + j, out_cols=NA_HEADS * HEAD_DIM,
                         safe=_logits_bounded(na_q_g[i], na_k_g[i], HEAD_DIM, bias=na_rpb[i]), **common)
            oa = _attention(p, name="mla_attn", main="full", ctx_queries=False, **mla_kw)
            ob = _attention(p, name="na_attn", main="na", ctx_queries=False, bias=bias, **na_kw)
            if need_ctx:
                oa_c = _attention(p, name="mla_attn_ctx", main=None, ctx_queries=True, **mla_kw)
                ob_c = _attention(p, name="na_attn_ctx", main=None, ctx_queries=True, **na_kw)
        if need_ctx:
            oa = jnp.concatenate([oa, oa_c], axis=0)
            ob = jnp.concatenate([ob, ob_c], axis=0)
        xa = _post(xa, oa, ob, w_out[layer].astype(BF16), g1, n2, sh2, sc2, g2,
                   mlp_w1[layer].astype(BF16), mlp_w2[layer].astype(BF16), n_tok, bpb, nbatch)
    return xa.reshape(nbatch, seq, d)
```

```python
import functools
import math

import numpy as np
import jax
import jax.numpy as jnp
from jax import lax
from jax.experimental import pallas as pl
from jax.experimental.pallas import tpu as pltpu

F32 = jnp.float32
BF16 = jnp.bfloat16

LANES = 128
HALF = LANES // 2
GRID_W = 64
HEAD_DIM = 64
EPS = 1e-6
ROPE_BASE = 10000.0
DIFF_HEADS = 4
SWA_Q_HEADS = 8
SWA_KV_HEADS = 2
WINDOW = 128
MLA_HEADS = 8
MLA_Q_LORA = 512
MLA_KV_LORA = 256
MLA_NOPE = 64
MLA_ROPE = 32
MLA_QK = MLA_NOPE + MLA_ROPE
MLA_V = 64
NA_HEADS = 8
NA_ROWS = 8
NA_COLS = 16
NEG = -1e30
LOG2E = math.log2(math.e)
MAX_FREE_LOGIT_BOUND = 50.0

TM = 512
TQ = 512
TQ_FULL = 1024
TK_FULL = 512
TK_MAX_FREE = 4096
ONES_ROWS = 16
MOD_TN = 1536
VMEM_LIMIT = 56 * 1024 * 1024


def _params(*sem):
    return pltpu.CompilerParams(dimension_semantics=sem, vmem_limit_bytes=VMEM_LIMIT)


def _mod_kernel(c_ref, w_ref, b_ref, o_ref):
    c = c_ref[...]
    a = (c * jax.nn.sigmoid(c)).astype(BF16)
    w = w_ref[0].astype(BF16)
    o_ref[0] = jnp.dot(a, w, preferred_element_type=F32) + b_ref[0]


def _modulation(cvec, ada_w, ada_b):
    depth, d, n = ada_w.shape
    rows = cvec.shape[0]
    return pl.pallas_call(
        _mod_kernel,
        grid=(depth, n // MOD_TN),
        in_specs=[
            pl.BlockSpec((rows, d), lambda l, j: (0, 0)),
            pl.BlockSpec((1, d, MOD_TN), lambda l, j: (l, 0, j)),
            pl.BlockSpec((1, 1, MOD_TN), lambda l, j: (l, 0, j)),
        ],
        out_specs=pl.BlockSpec((1, rows, MOD_TN), lambda l, j: (l, 0, j)),
        out_shape=jax.ShapeDtypeStruct((depth, rows, n), F32),
        compiler_params=_params("parallel", "parallel"),
        name="modulation",
    )(cvec, ada_w, ada_b.reshape(depth, 1, n))


def _modulated(x_ref, g_ref, sh_ref, sc_ref):
    x = x_ref[...]
    ms = jnp.mean(x * x, axis=-1, keepdims=True)
    y = x * lax.rsqrt(ms + EPS) * g_ref[...]
    return y * (1.0 + sc_ref[0]) + sh_ref[0]


def _group_meansq(chunks, g_ref, inv_n):
    rows = chunks[0].shape[0]
    sq = jnp.concatenate([(c * c).astype(BF16) for c in chunks], axis=0)
    ss = jnp.dot(sq, g_ref[...], preferred_element_type=F32) * inv_n
    return [ss[i * rows:(i + 1) * rows] for i in range(len(chunks))]


def _swap_quarters(x, quarter):
    lane = lax.broadcasted_iota(jnp.int32, x.shape, 1)
    odd = (lane & quarter) != 0
    from_lower = pltpu.roll(x, quarter, 1)
    from_upper = pltpu.roll(x, LANES - quarter, 1)
    return jnp.where(odd, from_lower, from_upper)


def _rope(x, cos, sin_signed, quarter):
    return x * cos + _swap_quarters(x, quarter) * sin_signed


def _norm_rope_chunk(acc, meansq, gain, cos, sin, quarter):
    y = acc * lax.rsqrt(meansq + EPS) * gain
    if quarter is None:
        return y
    return jnp.concatenate([_rope(y[:, :LANES], cos, sin, quarter),
                            _rope(y[:, LANES:], cos, sin, quarter)], axis=1)


def _token_specs(d, bpb, nbatch):
    row = lambda i, *_: (jnp.minimum(i // bpb, nbatch), 0, 0)
    return [
        pl.BlockSpec((TM, d), lambda i, *_: (i, 0)),
        pl.BlockSpec((1, d), lambda i, *_: (0, 0)),
        pl.BlockSpec((1, 1, d), row),
        pl.BlockSpec((1, 1, d), row),
    ], row


def _rope_spec(bpb, nbatch):
    return pl.BlockSpec((TM, LANES), lambda i: (jnp.where(i < bpb * nbatch, i % bpb, bpb), 0))


def _whole(shape):
    return pl.BlockSpec(shape, lambda *_: (0,) * len(shape))


EVEN_COLS = 2560
EVEN_NORM_CHUNKS = (0, 1, 2, 3, 6, 7, 8)
EVEN_DOT_RANGES = ((0, 4), (6, 9), (4, 6), (9, 10))


def _proj_even_kernel(x_ref, g_ref, sh_ref, sc_ref, w_ref, gain_ref, g64_ref, cos_ref, sin_ref, o_ref):
    h = _modulated(x_ref, g_ref, sh_ref, sc_ref).astype(BF16)
    cos = cos_ref[...]
    sin = sin_ref[...]
    for first, last in EVEN_DOT_RANGES:
        full = jnp.dot(h, w_ref[:, first * 256:last * 256], preferred_element_type=F32)
        chunk = lambda c: full[:, (c - first) * 256:(c - first + 1) * 256]
        normed = [c for c in range(first, last) if c in EVEN_NORM_CHUNKS]
        meansq = {}
        if normed:
            meansq = dict(zip(normed, _group_meansq([chunk(c) for c in normed], g64_ref, 1.0 / HEAD_DIM)))
        for c in range(first, last):
            cols = slice(c * 256, (c + 1) * 256)
            acc = chunk(c)
            if c in EVEN_NORM_CHUNKS:
                acc = _norm_rope_chunk(acc, meansq[c], gain_ref[:, cols], cos, sin, HEAD_DIM // 4)
            o_ref[:, cols] = acc.astype(BF16)


def _proj_even(xa, g, sh, sc, w, gain, g64, cos, sin, bpb, nbatch):
    na, d = xa.shape
    tok_specs, _ = _token_specs(d, bpb, nbatch)
    return pl.pallas_call(
        _proj_even_kernel,
        grid=(na // TM,),
        in_specs=tok_specs + [
            _whole(w.shape), _whole(gain.shape), _whole(g64.shape),
            _rope_spec(bpb, nbatch), _rope_spec(bpb, nbatch),
        ],
        out_specs=pl.BlockSpec((TM, EVEN_COLS), lambda i: (i, 0)),
        out_shape=jax.ShapeDtypeStruct((na, EVEN_COLS), BF16),
        compiler_params=_params("parallel"),
        name="proj_even",
    )(xa, g, sh, sc, w, gain, g64, cos, sin)


ODD_W_COLS = 2432
ODD_COLS = 4096
MLA_PAD = LANES


def _proj_odd_kernel(x_ref, g_ref, sh_ref, sc_ref, w_ref, wq_ref, wkv_ref, qa_g_ref, kva_g_ref,
                     mq_g_ref, mk_g_ref, nq_g_ref, nk_g_ref, g64_ref, g128_ref, cos_ref, sin_ref, o_ref):
    h = _modulated(x_ref, g_ref, sh_ref, sc_ref).astype(BF16)
    cos = cos_ref[...]
    sin = sin_ref[...]
    quarter = MLA_ROPE // 4

    full = jnp.dot(h, w_ref[...], preferred_element_type=F32)

    def low_rank(a, gain_ref):
        ms = jnp.mean(a * a, axis=-1, keepdims=True)
        return (a * lax.rsqrt(ms + EPS) * gain_ref[...]).astype(BF16)

    n_mla = MLA_HEADS * MLA_PAD // 256
    qa = low_rank(full[:, :MLA_Q_LORA], qa_g_ref)
    qf = jnp.dot(qa, wq_ref[...], preferred_element_type=F32)
    q_chunks = [qf[:, c * 256:(c + 1) * 256] for c in range(n_mla)]
    ms_q = _group_meansq(q_chunks, g128_ref, 1.0 / MLA_QK)
    kva = low_rank(full[:, MLA_Q_LORA:MLA_Q_LORA + MLA_KV_LORA], kva_g_ref)
    kvf = jnp.dot(kva, wkv_ref[...], preferred_element_type=F32)
    kslot = full[:, 2304:2432]
    kslot2 = jnp.concatenate([kslot, kslot], axis=1)
    k_rot = _swap_quarters(kslot * mk_g_ref[:, :LANES], quarter)
    k_rot2 = jnp.concatenate([k_rot, k_rot], axis=1)
    cos2 = jnp.concatenate([cos, cos], axis=1)
    sin2 = jnp.concatenate([sin, sin], axis=1)
    k_chunks = [kvf[:, c * 256:(c + 1) * 256] + kslot2 for c in range(n_mla)]
    ms_k = _group_meansq(k_chunks, g128_ref, 1.0 / MLA_QK)
    for c in range(n_mla):
        y = _norm_rope_chunk(q_chunks[c], ms_q[c], mq_g_ref[...], cos, sin, quarter)
        o_ref[:, c * 256:(c + 1) * 256] = y.astype(BF16)
    for c in range(n_mla):
        inv_rms = lax.rsqrt(ms_k[c] + EPS)
        y = (k_chunks[c] * inv_rms * mk_g_ref[...]) * cos2 + (inv_rms * k_rot2) * sin2
        o_ref[:, 1024 + c * 256:1024 + (c + 1) * 256] = y.astype(BF16)
    o_ref[:, 2048:2560] = kvf[:, 1024:1536].astype(BF16)

    n_chunks = [full[:, 768 + c * 256:768 + (c + 1) * 256] for c in range(4)]
    ms64 = _group_meansq(n_chunks, g64_ref, 1.0 / HEAD_DIM)
    for c in range(4):
        gain = nq_g_ref[...] if c < 2 else nk_g_ref[...]
        y = _norm_rope_chunk(n_chunks[c], ms64[c], gain, None, None, None)
        o_ref[:, 2560 + c * 256:2560 + (c + 1) * 256] = y.astype(BF16)
    o_ref[:, 3584:4096] = full[:, 1792:2304].astype(BF16)


def _proj_odd(xa, g, sh, sc, w, wq, wkv, gains, g64, g128, cos, sin, bpb, nbatch):
    na, d = xa.shape
    tok_specs, _ = _token_specs(d, bpb, nbatch)
    consts = (w, wq, wkv) + tuple(gains) + (g64, g128)
    return pl.pallas_call(
        _proj_odd_kernel,
        grid=(na // TM,),
        in_specs=tok_specs + [_whole(a.shape) for a in consts] + [_rope_spec(bpb, nbatch)] * 2,
        out_specs=pl.BlockSpec((TM, ODD_COLS), lambda i: (i, 0)),
        out_shape=jax.ShapeDtypeStruct((na, ODD_COLS), BF16),
        compiler_params=_params("parallel"),
        name="proj_odd",
    )(xa, g, sh, sc, *consts, cos, sin)


def _attn_kernel(*refs, mode, split, main, seq, has_sink, lam_init, tq):
    it = iter(refs)
    q_ref = next(it)
    k_ref = v_ref = bias_ref = sink_ref = lam_ref = subln_ref = None
    if main is not None:
        k_ref, v_ref = next(it), next(it)
    kc_ref, vc_ref = next(it), next(it)
    if main == "na":
        bias_ref = next(it)
    if has_sink:
        sink_ref = next(it)
    safe_ref = next(it)
    if mode == "diff":
        lam_ref, subln_ref = next(it), next(it)
    o_ref = next(it)
    vt_ref, vct_ref = (next(it), next(it)) if main == "full" else (None, None)

    pair = pl.program_id(1)
    qi = pl.program_id(2)
    q = q_ref[...]
    low = lax.broadcasted_iota(jnp.int32, (1, LANES), 1) < HALF
    if split == "mask":
        zero = jnp.zeros_like(q)
        qs = (jnp.where(low, q, zero), jnp.where(low, zero, q))
        k_of = lambda k, s: k
    else:
        qs = (q[:, :LANES], q[:, LANES:])
        k_of = lambda k, s: k[:, s * LANES:(s + 1) * LANES]
    sinks = [sink_ref[2 * pair + s] * LOG2E for s in range(2)] if has_sink else None

    def logits_of(s, k):
        return lax.dot_general(qs[s], k_of(k, s), (((1,), (1,)), ((), ())), preferred_element_type=F32)

    def finish(o_lo, o_hi):
        if mode == "diff":
            lv = lam_ref[...]
            lam = (jnp.exp(jnp.sum(lv[0:1] * lv[1:2], axis=-1, keepdims=True))
                   - jnp.exp(jnp.sum(lv[2:3] * lv[3:4], axis=-1, keepdims=True)) + lam_init)
            o = o_lo - lam * o_hi
            ms = jnp.mean(o * o, axis=-1, keepdims=True)
            o = o * lax.rsqrt(ms + EPS) * subln_ref[...] * (1.0 - lam_init)
        else:
            o = jnp.where(low, o_lo, o_hi)
        o_ref[...] = o.astype(BF16)

    def init(s):
        if has_sink:
            m0 = jnp.full((tq, 1), sinks[s], F32)
            l0 = jnp.ones((tq, 1), F32)
        else:
            m0 = jnp.full((tq, 1), NEG, F32)
            l0 = jnp.zeros((tq, 1), F32)
        return m0, l0, jnp.zeros((tq, LANES), F32)

    def segment(state, k, v, logit_fn):
        out = []
        for s in range(2):
            m, l, acc = state[s]
            logits = logits_of(s, k)
            if logit_fn is not None:
                logits = logit_fn(logits, s)
            m_new = jnp.maximum(m, jnp.max(logits, axis=-1, keepdims=True))
            alpha = jnp.exp2(m - m_new)
            p = jnp.exp2(logits - m_new)
            l = alpha * l + jnp.sum(p, axis=-1, keepdims=True)
            acc = alpha * acc + jnp.dot(p.astype(BF16), v, preferred_element_type=F32)
            out.append((m_new, l, acc))
        return tuple(out)

    def all_keys(step, state, tk):
        if main == "full":
            def body(c, st):
                off = pl.multiple_of(c * tk, tk)
                return step(st, k_ref[pl.ds(off, tk), :], v_ref[pl.ds(off, tk), :], None)
            state = lax.fori_loop(0, seq // tk, body, state)
        elif main == "window":
            span = tq + 2 * WINDOW
            q0 = qi * tq
            w0 = jnp.clip(q0 - WINDOW, 0, seq - span)
            off = pl.multiple_of(w0, WINDOW)
            delta = w0 - q0
            row = lax.broadcasted_iota(jnp.int32, (tq, span), 0)
            col = lax.broadcasted_iota(jnp.int32, (tq, span), 1)
            inside = jnp.abs(col - row + delta) <= WINDOW
            state = step(state, k_ref[pl.ds(off, span), :], v_ref[pl.ds(off, span), :],
                         lambda lg, s: jnp.where(inside, lg, NEG))
        elif main == "na":
            rows_q = tq // GRID_W
            span_rows = 2 * NA_ROWS
            w0 = jnp.clip(qi * rows_q - NA_ROWS // 2, 0, seq // GRID_W - span_rows)
            off = pl.multiple_of(w0 * GRID_W, GRID_W)
            span = span_rows * GRID_W
            state = step(state, k_ref[pl.ds(off, span), :], v_ref[pl.ds(off, span), :],
                         lambda lg, s: lg + bias_ref[0, s])
        return step(state, kc_ref[...], vc_ref[...], None)

    def online():
        (_, l_lo, acc_lo), (_, l_hi, acc_hi) = all_keys(segment, (init(0), init(1)), TK_FULL)
        finish(acc_lo / l_lo, acc_hi / l_hi)

    def accumulate(acc, k, v, logit_fn):
        v_ext = jnp.concatenate([v, jnp.ones_like(v)], axis=1)
        out = []
        for s in range(2):
            logits = logits_of(s, k)
            if logit_fn is not None:
                logits = logit_fn(logits, s)
            p = jnp.exp2(logits).astype(BF16)
            out.append(acc[s] + jnp.dot(p, v_ext, preferred_element_type=F32))
        return tuple(out)

    def max_free():
        zero_acc = jnp.zeros((tq, 2 * LANES), F32)
        acc = all_keys(accumulate, (zero_acc, zero_acc), TK_MAX_FREE)
        outs = []
        for s in range(2):
            den = acc[s][:, LANES:]
            if has_sink:
                den = den + jnp.exp2(jnp.full((1, LANES), sinks[s], F32))
            outs.append(acc[s][:, :LANES] / den)
        finish(*outs)

    def max_free_transposed():
        n_chunks = seq // TK_MAX_FREE
        ctx_len = vc_ref.shape[0]

        @pl.when(qi == 0)
        def _():
            ones = jnp.ones((ONES_ROWS, TK_MAX_FREE), BF16)
            for c in range(n_chunks):
                vt_ref[c, :LANES, :] = v_ref[c * TK_MAX_FREE:(c + 1) * TK_MAX_FREE, :].T
                vt_ref[c, LANES:, :] = ones
            vct_ref[:LANES, :] = vc_ref[...].T
            vct_ref[LANES:, :] = ones[:, :ctx_len]

        def accumulate_t(acc, k, vt):
            out = []
            for s in range(2):
                logits_t = lax.dot_general(k_of(k, s), qs[s], (((1,), (1,)), ((), ())),
                                           preferred_element_type=F32)
                p_t = jnp.exp2(logits_t).astype(BF16)
                out.append(acc[s] + jnp.dot(vt, p_t, preferred_element_type=F32))
            return tuple(out)

        def body(c, acc):
            off = pl.multiple_of(c * TK_MAX_FREE, TK_MAX_FREE)
            return accumulate_t(acc, k_ref[pl.ds(off, TK_MAX_FREE), :], vt_ref[c])

        zero_acc = jnp.zeros((LANES + ONES_ROWS, tq), F32)
        acc = lax.fori_loop(0, n_chunks, body, (zero_acc, zero_acc))
        acc = accumulate_t(acc, kc_ref[...], vct_ref[...])
        finish(*[(acc[s][:LANES] / acc[s][LANES:LANES + 1]).T for s in range(2)])

    safe = safe_ref[0] == 1
    pl.when(safe)(max_free_transposed if main == "full" else max_free)
    pl.when(jnp.logical_not(safe))(online)


def _attention(p, *, name, mode, split, main, n_pairs, q_blk, k_blk, v_blk, out_cols, ctx_queries,
               nbatch, seq, ctx_len, safe, bias=None, sink=None, lam=None, subln=None, lam_init=0.0):
    n_lat = nbatch * seq
    qw = LANES if split == "mask" else 2 * LANES
    if ctx_queries:
        tq, nq = ctx_len, 1
        q_row = lambda b, j, i: n_lat // ctx_len + b
        out_rows = nbatch * ctx_len
        out_row = lambda b, j, i: b
    else:
        tq = TQ_FULL if main == "full" else TQ
        nq = seq // tq
        q_row = lambda b, j, i: b * nq + i
        out_rows = n_lat
        out_row = q_row
    ctx_row = lambda b, j, i: n_lat // ctx_len + b

    in_specs = [pl.BlockSpec((tq, qw), lambda b, j, i: (q_row(b, j, i), q_blk(j)))]
    args = [p]
    if main is not None:
        in_specs += [pl.BlockSpec((seq, qw), lambda b, j, i: (b, k_blk(j))),
                     pl.BlockSpec((seq, LANES), lambda b, j, i: (b, v_blk(j)))]
        args += [p, p]
    in_specs += [pl.BlockSpec((ctx_len, qw), lambda b, j, i: (ctx_row(b, j, i), k_blk(j))),
                 pl.BlockSpec((ctx_len, LANES), lambda b, j, i: (ctx_row(b, j, i), v_blk(j)))]
    args += [p, p]
    if main == "na":
        span = 2 * NA_ROWS * GRID_W
        cls = lambda i: jnp.where(i == 0, 0, jnp.where(i == nq - 1, 2, 1))
        in_specs.append(pl.BlockSpec((1, 2, tq, span), lambda b, j, i: (cls(i), j, 0, 0)))
        args.append(bias)
    if sink is not None:
        in_specs.append(pl.BlockSpec(memory_space=pltpu.SMEM))
        args.append(sink)
    in_specs.append(pl.BlockSpec(memory_space=pltpu.SMEM))
    args.append(safe)
    if mode == "diff":
        in_specs += [_whole(lam.shape), _whole(subln.shape)]
        args += [lam, subln]

    kern = functools.partial(_attn_kernel, mode=mode, split=split, main=main, seq=seq,
                             has_sink=sink is not None, lam_init=lam_init, tq=tq)
    scratch = []
    if main == "full":
        scratch = [pltpu.VMEM((seq // TK_MAX_FREE, LANES + ONES_ROWS, TK_MAX_FREE), BF16),
                   pltpu.VMEM((LANES + ONES_ROWS, ctx_len), BF16)]
    return pl.pallas_call(
        kern,
        grid=(nbatch, n_pairs, nq),
        in_specs=in_specs,
        out_specs=pl.BlockSpec((tq, LANES), lambda b, j, i: (out_row(b, j, i), j)),
        out_shape=jax.ShapeDtypeStruct((out_rows, out_cols), BF16),
        scratch_shapes=scratch,
        compiler_params=_params("parallel", "parallel", "arbitrary"),
        name=name,
    )(*args)


def _na_bias_kernel(rpb_ref, o_ref, *, n_rows, rows_q):
    h = pl.program_id(0)
    n_dr = 2 * NA_ROWS - 1
    n_dc = 2 * NA_COLS - 1
    cq = lax.broadcasted_iota(jnp.int32, (GRID_W, LANES), 0)
    kc = lax.broadcasted_iota(jnp.int32, (GRID_W, LANES), 1) & (GRID_W - 1)
    dc_idx = jnp.clip(kc - cq, -(NA_COLS - 1), NA_COLS - 1) + NA_COLS - 1
    cs = jnp.clip(cq - NA_COLS // 2, 0, GRID_W - NA_COLS)
    col_ok = (kc >= cs) & (kc < cs + NA_COLS)
    tiles = [jnp.zeros((GRID_W, LANES), F32) for _ in range(n_dr)]
    for dc in range(n_dc):
        hit = dc_idx == dc
        for dr in range(n_dr):
            tiles[dr] = jnp.where(hit, rpb_ref[(h * n_dr + dr) * n_dc + dc] * LOG2E, tiles[dr])
    tiles = [jnp.where(col_ok, t, NEG) for t in tiles]
    masked = jnp.full((GRID_W, LANES), NEG, F32)
    low = lax.broadcasted_iota(jnp.int32, (GRID_W, LANES), 1) < HALF

    span_rows = 2 * NA_ROWS
    first_q_row = (0, span_rows, n_rows - rows_q)
    for c in range(3):
        r0 = first_q_row[c]
        w0 = min(max(r0 - NA_ROWS // 2, 0), n_rows - span_rows)
        for qr in range(rows_q):
            r = r0 + qr
            rs = min(max(r - NA_ROWS // 2, 0), n_rows - NA_ROWS)
            for m in range(span_rows // 2):
                halves = []
                for kr in (2 * m, 2 * m + 1):
                    k_abs = w0 + kr
                    halves.append(tiles[k_abs - r + NA_ROWS - 1] if rs <= k_abs < rs + NA_ROWS else masked)
                o_ref[c, 0, qr * GRID_W:(qr + 1) * GRID_W, m * LANES:(m + 1) * LANES] = (
                    jnp.where(low, halves[0], halves[1]))


def _na_bias(rpb, seq):
    n_rows = seq // GRID_W
    rows_q = TQ // GRID_W
    span = 2 * NA_ROWS * GRID_W
    kern = functools.partial(_na_bias_kernel, n_rows=n_rows, rows_q=rows_q)
    return pl.pallas_call(
        kern,
        grid=(NA_HEADS,),
        in_specs=[pl.BlockSpec(memory_space=pltpu.SMEM)],
        out_specs=pl.BlockSpec((3, 1, TQ, span), lambda h: (0, h, 0, 0)),
        out_shape=jax.ShapeDtypeStruct((3, NA_HEADS, TQ, span), F32),
        compiler_params=_params("parallel"),
        name="na_bias",
    )(rpb.reshape(-1))


def _post_kernel(x_ref, oa_ref, ob_ref, wo_ref, g1_ref, n2_ref, sh_ref, sc_ref, g2_ref, w1_ref, w2_ref, o_ref):
    half = oa_ref.shape[1]
    y = (jnp.dot(oa_ref[...], wo_ref[:half, :], preferred_element_type=F32)
         + jnp.dot(ob_ref[...], wo_ref[half:, :], preferred_element_type=F32))
    x1 = x_ref[...] + g1_ref[0] * y
    ms = jnp.mean(x1 * x1, axis=-1, keepdims=True)
    h = (x1 * lax.rsqrt(ms + EPS) * n2_ref[...]) * (1.0 + sc_ref[0]) + sh_ref[0]
    a = jnp.dot(h.astype(BF16), w1_ref[...], preferred_element_type=F32)
    a = jnp.square(jnp.maximum(a, 0.0)).astype(BF16)
    o_ref[...] = x1 + g2_ref[0] * jnp.dot(a, w2_ref[...], preferred_element_type=F32)


def _post(xa, oa, ob, wo, g1, n2, sh, sc, g2, w1, w2, n_tokens, bpb, nbatch):
    d = xa.shape[1]
    half = oa.shape[1]
    row = lambda i: (jnp.minimum(i // bpb, nbatch), 0, 0)
    mod_spec = pl.BlockSpec((1, 1, d), row)
    resident = lambda a: pl.BlockSpec(a.shape, lambda i: (0, 0), pipeline_mode=pl.Buffered(1))
    return pl.pallas_call(
        _post_kernel,
        grid=(n_tokens // TM,),
        in_specs=[
            pl.BlockSpec((TM, d), lambda i: (i, 0)),
            pl.BlockSpec((TM, half), lambda i: (i, 0)),
            pl.BlockSpec((TM, half), lambda i: (i, 0)),
            resident(wo), mod_spec, pl.BlockSpec((1, d), lambda i: (0, 0)), mod_spec, mod_spec, mod_spec,
            resident(w1), resident(w2),
        ],
        out_specs=pl.BlockSpec((TM, d), lambda i: (i, 0)),
        out_shape=jax.ShapeDtypeStruct((n_tokens, d), F32),
        compiler_params=_params("parallel"),
        name="post",
    )(xa, oa, ob, wo, g1, n2, sh, sc, g2, w1, w2)


def _block_diag_ones(group):
    idx = np.arange(256) // group
    return jnp.asarray(idx[:, None] == idx[None, :], dtype=BF16)


def _rope_tables(seq, rot_dim, lane0, pad_rows):
    t = jnp.arange(seq, dtype=jnp.int32)
    row = (t // GRID_W).astype(F32)
    col = (t % GRID_W).astype(F32)
    n_freq = rot_dim // 4
    freqs = jnp.power(ROPE_BASE, -jnp.arange(n_freq, dtype=F32) / n_freq)
    ar = row[:, None] * freqs[None, :]
    ac = col[:, None] * freqs[None, :]
    ang = jnp.concatenate([ar, ar, ac, ac], axis=-1)
    sign = jnp.asarray(np.tile(np.repeat([-1.0, 1.0], n_freq), 2), F32)
    cos, sin = jnp.cos(ang), jnp.sin(ang) * sign[None, :]
    if lane0 is None:
        reps = LANES // rot_dim
        cos, sin = jnp.tile(cos, (1, reps)), jnp.tile(sin, (1, reps))
    else:
        pad = ((0, 0), (lane0, LANES - lane0 - rot_dim))
        cos = jnp.pad(cos, pad, constant_values=1.0)
        sin = jnp.pad(sin, pad)
    cos = jnp.concatenate([cos, jnp.ones((pad_rows, LANES), F32)], axis=0)
    sin = jnp.concatenate([sin, jnp.zeros((pad_rows, LANES), F32)], axis=0)
    return cos, sin


def _logits_bounded(q_gain, k_gain, dim, bias=None, sink=None):
    bound = jnp.max(jnp.abs(q_gain)) * jnp.max(jnp.abs(k_gain)) * (dim ** 0.5) * 1.02
    if bias is not None:
        bound = bound + jnp.max(jnp.abs(bias))
    if sink is not None:
        bound = jnp.maximum(bound, jnp.max(jnp.abs(sink)))
    return (bound <= MAX_FREE_LOGIT_BOUND).astype(jnp.int32).reshape(1)


def _tile_gain(g, reps, scale=1.0):
    return (jnp.tile(g.astype(F32), reps) * scale)[None, :]


def _even_weights(w, q_g, k_g, sq_g, sk_g):
    nq = DIFF_HEADS * 2 * HEAD_DIM
    qa, ka, va = w[:, :nq], w[:, nq:2 * nq], w[:, 2 * nq:3 * nq]
    o = 3 * nq
    qb = w[:, o:o + SWA_Q_HEADS * HEAD_DIM]
    o += SWA_Q_HEADS * HEAD_DIM
    kb = [w[:, o + i * HEAD_DIM:o + (i + 1) * HEAD_DIM] for i in range(SWA_KV_HEADS)]
    o += SWA_KV_HEADS * HEAD_DIM
    vb = [w[:, o + i * HEAD_DIM:o + (i + 1) * HEAD_DIM] for i in range(SWA_KV_HEADS)]
    dup = lambda parts: [p for p in parts for _ in range(2)]
    w_new = jnp.concatenate([qa, ka, va, qb] + dup(kb) + dup(vb), axis=1).astype(BF16)
    scale = HEAD_DIM ** -0.5 * LOG2E
    ones = lambda n: jnp.ones((1, n), F32)
    gain = jnp.concatenate([
        _tile_gain(q_g, 2 * DIFF_HEADS, scale), _tile_gain(k_g, 2 * DIFF_HEADS), ones(nq),
        _tile_gain(sq_g, SWA_Q_HEADS, scale), _tile_gain(sk_g, 2 * SWA_KV_HEADS),
        ones(2 * SWA_KV_HEADS * HEAD_DIM)], axis=1)
    return w_new, gain


def _odd_weights(w, wq_up, wkv_up, qa_g, kva_g, mq_g, mk_g, nq_g, nk_g):
    d = w.shape[0]
    o = 0
    parts = []
    for n in (MLA_Q_LORA, MLA_KV_LORA, MLA_ROPE, NA_HEADS * HEAD_DIM, NA_HEADS * HEAD_DIM, NA_HEADS * HEAD_DIM):
        parts.append(w[:, o:o + n])
        o += n
    q_a, kv_a, k_r, nq, nk, nv = parts
    slot = jnp.concatenate([jnp.zeros((d, MLA_NOPE), w.dtype), k_r,
                            jnp.zeros((d, LANES - MLA_QK), w.dtype)], axis=1)
    w_new = jnp.concatenate([q_a, kv_a, nq, nk, nv, slot], axis=1).astype(BF16)
    wq = jnp.pad(wq_up.reshape(MLA_Q_LORA, MLA_HEADS, MLA_QK), ((0, 0), (0, 0), (0, MLA_PAD - MLA_QK)))
    wq = wq.reshape(MLA_Q_LORA, MLA_HEADS * MLA_PAD).astype(BF16)
    wkv = wkv_up.reshape(MLA_KV_LORA, MLA_HEADS, MLA_NOPE + MLA_V)
    wk = jnp.pad(wkv[..., :MLA_NOPE], ((0, 0), (0, 0), (0, MLA_PAD - MLA_NOPE))).reshape(MLA_KV_LORA, -1)
    wv = wkv[..., MLA_NOPE:].reshape(MLA_KV_LORA, MLA_HEADS * MLA_V)
    wkv_new = jnp.concatenate([wk, wv], axis=1).astype(BF16)
    pad_gain = lambda g, scale: _tile_gain(jnp.pad(g.astype(F32), (0, MLA_PAD - MLA_QK)), 2, scale)
    gains = (qa_g.astype(F32)[None, :], kva_g.astype(F32)[None, :],
             pad_gain(mq_g, MLA_QK ** -0.5 * LOG2E), pad_gain(mk_g, 1.0),
             _tile_gain(nq_g, 4, HEAD_DIM ** -0.5 * LOG2E), _tile_gain(nk_g, 4))
    return w_new, wq, wkv_new, gains


def kernel(x, c, ctx, c_ctx, ada_w, ada_b, norm1_g, norm2_g, w_out, mlp_w1, mlp_w2, ev_w_in, diff_q_g,
           diff_k_g, diff_lam, diff_subln_g, swa_q_g, swa_k_g, swa_sink, od_w_in, mla_qa_g, mla_kva_g,
           mla_wq_up, mla_wkv_up, mla_q_g, mla_k_g, na_q_g, na_k_g, na_rpb):
    nbatch, seq, d = x.shape
    ctx_len = ctx.shape[1]
    depth = ada_w.shape[0]
    n_lat = nbatch * seq
    n_ctx = nbatch * ctx_len
    assert seq % TM == 0 and n_ctx == TM and seq % TQ == 0 and seq % GRID_W == 0
    assert seq // GRID_W >= 2 * NA_ROWS + TQ // GRID_W and seq >= TQ + 2 * WINDOW
    bpb = seq // TM

    mod_rows = 8
    cvec = jnp.concatenate([c, c_ctx[None, :], jnp.zeros((mod_rows - nbatch - 1, d), F32)], axis=0)
    mod = _modulation(cvec, ada_w, ada_b)

    cos64, sin64 = _rope_tables(seq, HEAD_DIM, None, n_ctx)
    cos32, sin32 = _rope_tables(seq, MLA_ROPE, MLA_NOPE, n_ctx)
    g64 = _block_diag_ones(HEAD_DIM)
    g128 = _block_diag_ones(LANES)

    xa = jnp.concatenate([x.reshape(n_lat, d), ctx.reshape(n_ctx, d)], axis=0)
    common = dict(nbatch=nbatch, seq=seq, ctx_len=ctx_len)

    for layer in range(depth):
        need_ctx = layer < depth - 1
        n_tok = n_lat + n_ctx if need_ctx else n_lat
        m = mod[layer, :nbatch + 1]
        sh1, sc1, g1, sh2, sc2, g2 = [m[:, k * d:(k + 1) * d].reshape(nbatch + 1, 1, d) for k in range(6)]
        n1 = norm1_g[layer][None, :]
        n2 = norm2_g[layer][None, :]
        i = layer // 2
        if layer % 2 == 0:
            lam_init = 0.8 - 0.6 * math.exp(-0.3 * layer)
            w_in, gain = _even_weights(ev_w_in[i], diff_q_g[i], diff_k_g[i], swa_q_g[i], swa_k_g[i])
            p = _proj_even(xa, n1, sh1, sc1, w_in, gain, g64, cos64, sin64, bpb, nbatch)
            lam = diff_lam[i].astype(F32)
            subln = diff_subln_g[i].astype(F32)[None, :]
            sink = swa_sink[i].astype(F32)
            diff_kw = dict(mode="diff", split="mask", n_pairs=DIFF_HEADS, q_blk=lambda j: j,
                           k_blk=lambda j: 4 + j, v_blk=lambda j: 8 + j, out_cols=DIFF_HEADS * LANES,
                           lam=lam, subln=subln, lam_init=lam_init,
                           safe=_logits_bounded(diff_q_g[i], diff_k_g[i], HEAD_DIM), **common)
            swa_kw = dict(mode="pair", split="mask", n_pairs=SWA_Q_HEADS // 2, q_blk=lambda j: 12 + j,
                          k_blk=lambda j: 16 + j // 2, v_blk=lambda j: 18 + j // 2,
                          out_cols=SWA_Q_HEADS * HEAD_DIM, sink=sink,
                          safe=_logits_bounded(swa_q_g[i], swa_k_g[i], HEAD_DIM, sink=sink), **common)
            oa = _attention(p, name="diff_attn", main="full", ctx_queries=False, **diff_kw)
            ob = _attention(p, name="window_attn", main="window", ctx_queries=False, **swa_kw)
            if need_ctx:
                oa_c = _attention(p, name="diff_attn_ctx", main=None, ctx_queries=True, **diff_kw)
                ob_c = _attention(p, name="window_attn_ctx", main=None, ctx_queries=True, **swa_kw)
        else:
            w_in, wq, wkv, gains = _odd_weights(od_w_in[i], mla_wq_up[i], mla_wkv_up[i], mla_qa_g[i],
                                                mla_kva_g[i], mla_q_g[i], mla_k_g[i], na_q_g[i], na_k_g[i])
            p = _proj_odd(xa, n1, sh1, sc1, w_in, wq, wkv, gains, g64, g128, cos32, sin32, bpb, nbatch)
            bias = _na_bias(na_rpb[i].astype(F32), seq)
            mla_kw = dict(mode="pair", split="slice", n_pairs=MLA_HEADS // 2, q_blk=lambda j: j,
                          k_blk=lambda j: 4 + j, v_blk=lambda j: 16 + j, out_cols=MLA_HEADS * MLA_V,
                          safe=_logits_bounded(mla_q_g[i], mla_k_g[i], MLA_QK), **common)
            na_kw = dict(mode="pair", split="mask", n_pairs=NA_HEADS // 2, q_blk=lambda j: 20 + j,
                         k_blk=lambda j: 24 + j, v_blk=lambda j: 28 + j, out_cols=NA_HEADS * HEAD_DIM,
                         safe=_logits_bounded(na_q_g[i], na_k_g[i], HEAD_DIM, bias=na_rpb[i]), **common)
            oa = _attention(p, name="mla_attn", main="full", ctx_queries=False, **mla_kw)
            ob = _attention(p, name="na_attn", main="na", ctx_queries=False, bias=bias, **na_kw)
            if need_ctx:
                oa_c = _attention(p, name="mla_attn_ctx", main=None, ctx_queries=True, **mla_kw)
                ob_c = _attention(p, name="na_attn_ctx", main=None, ctx_queries=True, **na_kw)
        if need_ctx:
            oa = jnp.concatenate([oa, oa_c], axis=0)
            ob = jnp.concatenate([ob, ob_c], axis=0)
        xa = _post(xa, oa, ob, w_out[layer].astype(BF16), g1, n2, sh2, sc2, g2,
                   mlp_w1[layer].astype(BF16), mlp_w2[layer].astype(BF16), n_tok, bpb, nbatch)
    return xa.reshape(nbatch, seq, d)
```

```python
import functools
import math

import numpy as np
import jax
import jax.numpy as jnp
from jax import lax
from jax.experimental import pallas as pl
from jax.experimental.pallas import tpu as pltpu

F32 = jnp.float32
BF16 = jnp.bfloat16

LANES = 128
HALF = LANES // 2
GRID_W = 64
HEAD_DIM = 64
EPS = 1e-6
ROPE_BASE = 10000.0
DIFF_HEADS = 4
SWA_Q_HEADS = 8
SWA_KV_HEADS = 2
WINDOW = 128
MLA_HEADS = 8
MLA_Q_LORA = 512
MLA_KV_LORA = 256
MLA_NOPE = 64
MLA_ROPE = 32
MLA_QK = MLA_NOPE + MLA_ROPE
MLA_V = 64
NA_HEADS = 8
NA_ROWS = 8
NA_COLS = 16
NEG = -1e30
LOG2E = math.log2(math.e)
MAX_FREE_LOGIT_BOUND = 50.0

TM = 512
TQ = 512
TQ_FULL = 1024
TK_FULL = 512
TK_MAX_FREE = 4096
ONES_ROWS = 16
MOD_TN = 1536
VMEM_LIMIT = 56 * 1024 * 1024


def _params(*sem):
    return pltpu.CompilerParams(dimension_semantics=sem, vmem_limit_bytes=VMEM_LIMIT)


def _mod_kernel(c_ref, w_ref, b_ref, o_ref):
    c = c_ref[...]
    a = (c * jax.nn.sigmoid(c)).astype(BF16)
    w = w_ref[0].astype(BF16)
    o_ref[0] = jnp.dot(a, w, preferred_element_type=F32) + b_ref[0]


def _modulation(cvec, ada_w, ada_b):
    depth, d, n = ada_w.shape
    rows = cvec.shape[0]
    return pl.pallas_call(
        _mod_kernel,
        grid=(depth, n // MOD_TN),
        in_specs=[
            pl.BlockSpec((rows, d), lambda l, j: (0, 0)),
            pl.BlockSpec((1, d, MOD_TN), lambda l, j: (l, 0, j)),
            pl.BlockSpec((1, 1, MOD_TN), lambda l, j: (l, 0, j)),
        ],
        out_specs=pl.BlockSpec((1, rows, MOD_TN), lambda l, j: (l, 0, j)),
        out_shape=jax.ShapeDtypeStruct((depth, rows, n), F32),
        compiler_params=_params("parallel", "parallel"),
        name="modulation",
    )(cvec, ada_w, ada_b.reshape(depth, 1, n))


def _modulated(x, g_ref, sh_ref, sc_ref):
    ms = jnp.mean(x * x, axis=-1, keepdims=True)
    y = x * lax.rsqrt(ms + EPS) * g_ref[...]
    return y * (1.0 + sc_ref[0]) + sh_ref[0]


def _group_meansq(chunks, g_ref, inv_n):
    rows = chunks[0].shape[0]
    sq = jnp.concatenate([(c * c).astype(BF16) for c in chunks], axis=0)
    ss = jnp.dot(sq, g_ref[...], preferred_element_type=F32) * inv_n
    return [ss[i * rows:(i + 1) * rows] for i in range(len(chunks))]


def _swap_quarters(x, quarter):
    lane = lax.broadcasted_iota(jnp.int32, x.shape, 1)
    odd = (lane & quarter) != 0
    from_lower = pltpu.roll(x, quarter, 1)
    from_upper = pltpu.roll(x, LANES - quarter, 1)
    return jnp.where(odd, from_lower, from_upper)


def _rope(x, cos, sin_signed, quarter):
    return x * cos + _swap_quarters(x, quarter) * sin_signed


def _norm_rope_chunk(acc, meansq, gain, cos, sin, quarter):
    y = acc * lax.rsqrt(meansq + EPS) * gain
    if quarter is None:
        return y
    return jnp.concatenate([_rope(y[:, :LANES], cos, sin, quarter),
                            _rope(y[:, LANES:], cos, sin, quarter)], axis=1)


def _stream_specs(src, cols, n_lat_blocks):
    if len(src) == 1:
        return [pl.BlockSpec((TM, cols), lambda i: (i, 0))]
    return [pl.BlockSpec((TM, cols), lambda i: (jnp.minimum(i, n_lat_blocks - 1), 0)),
            pl.BlockSpec((TM, cols), lambda i: (0, 0))]


def _stream_block(refs, n_lat_blocks):
    if len(refs) == 1:
        return refs[0][...]
    return jnp.where(pl.program_id(0) < n_lat_blocks, refs[0][...], refs[1][...])


def _mod_specs(d, bpb, nbatch):
    row = lambda i: (jnp.minimum(i // bpb, nbatch), 0, 0)
    return pl.BlockSpec((1, d), lambda i: (0, 0)), pl.BlockSpec((1, 1, d), row)


def _rope_spec(bpb, nbatch):
    return pl.BlockSpec((TM, LANES), lambda i: (jnp.where(i < bpb * nbatch, i % bpb, bpb), 0))


def _whole(shape):
    return pl.BlockSpec(shape, lambda *_: (0,) * len(shape))


EVEN_COLS = 2560
EVEN_NORM_CHUNKS = (0, 1, 2, 3, 6, 7, 8)
EVEN_DOT_RANGES = ((0, 4), (6, 9), (4, 6), (9, 10))


def _proj_even_kernel(*refs, n_x, n_lat_blocks):
    x_refs = refs[:n_x]
    g_ref, sh_ref, sc_ref, w_ref, gain_ref, g64_ref, cos_ref, sin_ref, o_ref = refs[n_x:]
    h = _modulated(_stream_block(x_refs, n_lat_blocks), g_ref, sh_ref, sc_ref).astype(BF16)
    cos = cos_ref[...]
    sin = sin_ref[...]
    for first, last in EVEN_DOT_RANGES:
        full = jnp.dot(h, w_ref[:, first * 256:last * 256], preferred_element_type=F32)
        chunk = lambda c: full[:, (c - first) * 256:(c - first + 1) * 256]
        normed = [c for c in range(first, last) if c in EVEN_NORM_CHUNKS]
        meansq = {}
        if normed:
            meansq = dict(zip(normed, _group_meansq([chunk(c) for c in normed], g64_ref, 1.0 / HEAD_DIM)))
        for c in range(first, last):
            cols = slice(c * 256, (c + 1) * 256)
            acc = chunk(c)
            if c in EVEN_NORM_CHUNKS:
                acc = _norm_rope_chunk(acc, meansq[c], gain_ref[:, cols], cos, sin, HEAD_DIM // 4)
            o_ref[:, cols] = acc.astype(BF16)


def _proj_even(x_src, g, sh, sc, w, gain, g64, cos, sin, bpb, nbatch):
    d = x_src[0].shape[1]
    n_lat_blocks = bpb * nbatch
    gain_spec, mod_spec = _mod_specs(d, bpb, nbatch)
    kern = functools.partial(_proj_even_kernel, n_x=len(x_src), n_lat_blocks=n_lat_blocks)
    return pl.pallas_call(
        kern,
        grid=(n_lat_blocks + 1,),
        in_specs=_stream_specs(x_src, d, n_lat_blocks) + [
            gain_spec, mod_spec, mod_spec, _whole(w.shape), _whole(gain.shape), _whole(g64.shape),
            _rope_spec(bpb, nbatch), _rope_spec(bpb, nbatch),
        ],
        out_specs=pl.BlockSpec((TM, EVEN_COLS), lambda i: (i, 0)),
        out_shape=jax.ShapeDtypeStruct(((n_lat_blocks + 1) * TM, EVEN_COLS), BF16),
        compiler_params=_params("parallel"),
        name="proj_even",
    )(*x_src, g, sh, sc, w, gain, g64, cos, sin)


ODD_W_COLS = 2432
ODD_COLS = 4096
MLA_PAD = LANES


def _proj_odd_kernel(x_ref, g_ref, sh_ref, sc_ref, w_ref, wq_ref, wkv_ref, qa_g_ref, kva_g_ref,
                     mq_g_ref, mk_g_ref, nq_g_ref, nk_g_ref, g64_ref, g128_ref, cos_ref, sin_ref, o_ref):
    h = _modulated(x_ref[...], g_ref, sh_ref, sc_ref).astype(BF16)
    cos = cos_ref[...]
    sin = sin_ref[...]
    quarter = MLA_ROPE // 4

    full = jnp.dot(h, w_ref[...], preferred_element_type=F32)

    def low_rank(a, gain_ref):
        ms = jnp.mean(a * a, axis=-1, keepdims=True)
        return (a * lax.rsqrt(ms + EPS) * gain_ref[...]).astype(BF16)

    n_mla = MLA_HEADS * MLA_PAD // 256
    qa = low_rank(full[:, :MLA_Q_LORA], qa_g_ref)
    qf = jnp.dot(qa, wq_ref[...], preferred_element_type=F32)
    q_chunks = [qf[:, c * 256:(c + 1) * 256] for c in range(n_mla)]
    ms_q = _group_meansq(q_chunks, g128_ref, 1.0 / MLA_QK)
    kva = low_rank(full[:, MLA_Q_LORA:MLA_Q_LORA + MLA_KV_LORA], kva_g_ref)
    kvf = jnp.dot(kva, wkv_ref[...], preferred_element_type=F32)
    kslot = full[:, 2304:2432]
    kslot2 = jnp.concatenate([kslot, kslot], axis=1)
    k_rot = _swap_quarters(kslot * mk_g_ref[:, :LANES], quarter)
    k_rot2 = jnp.concatenate([k_rot, k_rot], axis=1)
    cos2 = jnp.concatenate([cos, cos], axis=1)
    sin2 = jnp.concatenate([sin, sin], axis=1)
    k_chunks = [kvf[:, c * 256:(c + 1) * 256] + kslot2 for c in range(n_mla)]
    ms_k = _group_meansq(k_chunks, g128_ref, 1.0 / MLA_QK)
    for c in range(n_mla):
        y = _norm_rope_chunk(q_chunks[c], ms_q[c], mq_g_ref[...], cos, sin, quarter)
        o_ref[:, c * 256:(c + 1) * 256] = y.astype(BF16)
    for c in range(n_mla):
        inv_rms = lax.rsqrt(ms_k[c] + EPS)
        y = (k_chunks[c] * inv_rms * mk_g_ref[...]) * cos2 + (inv_rms * k_rot2) * sin2
        o_ref[:, 1024 + c * 256:1024 + (c + 1) * 256] = y.astype(BF16)
    o_ref[:, 2048:2560] = kvf[:, 1024:1536].astype(BF16)

    n_chunks = [full[:, 768 + c * 256:768 + (c + 1) * 256] for c in range(4)]
    ms64 = _group_meansq(n_chunks, g64_ref, 1.0 / HEAD_DIM)
    for c in range(4):
        gain = nq_g_ref[...] if c < 2 else nk_g_ref[...]
        y = _norm_rope_chunk(n_chunks[c], ms64[c], gain, None, None, None)
        o_ref[:, 2560 + c * 256:2560 + (c + 1) * 256] = y.astype(BF16)
    o_ref[:, 3584:4096] = full[:, 1792:2304].astype(BF16)


def _proj_odd(xa, g, sh, sc, w, wq, wkv, gains, g64, g128, cos, sin, bpb, nbatch):
    na, d = xa.shape
    gain_spec, mod_spec = _mod_specs(d, bpb, nbatch)
    tok_specs = _stream_specs((xa,), d, bpb * nbatch) + [gain_spec, mod_spec, mod_spec]
    consts = (w, wq, wkv) + tuple(gains) + (g64, g128)
    return pl.pallas_call(
        _proj_odd_kernel,
        grid=(na // TM,),
        in_specs=tok_specs + [_whole(a.shape) for a in consts] + [_rope_spec(bpb, nbatch)] * 2,
        out_specs=pl.BlockSpec((TM, ODD_COLS), lambda i: (i, 0)),
        out_shape=jax.ShapeDtypeStruct((na, ODD_COLS), BF16),
        compiler_params=_params("parallel"),
        name="proj_odd",
    )(xa, g, sh, sc, *consts, cos, sin)


def _attn_kernel(*refs, mode, split, main, seq, has_sink, lam_init, tq):
    it = iter(refs)
    q_ref = next(it)
    k_ref = v_ref = bias_ref = sink_ref = lam_ref = subln_ref = None
    if main is not None:
        k_ref, v_ref = next(it), next(it)
    kc_ref, vc_ref = next(it), next(it)
    if main == "na":
        bias_ref = next(it)
    if has_sink:
        sink_ref = next(it)
    safe_ref = next(it)
    if mode == "diff":
        lam_ref, subln_ref = next(it), next(it)
    o_ref = next(it)
    vt_ref, vct_ref = (next(it), next(it)) if main == "full" else (None, None)

    pair = pl.program_id(1)
    qi = pl.program_id(2)
    q = q_ref[...]
    low = lax.broadcasted_iota(jnp.int32, (1, LANES), 1) < HALF
    if split == "mask":
        zero = jnp.zeros_like(q)
        qs = (jnp.where(low, q, zero), jnp.where(low, zero, q))
        k_of = lambda k, s: k
    else:
        qs = (q[:, :LANES], q[:, LANES:])
        k_of = lambda k, s: k[:, s * LANES:(s + 1) * LANES]
    sinks = [sink_ref[2 * pair + s] * LOG2E for s in range(2)] if has_sink else None

    def logits_of(s, k):
        return lax.dot_general(qs[s], k_of(k, s), (((1,), (1,)), ((), ())), preferred_element_type=F32)

    def finish(o_lo, o_hi):
        if mode == "diff":
            lv = lam_ref[...]
            lam = (jnp.exp(jnp.sum(lv[0:1] * lv[1:2], axis=-1, keepdims=True))
                   - jnp.exp(jnp.sum(lv[2:3] * lv[3:4], axis=-1, keepdims=True)) + lam_init)
            o = o_lo - lam * o_hi
            ms = jnp.mean(o * o, axis=-1, keepdims=True)
            o = o * lax.rsqrt(ms + EPS) * subln_ref[...] * (1.0 - lam_init)
        else:
            o = jnp.where(low, o_lo, o_hi)
        o_ref[...] = o.astype(BF16)

    def init(s):
        if has_sink:
            m0 = jnp.full((tq, 1), sinks[s], F32)
            l0 = jnp.ones((tq, 1), F32)
        else:
            m0 = jnp.full((tq, 1), NEG, F32)
            l0 = jnp.zeros((tq, 1), F32)
        return m0, l0, jnp.zeros((tq, LANES), F32)

    def segment(state, k, v, logit_fn):
        out = []
        for s in range(2):
            m, l, acc = state[s]
            logits = logits_of(s, k)
            if logit_fn is not None:
                logits = logit_fn(logits, s)
            m_new = jnp.maximum(m, jnp.max(logits, axis=-1, keepdims=True))
            alpha = jnp.exp2(m - m_new)
            p = jnp.exp2(logits - m_new)
            l = alpha * l + jnp.sum(p, axis=-1, keepdims=True)
            acc = alpha * acc + jnp.dot(p.astype(BF16), v, preferred_element_type=F32)
            out.append((m_new, l, acc))
        return tuple(out)

    def all_keys(step, state, tk):
        if main == "full":
            def body(c, st):
                off = pl.multiple_of(c * tk, tk)
                return step(st, k_ref[pl.ds(off, tk), :], v_ref[pl.ds(off, tk), :], None)
            state = lax.fori_loop(0, seq // tk, body, state)
        elif main == "window":
            span = tq + 2 * WINDOW
            q0 = qi * tq
            w0 = jnp.clip(q0 - WINDOW, 0, seq - span)
            off = pl.multiple_of(w0, WINDOW)
            delta = w0 - q0
            row = lax.broadcasted_iota(jnp.int32, (tq, span), 0)
            col = lax.broadcasted_iota(jnp.int32, (tq, span), 1)
            inside = jnp.abs(col - row + delta) <= WINDOW
            state = step(state, k_ref[pl.ds(off, span), :], v_ref[pl.ds(off, span), :],
                         lambda lg, s: jnp.where(inside, lg, NEG))
        elif main == "na":
            rows_q = tq // GRID_W
            span_rows = 2 * NA_ROWS
            w0 = jnp.clip(qi * rows_q - NA_ROWS // 2, 0, seq // GRID_W - span_rows)
            off = pl.multiple_of(w0 * GRID_W, GRID_W)
            span = span_rows * GRID_W
            state = step(state, k_ref[pl.ds(off, span), :], v_ref[pl.ds(off, span), :],
                         lambda lg, s: lg + bias_ref[0, s])
        return step(state, kc_ref[...], vc_ref[...], None)

    def online():
        (_, l_lo, acc_lo), (_, l_hi, acc_hi) = all_keys(segment, (init(0), init(1)), TK_FULL)
        finish(acc_lo / l_lo, acc_hi / l_hi)

    def accumulate(acc, k, v, logit_fn):
        v_ext = jnp.concatenate([v, jnp.ones_like(v)], axis=1)
        out = []
        for s in range(2):
            logits = logits_of(s, k)
            if logit_fn is not None:
                logits = logit_fn(logits, s)
            p = jnp.exp2(logits).astype(BF16)
            out.append(acc[s] + jnp.dot(p, v_ext, preferred_element_type=F32))
        return tuple(out)

    def max_free():
        zero_acc = jnp.zeros((tq, 2 * LANES), F32)
        acc = all_keys(accumulate, (zero_acc, zero_acc), TK_MAX_FREE)
        outs = []
        for s in range(2):
            den = acc[s][:, LANES:]
            if has_sink:
                den = den + jnp.exp2(jnp.full((1, LANES), sinks[s], F32))
            outs.append(acc[s][:, :LANES] / den)
        finish(*outs)

    def max_free_transposed():
        n_chunks = seq // TK_MAX_FREE
        ctx_len = vc_ref.shape[0]

        @pl.when(qi == 0)
        def _():
            ones = jnp.ones((ONES_ROWS, TK_MAX_FREE), BF16)
            for c in range(n_chunks):
                vt_ref[c, :LANES, :] = v_ref[c * TK_MAX_FREE:(c + 1) * TK_MAX_FREE, :].T
                vt_ref[c, LANES:, :] = ones
            vct_ref[:LANES, :] = vc_ref[...].T
            vct_ref[LANES:, :] = ones[:, :ctx_len]

        def accumulate_t(acc, k, vt):
            out = []
            for s in range(2):
                logits_t = lax.dot_general(k_of(k, s), qs[s], (((1,), (1,)), ((), ())),
                                           preferred_element_type=F32)
                p_t = jnp.exp2(logits_t).astype(BF16)
                out.append(acc[s] + jnp.dot(vt, p_t, preferred_element_type=F32))
            return tuple(out)

        def body(c, acc):
            off = pl.multiple_of(c * TK_MAX_FREE, TK_MAX_FREE)
            return accumulate_t(acc, k_ref[pl.ds(off, TK_MAX_FREE), :], vt_ref[c])

        zero_acc = jnp.zeros((LANES + ONES_ROWS, tq), F32)
        acc = lax.fori_loop(0, n_chunks, body, (zero_acc, zero_acc))
        acc = accumulate_t(acc, kc_ref[...], vct_ref[...])
        finish(*[(acc[s][:LANES] / acc[s][LANES:LANES + 1]).T for s in range(2)])

    safe = safe_ref[0] == 1
    pl.when(safe)(max_free_transposed if main == "full" else max_free)
    pl.when(jnp.logical_not(safe))(online)


def _attention(p, *, name, mode, split, main, n_pairs, q_blk, k_blk, v_blk, out_cols, ctx_queries,
               nbatch, seq, ctx_len, safe, bias=None, sink=None, lam=None, subln=None, lam_init=0.0):
    n_lat = nbatch * seq
    qw = LANES if split == "mask" else 2 * LANES
    if ctx_queries:
        tq, nq = ctx_len, 1
        q_row = lambda b, j, i: n_lat // ctx_len + b
        out_rows = nbatch * ctx_len
        out_row = lambda b, j, i: b
    else:
        tq = TQ_FULL if main == "full" else TQ
        nq = seq // tq
        q_row = lambda b, j, i: b * nq + i
        out_rows = n_lat
        out_row = q_row
    ctx_row = lambda b, j, i: n_lat // ctx_len + b

    in_specs = [pl.BlockSpec((tq, qw), lambda b, j, i: (q_row(b, j, i), q_blk(j)))]
    args = [p]
    if main is not None:
        in_specs += [pl.BlockSpec((seq, qw), lambda b, j, i: (b, k_blk(j))),
                     pl.BlockSpec((seq, LANES), lambda b, j, i: (b, v_blk(j)))]
        args += [p, p]
    in_specs += [pl.BlockSpec((ctx_len, qw), lambda b, j, i: (ctx_row(b, j, i), k_blk(j))),
                 pl.BlockSpec((ctx_len, LANES), lambda b, j, i: (ctx_row(b, j, i), v_blk(j)))]
    args += [p, p]
    if main == "na":
        span = 2 * NA_ROWS * GRID_W
        cls = lambda i: jnp.where(i == 0, 0, jnp.where(i == nq - 1, 2, 1))
        in_specs.append(pl.BlockSpec((1, 2, tq, span), lambda b, j, i: (cls(i), j, 0, 0)))
        args.append(bias)
    if sink is not None:
        in_specs.append(pl.BlockSpec(memory_space=pltpu.SMEM))
        args.append(sink)
    in_specs.append(pl.BlockSpec(memory_space=pltpu.SMEM))
    args.append(safe)
    if mode == "diff":
        in_specs += [_whole(lam.shape), _whole(subln.shape)]
        args += [lam, subln]

    kern = functools.partial(_attn_kernel, mode=mode, split=split, main=main, seq=seq,
                             has_sink=sink is not None, lam_init=lam_init, tq=tq)
    scratch = []
    if main == "full":
        scratch = [pltpu.VMEM((seq // TK_MAX_FREE, LANES + ONES_ROWS, TK_MAX_FREE), BF16),
                   pltpu.VMEM((LANES + ONES_ROWS, ctx_len), BF16)]
    return pl.pallas_call(
        kern,
        grid=(nbatch, n_pairs, nq),
        in_specs=in_specs,
        out_specs=pl.BlockSpec((tq, LANES), lambda b, j, i: (out_row(b, j, i), j)),
        out_shape=jax.ShapeDtypeStruct((out_rows, out_cols), BF16),
        scratch_shapes=scratch,
        compiler_params=_params("parallel", "parallel", "arbitrary"),
        name=name,
    )(*args)


def _na_bias_kernel(rpb_ref, o_ref, *, n_rows, rows_q):
    h = pl.program_id(0)
    n_dr = 2 * NA_ROWS - 1
    n_dc = 2 * NA_COLS - 1
    cq = lax.broadcasted_iota(jnp.int32, (GRID_W, LANES), 0)
    kc = lax.broadcasted_iota(jnp.int32, (GRID_W, LANES), 1) & (GRID_W - 1)
    dc_idx = jnp.clip(kc - cq, -(NA_COLS - 1), NA_COLS - 1) + NA_COLS - 1
    cs = jnp.clip(cq - NA_COLS // 2, 0, GRID_W - NA_COLS)
    col_ok = (kc >= cs) & (kc < cs + NA_COLS)
    tiles = [jnp.zeros((GRID_W, LANES), F32) for _ in range(n_dr)]
    for dc in range(n_dc):
        hit = dc_idx == dc
        for dr in range(n_dr):
            tiles[dr] = jnp.where(hit, rpb_ref[(h * n_dr + dr) * n_dc + dc] * LOG2E, tiles[dr])
    tiles = [jnp.where(col_ok, t, NEG) for t in tiles]
    masked = jnp.full((GRID_W, LANES), NEG, F32)
    low = lax.broadcasted_iota(jnp.int32, (GRID_W, LANES), 1) < HALF

    span_rows = 2 * NA_ROWS
    first_q_row = (0, span_rows, n_rows - rows_q)
    for c in range(3):
        r0 = first_q_row[c]
        w0 = min(max(r0 - NA_ROWS // 2, 0), n_rows - span_rows)
        for qr in range(rows_q):
            r = r0 + qr
            rs = min(max(r - NA_ROWS // 2, 0), n_rows - NA_ROWS)
            for m in range(span_rows // 2):
                halves = []
                for kr in (2 * m, 2 * m + 1):
                    k_abs = w0 + kr
                    halves.append(tiles[k_abs - r + NA_ROWS - 1] if rs <= k_abs < rs + NA_ROWS else masked)
                o_ref[c, 0, qr * GRID_W:(qr + 1) * GRID_W, m * LANES:(m + 1) * LANES] = (
                    jnp.where(low, halves[0], halves[1]))


def _na_bias(rpb, seq):
    n_rows = seq // GRID_W
    rows_q = TQ // GRID_W
    span = 2 * NA_ROWS * GRID_W
    kern = functools.partial(_na_bias_kernel, n_rows=n_rows, rows_q=rows_q)
    return pl.pallas_call(
        kern,
        grid=(NA_HEADS,),
        in_specs=[pl.BlockSpec(memory_space=pltpu.SMEM)],
        out_specs=pl.BlockSpec((3, 1, TQ, span), lambda h: (0, h, 0, 0)),
        out_shape=jax.ShapeDtypeStruct((3, NA_HEADS, TQ, span), F32),
        compiler_params=_params("parallel"),
        name="na_bias",
    )(rpb.reshape(-1))


def _post_kernel(*refs, n_x, n_o, n_lat_blocks):
    x_refs, oa_refs, ob_refs = refs[:n_x], refs[n_x:n_x + n_o], refs[n_x + n_o:n_x + 2 * n_o]
    wo_ref, g1_ref, n2_ref, sh_ref, sc_ref, g2_ref, w1_ref, w2_ref, o_ref = refs[n_x + 2 * n_o:]
    oa = _stream_block(oa_refs, n_lat_blocks)
    ob = _stream_block(ob_refs, n_lat_blocks)
    half = oa.shape[1]
    y = (jnp.dot(oa, wo_ref[:half, :], preferred_element_type=F32)
         + jnp.dot(ob, wo_ref[half:, :], preferred_element_type=F32))
    x1 = _stream_block(x_refs, n_lat_blocks) + g1_ref[0] * y
    h = _modulated(x1, n2_ref, sh_ref, sc_ref).astype(BF16)
    a = jnp.dot(h, w1_ref[...], preferred_element_type=F32)
    a = jnp.square(jnp.maximum(a, 0.0)).astype(BF16)
    o_ref[...] = x1 + g2_ref[0] * jnp.dot(a, w2_ref[...], preferred_element_type=F32)


def _post(x_src, oa_src, ob_src, wo, g1, n2, sh, sc, g2, w1, w2, layer, n_blocks, bpb, nbatch):
    d = x_src[0].shape[1]
    half = oa_src[0].shape[1]
    n_lat_blocks = bpb * nbatch
    gain_spec, mod_spec = _mod_specs(d, bpb, nbatch)
    resident = lambda a: pl.BlockSpec((None,) + a.shape[1:], lambda i: (layer, 0, 0),
                                      pipeline_mode=pl.Buffered(1))
    kern = functools.partial(_post_kernel, n_x=len(x_src), n_o=len(oa_src), n_lat_blocks=n_lat_blocks)
    return pl.pallas_call(
        kern,
        grid=(n_blocks,),
        in_specs=(_stream_specs(x_src, d, n_lat_blocks) + _stream_specs(oa_src, half, n_lat_blocks)
                  + _stream_specs(ob_src, half, n_lat_blocks)
                  + [resident(wo), mod_spec, gain_spec, mod_spec, mod_spec, mod_spec, resident(w1), resident(w2)]),
        out_specs=pl.BlockSpec((TM, d), lambda i: (i, 0)),
        out_shape=jax.ShapeDtypeStruct((n_blocks * TM, d), F32),
        compiler_params=_params("parallel"),
        name="post",
    )(*x_src, *oa_src, *ob_src, wo, g1, n2, sh, sc, g2, w1, w2)


def _block_diag_ones(group):
    idx = np.arange(256) // group
    return jnp.asarray(idx[:, None] == idx[None, :], dtype=BF16)


def _rope_tables(seq, rot_dim, lane0, pad_rows):
    t = jnp.arange(seq, dtype=jnp.int32)
    row = (t // GRID_W).astype(F32)
    col = (t % GRID_W).astype(F32)
    n_freq = rot_dim // 4
    freqs = jnp.power(ROPE_BASE, -jnp.arange(n_freq, dtype=F32) / n_freq)
    ar = row[:, None] * freqs[None, :]
    ac = col[:, None] * freqs[None, :]
    ang = jnp.concatenate([ar, ar, ac, ac], axis=-1)
    sign = jnp.asarray(np.tile(np.repeat([-1.0, 1.0], n_freq), 2), F32)
    cos, sin = jnp.cos(ang), jnp.sin(ang) * sign[None, :]
    if lane0 is None:
        reps = LANES // rot_dim
        cos, sin = jnp.tile(cos, (1, reps)), jnp.tile(sin, (1, reps))
    else:
        pad = ((0, 0), (lane0, LANES - lane0 - rot_dim))
        cos = jnp.pad(cos, pad, constant_values=1.0)
        sin = jnp.pad(sin, pad)
    cos = jnp.concatenate([cos, jnp.ones((pad_rows, LANES), F32)], axis=0)
    sin = jnp.concatenate([sin, jnp.zeros((pad_rows, LANES), F32)], axis=0)
    return cos, sin


def _logits_bounded(q_gain, k_gain, dim, bias=None, sink=None):
    bound = jnp.max(jnp.abs(q_gain)) * jnp.max(jnp.abs(k_gain)) * (dim ** 0.5) * 1.02
    if bias is not None:
        bound = bound + jnp.max(jnp.abs(bias))
    if sink is not None:
        bound = jnp.maximum(bound, jnp.max(jnp.abs(sink)))
    return (bound <= MAX_FREE_LOGIT_BOUND).astype(jnp.int32).reshape(1)


def _tile_gain(g, reps, scale=1.0):
    return (jnp.tile(g.astype(F32), reps) * scale)[None, :]


def _even_weights(w, q_g, k_g, sq_g, sk_g):
    nq = DIFF_HEADS * 2 * HEAD_DIM
    qa, ka, va = w[:, :nq], w[:, nq:2 * nq], w[:, 2 * nq:3 * nq]
    o = 3 * nq
    qb = w[:, o:o + SWA_Q_HEADS * HEAD_DIM]
    o += SWA_Q_HEADS * HEAD_DIM
    kb = [w[:, o + i * HEAD_DIM:o + (i + 1) * HEAD_DIM] for i in range(SWA_KV_HEADS)]
    o += SWA_KV_HEADS * HEAD_DIM
    vb = [w[:, o + i * HEAD_DIM:o + (i + 1) * HEAD_DIM] for i in range(SWA_KV_HEADS)]
    dup = lambda parts: [p for p in parts for _ in range(2)]
    w_new = jnp.concatenate([qa, ka, va, qb] + dup(kb) + dup(vb), axis=1).astype(BF16)
    scale = HEAD_DIM ** -0.5 * LOG2E
    ones = lambda n: jnp.ones((1, n), F32)
    gain = jnp.concatenate([
        _tile_gain(q_g, 2 * DIFF_HEADS, scale), _tile_gain(k_g, 2 * DIFF_HEADS), ones(nq),
        _tile_gain(sq_g, SWA_Q_HEADS, scale), _tile_gain(sk_g, 2 * SWA_KV_HEADS),
        ones(2 * SWA_KV_HEADS * HEAD_DIM)], axis=1)
    return w_new, gain


def _odd_weights(w, wq_up, wkv_up, qa_g, kva_g, mq_g, mk_g, nq_g, nk_g):
    d = w.shape[0]
    o = 0
    parts = []
    for n in (MLA_Q_LORA, MLA_KV_LORA, MLA_ROPE, NA_HEADS * HEAD_DIM, NA_HEADS * HEAD_DIM, NA_HEADS * HEAD_DIM):
        parts.append(w[:, o:o + n])
        o += n
    q_a, kv_a, k_r, nq, nk, nv = parts
    slot = jnp.concatenate([jnp.zeros((d, MLA_NOPE), w.dtype), k_r,
                            jnp.zeros((d, LANES - MLA_QK), w.dtype)], axis=1)
    w_new = jnp.concatenate([q_a, kv_a, nq, nk, nv, slot], axis=1).astype(BF16)
    wq = jnp.pad(wq_up.reshape(MLA_Q_LORA, MLA_HEADS, MLA_QK), ((0, 0), (0, 0), (0, MLA_PAD - MLA_QK)))
    wq = wq.reshape(MLA_Q_LORA, MLA_HEADS * MLA_PAD).astype(BF16)
    wkv = wkv_up.reshape(MLA_KV_LORA, MLA_HEADS, MLA_NOPE + MLA_V)
    wk = jnp.pad(wkv[..., :MLA_NOPE], ((0, 0), (0, 0), (0, MLA_PAD - MLA_NOPE))).reshape(MLA_KV_LORA, -1)
    wv = wkv[..., MLA_NOPE:].reshape(MLA_KV_LORA, MLA_HEADS * MLA_V)
    wkv_new = jnp.concatenate([wk, wv], axis=1).astype(BF16)
    pad_gain = lambda g, scale: _tile_gain(jnp.pad(g.astype(F32), (0, MLA_PAD - MLA_QK)), 2, scale)
    gains = (qa_g.astype(F32)[None, :], kva_g.astype(F32)[None, :],
             pad_gain(mq_g, MLA_QK ** -0.5 * LOG2E), pad_gain(mk_g, 1.0),
             _tile_gain(nq_g, 4, HEAD_DIM ** -0.5 * LOG2E), _tile_gain(nk_g, 4))
    return w_new, wq, wkv_new, gains


def kernel(x, c, ctx, c_ctx, ada_w, ada_b, norm1_g, norm2_g, w_out, mlp_w1, mlp_w2, ev_w_in, diff_q_g,
           diff_k_g, diff_lam, diff_subln_g, swa_q_g, swa_k_g, swa_sink, od_w_in, mla_qa_g, mla_kva_g,
           mla_wq_up, mla_wkv_up, mla_q_g, mla_k_g, na_q_g, na_k_g, na_rpb):
    nbatch, seq, d = x.shape
    ctx_len = ctx.shape[1]
    depth = ada_w.shape[0]
    n_lat = nbatch * seq
    n_ctx = nbatch * ctx_len
    assert seq % TM == 0 and n_ctx == TM and seq % TQ == 0 and seq % GRID_W == 0
    assert seq // GRID_W >= 2 * NA_ROWS + TQ // GRID_W and seq >= TQ + 2 * WINDOW
    bpb = seq // TM

    mod_rows = 8
    cvec = jnp.concatenate([c, c_ctx[None, :], jnp.zeros((mod_rows - nbatch - 1, d), F32)], axis=0)
    mod = _modulation(cvec, ada_w, ada_b)

    cos64, sin64 = _rope_tables(seq, HEAD_DIM, None, n_ctx)
    cos32, sin32 = _rope_tables(seq, MLA_ROPE, MLA_NOPE, n_ctx)
    g64 = _block_diag_ones(HEAD_DIM)
    g128 = _block_diag_ones(LANES)

    x_src = (x.reshape(n_lat, d), ctx.reshape(n_ctx, d))
    wo_all, w1_all, w2_all = w_out.astype(BF16), mlp_w1.astype(BF16), mlp_w2.astype(BF16)
    common = dict(nbatch=nbatch, seq=seq, ctx_len=ctx_len)

    for layer in range(depth):
        need_ctx = layer < depth - 1
        n_blocks = bpb * nbatch + (1 if need_ctx else 0)
        m = mod[layer, :nbatch + 1]
        sh1, sc1, g1, sh2, sc2, g2 = [m[:, k * d:(k + 1) * d].reshape(nbatch + 1, 1, d) for k in range(6)]
        n1 = norm1_g[layer][None, :]
        n2 = norm2_g[layer][None, :]
        i = layer // 2
        if layer % 2 == 0:
            lam_init = 0.8 - 0.6 * math.exp(-0.3 * layer)
            w_in, gain = _even_weights(ev_w_in[i], diff_q_g[i], diff_k_g[i], swa_q_g[i], swa_k_g[i])
            p = _proj_even(x_src, n1, sh1, sc1, w_in, gain, g64, cos64, sin64, bpb, nbatch)
            lam = diff_lam[i].astype(F32)
            subln = diff_subln_g[i].astype(F32)[None, :]
            sink = swa_sink[i].astype(F32)
            diff_kw = dict(mode="diff", split="mask", n_pairs=DIFF_HEADS, q_blk=lambda j: j,
                           k_blk=lambda j: 4 + j, v_blk=lambda j: 8 + j, out_cols=DIFF_HEADS * LANES,
                           lam=lam, subln=subln, lam_init=lam_init,
                           safe=_logits_bounded(diff_q_g[i], diff_k_g[i], HEAD_DIM), **common)
            swa_kw = dict(mode="pair", split="mask", n_pairs=SWA_Q_HEADS // 2, q_blk=lambda j: 12 + j,
                          k_blk=lambda j: 16 + j // 2, v_blk=lambda j: 18 + j // 2,
                          out_cols=SWA_Q_HEADS * HEAD_DIM, sink=sink,
                          safe=_logits_bounded(swa_q_g[i], swa_k_g[i], HEAD_DIM, sink=sink), **common)
            oa = _attention(p, name="diff_attn", main="full", ctx_queries=False, **diff_kw)
            ob = _attention(p, name="window_attn", main="window", ctx_queries=False, **swa_kw)
            if need_ctx:
                oa_c = _attention(p, name="diff_attn_ctx", main=None, ctx_queries=True, **diff_kw)
                ob_c = _attention(p, name="window_attn_ctx", main=None, ctx_queries=True, **swa_kw)
        else:
            w_in, wq, wkv, gains = _odd_weights(od_w_in[i], mla_wq_up[i], mla_wkv_up[i], mla_qa_g[i],
                                                mla_kva_g[i], mla_q_g[i], mla_k_g[i], na_q_g[i], na_k_g[i])
            p = _proj_odd(x_src[0], n1, sh1, sc1, w_in, wq, wkv, gains, g64, g128, cos32, sin32, bpb, nbatch)
            bias = _na_bias(na_rpb[i].astype(F32), seq)
            mla_kw = dict(mode="pair", split="slice", n_pairs=MLA_HEADS // 2, q_blk=lambda j: j,
                          k_blk=lambda j: 4 + j, v_blk=lambda j: 16 + j, out_cols=MLA_HEADS * MLA_V,
                          safe=_logits_bounded(mla_q_g[i], mla_k_g[i], MLA_QK), **common)
            na_kw = dict(mode="pair", split="mask", n_pairs=NA_HEADS // 2, q_blk=lambda j: 20 + j,
                         k_blk=lambda j: 24 + j, v_blk=lambda j: 28 + j, out_cols=NA_HEADS * HEAD_DIM,
                         safe=_logits_bounded(na_q_g[i], na_k_g[i], HEAD_DIM, bias=na_rpb[i]), **common)
            oa = _attention(p, name="mla_attn", main="full", ctx_queries=False, **mla_kw)
            ob = _attention(p, name="na_attn", main="na", ctx_queries=False, bias=bias, **na_kw)
            if need_ctx:
                oa_c = _attention(p, name="mla_attn_ctx", main=None, ctx_queries=True, **mla_kw)
                ob_c = _attention(p, name="na_attn_ctx", main=None, ctx_queries=True, **na_kw)
        oa_src, ob_src = ((oa, oa_c), (ob, ob_c)) if need_ctx else ((oa,), (ob,))
        x_src = (_post(x_src, oa_src, ob_src, wo_all, g1, n2, sh2, sc2, g2, w1_all, w2_all,
                       layer, n_blocks, bpb, nbatch),)
    return x_src[0].reshape(nbatch, seq, d)
```

```python
import functools
import math

import numpy as np
import jax
import jax.numpy as jnp
from jax import lax
from jax.experimental import pallas as pl
from jax.experimental.pallas import tpu as pltpu

F32 = jnp.float32
BF16 = jnp.bfloat16

LANES = 128
HALF = LANES // 2
GRID_W = 64
HEAD_DIM = 64
EPS = 1e-6
ROPE_BASE = 10000.0
DIFF_HEADS = 4
SWA_Q_HEADS = 8
SWA_KV_HEADS = 2
WINDOW = 128
MLA_HEADS = 8
MLA_Q_LORA = 512
MLA_KV_LORA = 256
MLA_NOPE = 64
MLA_ROPE = 32
MLA_QK = MLA_NOPE + MLA_ROPE
MLA_V = 64
NA_HEADS = 8
NA_ROWS = 8
NA_COLS = 16
NEG = -1e30
LOG2E = math.log2(math.e)
MAX_FREE_LOGIT_BOUND = 50.0

TM = 512
TQ = 1024
TQ_SUB = 256
TQ_FULL = 1024
TK_FULL = 512
TK_MAX_FREE = 4096
ONES_ROWS = 16
MOD_TN = 1536
VMEM_LIMIT = 56 * 1024 * 1024


def _params(*sem):
    return pltpu.CompilerParams(dimension_semantics=sem, vmem_limit_bytes=VMEM_LIMIT)


def _mod_kernel(c_ref, w_ref, b_ref, o_ref):
    c = c_ref[...]
    a = (c * jax.nn.sigmoid(c)).astype(BF16)
    w = w_ref[0].astype(BF16)
    o_ref[0] = jnp.dot(a, w, preferred_element_type=F32) + b_ref[0]


def _modulation(cvec, ada_w, ada_b):
    depth, d, n = ada_w.shape
    rows = cvec.shape[0]
    return pl.pallas_call(
        _mod_kernel,
        grid=(depth, n // MOD_TN),
        in_specs=[
            pl.BlockSpec((rows, d), lambda l, j: (0, 0)),
            pl.BlockSpec((1, d, MOD_TN), lambda l, j: (l, 0, j)),
            pl.BlockSpec((1, 1, MOD_TN), lambda l, j: (l, 0, j)),
        ],
        out_specs=pl.BlockSpec((1, rows, MOD_TN), lambda l, j: (l, 0, j)),
        out_shape=jax.ShapeDtypeStruct((depth, rows, n), F32),
        compiler_params=_params("parallel", "parallel"),
        name="modulation",
    )(cvec, ada_w, ada_b.reshape(depth, 1, n))


def _modulated(x, g_ref, sh_ref, sc_ref):
    ms = jnp.mean(x * x, axis=-1, keepdims=True)
    y = x * lax.rsqrt(ms + EPS) * g_ref[...]
    return y * (1.0 + sc_ref[0]) + sh_ref[0]


def _group_meansq(chunks, g_ref, inv_n):
    rows = chunks[0].shape[0]
    sq = jnp.concatenate([(c * c).astype(BF16) for c in chunks], axis=0)
    ss = jnp.dot(sq, g_ref[...], preferred_element_type=F32) * inv_n
    return [ss[i * rows:(i + 1) * rows] for i in range(len(chunks))]


def _swap_quarters(x, quarter):
    lane = lax.broadcasted_iota(jnp.int32, x.shape, 1)
    odd = (lane & quarter) != 0
    from_lower = pltpu.roll(x, quarter, 1)
    from_upper = pltpu.roll(x, LANES - quarter, 1)
    return jnp.where(odd, from_lower, from_upper)


def _rope(x, cos, sin_signed, quarter):
    return x * cos + _swap_quarters(x, quarter) * sin_signed


def _norm_rope_chunk(acc, meansq, gain, cos, sin, quarter):
    y = acc * lax.rsqrt(meansq + EPS) * gain
    if quarter is None:
        return y
    return jnp.concatenate([_rope(y[:, :LANES], cos, sin, quarter),
                            _rope(y[:, LANES:], cos, sin, quarter)], axis=1)


def _stream_specs(src, cols, n_lat_blocks):
    if len(src) == 1:
        return [pl.BlockSpec((TM, cols), lambda i: (i, 0))]
    return [pl.BlockSpec((TM, cols), lambda i: (jnp.minimum(i, n_lat_blocks - 1), 0)),
            pl.BlockSpec((TM, cols), lambda i: (0, 0))]


def _stream_block(refs, n_lat_blocks):
    if len(refs) == 1:
        return refs[0][...]
    return jnp.where(pl.program_id(0) < n_lat_blocks, refs[0][...], refs[1][...])


def _mod_specs(d, bpb, nbatch):
    row = lambda i: (jnp.minimum(i // bpb, nbatch), 0, 0)
    return pl.BlockSpec((1, d), lambda i: (0, 0)), pl.BlockSpec((1, 1, d), row)


def _rope_spec(bpb, nbatch):
    return pl.BlockSpec((TM, LANES), lambda i: (jnp.where(i < bpb * nbatch, i % bpb, bpb), 0))


def _whole(shape):
    return pl.BlockSpec(shape, lambda *_: (0,) * len(shape))


EVEN_COLS = 2560
EVEN_NORM_CHUNKS = (0, 1, 2, 3, 6, 7, 8)
EVEN_DOT_RANGES = ((0, 4), (6, 9), (4, 6), (9, 10))


def _proj_even_kernel(*refs, n_x, n_lat_blocks):
    x_refs = refs[:n_x]
    g_ref, sh_ref, sc_ref, w_ref, gain_ref, g64_ref, cos_ref, sin_ref, o_ref = refs[n_x:]
    h = _modulated(_stream_block(x_refs, n_lat_blocks), g_ref, sh_ref, sc_ref).astype(BF16)
    cos = cos_ref[...]
    sin = sin_ref[...]
    for first, last in EVEN_DOT_RANGES:
        full = jnp.dot(h, w_ref[:, first * 256:last * 256], preferred_element_type=F32)
        chunk = lambda c: full[:, (c - first) * 256:(c - first + 1) * 256]
        normed = [c for c in range(first, last) if c in EVEN_NORM_CHUNKS]
        meansq = {}
        if normed:
            meansq = dict(zip(normed, _group_meansq([chunk(c) for c in normed], g64_ref, 1.0 / HEAD_DIM)))
        for c in range(first, last):
            cols = slice(c * 256, (c + 1) * 256)
            acc = chunk(c)
            if c in EVEN_NORM_CHUNKS:
                acc = _norm_rope_chunk(acc, meansq[c], gain_ref[:, cols], cos, sin, HEAD_DIM // 4)
            o_ref[:, cols] = acc.astype(BF16)


def _proj_even(x_src, g, sh, sc, w, gain, g64, cos, sin, bpb, nbatch):
    d = x_src[0].shape[1]
    n_lat_blocks = bpb * nbatch
    gain_spec, mod_spec = _mod_specs(d, bpb, nbatch)
    kern = functools.partial(_proj_even_kernel, n_x=len(x_src), n_lat_blocks=n_lat_blocks)
    return pl.pallas_call(
        kern,
        grid=(n_lat_blocks + 1,),
        in_specs=_stream_specs(x_src, d, n_lat_blocks) + [
            gain_spec, mod_spec, mod_spec, _whole(w.shape), _whole(gain.shape), _whole(g64.shape),
            _rope_spec(bpb, nbatch), _rope_spec(bpb, nbatch),
        ],
        out_specs=pl.BlockSpec((TM, EVEN_COLS), lambda i: (i, 0)),
        out_shape=jax.ShapeDtypeStruct(((n_lat_blocks + 1) * TM, EVEN_COLS), BF16),
        compiler_params=_params("parallel"),
        name="proj_even",
    )(*x_src, g, sh, sc, w, gain, g64, cos, sin)


ODD_W_COLS = 2432
ODD_COLS = 4096
MLA_PAD = LANES


def _proj_odd_kernel(x_ref, g_ref, sh_ref, sc_ref, w_ref, wq_ref, wkv_ref, qa_g_ref, kva_g_ref,
                     mq_g_ref, mk_g_ref, nq_g_ref, nk_g_ref, g64_ref, g128_ref, cos_ref, sin_ref, o_ref):
    h = _modulated(x_ref[...], g_ref, sh_ref, sc_ref).astype(BF16)
    cos = cos_ref[...]
    sin = sin_ref[...]
    quarter = MLA_ROPE // 4

    full = jnp.dot(h, w_ref[...], preferred_element_type=F32)

    def low_rank(a, gain_ref):
        ms = jnp.mean(a * a, axis=-1, keepdims=True)
        return (a * lax.rsqrt(ms + EPS) * gain_ref[...]).astype(BF16)

    n_mla = MLA_HEADS * MLA_PAD // 256
    qa = low_rank(full[:, :MLA_Q_LORA], qa_g_ref)
    qf = jnp.dot(qa, wq_ref[...], preferred_element_type=F32)
    q_chunks = [qf[:, c * 256:(c + 1) * 256] for c in range(n_mla)]
    ms_q = _group_meansq(q_chunks, g128_ref, 1.0 / MLA_QK)
    kva = low_rank(full[:, MLA_Q_LORA:MLA_Q_LORA + MLA_KV_LORA], kva_g_ref)
    kvf = jnp.dot(kva, wkv_ref[...], preferred_element_type=F32)
    kslot = full[:, 2304:2432]
    kslot2 = jnp.concatenate([kslot, kslot], axis=1)
    k_rot = _swap_quarters(kslot * mk_g_ref[:, :LANES], quarter)
    k_rot2 = jnp.concatenate([k_rot, k_rot], axis=1)
    cos2 = jnp.concatenate([cos, cos], axis=1)
    sin2 = jnp.concatenate([sin, sin], axis=1)
    k_chunks = [kvf[:, c * 256:(c + 1) * 256] + kslot2 for c in range(n_mla)]
    ms_k = _group_meansq(k_chunks, g128_ref, 1.0 / MLA_QK)
    for c in range(n_mla):
        y = _norm_rope_chunk(q_chunks[c], ms_q[c], mq_g_ref[...], cos, sin, quarter)
        o_ref[:, c * 256:(c + 1) * 256] = y.astype(BF16)
    for c in range(n_mla):
        inv_rms = lax.rsqrt(ms_k[c] + EPS)
        y = (k_chunks[c] * inv_rms * mk_g_ref[...]) * cos2 + (inv_rms * k_rot2) * sin2
        o_ref[:, 1024 + c * 256:1024 + (c + 1) * 256] = y.astype(BF16)
    o_ref[:, 2048:2560] = kvf[:, 1024:1536].astype(BF16)

    n_chunks = [full[:, 768 + c * 256:768 + (c + 1) * 256] for c in range(4)]
    ms64 = _group_meansq(n_chunks, g64_ref, 1.0 / HEAD_DIM)
    for c in range(4):
        gain = nq_g_ref[...] if c < 2 else nk_g_ref[...]
        y = _norm_rope_chunk(n_chunks[c], ms64[c], gain, None, None, None)
        o_ref[:, 2560 + c * 256:2560 + (c + 1) * 256] = y.astype(BF16)
    o_ref[:, 3584:4096] = full[:, 1792:2304].astype(BF16)


def _proj_odd(xa, g, sh, sc, w, wq, wkv, gains, g64, g128, cos, sin, bpb, nbatch):
    na, d = xa.shape
    gain_spec, mod_spec = _mod_specs(d, bpb, nbatch)
    tok_specs = _stream_specs((xa,), d, bpb * nbatch) + [gain_spec, mod_spec, mod_spec]
    consts = (w, wq, wkv) + tuple(gains) + (g64, g128)
    return pl.pallas_call(
        _proj_odd_kernel,
        grid=(na // TM,),
        in_specs=tok_specs + [_whole(a.shape) for a in consts] + [_rope_spec(bpb, nbatch)] * 2,
        out_specs=pl.BlockSpec((TM, ODD_COLS), lambda i: (i, 0)),
        out_shape=jax.ShapeDtypeStruct((na, ODD_COLS), BF16),
        compiler_params=_params("parallel"),
        name="proj_odd",
    )(xa, g, sh, sc, *consts, cos, sin)


def _attn_kernel(*refs, mode, split, main, seq, has_sink, lam_init, tq, tq_sub):
    n_sub = tq // tq_sub
    it = iter(refs)
    q_ref = next(it)
    k_ref = v_ref = sink_ref = lam_ref = subln_ref = None
    bias_refs = ()
    if main is not None:
        k_ref, v_ref = next(it), next(it)
    kc_ref, vc_ref = next(it), next(it)
    if main == "na":
        bias_refs = tuple(next(it) for _ in range(n_sub))
    if has_sink:
        sink_ref = next(it)
    safe_ref = next(it)
    if mode == "diff":
        lam_ref, subln_ref = next(it), next(it)
    o_ref = next(it)
    vt_ref, vct_ref = (next(it), next(it)) if main == "full" else (None, None)

    pair = pl.program_id(1)
    qi = pl.program_id(2)
    q = q_ref[...]
    low = lax.broadcasted_iota(jnp.int32, (1, LANES), 1) < HALF
    if split == "mask":
        zero = jnp.zeros_like(q)
        qs = (jnp.where(low, q, zero), jnp.where(low, zero, q))
        k_of = lambda k, s: k
    else:
        qs = (q[:, :LANES], q[:, LANES:])
        k_of = lambda k, s: k[:, s * LANES:(s + 1) * LANES]
    q_of = lambda s, t: qs[s][t * tq_sub:(t + 1) * tq_sub]
    sinks = [sink_ref[2 * pair + s] * LOG2E for s in range(2)] if has_sink else None
    nt_dims = (((1,), (1,)), ((), ()))

    def finish(o_lo, o_hi):
        if mode == "diff":
            lv = lam_ref[...]
            lam = (jnp.exp(jnp.sum(lv[0:1] * lv[1:2], axis=-1, keepdims=True))
                   - jnp.exp(jnp.sum(lv[2:3] * lv[3:4], axis=-1, keepdims=True)) + lam_init)
            o = o_lo - lam * o_hi
            ms = jnp.mean(o * o, axis=-1, keepdims=True)
            o = o * lax.rsqrt(ms + EPS) * subln_ref[...] * (1.0 - lam_init)
        else:
            o = jnp.where(low, o_lo, o_hi)
        o_ref[...] = o.astype(BF16)

    def local_keys(t):
        if main == "window":
            span = tq_sub + 2 * WINDOW
            q0 = qi * tq + t * tq_sub
            w0 = jnp.clip(q0 - WINDOW, 0, seq - span)
            off = pl.multiple_of(w0, WINDOW)
            key = lax.broadcasted_iota(jnp.int32, (span, tq_sub), 0)
            qry = lax.broadcasted_iota(jnp.int32, (span, tq_sub), 1)
            inside = jnp.abs(key - qry + (w0 - q0)) <= WINDOW
            logit_fn = lambda lg_t, s: jnp.where(inside, lg_t, NEG)
        else:
            rows_q = tq_sub // GRID_W
            span_rows = rows_q + NA_ROWS
            w0 = jnp.clip(qi * (tq // GRID_W) + t * rows_q - NA_ROWS // 2, 0, seq // GRID_W - span_rows)
            off = pl.multiple_of(w0 * GRID_W, GRID_W)
            span = span_rows * GRID_W
            logit_fn = lambda lg_t, s: lg_t + bias_refs[t][0, s]
        return k_ref[pl.ds(off, span), :], v_ref[pl.ds(off, span), :], logit_fn

    def online_sub(t):
        def init(s):
            if has_sink:
                m0 = jnp.full((tq_sub, 1), sinks[s], F32)
                l0 = jnp.ones((tq_sub, 1), F32)
            else:
                m0 = jnp.full((tq_sub, 1), NEG, F32)
                l0 = jnp.zeros((tq_sub, 1), F32)
            return m0, l0, jnp.zeros((tq_sub, LANES), F32)

        def segment(state, k, v, logit_fn):
            out = []
            for s in range(2):
                m, l, acc = state[s]
                logits = lax.dot_general(q_of(s, t), k_of(k, s), nt_dims, preferred_element_type=F32)
                if logit_fn is not None:
                    logits = logit_fn(logits.T, s).T
                m_new = jnp.maximum(m, jnp.max(logits, axis=-1, keepdims=True))
                alpha = jnp.exp2(m - m_new)
                p = jnp.exp2(logits - m_new)
                l = alpha * l + jnp.sum(p, axis=-1, keepdims=True)
                acc = alpha * acc + jnp.dot(p.astype(BF16), v, preferred_element_type=F32)
                out.append((m_new, l, acc))
            return tuple(out)

        state = (init(0), init(1))
        if main == "full":
            def body(c, st):
                off = pl.multiple_of(c * TK_FULL, TK_FULL)
                return segment(st, k_ref[pl.ds(off, TK_FULL), :], v_ref[pl.ds(off, TK_FULL), :], None)
            state = lax.fori_loop(0, seq // TK_FULL, body, state)
        elif main is not None:
            state = segment(state, *local_keys(t))
        (_, l_lo, acc_lo), (_, l_hi, acc_hi) = segment(state, kc_ref[...], vc_ref[...], None)
        return acc_lo / l_lo, acc_hi / l_hi

    def with_ones_rows(v):
        return jnp.concatenate([v.T, jnp.ones((ONES_ROWS, v.shape[0]), BF16)], axis=0)

    def probs_t(k, q_rows, s, logit_fn):
        logits_t = lax.dot_general(k_of(k, s), q_rows, nt_dims, preferred_element_type=F32)
        if logit_fn is not None:
            logits_t = logit_fn(logits_t, s)
        return jnp.exp2(logits_t).astype(BF16)

    def max_free():
        acc = None
        if main == "full":
            @pl.when(qi == 0)
            def _():
                ones = jnp.ones((ONES_ROWS, TK_MAX_FREE), BF16)
                for c in range(seq // TK_MAX_FREE):
                    vt_ref[c, :LANES, :] = v_ref[c * TK_MAX_FREE:(c + 1) * TK_MAX_FREE, :].T
                    vt_ref[c, LANES:, :] = ones
                vct_ref[...] = with_ones_rows(vc_ref[...])

            def body(c, acc):
                off = pl.multiple_of(c * TK_MAX_FREE, TK_MAX_FREE)
                k = k_ref[pl.ds(off, TK_MAX_FREE), :]
                return tuple(acc[s] + jnp.dot(vt_ref[c], probs_t(k, qs[s], s, None), preferred_element_type=F32)
                             for s in range(2))

            zero_acc = jnp.zeros((LANES + ONES_ROWS, tq), F32)
            acc = lax.fori_loop(0, seq // TK_MAX_FREE, body, (zero_acc, zero_acc))
            vct = vct_ref[...]
        else:
            vct = with_ones_rows(vc_ref[...])
        p_ctx = [probs_t(kc_ref[...], qs[s], s, None) for s in range(2)]
        if main in ("window", "na"):
            windows = [local_keys(t) for t in range(n_sub)]
            p_loc = [[probs_t(k, q_of(s, t), s, fn) for s in range(2)] for t, (k, _, fn) in enumerate(windows)]
            vts = [with_ones_rows(v) for _, v, _ in windows]
        outs = []
        for s in range(2):
            acc_s = jnp.dot(vct, p_ctx[s], preferred_element_type=F32)
            if main == "full":
                acc_s = acc_s + acc[s]
            elif main is not None:
                acc_s = acc_s + jnp.concatenate(
                    [jnp.dot(vts[t], p_loc[t][s], preferred_element_type=F32) for t in range(n_sub)], axis=1)
            den = acc_s[LANES:LANES + 1]
            if has_sink:
                den = den + jnp.exp2(jnp.full((1, tq), sinks[s], F32))
            outs.append((acc_s[:LANES] / den).T)
        finish(*outs)

    def online():
        parts = [online_sub(t) for t in range(n_sub)]
        finish(*[jnp.concatenate([p[s] for p in parts], axis=0) if n_sub > 1 else parts[0][s]
                 for s in range(2)])

    safe = safe_ref[0] == 1
    pl.when(safe)(max_free)
    pl.when(jnp.logical_not(safe))(online)


def _attention(p, *, name, mode, split, main, n_pairs, q_blk, k_blk, v_blk, out_cols, ctx_queries,
               nbatch, seq, ctx_len, safe, bias=None, sink=None, lam=None, subln=None, lam_init=0.0):
    n_lat = nbatch * seq
    qw = LANES if split == "mask" else 2 * LANES
    if ctx_queries:
        tq, nq = ctx_len, 1
        q_row = lambda b, j, i: n_lat // ctx_len + b
        out_rows = nbatch * ctx_len
        out_row = lambda b, j, i: b
    else:
        tq = TQ_FULL if main == "full" else TQ
        nq = seq // tq
        q_row = lambda b, j, i: b * nq + i
        out_rows = n_lat
        out_row = q_row
    ctx_row = lambda b, j, i: n_lat // ctx_len + b

    in_specs = [pl.BlockSpec((tq, qw), lambda b, j, i: (q_row(b, j, i), q_blk(j)))]
    args = [p]
    if main is not None:
        in_specs += [pl.BlockSpec((seq, qw), lambda b, j, i: (b, k_blk(j))),
                     pl.BlockSpec((seq, LANES), lambda b, j, i: (b, v_blk(j)))]
        args += [p, p]
    in_specs += [pl.BlockSpec((ctx_len, qw), lambda b, j, i: (ctx_row(b, j, i), k_blk(j))),
                 pl.BlockSpec((ctx_len, LANES), lambda b, j, i: (ctx_row(b, j, i), v_blk(j)))]
    args += [p, p]
    tq_sub = TQ_SUB if main in ("window", "na") else tq
    n_sub = tq // tq_sub
    if main == "na":
        span = (tq_sub // GRID_W + NA_ROWS) * GRID_W
        last = nq * n_sub - 1
        cls = lambda g: jnp.where(g == 0, 0, jnp.where(g == last, 2, 1))
        for t in range(n_sub):
            in_specs.append(pl.BlockSpec((1, 2, span, tq_sub),
                                         lambda b, j, i, t=t: (cls(i * n_sub + t), j, 0, 0)))
            args.append(bias)
    if sink is not None:
        in_specs.append(pl.BlockSpec(memory_space=pltpu.SMEM))
        args.append(sink)
    in_specs.append(pl.BlockSpec(memory_space=pltpu.SMEM))
    args.append(safe)
    if mode == "diff":
        in_specs += [_whole(lam.shape), _whole(subln.shape)]
        args += [lam, subln]

    kern = functools.partial(_attn_kernel, mode=mode, split=split, main=main, seq=seq,
                             has_sink=sink is not None, lam_init=lam_init, tq=tq, tq_sub=tq_sub)
    scratch = []
    if main == "full":
        scratch = [pltpu.VMEM((seq // TK_MAX_FREE, LANES + ONES_ROWS, TK_MAX_FREE), BF16),
                   pltpu.VMEM((LANES + ONES_ROWS, ctx_len), BF16)]
    return pl.pallas_call(
        kern,
        grid=(nbatch, n_pairs, nq),
        in_specs=in_specs,
        out_specs=pl.BlockSpec((tq, LANES), lambda b, j, i: (out_row(b, j, i), j)),
        out_shape=jax.ShapeDtypeStruct((out_rows, out_cols), BF16),
        scratch_shapes=scratch,
        compiler_params=_params("parallel", "parallel", "arbitrary"),
        name=name,
    )(*args)


def _na_bias_kernel(rpb_ref, o_ref, *, n_rows, rows_q):
    h = pl.program_id(0)
    n_dr = 2 * NA_ROWS - 1
    n_dc = 2 * NA_COLS - 1
    kc = lax.broadcasted_iota(jnp.int32, (GRID_W, LANES), 0)
    cq = lax.broadcasted_iota(jnp.int32, (GRID_W, LANES), 1) & (GRID_W - 1)
    dc_idx = jnp.clip(kc - cq, -(NA_COLS - 1), NA_COLS - 1) + NA_COLS - 1
    cs = jnp.clip(cq - NA_COLS // 2, 0, GRID_W - NA_COLS)
    col_ok = (kc >= cs) & (kc < cs + NA_COLS)
    tiles = [jnp.zeros((GRID_W, LANES), F32) for _ in range(n_dr)]
    for dc in range(n_dc):
        hit = dc_idx == dc
        for dr in range(n_dr):
            tiles[dr] = jnp.where(hit, rpb_ref[(h * n_dr + dr) * n_dc + dc] * LOG2E, tiles[dr])
    tiles = [jnp.where(col_ok, t, NEG) for t in tiles]
    masked = jnp.full((GRID_W, LANES), NEG, F32)
    low = lax.broadcasted_iota(jnp.int32, (GRID_W, LANES), 1) < HALF

    span_rows = rows_q + NA_ROWS
    first_q_row = (0, span_rows, n_rows - rows_q)
    for c in range(3):
        r0 = first_q_row[c]
        w0 = min(max(r0 - NA_ROWS // 2, 0), n_rows - span_rows)
        for kr in range(span_rows):
            k_abs = w0 + kr
            for m in range(rows_q // 2):
                halves = []
                for r in (r0 + 2 * m, r0 + 2 * m + 1):
                    rs = min(max(r - NA_ROWS // 2, 0), n_rows - NA_ROWS)
                    halves.append(tiles[k_abs - r + NA_ROWS - 1] if rs <= k_abs < rs + NA_ROWS else masked)
                o_ref[c, 0, kr * GRID_W:(kr + 1) * GRID_W, m * LANES:(m + 1) * LANES] = (
                    jnp.where(low, halves[0], halves[1]))


def _na_bias(rpb, seq):
    n_rows = seq // GRID_W
    rows_q = TQ_SUB // GRID_W
    span = (rows_q + NA_ROWS) * GRID_W
    kern = functools.partial(_na_bias_kernel, n_rows=n_rows, rows_q=rows_q)
    return pl.pallas_call(
        kern,
        grid=(NA_HEADS,),
        in_specs=[pl.BlockSpec(memory_space=pltpu.SMEM)],
        out_specs=pl.BlockSpec((3, 1, span, TQ_SUB), lambda h: (0, h, 0, 0)),
        out_shape=jax.ShapeDtypeStruct((3, NA_HEADS, span, TQ_SUB), F32),
        compiler_params=_params("parallel"),
        name="na_bias",
    )(rpb.reshape(-1))


def _post_kernel(*refs, n_x, n_o, n_lat_blocks):
    x_refs, oa_refs, ob_refs = refs[:n_x], refs[n_x:n_x + n_o], refs[n_x + n_o:n_x + 2 * n_o]
    wo_ref, g1_ref, n2_ref, sh_ref, sc_ref, g2_ref, w1_ref, w2_ref, o_ref = refs[n_x + 2 * n_o:]
    oa = _stream_block(oa_refs, n_lat_blocks)
    ob = _stream_block(ob_refs, n_lat_blocks)
    half = oa.shape[1]
    y = (jnp.dot(oa, wo_ref[:half, :], preferred_element_type=F32)
         + jnp.dot(ob, wo_ref[half:, :], preferred_element_type=F32))
    x1 = _stream_block(x_refs, n_lat_blocks) + g1_ref[0] * y
    h = _modulated(x1, n2_ref, sh_ref, sc_ref).astype(BF16)
    a = jnp.dot(h, w1_ref[...], preferred_element_type=F32)
    a = jnp.square(jnp.maximum(a, 0.0)).astype(BF16)
    o_ref[...] = x1 + g2_ref[0] * jnp.dot(a, w2_ref[...], preferred_element_type=F32)


def _post(x_src, oa_src, ob_src, wo, g1, n2, sh, sc, g2, w1, w2, layer, n_blocks, bpb, nbatch):
    d = x_src[0].shape[1]
    half = oa_src[0].shape[1]
    n_lat_blocks = bpb * nbatch
    gain_spec, mod_spec = _mod_specs(d, bpb, nbatch)
    resident = lambda a: pl.BlockSpec((None,) + a.shape[1:], lambda i: (layer, 0, 0),
                                      pipeline_mode=pl.Buffered(1))
    kern = functools.partial(_post_kernel, n_x=len(x_src), n_o=len(oa_src), n_lat_blocks=n_lat_blocks)
    return pl.pallas_call(
        kern,
        grid=(n_blocks,),
        in_specs=(_stream_specs(x_src, d, n_lat_blocks) + _stream_specs(oa_src, half, n_lat_blocks)
                  + _stream_specs(ob_src, half, n_lat_blocks)
                  + [resident(wo), mod_spec, gain_spec, mod_spec, mod_spec, mod_spec, resident(w1), resident(w2)]),
        out_specs=pl.BlockSpec((TM, d), lambda i: (i, 0)),
        out_shape=jax.ShapeDtypeStruct((n_blocks * TM, d), F32),
        compiler_params=_params("parallel"),
        name="post",
    )(*x_src, *oa_src, *ob_src, wo, g1, n2, sh, sc, g2, w1, w2)


def _block_diag_ones(group):
    idx = np.arange(256) // group
    return jnp.asarray(idx[:, None] == idx[None, :], dtype=BF16)


def _rope_tables(seq, rot_dim, lane0, pad_rows):
    t = jnp.arange(seq, dtype=jnp.int32)
    row = (t // GRID_W).astype(F32)
    col = (t % GRID_W).astype(F32)
    n_freq = rot_dim // 4
    freqs = jnp.power(ROPE_BASE, -jnp.arange(n_freq, dtype=F32) / n_freq)
    ar = row[:, None] * freqs[None, :]
    ac = col[:, None] * freqs[None, :]
    ang = jnp.concatenate([ar, ar, ac, ac], axis=-1)
    sign = jnp.asarray(np.tile(np.repeat([-1.0, 1.0], n_freq), 2), F32)
    cos, sin = jnp.cos(ang), jnp.sin(ang) * sign[None, :]
    if lane0 is None:
        reps = LANES // rot_dim
        cos, sin = jnp.tile(cos, (1, reps)), jnp.tile(sin, (1, reps))
    else:
        pad = ((0, 0), (lane0, LANES - lane0 - rot_dim))
        cos = jnp.pad(cos, pad, constant_values=1.0)
        sin = jnp.pad(sin, pad)
    cos = jnp.concatenate([cos, jnp.ones((pad_rows, LANES), F32)], axis=0)
    sin = jnp.concatenate([sin, jnp.zeros((pad_rows, LANES), F32)], axis=0)
    return cos, sin


def _logits_bounded(q_gain, k_gain, dim, bias=None, sink=None):
    bound = jnp.max(jnp.abs(q_gain)) * jnp.max(jnp.abs(k_gain)) * (dim ** 0.5) * 1.02
    if bias is not None:
        bound = bound + jnp.max(jnp.abs(bias))
    if sink is not None:
        bound = jnp.maximum(bound, jnp.max(jnp.abs(sink)))
    return (bound <= MAX_FREE_LOGIT_BOUND).astype(jnp.int32).reshape(1)


def _tile_gain(g, reps, scale=1.0):
    return (jnp.tile(g.astype(F32), reps) * scale)[None, :]


def _even_weights(w, q_g, k_g, sq_g, sk_g):
    nq = DIFF_HEADS * 2 * HEAD_DIM
    qa, ka, va = w[:, :nq], w[:, nq:2 * nq], w[:, 2 * nq:3 * nq]
    o = 3 * nq
    qb = w[:, o:o + SWA_Q_HEADS * HEAD_DIM]
    o += SWA_Q_HEADS * HEAD_DIM
    kb = [w[:, o + i * HEAD_DIM:o + (i + 1) * HEAD_DIM] for i in range(SWA_KV_HEADS)]
    o += SWA_KV_HEADS * HEAD_DIM
    vb = [w[:, o + i * HEAD_DIM:o + (i + 1) * HEAD_DIM] for i in range(SWA_KV_HEADS)]
    dup = lambda parts: [p for p in parts for _ in range(2)]
    w_new = jnp.concatenate([qa, ka, va, qb] + dup(kb) + dup(vb), axis=1).astype(BF16)
    scale = HEAD_DIM ** -0.5 * LOG2E
    ones = lambda n: jnp.ones((1, n), F32)
    gain = jnp.concatenate([
        _tile_gain(q_g, 2 * DIFF_HEADS, scale), _tile_gain(k_g, 2 * DIFF_HEADS), ones(nq),
        _tile_gain(sq_g, SWA_Q_HEADS, scale), _tile_gain(sk_g, 2 * SWA_KV_HEADS),
        ones(2 * SWA_KV_HEADS * HEAD_DIM)], axis=1)
    return w_new, gain


def _odd_weights(w, wq_up, wkv_up, qa_g, kva_g, mq_g, mk_g, nq_g, nk_g):
    d = w.shape[0]
    o = 0
    parts = []
    for n in (MLA_Q_LORA, MLA_KV_LORA, MLA_ROPE, NA_HEADS * HEAD_DIM, NA_HEADS * HEAD_DIM, NA_HEADS * HEAD_DIM):
        parts.append(w[:, o:o + n])
        o += n
    q_a, kv_a, k_r, nq, nk, nv = parts
    slot = jnp.concatenate([jnp.zeros((d, MLA_NOPE), w.dtype), k_r,
                            jnp.zeros((d, LANES - MLA_QK), w.dtype)], axis=1)
    w_new = jnp.concatenate([q_a, kv_a, nq, nk, nv, slot], axis=1).astype(BF16)
    wq = jnp.pad(wq_up.reshape(MLA_Q_LORA, MLA_HEADS, MLA_QK), ((0, 0), (0, 0), (0, MLA_PAD - MLA_QK)))
    wq = wq.reshape(MLA_Q_LORA, MLA_HEADS * MLA_PAD).astype(BF16)
    wkv = wkv_up.reshape(MLA_KV_LORA, MLA_HEADS, MLA_NOPE + MLA_V)
    wk = jnp.pad(wkv[..., :MLA_NOPE], ((0, 0), (0, 0), (0, MLA_PAD - MLA_NOPE))).reshape(MLA_KV_LORA, -1)
    wv = wkv[..., MLA_NOPE:].reshape(MLA_KV_LORA, MLA_HEADS * MLA_V)
    wkv_new = jnp.concatenate([wk, wv], axis=1).astype(BF16)
    pad_gain = lambda g, scale: _tile_gain(jnp.pad(g.astype(F32), (0, MLA_PAD - MLA_QK)), 2, scale)
    gains = (qa_g.astype(F32)[None, :], kva_g.astype(F32)[None, :],
             pad_gain(mq_g, MLA_QK ** -0.5 * LOG2E), pad_gain(mk_g, 1.0),
             _tile_gain(nq_g, 4, HEAD_DIM ** -0.5 * LOG2E), _tile_gain(nk_g, 4))
    return w_new, wq, wkv_new, gains


def kernel(x, c, ctx, c_ctx, ada_w, ada_b, norm1_g, norm2_g, w_out, mlp_w1, mlp_w2, ev_w_in, diff_q_g,
           diff_k_g, diff_lam, diff_subln_g, swa_q_g, swa_k_g, swa_sink, od_w_in, mla_qa_g, mla_kva_g,
           mla_wq_up, mla_wkv_up, mla_q_g, mla_k_g, na_q_g, na_k_g, na_rpb):
    nbatch, seq, d = x.shape
    ctx_len = ctx.shape[1]
    depth = ada_w.shape[0]
    n_lat = nbatch * seq
    n_ctx = nbatch * ctx_len
    assert seq % TM == 0 and n_ctx == TM and seq % TQ == 0 and seq % GRID_W == 0
    assert seq // GRID_W >= 2 * (NA_ROWS + TQ_SUB // GRID_W) and seq >= TQ_SUB + 2 * WINDOW
    bpb = seq // TM

    mod_rows = 8
    cvec = jnp.concatenate([c, c_ctx[None, :], jnp.zeros((mod_rows - nbatch - 1, d), F32)], axis=0)
    mod = _modulation(cvec, ada_w, ada_b)

    cos64, sin64 = _rope_tables(seq, HEAD_DIM, None, n_ctx)
    cos32, sin32 = _rope_tables(seq, MLA_ROPE, MLA_NOPE, n_ctx)
    g64 = _block_diag_ones(HEAD_DIM)
    g128 = _block_diag_ones(LANES)

    x_src = (x.reshape(n_lat, d), ctx.reshape(n_ctx, d))
    wo_all, w1_all, w2_all = w_out.astype(BF16), mlp_w1.astype(BF16), mlp_w2.astype(BF16)
    common = dict(nbatch=nbatch, seq=seq, ctx_len=ctx_len)

    for layer in range(depth):
        need_ctx = layer < depth - 1
        n_blocks = bpb * nbatch + (1 if need_ctx else 0)
        m = mod[layer, :nbatch + 1]
        sh1, sc1, g1, sh2, sc2, g2 = [m[:, k * d:(k + 1) * d].reshape(nbatch + 1, 1, d) for k in range(6)]
        n1 = norm1_g[layer][None, :]
        n2 = norm2_g[layer][None, :]
        i = layer // 2
        if layer % 2 == 0:
            lam_init = 0.8 - 0.6 * math.exp(-0.3 * layer)
            w_in, gain = _even_weights(ev_w_in[i], diff_q_g[i], diff_k_g[i], swa_q_g[i], swa_k_g[i])
            p = _proj_even(x_src, n1, sh1, sc1, w_in, gain, g64, cos64, sin64, bpb, nbatch)
            lam = diff_lam[i].astype(F32)
            subln = diff_subln_g[i].astype(F32)[None, :]
            sink = swa_sink[i].astype(F32)
            diff_kw = dict(mode="diff", split="mask", n_pairs=DIFF_HEADS, q_blk=lambda j: j,
                           k_blk=lambda j: 4 + j, v_blk=lambda j: 8 + j, out_cols=DIFF_HEADS * LANES,
                           lam=lam, subln=subln, lam_init=lam_init,
                           safe=_logits_bounded(diff_q_g[i], diff_k_g[i], HEAD_DIM), **common)
            swa_kw = dict(mode="pair", split="mask", n_pairs=SWA_Q_HEADS // 2, q_blk=lambda j: 12 + j,
                          k_blk=lambda j: 16 + j // 2, v_blk=lambda j: 18 + j // 2,
                          out_cols=SWA_Q_HEADS * HEAD_DIM, sink=sink,
                          safe=_logits_bounded(swa_q_g[i], swa_k_g[i], HEAD_DIM, sink=sink), **common)
            oa = _attention(p, name="diff_attn", main="full", ctx_queries=False, **diff_kw)
            ob = _attention(p, name="window_attn", main="window", ctx_queries=False, **swa_kw)
            if need_ctx:
                oa_c = _attention(p, name="diff_attn_ctx", main=None, ctx_queries=True, **diff_kw)
                ob_c = _attention(p, name="window_attn_ctx", main=None, ctx_queries=True, **swa_kw)
        else:
            w_in, wq, wkv, gains = _odd_weights(od_w_in[i], mla_wq_up[i], mla_wkv_up[i], mla_qa_g[i],
                                                mla_kva_g[i], mla_q_g[i], mla_k_g[i], na_q_g[i], na_k_g[i])
            p = _proj_odd(x_src[0], n1, sh1, sc1, w_in, wq, wkv, gains, g64, g128, cos32, sin32, bpb, nbatch)
            bias = _na_bias(na_rpb[i].astype(F32), seq)
            mla_kw = dict(mode="pair", split="slice", n_pairs=MLA_HEADS // 2, q_blk=lambda j: j,
                          k_blk=lambda j: 4 + j, v_blk=lambda j: 16 + j, out_cols=MLA_HEADS * MLA_V,
                          safe=_logits_bounded(mla_q_g[i], mla_k_g[i], MLA_QK), **common)
            na_kw = dict(mode="pair", split="mask", n_pairs=NA_HEADS // 2, q_blk=lambda j: 20 + j,
                         k_blk=lambda j: 24 + j, v_blk=lambda j: 28 + j, out_cols=NA_HEADS * HEAD_DIM,
                         safe=_logits_bounded(na_q_g[i], na_k_g[i], HEAD_DIM, bias=na_rpb[i]), **common)
            oa = _attention(p, name="mla_attn", main="full", ctx_queries=False, **mla_kw)
            ob = _attention(p, name="na_attn", main="na", ctx_queries=False, bias=bias, **na_kw)
            if need_ctx:
                oa_c = _attention(p, name="mla_attn_ctx", main=None, ctx_queries=True, **mla_kw)
                ob_c = _attention(p, name="na_attn_ctx", main=None, ctx_queries=True, **na_kw)
        oa_src, ob_src = ((oa, oa_c), (ob, ob_c)) if need_ctx else ((oa,), (ob,))
        x_src = (_post(x_src, oa_src, ob_src, wo_all, g1, n2, sh2, sc2, g2, w1_all, w2_all,
                       layer, n_blocks, bpb, nbatch),)
    return x_src[0].reshape(nbatch, seq, d)
```

```python
import functools
import math

import numpy as np
import jax
import jax.numpy as jnp
from jax import lax
from jax.experimental import pallas as pl
from jax.experimental.pallas import tpu as pltpu

F32 = jnp.float32
BF16 = jnp.bfloat16

LANES = 128
HALF = LANES // 2
CHUNK = 2 * LANES
GRID_W = 64
HEAD_DIM = 64
EPS = 1e-6
ROPE_BASE = 10000.0
DIFF_HEADS = 4
SWA_Q_HEADS = 8
SWA_KV_HEADS = 2
WINDOW = 128
MLA_HEADS = 8
MLA_Q_LORA = 512
MLA_KV_LORA = 256
MLA_NOPE = 64
MLA_ROPE = 32
MLA_QK = MLA_NOPE + MLA_ROPE
MLA_V = 64
NA_HEADS = 8
NA_ROWS = 8
NA_COLS = 16
NEG = -1e30
LOG2E = math.log2(math.e)
MAX_FREE_LOGIT_BOUND = 50.0
BF16_ROUNDING_MARGIN = 1.02

TM = 512
TQ = 2048
TQ_SUB = 256
TQ_FULL = 1024
TK_FULL = 512
TK_MAX_FREE = 4096
ONES_ROWS = 16
MOD_TN = 1536
VMEM_LIMIT = 56 * 1024 * 1024


def _params(*sem):
    return pltpu.CompilerParams(dimension_semantics=sem, vmem_limit_bytes=VMEM_LIMIT)


def _mod_kernel(c_ref, w_ref, b_ref, o_ref):
    c = c_ref[...]
    a = (c * jax.nn.sigmoid(c)).astype(BF16)
    w = w_ref[0].astype(BF16)
    o_ref[0] = jnp.dot(a, w, preferred_element_type=F32) + b_ref[0]


def _modulation(cvec, ada_w, ada_b):
    depth, d, n = ada_w.shape
    rows = cvec.shape[0]
    return pl.pallas_call(
        _mod_kernel,
        grid=(depth, n // MOD_TN),
        in_specs=[
            pl.BlockSpec((rows, d), lambda l, j: (0, 0)),
            pl.BlockSpec((1, d, MOD_TN), lambda l, j: (l, 0, j)),
            pl.BlockSpec((1, 1, MOD_TN), lambda l, j: (l, 0, j)),
        ],
        out_specs=pl.BlockSpec((1, rows, MOD_TN), lambda l, j: (l, 0, j)),
        out_shape=jax.ShapeDtypeStruct((depth, rows, n), F32),
        compiler_params=_params("parallel", "parallel"),
        name="modulation",
    )(cvec, ada_w, ada_b.reshape(depth, 1, n))


def _modulated(x, g_ref, sh_ref, sc_ref):
    ms = jnp.mean(x * x, axis=-1, keepdims=True)
    y = x * lax.rsqrt(ms + EPS) * g_ref[...]
    return y * (1.0 + sc_ref[0]) + sh_ref[0]


def _group_meansq(chunks, g_ref, inv_n):
    rows = chunks[0].shape[0]
    sq = jnp.concatenate([(c * c).astype(BF16) for c in chunks], axis=0)
    ss = jnp.dot(sq, g_ref[...], preferred_element_type=F32) * inv_n
    return [ss[i * rows:(i + 1) * rows] for i in range(len(chunks))]


def _swap_quarters(x, quarter):
    lane = lax.broadcasted_iota(jnp.int32, x.shape, 1)
    odd = (lane & quarter) != 0
    from_lower = pltpu.roll(x, quarter, 1)
    from_upper = pltpu.roll(x, LANES - quarter, 1)
    return jnp.where(odd, from_lower, from_upper)


def _rope(x, cos, sin_signed, quarter):
    return x * cos + _swap_quarters(x, quarter) * sin_signed


def _norm_rope_chunk(acc, meansq, gain, cos, sin, quarter):
    y = acc * lax.rsqrt(meansq + EPS) * gain
    if quarter is None:
        return y
    return jnp.concatenate([_rope(y[:, :LANES], cos, sin, quarter),
                            _rope(y[:, LANES:], cos, sin, quarter)], axis=1)


def _stream_specs(src, cols, n_lat_blocks):
    if len(src) == 1:
        return [pl.BlockSpec((TM, cols), lambda i: (i, 0))]
    return [pl.BlockSpec((TM, cols), lambda i: (jnp.minimum(i, n_lat_blocks - 1), 0)),
            pl.BlockSpec((TM, cols), lambda i: (0, 0))]


def _stream_block(refs, n_lat_blocks):
    if len(refs) == 1:
        return refs[0][...]
    return jnp.where(pl.program_id(0) < n_lat_blocks, refs[0][...], refs[1][...])


def _mod_specs(d, bpb, nbatch):
    row = lambda i: (jnp.minimum(i // bpb, nbatch), 0, 0)
    return pl.BlockSpec((1, d), lambda i: (0, 0)), pl.BlockSpec((1, 1, d), row)


def _rope_spec(bpb, nbatch):
    return pl.BlockSpec((TM, LANES), lambda i: (jnp.where(i < bpb * nbatch, i % bpb, bpb), 0))


def _whole(shape):
    return pl.BlockSpec(shape, lambda *_: (0,) * len(shape))


EVEN_QA = 0
EVEN_KA = EVEN_QA + DIFF_HEADS * 2 * HEAD_DIM
EVEN_VA = EVEN_KA + DIFF_HEADS * 2 * HEAD_DIM
EVEN_QB = EVEN_VA + DIFF_HEADS * 2 * HEAD_DIM
EVEN_KB = EVEN_QB + SWA_Q_HEADS * HEAD_DIM
EVEN_VB = EVEN_KB + 2 * SWA_KV_HEADS * HEAD_DIM
EVEN_COLS = EVEN_VB + 2 * SWA_KV_HEADS * HEAD_DIM
EVEN_NORM_CHUNKS = tuple(c for c in range(EVEN_COLS // CHUNK)
                         if c * CHUNK < EVEN_VA or EVEN_QB <= c * CHUNK < EVEN_VB)
EVEN_DOT_RANGES = ((EVEN_QA // CHUNK, EVEN_VA // CHUNK), (EVEN_QB // CHUNK, EVEN_VB // CHUNK),
                   (EVEN_VA // CHUNK, EVEN_QB // CHUNK), (EVEN_VB // CHUNK, EVEN_COLS // CHUNK))


def _proj_even_kernel(*refs, n_x, n_lat_blocks):
    x_refs = refs[:n_x]
    g_ref, sh_ref, sc_ref, w_ref, gain_ref, g64_ref, cos_ref, sin_ref, o_ref = refs[n_x:]
    h = _modulated(_stream_block(x_refs, n_lat_blocks), g_ref, sh_ref, sc_ref).astype(BF16)
    cos = cos_ref[...]
    sin = sin_ref[...]
    for first, last in EVEN_DOT_RANGES:
        full = jnp.dot(h, w_ref[:, first * CHUNK:last * CHUNK], preferred_element_type=F32)
        chunk = lambda c: full[:, (c - first) * CHUNK:(c - first + 1) * CHUNK]
        normed = [c for c in range(first, last) if c in EVEN_NORM_CHUNKS]
        meansq = {}
        if normed:
            meansq = dict(zip(normed, _group_meansq([chunk(c) for c in normed], g64_ref, 1.0 / HEAD_DIM)))
        for c in range(first, last):
            cols = slice(c * CHUNK, (c + 1) * CHUNK)
            acc = chunk(c)
            if c in EVEN_NORM_CHUNKS:
                acc = _norm_rope_chunk(acc, meansq[c], gain_ref[:, cols], cos, sin, HEAD_DIM // 4)
            o_ref[:, cols] = acc.astype(BF16)


def _proj_even(x_src, g, sh, sc, w, gain, g64, cos, sin, bpb, nbatch):
    d = x_src[0].shape[1]
    n_lat_blocks = bpb * nbatch
    gain_spec, mod_spec = _mod_specs(d, bpb, nbatch)
    kern = functools.partial(_proj_even_kernel, n_x=len(x_src), n_lat_blocks=n_lat_blocks)
    return pl.pallas_call(
        kern,
        grid=(n_lat_blocks + 1,),
        in_specs=_stream_specs(x_src, d, n_lat_blocks) + [
            gain_spec, mod_spec, mod_spec, _whole(w.shape), _whole(gain.shape), _whole(g64.shape),
            _rope_spec(bpb, nbatch), _rope_spec(bpb, nbatch),
        ],
        out_specs=pl.BlockSpec((TM, EVEN_COLS), lambda i: (i, 0)),
        out_shape=jax.ShapeDtypeStruct(((n_lat_blocks + 1) * TM, EVEN_COLS), BF16),
        compiler_params=_params("parallel"),
        name="proj_even",
    )(*x_src, g, sh, sc, w, gain, g64, cos, sin)


NA_WIDTH = NA_HEADS * HEAD_DIM
MLA_PAD = LANES
MLA_WIDTH = MLA_HEADS * MLA_PAD
ODD_W_KVA = MLA_Q_LORA
ODD_W_NQ = ODD_W_KVA + MLA_KV_LORA
ODD_W_NV = ODD_W_NQ + 2 * NA_WIDTH
ODD_W_SLOT = ODD_W_NV + NA_WIDTH
ODD_W_COLS = ODD_W_SLOT + LANES
ODD_K = MLA_WIDTH
ODD_V = ODD_K + MLA_WIDTH
ODD_NQ = ODD_V + MLA_HEADS * MLA_V
ODD_NK = ODD_NQ + NA_WIDTH
ODD_NV = ODD_NK + NA_WIDTH
ODD_COLS = ODD_NV + NA_WIDTH


def _proj_odd_kernel(x_ref, g_ref, sh_ref, sc_ref, w_ref, wq_ref, wkv_ref, qa_g_ref, kva_g_ref,
                     mq_g_ref, mk_g_ref, nq_g_ref, nk_g_ref, g64_ref, g128_ref, cos_ref, sin_ref, o_ref):
    h = _modulated(x_ref[...], g_ref, sh_ref, sc_ref).astype(BF16)
    cos = cos_ref[...]
    sin = sin_ref[...]
    quarter = MLA_ROPE // 4

    lora = jnp.dot(h, w_ref[:, :ODD_W_NQ], preferred_element_type=F32)
    rest = jnp.dot(h, w_ref[:, ODD_W_NQ:], preferred_element_type=F32)

    def low_rank(a, gain_ref):
        ms = jnp.mean(a * a, axis=-1, keepdims=True)
        return (a * lax.rsqrt(ms + EPS) * gain_ref[...]).astype(BF16)

    n_mla = MLA_WIDTH // CHUNK
    chunks_of = lambda a, first, n: [a[:, first + c * CHUNK:first + (c + 1) * CHUNK] for c in range(n)]
    qa = low_rank(lora[:, :MLA_Q_LORA], qa_g_ref)
    qf = jnp.dot(qa, wq_ref[...], preferred_element_type=F32)
    q_chunks = chunks_of(qf, 0, n_mla)
    ms_q = _group_meansq(q_chunks, g128_ref, 1.0 / MLA_QK)
    kva = low_rank(lora[:, ODD_W_KVA:], kva_g_ref)
    kvf = jnp.dot(kva, wkv_ref[...], preferred_element_type=F32)
    kslot = rest[:, ODD_W_SLOT - ODD_W_NQ:]
    kslot2 = jnp.concatenate([kslot, kslot], axis=1)
    k_rot = _swap_quarters(kslot * mk_g_ref[:, :LANES], quarter)
    k_rot2 = jnp.concatenate([k_rot, k_rot], axis=1)
    cos2 = jnp.concatenate([cos, cos], axis=1)
    sin2 = jnp.concatenate([sin, sin], axis=1)
    k_chunks = [kc + kslot2 for kc in chunks_of(kvf, 0, n_mla)]
    ms_k = _group_meansq(k_chunks, g128_ref, 1.0 / MLA_QK)
    for c in range(n_mla):
        y = _norm_rope_chunk(q_chunks[c], ms_q[c], mq_g_ref[...], cos, sin, quarter)
        o_ref[:, c * CHUNK:(c + 1) * CHUNK] = y.astype(BF16)
    for c in range(n_mla):
        inv_rms = lax.rsqrt(ms_k[c] + EPS)
        y = (k_chunks[c] * inv_rms * mk_g_ref[...]) * cos2 + (inv_rms * k_rot2) * sin2
        o_ref[:, ODD_K + c * CHUNK:ODD_K + (c + 1) * CHUNK] = y.astype(BF16)
    o_ref[:, ODD_V:ODD_NQ] = kvf[:, MLA_WIDTH:].astype(BF16)

    n_na = NA_WIDTH // CHUNK
    n_chunks = chunks_of(rest, 0, 2 * n_na)
    ms64 = _group_meansq(n_chunks, g64_ref, 1.0 / HEAD_DIM)
    for c in range(2 * n_na):
        gain = nq_g_ref[...] if c < n_na else nk_g_ref[...]
        y = _norm_rope_chunk(n_chunks[c], ms64[c], gain, None, None, None)
        o_ref[:, ODD_NQ + c * CHUNK:ODD_NQ + (c + 1) * CHUNK] = y.astype(BF16)
    o_ref[:, ODD_NV:ODD_COLS] = rest[:, ODD_W_NV - ODD_W_NQ:ODD_W_SLOT - ODD_W_NQ].astype(BF16)


def _proj_odd(xa, g, sh, sc, w, wq, wkv, gains, g64, g128, cos, sin, bpb, nbatch):
    na, d = xa.shape
    gain_spec, mod_spec = _mod_specs(d, bpb, nbatch)
    tok_specs = _stream_specs((xa,), d, bpb * nbatch) + [gain_spec, mod_spec, mod_spec]
    consts = (w, wq, wkv) + tuple(gains) + (g64, g128)
    return pl.pallas_call(
        _proj_odd_kernel,
        grid=(na // TM,),
        in_specs=tok_specs + [_whole(a.shape) for a in consts] + [_rope_spec(bpb, nbatch)] * 2,
        out_specs=pl.BlockSpec((TM, ODD_COLS), lambda i: (i, 0)),
        out_shape=jax.ShapeDtypeStruct((na, ODD_COLS), BF16),
        compiler_params=_params("parallel"),
        name="proj_odd",
    )(xa, g, sh, sc, *consts, cos, sin)


def _attn_kernel(*refs, mode, split, main, seq, has_sink, lam_init, tq, tq_sub):
    n_sub = tq // tq_sub
    it = iter(refs)
    q_ref = next(it)
    k_ref = v_ref = sink_ref = lam_ref = subln_ref = None
    bias_refs = ()
    if main is not None:
        k_ref, v_ref = next(it), next(it)
    kc_ref, vc_ref = next(it), next(it)
    if main == "na":
        bias_refs = tuple(next(it) for _ in range(n_sub))
    if has_sink:
        sink_ref = next(it)
    safe_ref = next(it)
    if mode == "diff":
        lam_ref, subln_ref = next(it), next(it)
    o_ref = next(it)
    vt_ref, vct_ref = (next(it), next(it)) if main == "full" else (None, None)

    pair = pl.program_id(1)
    qi = pl.program_id(2)
    q = q_ref[...]
    low = lax.broadcasted_iota(jnp.int32, (1, LANES), 1) < HALF
    if split == "mask":
        zero = jnp.zeros_like(q)
        qs = (jnp.where(low, q, zero), jnp.where(low, zero, q))
        k_of = lambda k, s: k
    else:
        qs = (q[:, :LANES], q[:, LANES:])
        k_of = lambda k, s: k[:, s * LANES:(s + 1) * LANES]
    q_of = lambda s, t: qs[s][t * tq_sub:(t + 1) * tq_sub]
    sinks = [sink_ref[2 * pair + s] * LOG2E for s in range(2)] if has_sink else None
    nt_dims = (((1,), (1,)), ((), ()))

    def finish(o_lo, o_hi):
        if mode == "diff":
            lv = lam_ref[...]
            lam = (jnp.exp(jnp.sum(lv[0:1] * lv[1:2], axis=-1, keepdims=True))
                   - jnp.exp(jnp.sum(lv[2:3] * lv[3:4], axis=-1, keepdims=True)) + lam_init)
            o = o_lo - lam * o_hi
            ms = jnp.mean(o * o, axis=-1, keepdims=True)
            o = o * lax.rsqrt(ms + EPS) * subln_ref[...] * (1.0 - lam_init)
        else:
            o = jnp.where(low, o_lo, o_hi)
        o_ref[...] = o.astype(BF16)

    def local_keys(t):
        if main == "window":
            span = tq_sub + 2 * WINDOW
            q0 = qi * tq + t * tq_sub
            w0 = jnp.clip(q0 - WINDOW, 0, seq - span)
            off = pl.multiple_of(w0, WINDOW)
            key = lax.broadcasted_iota(jnp.int32, (span, tq_sub), 0)
            qry = lax.broadcasted_iota(jnp.int32, (span, tq_sub), 1)
            inside = jnp.abs(key - qry + (w0 - q0)) <= WINDOW
            logit_fn = lambda lg_t, s: jnp.where(inside, lg_t, NEG)
        else:
            rows_q = tq_sub // GRID_W
            span_rows = rows_q + NA_ROWS
            w0 = jnp.clip(qi * (tq // GRID_W) + t * rows_q - NA_ROWS // 2, 0, seq // GRID_W - span_rows)
            off = pl.multiple_of(w0 * GRID_W, GRID_W)
            span = span_rows * GRID_W
            logit_fn = lambda lg_t, s: lg_t + bias_refs[t][0, s]
        return k_ref[pl.ds(off, span), :], v_ref[pl.ds(off, span), :], logit_fn

    def online_sub(t):
        def init(s):
            if has_sink:
                m0 = jnp.full((tq_sub, 1), sinks[s], F32)
                l0 = jnp.ones((tq_sub, 1), F32)
            else:
                m0 = jnp.full((tq_sub, 1), NEG, F32)
                l0 = jnp.zeros((tq_sub, 1), F32)
            return m0, l0, jnp.zeros((tq_sub, LANES), F32)

        def segment(state, k, v, logit_fn):
            out = []
            for s in range(2):
                m, l, acc = state[s]
                logits = lax.dot_general(q_of(s, t), k_of(k, s), nt_dims, preferred_element_type=F32)
                if logit_fn is not None:
                    logits = logit_fn(logits.T, s).T
                m_new = jnp.maximum(m, jnp.max(logits, axis=-1, keepdims=True))
                alpha = jnp.exp2(m - m_new)
                p = jnp.exp2(logits - m_new)
                l = alpha * l + jnp.sum(p, axis=-1, keepdims=True)
                acc = alpha * acc + jnp.dot(p.astype(BF16), v, preferred_element_type=F32)
                out.append((m_new, l, acc))
            return tuple(out)

        state = (init(0), init(1))
        if main == "full":
            def body(c, st):
                off = pl.multiple_of(c * TK_FULL, TK_FULL)
                return segment(st, k_ref[pl.ds(off, TK_FULL), :], v_ref[pl.ds(off, TK_FULL), :], None)
            state = lax.fori_loop(0, seq // TK_FULL, body, state)
        elif main is not None:
            state = segment(state, *local_keys(t))
        (_, l_lo, acc_lo), (_, l_hi, acc_hi) = segment(state, kc_ref[...], vc_ref[...], None)
        return acc_lo / l_lo, acc_hi / l_hi

    def with_ones_rows(v):
        return jnp.concatenate([v.T, jnp.ones((ONES_ROWS, v.shape[0]), BF16)], axis=0)

    def probs_t(k, q_rows, s, logit_fn):
        logits_t = lax.dot_general(k_of(k, s), q_rows, nt_dims, preferred_element_type=F32)
        if logit_fn is not None:
            logits_t = logit_fn(logits_t, s)
        return jnp.exp2(logits_t).astype(BF16)

    def max_free():
        acc = None
        if main == "full":
            @pl.when(qi == 0)
            def _():
                ones = jnp.ones((ONES_ROWS, TK_MAX_FREE), BF16)
                for c in range(seq // TK_MAX_FREE):
                    vt_ref[c, :LANES, :] = v_ref[c * TK_MAX_FREE:(c + 1) * TK_MAX_FREE, :].T
                    vt_ref[c, LANES:, :] = ones
                vct_ref[...] = with_ones_rows(vc_ref[...])

            def body(c, acc):
                off = pl.multiple_of(c * TK_MAX_FREE, TK_MAX_FREE)
                k = k_ref[pl.ds(off, TK_MAX_FREE), :]
                return tuple(acc[s] + jnp.dot(vt_ref[c], probs_t(k, qs[s], s, None), preferred_element_type=F32)
                             for s in range(2))

            zero_acc = jnp.zeros((LANES + ONES_ROWS, tq), F32)
            acc = lax.fori_loop(0, seq // TK_MAX_FREE, body, (zero_acc, zero_acc))
            vct = vct_ref[...]
        else:
            vct = with_ones_rows(vc_ref[...])
        p_ctx = [probs_t(kc_ref[...], qs[s], s, None) for s in range(2)]
        if main in ("window", "na"):
            windows = [local_keys(t) for t in range(n_sub)]
            p_loc = [[probs_t(k, q_of(s, t), s, fn) for s in range(2)] for t, (k, _, fn) in enumerate(windows)]
            vts = [with_ones_rows(v) for _, v, _ in windows]
        outs = []
        for s in range(2):
            acc_s = jnp.dot(vct, p_ctx[s], preferred_element_type=F32)
            if main == "full":
                acc_s = acc_s + acc[s]
            elif main is not None:
                acc_s = acc_s + jnp.concatenate(
                    [jnp.dot(vts[t], p_loc[t][s], preferred_element_type=F32) for t in range(n_sub)], axis=1)
            den = acc_s[LANES:LANES + 1]
            if has_sink:
                den = den + jnp.exp2(jnp.full((1, tq), sinks[s], F32))
            outs.append((acc_s[:LANES] / den).T)
        finish(*outs)

    def online():
        parts = [online_sub(t) for t in range(n_sub)]
        finish(*[jnp.concatenate([p[s] for p in parts], axis=0) if n_sub > 1 else parts[0][s]
                 for s in range(2)])

    safe = safe_ref[0] == 1
    pl.when(safe)(max_free)
    pl.when(jnp.logical_not(safe))(online)


def _attention(p, *, name, mode, split, main, n_pairs, q_blk, k_blk, v_blk, out_cols, ctx_queries,
               nbatch, seq, ctx_len, safe, bias=None, sink=None, lam=None, subln=None, lam_init=0.0):
    n_lat = nbatch * seq
    qw = LANES if split == "mask" else 2 * LANES
    if ctx_queries:
        tq, nq = ctx_len, 1
        q_row = lambda b, j, i: n_lat // ctx_len + b
        out_rows = nbatch * ctx_len
        out_row = lambda b, j, i: b
    else:
        tq = TQ_FULL if main == "full" else TQ
        nq = seq // tq
        q_row = lambda b, j, i: b * nq + i
        out_rows = n_lat
        out_row = q_row
    ctx_row = lambda b, j, i: n_lat // ctx_len + b

    in_specs = [pl.BlockSpec((tq, qw), lambda b, j, i: (q_row(b, j, i), q_blk(j)))]
    args = [p]
    if main is not None:
        in_specs += [pl.BlockSpec((seq, qw), lambda b, j, i: (b, k_blk(j))),
                     pl.BlockSpec((seq, LANES), lambda b, j, i: (b, v_blk(j)))]
        args += [p, p]
    in_specs += [pl.BlockSpec((ctx_len, qw), lambda b, j, i: (ctx_row(b, j, i), k_blk(j))),
                 pl.BlockSpec((ctx_len, LANES), lambda b, j, i: (ctx_row(b, j, i), v_blk(j)))]
    args += [p, p]
    tq_sub = TQ_SUB if main in ("window", "na") else tq
    n_sub = tq // tq_sub
    if main == "na":
        span = (tq_sub // GRID_W + NA_ROWS) * GRID_W
        last = nq * n_sub - 1
        cls = lambda g: jnp.where(g == 0, 0, jnp.where(g == last, 2, 1))
        for t in range(n_sub):
            in_specs.append(pl.BlockSpec((1, 2, span, tq_sub),
                                         lambda b, j, i, t=t: (cls(i * n_sub + t), j, 0, 0)))
            args.append(bias)
    if sink is not None:
        in_specs.append(pl.BlockSpec(memory_space=pltpu.SMEM))
        args.append(sink)
    in_specs.append(pl.BlockSpec(memory_space=pltpu.SMEM))
    args.append(safe)
    if mode == "diff":
        in_specs += [_whole(lam.shape), _whole(subln.shape)]
        args += [lam, subln]

    kern = functools.partial(_attn_kernel, mode=mode, split=split, main=main, seq=seq,
                             has_sink=sink is not None, lam_init=lam_init, tq=tq, tq_sub=tq_sub)
    scratch = []
    if main == "full":
        scratch = [pltpu.VMEM((seq // TK_MAX_FREE, LANES + ONES_ROWS, TK_MAX_FREE), BF16),
                   pltpu.VMEM((LANES + ONES_ROWS, ctx_len), BF16)]
    return pl.pallas_call(
        kern,
        grid=(nbatch, n_pairs, nq),
        in_specs=in_specs,
        out_specs=pl.BlockSpec((tq, LANES), lambda b, j, i: (out_row(b, j, i), j)),
        out_shape=jax.ShapeDtypeStruct((out_rows, out_cols), BF16),
        scratch_shapes=scratch,
        compiler_params=_params("parallel", "parallel", "arbitrary"),
        name=name,
    )(*args)


def _na_bias_kernel(rpb_ref, o_ref, *, n_rows, rows_q):
    h = pl.program_id(0)
    n_dr = 2 * NA_ROWS - 1
    n_dc = 2 * NA_COLS - 1
    kc = lax.broadcasted_iota(jnp.int32, (GRID_W, LANES), 0)
    cq = lax.broadcasted_iota(jnp.int32, (GRID_W, LANES), 1) & (GRID_W - 1)
    dc_idx = jnp.clip(kc - cq, -(NA_COLS - 1), NA_COLS - 1) + NA_COLS - 1
    cs = jnp.clip(cq - NA_COLS // 2, 0, GRID_W - NA_COLS)
    col_ok = (kc >= cs) & (kc < cs + NA_COLS)
    tiles = [jnp.zeros((GRID_W, LANES), F32) for _ in range(n_dr)]
    for dc in range(n_dc):
        hit = dc_idx == dc
        for dr in range(n_dr):
            tiles[dr] = jnp.where(hit, rpb_ref[(h * n_dr + dr) * n_dc + dc] * LOG2E, tiles[dr])
    tiles = [jnp.where(col_ok, t, NEG) for t in tiles]
    masked = jnp.full((GRID_W, LANES), NEG, F32)
    low = lax.broadcasted_iota(jnp.int32, (GRID_W, LANES), 1) < HALF

    span_rows = rows_q + NA_ROWS
    first_q_row = (0, span_rows, n_rows - rows_q)
    for c in range(3):
        r0 = first_q_row[c]
        w0 = min(max(r0 - NA_ROWS // 2, 0), n_rows - span_rows)
        for kr in range(span_rows):
            k_abs = w0 + kr
            for m in range(rows_q // 2):
                halves = []
                for r in (r0 + 2 * m, r0 + 2 * m + 1):
                    rs = min(max(r - NA_ROWS // 2, 0), n_rows - NA_ROWS)
                    halves.append(tiles[k_abs - r + NA_ROWS - 1] if rs <= k_abs < rs + NA_ROWS else masked)
                o_ref[c, 0, kr * GRID_W:(kr + 1) * GRID_W, m * LANES:(m + 1) * LANES] = (
                    jnp.where(low, halves[0], halves[1]))


def _na_bias(rpb, seq):
    n_rows = seq // GRID_W
    rows_q = TQ_SUB // GRID_W
    span = (rows_q + NA_ROWS) * GRID_W
    kern = functools.partial(_na_bias_kernel, n_rows=n_rows, rows_q=rows_q)
    return pl.pallas_call(
        kern,
        grid=(NA_HEADS,),
        in_specs=[pl.BlockSpec(memory_space=pltpu.SMEM)],
        out_specs=pl.BlockSpec((3, 1, span, TQ_SUB), lambda h: (0, h, 0, 0)),
        out_shape=jax.ShapeDtypeStruct((3, NA_HEADS, span, TQ_SUB), F32),
        compiler_params=_params("parallel"),
        name="na_bias",
    )(rpb.reshape(-1))


def _post_kernel(*refs, n_x, n_o, n_lat_blocks):
    x_refs, oa_refs, ob_refs = refs[:n_x], refs[n_x:n_x + n_o], refs[n_x + n_o:n_x + 2 * n_o]
    wo_ref, g1_ref, n2_ref, sh_ref, sc_ref, g2_ref, w1_ref, w2_ref, o_ref = refs[n_x + 2 * n_o:]
    oa = _stream_block(oa_refs, n_lat_blocks)
    ob = _stream_block(ob_refs, n_lat_blocks)
    half = oa.shape[1]
    y = (jnp.dot(oa, wo_ref[:half, :], preferred_element_type=F32)
         + jnp.dot(ob, wo_ref[half:, :], preferred_element_type=F32))
    x1 = _stream_block(x_refs, n_lat_blocks) + g1_ref[0] * y
    h = _modulated(x1, n2_ref, sh_ref, sc_ref).astype(BF16)
    a = jnp.dot(h, w1_ref[...], preferred_element_type=F32)
    a = jnp.square(jnp.maximum(a, 0.0)).astype(BF16)
    o_ref[...] = x1 + g2_ref[0] * jnp.dot(a, w2_ref[...], preferred_element_type=F32)


def _post(x_src, oa_src, ob_src, wo, g1, n2, sh, sc, g2, w1, w2, layer, n_blocks, bpb, nbatch):
    d = x_src[0].shape[1]
    half = oa_src[0].shape[1]
    n_lat_blocks = bpb * nbatch
    gain_spec, mod_spec = _mod_specs(d, bpb, nbatch)
    resident = lambda a: pl.BlockSpec((None,) + a.shape[1:], lambda i: (layer, 0, 0),
                                      pipeline_mode=pl.Buffered(1))
    kern = functools.partial(_post_kernel, n_x=len(x_src), n_o=len(oa_src), n_lat_blocks=n_lat_blocks)
    return pl.pallas_call(
        kern,
        grid=(n_blocks,),
        in_specs=(_stream_specs(x_src, d, n_lat_blocks) + _stream_specs(oa_src, half, n_lat_blocks)
                  + _stream_specs(ob_src, half, n_lat_blocks)
                  + [resident(wo), mod_spec, gain_spec, mod_spec, mod_spec, mod_spec, resident(w1), resident(w2)]),
        out_specs=pl.BlockSpec((TM, d), lambda i: (i, 0)),
        out_shape=jax.ShapeDtypeStruct((n_blocks * TM, d), F32),
        compiler_params=_params("parallel"),
        name="post",
    )(*x_src, *oa_src, *ob_src, wo, g1, n2, sh, sc, g2, w1, w2)


def _block_diag_ones(group):
    idx = np.arange(CHUNK) // group
    return jnp.asarray(idx[:, None] == idx[None, :], dtype=BF16)


def _rope_tables(seq, rot_dim, lane0, pad_rows):
    t = jnp.arange(seq, dtype=jnp.int32)
    row = (t // GRID_W).astype(F32)
    col = (t % GRID_W).astype(F32)
    n_freq = rot_dim // 4
    freqs = jnp.power(ROPE_BASE, -jnp.arange(n_freq, dtype=F32) / n_freq)
    ar = row[:, None] * freqs[None, :]
    ac = col[:, None] * freqs[None, :]
    ang = jnp.concatenate([ar, ar, ac, ac], axis=-1)
    sign = jnp.asarray(np.tile(np.repeat([-1.0, 1.0], n_freq), 2), F32)
    cos, sin = jnp.cos(ang), jnp.sin(ang) * sign[None, :]
    if lane0 is None:
        reps = LANES // rot_dim
        cos, sin = jnp.tile(cos, (1, reps)), jnp.tile(sin, (1, reps))
    else:
        pad = ((0, 0), (lane0, LANES - lane0 - rot_dim))
        cos = jnp.pad(cos, pad, constant_values=1.0)
        sin = jnp.pad(sin, pad)
    cos = jnp.concatenate([cos, jnp.ones((pad_rows, LANES), F32)], axis=0)
    sin = jnp.concatenate([sin, jnp.zeros((pad_rows, LANES), F32)], axis=0)
    return cos, sin


def _logits_bounded(q_gain, k_gain, dim, bias=None, sink=None):
    bound = jnp.max(jnp.abs(q_gain)) * jnp.max(jnp.abs(k_gain)) * (dim ** 0.5) * BF16_ROUNDING_MARGIN
    if bias is not None:
        bound = bound + jnp.max(jnp.abs(bias))
    if sink is not None:
        bound = jnp.maximum(bound, jnp.max(jnp.abs(sink)))
    return (bound <= MAX_FREE_LOGIT_BOUND).astype(jnp.int32).reshape(1)


def _tile_gain(g, reps, scale=1.0):
    return (jnp.tile(g.astype(F32), reps) * scale)[None, :]


def _even_weights(w, q_g, k_g, sq_g, sk_g):
    nq = DIFF_HEADS * 2 * HEAD_DIM
    qa, ka, va = w[:, :nq], w[:, nq:2 * nq], w[:, 2 * nq:3 * nq]
    o = 3 * nq
    qb = w[:, o:o + SWA_Q_HEADS * HEAD_DIM]
    o += SWA_Q_HEADS * HEAD_DIM
    kb = [w[:, o + i * HEAD_DIM:o + (i + 1) * HEAD_DIM] for i in range(SWA_KV_HEADS)]
    o += SWA_KV_HEADS * HEAD_DIM
    vb = [w[:, o + i * HEAD_DIM:o + (i + 1) * HEAD_DIM] for i in range(SWA_KV_HEADS)]
    dup = lambda parts: [p for p in parts for _ in range(2)]
    w_new = jnp.concatenate([qa, ka, va, qb] + dup(kb) + dup(vb), axis=1).astype(BF16)
    scale = HEAD_DIM ** -0.5 * LOG2E
    ones = lambda n: jnp.ones((1, n), F32)
    gain = jnp.concatenate([
        _tile_gain(q_g, 2 * DIFF_HEADS, scale), _tile_gain(k_g, 2 * DIFF_HEADS), ones(nq),
        _tile_gain(sq_g, SWA_Q_HEADS, scale), _tile_gain(sk_g, 2 * SWA_KV_HEADS),
        ones(2 * SWA_KV_HEADS * HEAD_DIM)], axis=1)
    return w_new, gain


def _odd_weights(w, wq_up, wkv_up, qa_g, kva_g, mq_g, mk_g, nq_g, nk_g):
    d = w.shape[0]
    o = 0
    parts = []
    for n in (MLA_Q_LORA, MLA_KV_LORA, MLA_ROPE, NA_HEADS * HEAD_DIM, NA_HEADS * HEAD_DIM, NA_HEADS * HEAD_DIM):
        parts.append(w[:, o:o + n])
        o += n
    q_a, kv_a, k_r, nq, nk, nv = parts
    slot = jnp.concatenate([jnp.zeros((d, MLA_NOPE), w.dtype), k_r,
                            jnp.zeros((d, LANES - MLA_QK), w.dtype)], axis=1)
    w_new = jnp.concatenate([q_a, kv_a, nq, nk, nv, slot], axis=1).astype(BF16)
    wq = jnp.pad(wq_up.reshape(MLA_Q_LORA, MLA_HEADS, MLA_QK), ((0, 0), (0, 0), (0, MLA_PAD - MLA_QK)))
    wq = wq.reshape(MLA_Q_LORA, MLA_HEADS * MLA_PAD).astype(BF16)
    wkv = wkv_up.reshape(MLA_KV_LORA, MLA_HEADS, MLA_NOPE + MLA_V)
    wk = jnp.pad(wkv[..., :MLA_NOPE], ((0, 0), (0, 0), (0, MLA_PAD - MLA_NOPE))).reshape(MLA_KV_LORA, -1)
    wv = wkv[..., MLA_NOPE:].reshape(MLA_KV_LORA, MLA_HEADS * MLA_V)
    wkv_new = jnp.concatenate([wk, wv], axis=1).astype(BF16)
    pad_gain = lambda g, scale: _tile_gain(jnp.pad(g.astype(F32), (0, MLA_PAD - MLA_QK)), CHUNK // MLA_PAD, scale)
    gains = (qa_g.astype(F32)[None, :], kva_g.astype(F32)[None, :],
             pad_gain(mq_g, MLA_QK ** -0.5 * LOG2E), pad_gain(mk_g, 1.0),
             _tile_gain(nq_g, CHUNK // HEAD_DIM, HEAD_DIM ** -0.5 * LOG2E), _tile_gain(nk_g, CHUNK // HEAD_DIM))
    return w_new, wq, wkv_new, gains


def kernel(x, c, ctx, c_ctx, ada_w, ada_b, norm1_g, norm2_g, w_out, mlp_w1, mlp_w2, ev_w_in, diff_q_g,
           diff_k_g, diff_lam, diff_subln_g, swa_q_g, swa_k_g, swa_sink, od_w_in, mla_qa_g, mla_kva_g,
           mla_wq_up, mla_wkv_up, mla_q_g, mla_k_g, na_q_g, na_k_g, na_rpb):
    nbatch, seq, d = x.shape
    ctx_len = ctx.shape[1]
    depth = ada_w.shape[0]
    n_lat = nbatch * seq
    n_ctx = nbatch * ctx_len
    assert seq % TM == 0 and n_ctx == TM and seq % TQ == 0 and seq % GRID_W == 0
    assert seq // GRID_W >= 2 * (NA_ROWS + TQ_SUB // GRID_W) and seq >= TQ_SUB + 2 * WINDOW
    bpb = seq // TM

    mod_rows = 8
    cvec = jnp.concatenate([c, c_ctx[None, :], jnp.zeros((mod_rows - nbatch - 1, d), F32)], axis=0)
    mod = _modulation(cvec, ada_w, ada_b)

    cos64, sin64 = _rope_tables(seq, HEAD_DIM, None, n_ctx)
    cos32, sin32 = _rope_tables(seq, MLA_ROPE, MLA_NOPE, n_ctx)
    g64 = _block_diag_ones(HEAD_DIM)
    g128 = _block_diag_ones(LANES)

    x_src = (x.reshape(n_lat, d), ctx.reshape(n_ctx, d))
    wo_all, w1_all, w2_all = w_out.astype(BF16), mlp_w1.astype(BF16), mlp_w2.astype(BF16)
    common = dict(nbatch=nbatch, seq=seq, ctx_len=ctx_len)

    for layer in range(depth):
        need_ctx = layer < depth - 1
        n_blocks = bpb * nbatch + (1 if need_ctx else 0)
        m = mod[layer, :nbatch + 1]
        sh1, sc1, g1, sh2, sc2, g2 = [m[:, k * d:(k + 1) * d].reshape(nbatch + 1, 1, d) for k in range(6)]
        n1 = norm1_g[layer][None, :]
        n2 = norm2_g[layer][None, :]
        i = layer // 2
        if layer % 2 == 0:
            lam_init = 0.8 - 0.6 * math.exp(-0.3 * layer)
            w_in, gain = _even_weights(ev_w_in[i], diff_q_g[i], diff_k_g[i], swa_q_g[i], swa_k_g[i])
            p = _proj_even(x_src, n1, sh1, sc1, w_in, gain, g64, cos64, sin64, bpb, nbatch)
            lam = diff_lam[i].astype(F32)
            subln = diff_subln_g[i].astype(F32)[None, :]
            sink = swa_sink[i].astype(F32)
            diff_kw = dict(mode="diff", split="mask", n_pairs=DIFF_HEADS, q_blk=lambda j: EVEN_QA // LANES + j,
                           k_blk=lambda j: EVEN_KA // LANES + j, v_blk=lambda j: EVEN_VA // LANES + j,
                           out_cols=DIFF_HEADS * LANES,
                           lam=lam, subln=subln, lam_init=lam_init,
                           safe=_logits_bounded(diff_q_g[i], diff_k_g[i], HEAD_DIM), **common)
            swa_kw = dict(mode="pair", split="mask", n_pairs=SWA_Q_HEADS // 2,
                          q_blk=lambda j: EVEN_QB // LANES + j, k_blk=lambda j: EVEN_KB // LANES + j // 2,
                          v_blk=lambda j: EVEN_VB // LANES + j // 2,
                          out_cols=SWA_Q_HEADS * HEAD_DIM, sink=sink,
                          safe=_logits_bounded(swa_q_g[i], swa_k_g[i], HEAD_DIM, sink=sink), **common)
            oa = _attention(p, name="diff_attn", main="full", ctx_queries=False, **diff_kw)
            ob = _attention(p, name="window_attn", main="window", ctx_queries=False, **swa_kw)
            if need_ctx:
                oa_c = _attention(p, name="diff_attn_ctx", main=None, ctx_queries=True, **diff_kw)
                ob_c = _attention(p, name="window_attn_ctx", main=None, ctx_queries=True, **swa_kw)
        else:
            w_in, wq, wkv, gains = _odd_weights(od_w_in[i], mla_wq_up[i], mla_wkv_up[i], mla_qa_g[i],
                                                mla_kva_g[i], mla_q_g[i], mla_k_g[i], na_q_g[i], na_k_g[i])
            p = _proj_odd(x_src[0], n1, sh1, sc1, w_in, wq, wkv, gains, g64, g128, cos32, sin32, bpb, nbatch)
            bias = _na_bias(na_rpb[i].astype(F32), seq)
            mla_kw = dict(mode="pair", split="slice", n_pairs=MLA_HEADS // 2, q_blk=lambda j: j,
                          k_blk=lambda j: ODD_K // CHUNK + j, v_blk=lambda j: ODD_V // LANES + j,
                          out_cols=MLA_HEADS * MLA_V,
                          safe=_logits_bounded(mla_q_g[i], mla_k_g[i], MLA_QK), **common)
            na_kw = dict(mode="pair", split="mask", n_pairs=NA_HEADS // 2,
                         q_blk=lambda j: ODD_NQ // LANES + j, k_blk=lambda j: ODD_NK // LANES + j,
                         v_blk=lambda j: ODD_NV // LANES + j, out_cols=NA_HEADS * HEAD_DIM,
                         safe=_logits_bounded(na_q_g[i], na_k_g[i], HEAD_DIM, bias=na_rpb[i]), **common)
            oa = _attention(p, name="mla_attn", main="full", ctx_queries=False, **mla_kw)
            ob = _attention(p, name="na_attn", main="na", ctx_queries=False, bias=bias, **na_kw)
            if need_ctx:
                oa_c = _attention(p, name="mla_attn_ctx", main=None, ctx_queries=True, **mla_kw)
                ob_c = _attention(p, name="na_attn_ctx", main=None, ctx_queries=True, **na_kw)
        oa_src, ob_src = ((oa, oa_c), (ob, ob_c)) if need_ctx else ((oa,), (ob,))
        x_src = (_post(x_src, oa_src, ob_src, wo_all, g1, n2, sh2, sc2, g2, w1_all, w2_all,
                       layer, n_blocks, bpb, nbatch),)
    return x_src[0].reshape(nbatch, seq, d)
```

```python
import functools
import math

import numpy as np
import jax
import jax.numpy as jnp
from jax import lax
from jax.experimental import pallas as pl
from jax.experimental.pallas import tpu as pltpu

F32 = jnp.float32
BF16 = jnp.bfloat16

LANES = 128
HALF = LANES // 2
CHUNK = 2 * LANES
GRID_W = 64
HEAD_DIM = 64
EPS = 1e-6
ROPE_BASE = 10000.0
DIFF_HEADS = 4
SWA_Q_HEADS = 8
SWA_KV_HEADS = 2
WINDOW = 128
MLA_HEADS = 8
MLA_Q_LORA = 512
MLA_KV_LORA = 256
MLA_NOPE = 64
MLA_ROPE = 32
MLA_QK = MLA_NOPE + MLA_ROPE
MLA_V = 64
NA_HEADS = 8
NA_ROWS = 8
NA_COLS = 16
NEG = -1e30
LOG2E = math.log2(math.e)
MAX_FREE_LOGIT_BOUND = 50.0
BF16_ROUNDING_MARGIN = 1.02

TM = 512
TQ = 2048
TQ_SUB = 256
TQ_FULL = 1024
TK_FULL = 512
TK_MAX_FREE = 4096
ONES_ROWS = 16
MOD_TN = 1536
VMEM_LIMIT = 56 * 1024 * 1024


def _params(*sem):
    return pltpu.CompilerParams(dimension_semantics=sem, vmem_limit_bytes=VMEM_LIMIT)


def _mod_kernel(c_ref, w_ref, b_ref, o_ref):
    c = c_ref[...]
    a = (c * jax.nn.sigmoid(c)).astype(BF16)
    w = w_ref[0].astype(BF16)
    o_ref[0] = jnp.dot(a, w, preferred_element_type=F32) + b_ref[0]


def _modulation(cvec, ada_w, ada_b):
    depth, d, n = ada_w.shape
    rows = cvec.shape[0]
    return pl.pallas_call(
        _mod_kernel,
        grid=(depth, n // MOD_TN),
        in_specs=[
            pl.BlockSpec((rows, d), lambda l, j: (0, 0)),
            pl.BlockSpec((1, d, MOD_TN), lambda l, j: (l, 0, j)),
            pl.BlockSpec((1, 1, MOD_TN), lambda l, j: (l, 0, j)),
        ],
        out_specs=pl.BlockSpec((1, rows, MOD_TN), lambda l, j: (l, 0, j)),
        out_shape=jax.ShapeDtypeStruct((depth, rows, n), F32),
        compiler_params=_params("parallel", "parallel"),
        name="modulation",
    )(cvec, ada_w, ada_b.reshape(depth, 1, n))


def _modulated(x, g_ref, shift, scale):
    ms = jnp.mean(x * x, axis=-1, keepdims=True)
    y = x * lax.rsqrt(ms + EPS) * g_ref[...]
    return y * (1.0 + scale) + shift


def _group_meansq(chunks, g_ref, inv_n):
    rows = chunks[0].shape[0]
    sq = jnp.concatenate([(c * c).astype(BF16) for c in chunks], axis=0)
    ss = jnp.dot(sq, g_ref[...], preferred_element_type=F32) * inv_n
    return [ss[i * rows:(i + 1) * rows] for i in range(len(chunks))]


def _swap_quarters(x, quarter):
    lane = lax.broadcasted_iota(jnp.int32, x.shape, 1)
    odd = (lane & quarter) != 0
    from_lower = pltpu.roll(x, quarter, 1)
    from_upper = pltpu.roll(x, LANES - quarter, 1)
    return jnp.where(odd, from_lower, from_upper)


def _rope(x, cos, sin_signed, quarter):
    return x * cos + _swap_quarters(x, quarter) * sin_signed


def _norm_rope_chunk(acc, meansq, gain, cos, sin, quarter):
    y = acc * lax.rsqrt(meansq + EPS) * gain
    if quarter is None:
        return y
    return jnp.concatenate([_rope(y[:, :LANES], cos, sin, quarter),
                            _rope(y[:, LANES:], cos, sin, quarter)], axis=1)


def _stream_specs(src, cols, n_lat_blocks):
    if len(src) == 1:
        return [pl.BlockSpec((TM, cols), lambda i: (i, 0))]
    return [pl.BlockSpec((TM, cols), lambda i: (jnp.minimum(i, n_lat_blocks - 1), 0)),
            pl.BlockSpec((TM, cols), lambda i: (0, 0))]


def _stream_block(refs, n_lat_blocks):
    if len(refs) == 1:
        return refs[0][...]
    return jnp.where(pl.program_id(0) < n_lat_blocks, refs[0][...], refs[1][...])


SHIFT1, SCALE1, GATE1, SHIFT2, SCALE2, GATE2 = range(6)


def _mod_specs(mod, layer, chunks):
    d = mod.shape[2] // 6
    return [pl.BlockSpec((None, mod.shape[1], d), lambda i, k=k: (layer, 0, k)) for k in chunks]


def _mod_row(ref, bpb, nbatch):
    row = jnp.minimum(pl.program_id(0) // bpb, nbatch)
    return ref[pl.ds(row, 1), :]


def _rope_spec(bpb, nbatch):
    return pl.BlockSpec((TM, LANES), lambda i: (jnp.where(i < bpb * nbatch, i % bpb, bpb), 0))


def _whole(shape):
    return pl.BlockSpec(shape, lambda *_: (0,) * len(shape))


EVEN_QA = 0
EVEN_KA = EVEN_QA + DIFF_HEADS * 2 * HEAD_DIM
EVEN_VA = EVEN_KA + DIFF_HEADS * 2 * HEAD_DIM
EVEN_QB = EVEN_VA + DIFF_HEADS * 2 * HEAD_DIM
EVEN_KB = EVEN_QB + SWA_Q_HEADS * HEAD_DIM
EVEN_VB = EVEN_KB + 2 * SWA_KV_HEADS * HEAD_DIM
EVEN_COLS = EVEN_VB + 2 * SWA_KV_HEADS * HEAD_DIM
EVEN_NORM_CHUNKS = tuple(c for c in range(EVEN_COLS // CHUNK)
                         if c * CHUNK < EVEN_VA or EVEN_QB <= c * CHUNK < EVEN_VB)
EVEN_DOT_RANGES = ((EVEN_QA // CHUNK, EVEN_VA // CHUNK), (EVEN_QB // CHUNK, EVEN_VB // CHUNK),
                   (EVEN_VA // CHUNK, EVEN_QB // CHUNK), (EVEN_VB // CHUNK, EVEN_COLS // CHUNK))


def _proj_even_kernel(*refs, n_x, bpb, nbatch):
    n_lat_blocks = bpb * nbatch
    x_refs = refs[:n_x]
    g_ref, sh_ref, sc_ref, w_ref, gain_ref, g64_ref, cos_ref, sin_ref, o_ref = refs[n_x:]
    h = _modulated(_stream_block(x_refs, n_lat_blocks), g_ref, _mod_row(sh_ref, bpb, nbatch),
                   _mod_row(sc_ref, bpb, nbatch)).astype(BF16)
    cos = cos_ref[...]
    sin = sin_ref[...]
    for first, last in EVEN_DOT_RANGES:
        full = jnp.dot(h, w_ref[:, first * CHUNK:last * CHUNK], preferred_element_type=F32)
        chunk = lambda c: full[:, (c - first) * CHUNK:(c - first + 1) * CHUNK]
        normed = [c for c in range(first, last) if c in EVEN_NORM_CHUNKS]
        meansq = {}
        if normed:
            meansq = dict(zip(normed, _group_meansq([chunk(c) for c in normed], g64_ref, 1.0 / HEAD_DIM)))
        for c in range(first, last):
            cols = slice(c * CHUNK, (c + 1) * CHUNK)
            acc = chunk(c)
            if c in EVEN_NORM_CHUNKS:
                acc = _norm_rope_chunk(acc, meansq[c], gain_ref[:, cols], cos, sin, HEAD_DIM // 4)
            o_ref[:, cols] = acc.astype(BF16)


def _proj_even(x_src, g, mod, layer, w, gain, g64, cos, sin, bpb, nbatch):
    d = x_src[0].shape[1]
    n_lat_blocks = bpb * nbatch
    kern = functools.partial(_proj_even_kernel, n_x=len(x_src), bpb=bpb, nbatch=nbatch)
    return pl.pallas_call(
        kern,
        grid=(n_lat_blocks + 1,),
        in_specs=_stream_specs(x_src, d, n_lat_blocks) + [_whole(g.shape)] + _mod_specs(mod, layer, (SHIFT1, SCALE1)) + [
            _whole(w.shape), _whole(gain.shape), _whole(g64.shape),
            _rope_spec(bpb, nbatch), _rope_spec(bpb, nbatch),
        ],
        out_specs=pl.BlockSpec((TM, EVEN_COLS), lambda i: (i, 0)),
        out_shape=jax.ShapeDtypeStruct(((n_lat_blocks + 1) * TM, EVEN_COLS), BF16),
        compiler_params=_params("parallel"),
        name="proj_even",
    )(*x_src, g, mod, mod, w, gain, g64, cos, sin)


NA_WIDTH = NA_HEADS * HEAD_DIM
MLA_PAD = LANES
MLA_WIDTH = MLA_HEADS * MLA_PAD
ODD_W_KVA = MLA_Q_LORA
ODD_W_NQ = ODD_W_KVA + MLA_KV_LORA
ODD_W_NV = ODD_W_NQ + 2 * NA_WIDTH
ODD_W_SLOT = ODD_W_NV + NA_WIDTH
ODD_W_COLS = ODD_W_SLOT + LANES
ODD_K = MLA_WIDTH
ODD_V = ODD_K + MLA_WIDTH
ODD_NQ = ODD_V + MLA_HEADS * MLA_V
ODD_NK = ODD_NQ + NA_WIDTH
ODD_NV = ODD_NK + NA_WIDTH
ODD_COLS = ODD_NV + NA_WIDTH


def _proj_odd_kernel(x_ref, g_ref, sh_ref, sc_ref, w_ref, wq_ref, wkv_ref, qa_g_ref, kva_g_ref,
                     mq_g_ref, mk_g_ref, nq_g_ref, nk_g_ref, g64_ref, g128_ref, cos_ref, sin_ref, o_ref,
                     *, bpb, nbatch):
    h = _modulated(x_ref[...], g_ref, _mod_row(sh_ref, bpb, nbatch), _mod_row(sc_ref, bpb, nbatch)).astype(BF16)
    cos = cos_ref[...]
    sin = sin_ref[...]
    quarter = MLA_ROPE // 4

    lora = jnp.dot(h, w_ref[:, :ODD_W_NQ], preferred_element_type=F32)
    rest = jnp.dot(h, w_ref[:, ODD_W_NQ:], preferred_element_type=F32)

    def low_rank(a, gain_ref):
        ms = jnp.mean(a * a, axis=-1, keepdims=True)
        return (a * lax.rsqrt(ms + EPS) * gain_ref[...]).astype(BF16)

    n_mla = MLA_WIDTH // CHUNK
    chunks_of = lambda a, first, n: [a[:, first + c * CHUNK:first + (c + 1) * CHUNK] for c in range(n)]
    qa = low_rank(lora[:, :MLA_Q_LORA], qa_g_ref)
    qf = jnp.dot(qa, wq_ref[...], preferred_element_type=F32)
    q_chunks = chunks_of(qf, 0, n_mla)
    ms_q = _group_meansq(q_chunks, g128_ref, 1.0 / MLA_QK)
    kva = low_rank(lora[:, ODD_W_KVA:], kva_g_ref)
    kvf = jnp.dot(kva, wkv_ref[...], preferred_element_type=F32)
    kslot = rest[:, ODD_W_SLOT - ODD_W_NQ:]
    kslot2 = jnp.concatenate([kslot, kslot], axis=1)
    k_rot = _swap_quarters(kslot * mk_g_ref[:, :LANES], quarter)
    k_rot2 = jnp.concatenate([k_rot, k_rot], axis=1)
    cos2 = jnp.concatenate([cos, cos], axis=1)
    sin2 = jnp.concatenate([sin, sin], axis=1)
    k_chunks = [kc + kslot2 for kc in chunks_of(kvf, 0, n_mla)]
    ms_k = _group_meansq(k_chunks, g128_ref, 1.0 / MLA_QK)
    for c in range(n_mla):
        y = _norm_rope_chunk(q_chunks[c], ms_q[c], mq_g_ref[...], cos, sin, quarter)
        o_ref[:, c * CHUNK:(c + 1) * CHUNK] = y.astype(BF16)
    for c in range(n_mla):
        inv_rms = lax.rsqrt(ms_k[c] + EPS)
        y = (k_chunks[c] * inv_rms * mk_g_ref[...]) * cos2 + (inv_rms * k_rot2) * sin2
        o_ref[:, ODD_K + c * CHUNK:ODD_K + (c + 1) * CHUNK] = y.astype(BF16)
    o_ref[:, ODD_V:ODD_NQ] = kvf[:, MLA_WIDTH:].astype(BF16)

    n_na = NA_WIDTH // CHUNK
    n_chunks = chunks_of(rest, 0, 2 * n_na)
    ms64 = _group_meansq(n_chunks, g64_ref, 1.0 / HEAD_DIM)
    for c in range(2 * n_na):
        gain = nq_g_ref[...] if c < n_na else nk_g_ref[...]
        y = _norm_rope_chunk(n_chunks[c], ms64[c], gain, None, None, None)
        o_ref[:, ODD_NQ + c * CHUNK:ODD_NQ + (c + 1) * CHUNK] = y.astype(BF16)
    o_ref[:, ODD_NV:ODD_COLS] = rest[:, ODD_W_NV - ODD_W_NQ:ODD_W_SLOT - ODD_W_NQ].astype(BF16)


def _proj_odd(xa, g, mod, layer, w, wq, wkv, gains, g64, g128, cos, sin, bpb, nbatch):
    na, d = xa.shape
    tok_specs = _stream_specs((xa,), d, bpb * nbatch) + [_whole(g.shape)] + _mod_specs(mod, layer, (SHIFT1, SCALE1))
    consts = (w, wq, wkv) + tuple(gains) + (g64, g128)
    return pl.pallas_call(
        functools.partial(_proj_odd_kernel, bpb=bpb, nbatch=nbatch),
        grid=(na // TM,),
        in_specs=tok_specs + [_whole(a.shape) for a in consts] + [_rope_spec(bpb, nbatch)] * 2,
        out_specs=pl.BlockSpec((TM, ODD_COLS), lambda i: (i, 0)),
        out_shape=jax.ShapeDtypeStruct((na, ODD_COLS), BF16),
        compiler_params=_params("parallel"),
        name="proj_odd",
    )(xa, g, mod, mod, *consts, cos, sin)


def _attn_kernel(*refs, mode, split, main, seq, has_sink, lam_init, tq, tq_sub):
    n_sub = tq // tq_sub
    it = iter(refs)
    q_ref = next(it)
    k_ref = v_ref = sink_ref = lam_ref = subln_ref = None
    bias_refs = ()
    if main is not None:
        k_ref, v_ref = next(it), next(it)
    kc_ref, vc_ref = next(it), next(it)
    if main == "na":
        bias_refs = tuple(next(it) for _ in range(n_sub))
    if has_sink:
        sink_ref = next(it)
    safe_ref = next(it)
    if mode == "diff":
        lam_ref, subln_ref = next(it), next(it)
    o_ref = next(it)
    vt_ref, vct_ref = (next(it), next(it)) if main == "full" else (None, None)

    pair = pl.program_id(1)
    qi = pl.program_id(2)
    q = q_ref[...]
    low = lax.broadcasted_iota(jnp.int32, (1, LANES), 1) < HALF
    if split == "mask":
        zero = jnp.zeros_like(q)
        qs = (jnp.where(low, q, zero), jnp.where(low, zero, q))
        k_of = lambda k, s: k
    else:
        qs = (q[:, :LANES], q[:, LANES:])
        k_of = lambda k, s: k[:, s * LANES:(s + 1) * LANES]
    q_of = lambda s, t: qs[s][t * tq_sub:(t + 1) * tq_sub]
    sinks = [sink_ref[2 * pair + s] * LOG2E for s in range(2)] if has_sink else None
    nt_dims = (((1,), (1,)), ((), ()))

    def finish(o_lo, o_hi):
        if mode == "diff":
            lv = lam_ref[...]
            lam = (jnp.exp(jnp.sum(lv[0:1] * lv[1:2], axis=-1, keepdims=True))
                   - jnp.exp(jnp.sum(lv[2:3] * lv[3:4], axis=-1, keepdims=True)) + lam_init)
            o = o_lo - lam * o_hi
            ms = jnp.mean(o * o, axis=-1, keepdims=True)
            o = o * lax.rsqrt(ms + EPS) * subln_ref[...] * (1.0 - lam_init)
        else:
            o = jnp.where(low, o_lo, o_hi)
        o_ref[...] = o.astype(BF16)

    def local_keys(t):
        if main == "window":
            span = tq_sub + 2 * WINDOW
            q0 = qi * tq + t * tq_sub
            w0 = jnp.clip(q0 - WINDOW, 0, seq - span)
            off = pl.multiple_of(w0, WINDOW)
            key = lax.broadcasted_iota(jnp.int32, (span, tq_sub), 0)
            qry = lax.broadcasted_iota(jnp.int32, (span, tq_sub), 1)
            inside = jnp.abs(key - qry + (w0 - q0)) <= WINDOW
            logit_fn = lambda lg_t, s: jnp.where(inside, lg_t, NEG)
        else:
            rows_q = tq_sub // GRID_W
            span_rows = rows_q + NA_ROWS
            w0 = jnp.clip(qi * (tq // GRID_W) + t * rows_q - NA_ROWS // 2, 0, seq // GRID_W - span_rows)
            off = pl.multiple_of(w0 * GRID_W, GRID_W)
            span = span_rows * GRID_W
            logit_fn = lambda lg_t, s: lg_t + bias_refs[t][0, s]
        return k_ref[pl.ds(off, span), :], v_ref[pl.ds(off, span), :], logit_fn

    def online_sub(t):
        def init(s):
            if has_sink:
                m0 = jnp.full((tq_sub, 1), sinks[s], F32)
                l0 = jnp.ones((tq_sub, 1), F32)
            else:
                m0 = jnp.full((tq_sub, 1), NEG, F32)
                l0 = jnp.zeros((tq_sub, 1), F32)
            return m0, l0, jnp.zeros((tq_sub, LANES), F32)

        def segment(state, k, v, logit_fn):
            out = []
            for s in range(2):
                m, l, acc = state[s]
                logits = lax.dot_general(q_of(s, t), k_of(k, s), nt_dims, preferred_element_type=F32)
                if logit_fn is not None:
                    logits = logit_fn(logits.T, s).T
                m_new = jnp.maximum(m, jnp.max(logits, axis=-1, keepdims=True))
                alpha = jnp.exp2(m - m_new)
                p = jnp.exp2(logits - m_new)
                l = alpha * l + jnp.sum(p, axis=-1, keepdims=True)
                acc = alpha * acc + jnp.dot(p.astype(BF16), v, preferred_element_type=F32)
                out.append((m_new, l, acc))
            return tuple(out)

        state = (init(0), init(1))
        if main == "full":
            def body(c, st):
                off = pl.multiple_of(c * TK_FULL, TK_FULL)
                return segment(st, k_ref[pl.ds(off, TK_FULL), :], v_ref[pl.ds(off, TK_FULL), :], None)
            state = lax.fori_loop(0, seq // TK_FULL, body, state)
        elif main is not None:
            state = segment(state, *local_keys(t))
        (_, l_lo, acc_lo), (_, l_hi, acc_hi) = segment(state, kc_ref[...], vc_ref[...], None)
        return acc_lo / l_lo, acc_hi / l_hi

    def with_ones_rows(v):
        return jnp.concatenate([v.T, jnp.ones((ONES_ROWS, v.shape[0]), BF16)], axis=0)

    def probs_t(k, q_rows, s, logit_fn):
        logits_t = lax.dot_general(k_of(k, s), q_rows, nt_dims, preferred_element_type=F32)
        if logit_fn is not None:
            logits_t = logit_fn(logits_t, s)
        return jnp.exp2(logits_t).astype(BF16)

    def max_free():
        acc = None
        if main == "full":
            @pl.when(qi == 0)
            def _():
                ones = jnp.ones((ONES_ROWS, TK_MAX_FREE), BF16)
                for c in range(seq // TK_MAX_FREE):
                    vt_ref[c, :LANES, :] = v_ref[c * TK_MAX_FREE:(c + 1) * TK_MAX_FREE, :].T
                    vt_ref[c, LANES:, :] = ones
                vct_ref[...] = with_ones_rows(vc_ref[...])

            def body(c, acc):
                off = pl.multiple_of(c * TK_MAX_FREE, TK_MAX_FREE)
                k = k_ref[pl.ds(off, TK_MAX_FREE), :]
                return tuple(acc[s] + jnp.dot(vt_ref[c], probs_t(k, qs[s], s, None), preferred_element_type=F32)
                             for s in range(2))

            zero_acc = jnp.zeros((LANES + ONES_ROWS, tq), F32)
            acc = lax.fori_loop(0, seq // TK_MAX_FREE, body, (zero_acc, zero_acc))
            vct = vct_ref[...]
        else:
            vct = with_ones_rows(vc_ref[...])
        p_ctx = [probs_t(kc_ref[...], qs[s], s, None) for s in range(2)]
        if main in ("window", "na"):
            windows = [local_keys(t) for t in range(n_sub)]
            p_loc = [[probs_t(k, q_of(s, t), s, fn) for s in range(2)] for t, (k, _, fn) in enumerate(windows)]
            vts = [with_ones_rows(v) for _, v, _ in windows]
        outs = []
        for s in range(2):
            acc_s = jnp.dot(vct, p_ctx[s], preferred_element_type=F32)
            if main == "full":
                acc_s = acc_s + acc[s]
            elif main is not None:
                acc_s = acc_s + jnp.concatenate(
                    [jnp.dot(vts[t], p_loc[t][s], preferred_element_type=F32) for t in range(n_sub)], axis=1)
            den = acc_s[LANES:LANES + 1]
            if has_sink:
                den = den + jnp.exp2(jnp.full((1, tq), sinks[s], F32))
            outs.append((acc_s[:LANES] / den).T)
        finish(*outs)

    def online():
        parts = [online_sub(t) for t in range(n_sub)]
        finish(*[jnp.concatenate([p[s] for p in parts], axis=0) if n_sub > 1 else parts[0][s]
                 for s in range(2)])

    safe = safe_ref[0] == 1
    pl.when(safe)(max_free)
    pl.when(jnp.logical_not(safe))(online)


def _attention(p, *, name, mode, split, main, n_pairs, q_blk, k_blk, v_blk, out_cols, ctx_queries,
               nbatch, seq, ctx_len, safe, bias=None, sink=None, lam=None, subln=None, lam_init=0.0):
    n_lat = nbatch * seq
    qw = LANES if split == "mask" else 2 * LANES
    if ctx_queries:
        tq, nq = ctx_len, 1
        q_row = lambda b, j, i: n_lat // ctx_len + b
        out_rows = nbatch * ctx_len
        out_row = lambda b, j, i: b
    else:
        tq = TQ_FULL if main == "full" else TQ
        nq = seq // tq
        q_row = lambda b, j, i: b * nq + i
        out_rows = n_lat
        out_row = q_row
    ctx_row = lambda b, j, i: n_lat // ctx_len + b

    in_specs = [pl.BlockSpec((tq, qw), lambda b, j, i: (q_row(b, j, i), q_blk(j)))]
    args = [p]
    if main is not None:
        in_specs += [pl.BlockSpec((seq, qw), lambda b, j, i: (b, k_blk(j))),
                     pl.BlockSpec((seq, LANES), lambda b, j, i: (b, v_blk(j)))]
        args += [p, p]
    in_specs += [pl.BlockSpec((ctx_len, qw), lambda b, j, i: (ctx_row(b, j, i), k_blk(j))),
                 pl.BlockSpec((ctx_len, LANES), lambda b, j, i: (ctx_row(b, j, i), v_blk(j)))]
    args += [p, p]
    tq_sub = TQ_SUB if main in ("window", "na") else tq
    n_sub = tq // tq_sub
    if main == "na":
        span = (tq_sub // GRID_W + NA_ROWS) * GRID_W
        last = nq * n_sub - 1
        cls = lambda g: jnp.where(g == 0, 0, jnp.where(g == last, 2, 1))
        for t in range(n_sub):
            in_specs.append(pl.BlockSpec((1, 2, span, tq_sub),
                                         lambda b, j, i, t=t: (cls(i * n_sub + t), j, 0, 0)))
            args.append(bias)
    if sink is not None:
        in_specs.append(pl.BlockSpec(memory_space=pltpu.SMEM))
        args.append(sink)
    in_specs.append(pl.BlockSpec(memory_space=pltpu.SMEM))
    args.append(safe)
    if mode == "diff":
        in_specs += [_whole(lam.shape), _whole(subln.shape)]
        args += [lam, subln]

    kern = functools.partial(_attn_kernel, mode=mode, split=split, main=main, seq=seq,
                             has_sink=sink is not None, lam_init=lam_init, tq=tq, tq_sub=tq_sub)
    scratch = []
    if main == "full":
        scratch = [pltpu.VMEM((seq // TK_MAX_FREE, LANES + ONES_ROWS, TK_MAX_FREE), BF16),
                   pltpu.VMEM((LANES + ONES_ROWS, ctx_len), BF16)]
    return pl.pallas_call(
        kern,
        grid=(nbatch, n_pairs, nq),
        in_specs=in_specs,
        out_specs=pl.BlockSpec((tq, LANES), lambda b, j, i: (out_row(b, j, i), j)),
        out_shape=jax.ShapeDtypeStruct((out_rows, out_cols), BF16),
        scratch_shapes=scratch,
        compiler_params=_params("parallel", "parallel", "arbitrary"),
        name=name,
    )(*args)


def _na_bias_kernel(rpb_ref, o_ref, *, n_rows, rows_q):
    h = pl.program_id(0)
    n_dr = 2 * NA_ROWS - 1
    n_dc = 2 * NA_COLS - 1
    kc = lax.broadcasted_iota(jnp.int32, (GRID_W, LANES), 0)
    cq = lax.broadcasted_iota(jnp.int32, (GRID_W, LANES), 1) & (GRID_W - 1)
    dc_idx = jnp.clip(kc - cq, -(NA_COLS - 1), NA_COLS - 1) + NA_COLS - 1
    cs = jnp.clip(cq - NA_COLS // 2, 0, GRID_W - NA_COLS)
    col_ok = (kc >= cs) & (kc < cs + NA_COLS)
    tiles = [jnp.zeros((GRID_W, LANES), F32) for _ in range(n_dr)]
    for dc in range(n_dc):
        hit = dc_idx == dc
        for dr in range(n_dr):
            tiles[dr] = jnp.where(hit, rpb_ref[(h * n_dr + dr) * n_dc + dc] * LOG2E, tiles[dr])
    tiles = [jnp.where(col_ok, t, NEG) for t in tiles]
    masked = jnp.full((GRID_W, LANES), NEG, F32)
    low = lax.broadcasted_iota(jnp.int32, (GRID_W, LANES), 1) < HALF

    span_rows = rows_q + NA_ROWS
    first_q_row = (0, span_rows, n_rows - rows_q)
    for c in range(3):
        r0 = first_q_row[c]
        w0 = min(max(r0 - NA_ROWS // 2, 0), n_rows - span_rows)
        for kr in range(span_rows):
            k_abs = w0 + kr
            for m in range(rows_q // 2):
                halves = []
                for r in (r0 + 2 * m, r0 + 2 * m + 1):
                    rs = min(max(r - NA_ROWS // 2, 0), n_rows - NA_ROWS)
                    halves.append(tiles[k_abs - r + NA_ROWS - 1] if rs <= k_abs < rs + NA_ROWS else masked)
                o_ref[c, 0, kr * GRID_W:(kr + 1) * GRID_W, m * LANES:(m + 1) * LANES] = (
                    jnp.where(low, halves[0], halves[1]))


def _na_bias(rpb, seq):
    n_rows = seq // GRID_W
    rows_q = TQ_SUB // GRID_W
    span = (rows_q + NA_ROWS) * GRID_W
    kern = functools.partial(_na_bias_kernel, n_rows=n_rows, rows_q=rows_q)
    return pl.pallas_call(
        kern,
        grid=(NA_HEADS,),
        in_specs=[pl.BlockSpec(memory_space=pltpu.SMEM)],
        out_specs=pl.BlockSpec((3, 1, span, TQ_SUB), lambda h: (0, h, 0, 0)),
        out_shape=jax.ShapeDtypeStruct((3, NA_HEADS, span, TQ_SUB), F32),
        compiler_params=_params("parallel"),
        name="na_bias",
    )(rpb.reshape(-1))


def _post_kernel(*refs, n_x, n_o, bpb, nbatch):
    n_lat_blocks = bpb * nbatch
    x_refs, oa_refs, ob_refs = refs[:n_x], refs[n_x:n_x + n_o], refs[n_x + n_o:n_x + 2 * n_o]
    wo_ref, g1_ref, n2_ref, sh_ref, sc_ref, g2_ref, w1_ref, w2_ref, o_ref = refs[n_x + 2 * n_o:]
    oa = _stream_block(oa_refs, n_lat_blocks)
    ob = _stream_block(ob_refs, n_lat_blocks)
    half = oa.shape[1]
    y = (jnp.dot(oa, wo_ref[:half, :], preferred_element_type=F32)
         + jnp.dot(ob, wo_ref[half:, :], preferred_element_type=F32))
    x1 = _stream_block(x_refs, n_lat_blocks) + _mod_row(g1_ref, bpb, nbatch) * y
    h = _modulated(x1, n2_ref, _mod_row(sh_ref, bpb, nbatch), _mod_row(sc_ref, bpb, nbatch)).astype(BF16)
    a = jnp.dot(h, w1_ref[...], preferred_element_type=F32)
    a = jnp.square(jnp.maximum(a, 0.0)).astype(BF16)
    o_ref[...] = x1 + _mod_row(g2_ref, bpb, nbatch) * jnp.dot(a, w2_ref[...], preferred_element_type=F32)


def _post(x_src, oa_src, ob_src, wo, mod, n2, w1, w2, layer, n_blocks, bpb, nbatch):
    d = x_src[0].shape[1]
    half = oa_src[0].shape[1]
    n_lat_blocks = bpb * nbatch
    g1_spec, sh_spec, sc_spec, g2_spec = _mod_specs(mod, layer, (GATE1, SHIFT2, SCALE2, GATE2))
    resident = lambda a: pl.BlockSpec((None,) + a.shape[1:], lambda i: (layer, 0, 0),
                                      pipeline_mode=pl.Buffered(1))
    kern = functools.partial(_post_kernel, n_x=len(x_src), n_o=len(oa_src), bpb=bpb, nbatch=nbatch)
    return pl.pallas_call(
        kern,
        grid=(n_blocks,),
        in_specs=(_stream_specs(x_src, d, n_lat_blocks) + _stream_specs(oa_src, half, n_lat_blocks)
                  + _stream_specs(ob_src, half, n_lat_blocks)
                  + [resident(wo), g1_spec, _whole(n2.shape), sh_spec, sc_spec, g2_spec, resident(w1), resident(w2)]),
        out_specs=pl.BlockSpec((TM, d), lambda i: (i, 0)),
        out_shape=jax.ShapeDtypeStruct((n_blocks * TM, d), F32),
        compiler_params=_params("parallel"),
        name="post",
    )(*x_src, *oa_src, *ob_src, wo, mod, n2, mod, mod, mod, w1, w2)


def _block_diag_ones(group):
    idx = np.arange(CHUNK) // group
    return jnp.asarray(idx[:, None] == idx[None, :], dtype=BF16)


def _rope_tables(seq, rot_dim, lane0, pad_rows):
    t = jnp.arange(seq, dtype=jnp.int32)
    row = (t // GRID_W).astype(F32)
    col = (t % GRID_W).astype(F32)
    n_freq = rot_dim // 4
    freqs = jnp.power(ROPE_BASE, -jnp.arange(n_freq, dtype=F32) / n_freq)
    ar = row[:, None] * freqs[None, :]
    ac = col[:, None] * freqs[None, :]
    ang = jnp.concatenate([ar, ar, ac, ac], axis=-1)
    sign = jnp.asarray(np.tile(np.repeat([-1.0, 1.0], n_freq), 2), F32)
    cos, sin = jnp.cos(ang), jnp.sin(ang) * sign[None, :]
    if lane0 is None:
        reps = LANES // rot_dim
        cos, sin = jnp.tile(cos, (1, reps)), jnp.tile(sin, (1, reps))
    else:
        pad = ((0, 0), (lane0, LANES - lane0 - rot_dim))
        cos = jnp.pad(cos, pad, constant_values=1.0)
        sin = jnp.pad(sin, pad)
    cos = jnp.concatenate([cos, jnp.ones((pad_rows, LANES), F32)], axis=0)
    sin = jnp.concatenate([sin, jnp.zeros((pad_rows, LANES), F32)], axis=0)
    return cos, sin


def _logits_bounded(q_gain, k_gain, dim, bias=None, sink=None):
    bound = jnp.max(jnp.abs(q_gain)) * jnp.max(jnp.abs(k_gain)) * (dim ** 0.5) * BF16_ROUNDING_MARGIN
    if bias is not None:
        bound = bound + jnp.max(jnp.abs(bias))
    if sink is not None:
        bound = jnp.maximum(bound, jnp.max(jnp.abs(sink)))
    return (bound <= MAX_FREE_LOGIT_BOUND).astype(jnp.int32).reshape(1)


def _tile_gain(g, reps, scale=1.0):
    return (jnp.tile(g.astype(F32), reps) * scale)[None, :]


def _even_weights(w, q_g, k_g, sq_g, sk_g):
    nq = DIFF_HEADS * 2 * HEAD_DIM
    qa, ka, va = w[:, :nq], w[:, nq:2 * nq], w[:, 2 * nq:3 * nq]
    o = 3 * nq
    qb = w[:, o:o + SWA_Q_HEADS * HEAD_DIM]
    o += SWA_Q_HEADS * HEAD_DIM
    kb = [w[:, o + i * HEAD_DIM:o + (i + 1) * HEAD_DIM] for i in range(SWA_KV_HEADS)]
    o += SWA_KV_HEADS * HEAD_DIM
    vb = [w[:, o + i * HEAD_DIM:o + (i + 1) * HEAD_DIM] for i in range(SWA_KV_HEADS)]
    dup = lambda parts: [p for p in parts for _ in range(2)]
    w_new = jnp.concatenate([qa, ka, va, qb] + dup(kb) + dup(vb), axis=1).astype(BF16)
    scale = HEAD_DIM ** -0.5 * LOG2E
    ones = lambda n: jnp.ones((1, n), F32)
    gain = jnp.concatenate([
        _tile_gain(q_g, 2 * DIFF_HEADS, scale), _tile_gain(k_g, 2 * DIFF_HEADS), ones(nq),
        _tile_gain(sq_g, SWA_Q_HEADS, scale), _tile_gain(sk_g, 2 * SWA_KV_HEADS),
        ones(2 * SWA_KV_HEADS * HEAD_DIM)], axis=1)
    return w_new, gain


def _odd_weights(w, wq_up, wkv_up, qa_g, kva_g, mq_g, mk_g, nq_g, nk_g):
    d = w.shape[0]
    o = 0
    parts = []
    for n in (MLA_Q_LORA, MLA_KV_LORA, MLA_ROPE, NA_HEADS * HEAD_DIM, NA_HEADS * HEAD_DIM, NA_HEADS * HEAD_DIM):
        parts.append(w[:, o:o + n])
        o += n
    q_a, kv_a, k_r, nq, nk, nv = parts
    slot = jnp.concatenate([jnp.zeros((d, MLA_NOPE), w.dtype), k_r,
                            jnp.zeros((d, LANES - MLA_QK), w.dtype)], axis=1)
    w_new = jnp.concatenate([q_a, kv_a, nq, nk, nv, slot], axis=1).astype(BF16)
    wq = jnp.pad(wq_up.reshape(MLA_Q_LORA, MLA_HEADS, MLA_QK), ((0, 0), (0, 0), (0, MLA_PAD - MLA_QK)))
    wq = wq.reshape(MLA_Q_LORA, MLA_HEADS * MLA_PAD).astype(BF16)
    wkv = wkv_up.reshape(MLA_KV_LORA, MLA_HEADS, MLA_NOPE + MLA_V)
    wk = jnp.pad(wkv[..., :MLA_NOPE], ((0, 0), (0, 0), (0, MLA_PAD - MLA_NOPE))).reshape(MLA_KV_LORA, -1)
    wv = wkv[..., MLA_NOPE:].reshape(MLA_KV_LORA, MLA_HEADS * MLA_V)
    wkv_new = jnp.concatenate([wk, wv], axis=1).astype(BF16)
    pad_gain = lambda g, scale: _tile_gain(jnp.pad(g.astype(F32), (0, MLA_PAD - MLA_QK)), CHUNK // MLA_PAD, scale)
    gains = (qa_g.astype(F32)[None, :], kva_g.astype(F32)[None, :],
             pad_gain(mq_g, MLA_QK ** -0.5 * LOG2E), pad_gain(mk_g, 1.0),
             _tile_gain(nq_g, CHUNK // HEAD_DIM, HEAD_DIM ** -0.5 * LOG2E), _tile_gain(nk_g, CHUNK // HEAD_DIM))
    return w_new, wq, wkv_new, gains


def kernel(x, c, ctx, c_ctx, ada_w, ada_b, norm1_g, norm2_g, w_out, mlp_w1, mlp_w2, ev_w_in, diff_q_g,
           diff_k_g, diff_lam, diff_subln_g, swa_q_g, swa_k_g, swa_sink, od_w_in, mla_qa_g, mla_kva_g,
           mla_wq_up, mla_wkv_up, mla_q_g, mla_k_g, na_q_g, na_k_g, na_rpb):
    nbatch, seq, d = x.shape
    ctx_len = ctx.shape[1]
    depth = ada_w.shape[0]
    n_lat = nbatch * seq
    n_ctx = nbatch * ctx_len
    assert seq % TM == 0 and n_ctx == TM and seq % TQ == 0 and seq % GRID_W == 0
    assert seq % TQ_FULL == 0 and seq % TK_MAX_FREE == 0 and seq % TK_FULL == 0
    assert seq // GRID_W >= 2 * (NA_ROWS + TQ_SUB // GRID_W) and seq >= TQ_SUB + 2 * WINDOW
    bpb = seq // TM

    mod_rows = 8
    cvec = jnp.concatenate([c, c_ctx[None, :], jnp.zeros((mod_rows - nbatch - 1, d), F32)], axis=0)
    mod = _modulation(cvec, ada_w, ada_b)

    cos64, sin64 = _rope_tables(seq, HEAD_DIM, None, n_ctx)
    cos32, sin32 = _rope_tables(seq, MLA_ROPE, MLA_NOPE, n_ctx)
    g64 = _block_diag_ones(HEAD_DIM)
    g128 = _block_diag_ones(LANES)

    x_src = (x.reshape(n_lat, d), ctx.reshape(n_ctx, d))
    wo_all, w1_all, w2_all = w_out.astype(BF16), mlp_w1.astype(BF16), mlp_w2.astype(BF16)
    common = dict(nbatch=nbatch, seq=seq, ctx_len=ctx_len)

    for layer in range(depth):
        need_ctx = layer < depth - 1
        n_blocks = bpb * nbatch + (1 if need_ctx else 0)
        n1 = norm1_g[layer][None, :]
        n2 = norm2_g[layer][None, :]
        i = layer // 2
        if layer % 2 == 0:
            lam_init = 0.8 - 0.6 * math.exp(-0.3 * layer)
            w_in, gain = _even_weights(ev_w_in[i], diff_q_g[i], diff_k_g[i], swa_q_g[i], swa_k_g[i])
            p = _proj_even(x_src, n1, mod, layer, w_in, gain, g64, cos64, sin64, bpb, nbatch)
            lam = diff_lam[i].astype(F32)
            subln = diff_subln_g[i].astype(F32)[None, :]
            sink = swa_sink[i].astype(F32)
            diff_kw = dict(mode="diff", split="mask", n_pairs=DIFF_HEADS, q_blk=lambda j: EVEN_QA // LANES + j,
                           k_blk=lambda j: EVEN_KA // LANES + j, v_blk=lambda j: EVEN_VA // LANES + j,
                           out_cols=DIFF_HEADS * LANES,
                           lam=lam, subln=subln, lam_init=lam_init,
                           safe=_logits_bounded(diff_q_g[i], diff_k_g[i], HEAD_DIM), **common)
            swa_kw = dict(mode="pair", split="mask", n_pairs=SWA_Q_HEADS // 2,
                          q_blk=lambda j: EVEN_QB // LANES + j, k_blk=lambda j: EVEN_KB // LANES + j // 2,
                          v_blk=lambda j: EVEN_VB // LANES + j // 2,
                          out_cols=SWA_Q_HEADS * HEAD_DIM, sink=sink,
                          safe=_logits_bounded(swa_q_g[i], swa_k_g[i], HEAD_DIM, sink=sink), **common)
            oa = _attention(p, name="diff_attn", main="full", ctx_queries=False, **diff_kw)
            ob = _attention(p, name="window_attn", main="window", ctx_queries=False, **swa_kw)
            if need_ctx:
                oa_c = _attention(p, name="diff_attn_ctx", main=None, ctx_queries=True, **diff_kw)
                ob_c = _attention(p, name="window_attn_ctx", main=None, ctx_queries=True, **swa_kw)
        else:
            w_in, wq, wkv, gains = _odd_weights(od_w_in[i], mla_wq_up[i], mla_wkv_up[i], mla_qa_g[i],
                                                mla_kva_g[i], mla_q_g[i], mla_k_g[i], na_q_g[i], na_k_g[i])
            p = _proj_odd(x_src[0], n1, mod, layer, w_in, wq, wkv, gains, g64, g128, cos32, sin32, bpb, nbatch)
            bias = _na_bias(na_rpb[i].astype(F32), seq)
            mla_kw = dict(mode="pair", split="slice", n_pairs=MLA_HEADS // 2, q_blk=lambda j: j,
                          k_blk=lambda j: ODD_K // CHUNK + j, v_blk=lambda j: ODD_V // LANES + j,
                          out_cols=MLA_HEADS * MLA_V,
                          safe=_logits_bounded(mla_q_g[i], mla_k_g[i], MLA_QK), **common)
            na_kw = dict(mode="pair", split="mask", n_pairs=NA_HEADS // 2,
                         q_blk=lambda j: ODD_NQ // LANES + j, k_blk=lambda j: ODD_NK // LANES + j,
                         v_blk=lambda j: ODD_NV // LANES + j, out_cols=NA_HEADS * HEAD_DIM,
                         safe=_logits_bounded(na_q_g[i], na_k_g[i], HEAD_DIM, bias=na_rpb[i]), **common)
            oa = _attention(p, name="mla_attn", main="full", ctx_queries=False, **mla_kw)
            ob = _attention(p, name="na_attn", main="na", ctx_queries=False, bias=bias, **na_kw)
            if need_ctx:
                oa_c = _attention(p, name="mla_attn_ctx", main=None, ctx_queries=True, **mla_kw)
                ob_c = _attention(p, name="na_attn_ctx", main=None, ctx_queries=True, **na_kw)
        oa_src, ob_src = ((oa, oa_c), (ob, ob_c)) if need_ctx else ((oa,), (ob,))
        x_src = (_post(x_src, oa_src, ob_src, wo_all, mod, n2, w1_all, w2_all, layer, n_blocks, bpb, nbatch),)
    return x_src[0].reshape(nbatch, seq, d)
```

```python
import functools
import math

import numpy as np
import jax
import jax.numpy as jnp
from jax import lax
from jax.experimental import pallas as pl
from jax.experimental.pallas import tpu as pltpu

F32 = jnp.float32
BF16 = jnp.bfloat16

LANES = 128
HALF = LANES // 2
CHUNK = 2 * LANES
GRID_W = 64
HEAD_DIM = 64
EPS = 1e-6
ROPE_BASE = 10000.0
DIFF_HEADS = 4
SWA_Q_HEADS = 8
SWA_KV_HEADS = 2
WINDOW = 128
MLA_HEADS = 8
MLA_Q_LORA = 512
MLA_KV_LORA = 256
MLA_NOPE = 64
MLA_ROPE = 32
MLA_QK = MLA_NOPE + MLA_ROPE
MLA_V = 64
NA_HEADS = 8
NA_ROWS = 8
NA_COLS = 16
NEG = -1e30
LOG2E = math.log2(math.e)
MAX_FREE_LOGIT_BOUND = 50.0
BF16_ROUNDING_MARGIN = 1.02

TM = 512
TQ = 2048
TQ_SUB = 256
TQ_FULL = 1024
TK_FULL = 512
TK_MAX_FREE = 4096
ONES_ROWS = 16
MOD_TN = 1536
VMEM_LIMIT = 56 * 1024 * 1024


def _params(*sem):
    return pltpu.CompilerParams(dimension_semantics=sem, vmem_limit_bytes=VMEM_LIMIT)


def _mod_kernel(c_ref, w_ref, b_ref, o_ref):
    c = c_ref[...]
    a = (c * jax.nn.sigmoid(c)).astype(BF16)
    w = w_ref[0].astype(BF16)
    o_ref[0] = jnp.dot(a, w, preferred_element_type=F32) + b_ref[0]


def _modulation(cvec, ada_w, ada_b):
    depth, d, n = ada_w.shape
    rows = cvec.shape[0]
    return pl.pallas_call(
        _mod_kernel,
        grid=(depth, n // MOD_TN),
        in_specs=[
            pl.BlockSpec((rows, d), lambda l, j: (0, 0)),
            pl.BlockSpec((1, d, MOD_TN), lambda l, j: (l, 0, j)),
            pl.BlockSpec((1, 1, MOD_TN), lambda l, j: (l, 0, j)),
        ],
        out_specs=pl.BlockSpec((1, rows, MOD_TN), lambda l, j: (l, 0, j)),
        out_shape=jax.ShapeDtypeStruct((depth, rows, n), F32),
        compiler_params=_params("parallel", "parallel"),
        name="modulation",
    )(cvec, ada_w, ada_b.reshape(depth, 1, n))


def _modulated(x, g_ref, shift, scale):
    ms = jnp.mean(x * x, axis=-1, keepdims=True)
    y = x * lax.rsqrt(ms + EPS) * g_ref[...]
    return y * (1.0 + scale) + shift


def _group_meansq(chunks, g_ref, inv_n):
    rows = chunks[0].shape[0]
    sq = jnp.concatenate([(c * c).astype(BF16) for c in chunks], axis=0)
    ss = jnp.dot(sq, g_ref[...], preferred_element_type=F32) * inv_n
    return [ss[i * rows:(i + 1) * rows] for i in range(len(chunks))]


def _swap_quarters(x, quarter):
    lane = lax.broadcasted_iota(jnp.int32, x.shape, 1)
    odd = (lane & quarter) != 0
    from_lower = pltpu.roll(x, quarter, 1)
    from_upper = pltpu.roll(x, LANES - quarter, 1)
    return jnp.where(odd, from_lower, from_upper)


def _rope(x, cos, sin_signed, quarter):
    return x * cos + _swap_quarters(x, quarter) * sin_signed


def _norm_rope_chunk(acc, meansq, gain, cos, sin, quarter):
    y = acc * lax.rsqrt(meansq + EPS) * gain
    if quarter is None:
        return y
    return jnp.concatenate([_rope(y[:, :LANES], cos, sin, quarter),
                            _rope(y[:, LANES:], cos, sin, quarter)], axis=1)


def _stream_specs(src, cols, n_lat_blocks):
    if len(src) == 1:
        return [pl.BlockSpec((TM, cols), lambda i: (i, 0))]
    return [pl.BlockSpec((TM, cols), lambda i: (jnp.minimum(i, n_lat_blocks - 1), 0)),
            pl.BlockSpec((TM, cols), lambda i: (0, 0))]


def _stream_block(refs, n_lat_blocks):
    if len(refs) == 1:
        return refs[0][...]
    return jnp.where(pl.program_id(0) < n_lat_blocks, refs[0][...], refs[1][...])


SHIFT1, SCALE1, GATE1, SHIFT2, SCALE2, GATE2 = range(6)


def _mod_specs(mod, layer, chunks):
    d = mod.shape[2] // 6
    return [pl.BlockSpec((None, mod.shape[1], d), lambda i, k=k: (layer, 0, k)) for k in chunks]


def _mod_row(ref, bpb, nbatch):
    row = jnp.minimum(pl.program_id(0) // bpb, nbatch)
    return ref[pl.ds(row, 1), :]


def _rope_spec(bpb, nbatch):
    return pl.BlockSpec((TM, LANES), lambda i: (jnp.where(i < bpb * nbatch, i % bpb, bpb), 0))


def _whole(shape):
    return pl.BlockSpec(shape, lambda *_: (0,) * len(shape))


EVEN_QA = 0
EVEN_KA = EVEN_QA + DIFF_HEADS * 2 * HEAD_DIM
EVEN_VA = EVEN_KA + DIFF_HEADS * 2 * HEAD_DIM
EVEN_QB = EVEN_VA + DIFF_HEADS * 2 * HEAD_DIM
EVEN_KB = EVEN_QB + SWA_Q_HEADS * HEAD_DIM
EVEN_VB = EVEN_KB + 2 * SWA_KV_HEADS * HEAD_DIM
EVEN_COLS = EVEN_VB + 2 * SWA_KV_HEADS * HEAD_DIM
EVEN_NORM_CHUNKS = tuple(c for c in range(EVEN_COLS // CHUNK)
                         if c * CHUNK < EVEN_VA or EVEN_QB <= c * CHUNK < EVEN_VB)
EVEN_DOT_RANGES = ((EVEN_QA // CHUNK, EVEN_VA // CHUNK), (EVEN_QB // CHUNK, EVEN_VB // CHUNK),
                   (EVEN_VA // CHUNK, EVEN_QB // CHUNK), (EVEN_VB // CHUNK, EVEN_COLS // CHUNK))


def _proj_even_kernel(*refs, n_x, bpb, nbatch):
    n_lat_blocks = bpb * nbatch
    x_refs = refs[:n_x]
    g_ref, sh_ref, sc_ref, w_ref, gain_ref, g64_ref, cos_ref, sin_ref, o_ref = refs[n_x:]
    h = _modulated(_stream_block(x_refs, n_lat_blocks), g_ref, _mod_row(sh_ref, bpb, nbatch),
                   _mod_row(sc_ref, bpb, nbatch)).astype(BF16)
    cos = cos_ref[...]
    sin = sin_ref[...]
    for first, last in EVEN_DOT_RANGES:
        full = jnp.dot(h, w_ref[:, first * CHUNK:last * CHUNK], preferred_element_type=F32)
        chunk = lambda c: full[:, (c - first) * CHUNK:(c - first + 1) * CHUNK]
        normed = [c for c in range(first, last) if c in EVEN_NORM_CHUNKS]
        meansq = {}
        if normed:
            meansq = dict(zip(normed, _group_meansq([chunk(c) for c in normed], g64_ref, 1.0 / HEAD_DIM)))
        for c in range(first, last):
            cols = slice(c * CHUNK, (c + 1) * CHUNK)
            acc = chunk(c)
            if c in EVEN_NORM_CHUNKS:
                acc = _norm_rope_chunk(acc, meansq[c], gain_ref[:, cols], cos, sin, HEAD_DIM // 4)
            o_ref[:, cols] = acc.astype(BF16)


def _proj_even(x_src, g, mod, layer, w, gain, g64, cos, sin, bpb, nbatch):
    d = x_src[0].shape[1]
    n_lat_blocks = bpb * nbatch
    kern = functools.partial(_proj_even_kernel, n_x=len(x_src), bpb=bpb, nbatch=nbatch)
    return pl.pallas_call(
        kern,
        grid=(n_lat_blocks + 1,),
        in_specs=_stream_specs(x_src, d, n_lat_blocks) + [_whole(g.shape)] + _mod_specs(mod, layer, (SHIFT1, SCALE1)) + [
            _whole(w.shape), _whole(gain.shape), _whole(g64.shape),
            _rope_spec(bpb, nbatch), _rope_spec(bpb, nbatch),
        ],
        out_specs=pl.BlockSpec((TM, EVEN_COLS), lambda i: (i, 0)),
        out_shape=jax.ShapeDtypeStruct(((n_lat_blocks + 1) * TM, EVEN_COLS), BF16),
        compiler_params=_params("parallel"),
        name="proj_even",
    )(*x_src, g, mod, mod, w, gain, g64, cos, sin)


NA_WIDTH = NA_HEADS * HEAD_DIM
MLA_PAD = LANES
MLA_WIDTH = MLA_HEADS * MLA_PAD
ODD_W_KVA = MLA_Q_LORA
ODD_W_NQ = ODD_W_KVA + MLA_KV_LORA
ODD_W_NV = ODD_W_NQ + 2 * NA_WIDTH
ODD_W_SLOT = ODD_W_NV + NA_WIDTH
ODD_W_COLS = ODD_W_SLOT + LANES
ODD_K = MLA_WIDTH
ODD_V = ODD_K + MLA_WIDTH
ODD_NQ = ODD_V + MLA_HEADS * MLA_V
ODD_NK = ODD_NQ + NA_WIDTH
ODD_NV = ODD_NK + NA_WIDTH
ODD_COLS = ODD_NV + NA_WIDTH


def _proj_odd_kernel(x_ref, g_ref, sh_ref, sc_ref, w_ref, wq_ref, wkv_ref, qa_g_ref, kva_g_ref,
                     mq_g_ref, mk_g_ref, nq_g_ref, nk_g_ref, g64_ref, g128_ref, cos_ref, sin_ref, o_ref,
                     *, bpb, nbatch):
    h = _modulated(x_ref[...], g_ref, _mod_row(sh_ref, bpb, nbatch), _mod_row(sc_ref, bpb, nbatch)).astype(BF16)
    cos = cos_ref[...]
    sin = sin_ref[...]
    quarter = MLA_ROPE // 4

    lora = jnp.dot(h, w_ref[:, :ODD_W_NQ], preferred_element_type=F32)
    rest = jnp.dot(h, w_ref[:, ODD_W_NQ:], preferred_element_type=F32)

    def low_rank(a, gain_ref):
        ms = jnp.mean(a * a, axis=-1, keepdims=True)
        return (a * lax.rsqrt(ms + EPS) * gain_ref[...]).astype(BF16)

    n_mla = MLA_WIDTH // CHUNK
    chunks_of = lambda a, first, n: [a[:, first + c * CHUNK:first + (c + 1) * CHUNK] for c in range(n)]
    qa = low_rank(lora[:, :MLA_Q_LORA], qa_g_ref)
    qf = jnp.dot(qa, wq_ref[...], preferred_element_type=F32)
    q_chunks = chunks_of(qf, 0, n_mla)
    ms_q = _group_meansq(q_chunks, g128_ref, 1.0 / MLA_QK)
    kva = low_rank(lora[:, ODD_W_KVA:], kva_g_ref)
    kvf = jnp.dot(kva, wkv_ref[...], preferred_element_type=F32)
    kslot = rest[:, ODD_W_SLOT - ODD_W_NQ:]
    kslot2 = jnp.concatenate([kslot, kslot], axis=1)
    k_rot = _swap_quarters(kslot * mk_g_ref[:, :LANES], quarter)
    k_rot2 = jnp.concatenate([k_rot, k_rot], axis=1)
    cos2 = jnp.concatenate([cos, cos], axis=1)
    sin2 = jnp.concatenate([sin, sin], axis=1)
    k_chunks = [kc + kslot2 for kc in chunks_of(kvf, 0, n_mla)]
    ms_k = _group_meansq(k_chunks, g128_ref, 1.0 / MLA_QK)
    for c in range(n_mla):
        y = _norm_rope_chunk(q_chunks[c], ms_q[c], mq_g_ref[...], cos, sin, quarter)
        o_ref[:, c * CHUNK:(c + 1) * CHUNK] = y.astype(BF16)
    for c in range(n_mla):
        inv_rms = lax.rsqrt(ms_k[c] + EPS)
        y = (k_chunks[c] * inv_rms * mk_g_ref[...]) * cos2 + (inv_rms * k_rot2) * sin2
        o_ref[:, ODD_K + c * CHUNK:ODD_K + (c + 1) * CHUNK] = y.astype(BF16)
    o_ref[:, ODD_V:ODD_NQ] = kvf[:, MLA_WIDTH:].astype(BF16)

    n_na = NA_WIDTH // CHUNK
    n_chunks = chunks_of(rest, 0, 2 * n_na)
    ms64 = _group_meansq(n_chunks, g64_ref, 1.0 / HEAD_DIM)
    for c in range(2 * n_na):
        gain = nq_g_ref[...] if c < n_na else nk_g_ref[...]
        y = _norm_rope_chunk(n_chunks[c], ms64[c], gain, None, None, None)
        o_ref[:, ODD_NQ + c * CHUNK:ODD_NQ + (c + 1) * CHUNK] = y.astype(BF16)
    o_ref[:, ODD_NV:ODD_COLS] = rest[:, ODD_W_NV - ODD_W_NQ:ODD_W_SLOT - ODD_W_NQ].astype(BF16)


def _proj_odd(xa, g, mod, layer, w, wq, wkv, gains, g64, g128, cos, sin, bpb, nbatch):
    na, d = xa.shape
    tok_specs = _stream_specs((xa,), d, bpb * nbatch) + [_whole(g.shape)] + _mod_specs(mod, layer, (SHIFT1, SCALE1))
    consts = (w, wq, wkv) + tuple(gains) + (g64, g128)
    return pl.pallas_call(
        functools.partial(_proj_odd_kernel, bpb=bpb, nbatch=nbatch),
        grid=(na // TM,),
        in_specs=tok_specs + [_whole(a.shape) for a in consts] + [_rope_spec(bpb, nbatch)] * 2,
        out_specs=pl.BlockSpec((TM, ODD_COLS), lambda i: (i, 0)),
        out_shape=jax.ShapeDtypeStruct((na, ODD_COLS), BF16),
        compiler_params=_params("parallel"),
        name="proj_odd",
    )(xa, g, mod, mod, *consts, cos, sin)


def _attn_kernel(*refs, mode, split, main, seq, has_sink, lam_init, tq, tq_sub):
    n_sub = tq // tq_sub
    it = iter(refs)
    q_ref = next(it)
    k_ref = v_ref = sink_ref = lam_ref = subln_ref = None
    bias_refs = ()
    if main is not None:
        k_ref, v_ref = next(it), next(it)
    kc_ref, vc_ref = next(it), next(it)
    if main == "na":
        bias_refs = tuple(next(it) for _ in range(n_sub))
    if has_sink:
        sink_ref = next(it)
    safe_ref = next(it)
    if mode == "diff":
        lam_ref, subln_ref = next(it), next(it)
    o_ref = next(it)
    vt_ref, vct_ref = (next(it), next(it)) if main == "full" else (None, None)

    pair = pl.program_id(1)
    qi = pl.program_id(2)
    q = q_ref[...]
    low = lax.broadcasted_iota(jnp.int32, (1, LANES), 1) < HALF
    if split == "mask":
        zero = jnp.zeros_like(q)
        qs = (jnp.where(low, q, zero), jnp.where(low, zero, q))
        k_of = lambda k, s: k
    else:
        qs = (q[:, :LANES], q[:, LANES:])
        k_of = lambda k, s: k[:, s * LANES:(s + 1) * LANES]
    q_of = lambda s, t: qs[s][t * tq_sub:(t + 1) * tq_sub]
    sinks = [sink_ref[2 * pair + s] * LOG2E for s in range(2)] if has_sink else None
    nt_dims = (((1,), (1,)), ((), ()))

    def finish(o_lo, o_hi):
        if mode == "diff":
            lv = lam_ref[...]
            lam = (jnp.exp(jnp.sum(lv[0:1] * lv[1:2], axis=-1, keepdims=True))
                   - jnp.exp(jnp.sum(lv[2:3] * lv[3:4], axis=-1, keepdims=True)) + lam_init)
            o = o_lo - lam * o_hi
            ms = jnp.mean(o * o, axis=-1, keepdims=True)
            o = o * lax.rsqrt(ms + EPS) * subln_ref[...] * (1.0 - lam_init)
        else:
            o = jnp.where(low, o_lo, o_hi)
        o_ref[...] = o.astype(BF16)

    def local_keys(t):
        if main == "window":
            span = tq_sub + 2 * WINDOW
            q0 = qi * tq + t * tq_sub
            w0 = jnp.clip(q0 - WINDOW, 0, seq - span)
            off = pl.multiple_of(w0, WINDOW)
            key = lax.broadcasted_iota(jnp.int32, (span, tq_sub), 0)
            qry = lax.broadcasted_iota(jnp.int32, (span, tq_sub), 1)
            inside = jnp.abs(key - qry + (w0 - q0)) <= WINDOW
            logit_fn = lambda lg_t, s: jnp.where(inside, lg_t, NEG)
        else:
            rows_q = tq_sub // GRID_W
            span_rows = rows_q + NA_ROWS
            w0 = jnp.clip(qi * (tq // GRID_W) + t * rows_q - NA_ROWS // 2, 0, seq // GRID_W - span_rows)
            off = pl.multiple_of(w0 * GRID_W, GRID_W)
            span = span_rows * GRID_W
            logit_fn = lambda lg_t, s: lg_t + bias_refs[t][0, s]
        return k_ref[pl.ds(off, span), :], v_ref[pl.ds(off, span), :], logit_fn

    def online_sub(t):
        def init(s):
            if has_sink:
                m0 = jnp.full((tq_sub, 1), sinks[s], F32)
                l0 = jnp.ones((tq_sub, 1), F32)
            else:
                m0 = jnp.full((tq_sub, 1), NEG, F32)
                l0 = jnp.zeros((tq_sub, 1), F32)
            return m0, l0, jnp.zeros((tq_sub, LANES), F32)

        def segment(state, k, v, logit_fn):
            out = []
            for s in range(2):
                m, l, acc = state[s]
                logits = lax.dot_general(q_of(s, t), k_of(k, s), nt_dims, preferred_element_type=F32)
                if logit_fn is not None:
                    logits = logit_fn(logits.T, s).T
                m_new = jnp.maximum(m, jnp.max(logits, axis=-1, keepdims=True))
                alpha = jnp.exp2(m - m_new)
                p = jnp.exp2(logits - m_new)
                l = alpha * l + jnp.sum(p, axis=-1, keepdims=True)
                acc = alpha * acc + jnp.dot(p.astype(BF16), v, preferred_element_type=F32)
                out.append((m_new, l, acc))
            return tuple(out)

        state = (init(0), init(1))
        if main == "full":
            def body(c, st):
                off = pl.multiple_of(c * TK_FULL, TK_FULL)
                return segment(st, k_ref[pl.ds(off, TK_FULL), :], v_ref[pl.ds(off, TK_FULL), :], None)
            state = lax.fori_loop(0, seq // TK_FULL, body, state)
        elif main is not None:
            state = segment(state, *local_keys(t))
        (_, l_lo, acc_lo), (_, l_hi, acc_hi) = segment(state, kc_ref[...], vc_ref[...], None)
        return acc_lo / l_lo, acc_hi / l_hi

    def with_ones_rows(v):
        return jnp.concatenate([v.T, jnp.ones((ONES_ROWS, v.shape[0]), BF16)], axis=0)

    def probs_t(k, q_rows, s, logit_fn):
        logits_t = lax.dot_general(k_of(k, s), q_rows, nt_dims, preferred_element_type=F32)
        if logit_fn is not None:
            logits_t = logit_fn(logits_t, s)
        return jnp.exp2(logits_t).astype(BF16)

    def max_free():
        acc = None
        if main == "full":
            @pl.when(qi == 0)
            def _():
                ones = jnp.ones((ONES_ROWS, TK_MAX_FREE), BF16)
                for c in range(seq // TK_MAX_FREE):
                    vt_ref[c, :LANES, :] = v_ref[c * TK_MAX_FREE:(c + 1) * TK_MAX_FREE, :].T
                    vt_ref[c, LANES:, :] = ones
                vct_ref[...] = with_ones_rows(vc_ref[...])

            def body(c, carry):
                off = pl.multiple_of(c * TK_MAX_FREE, TK_MAX_FREE)
                k = k_ref[pl.ds(off, TK_MAX_FREE), :]
                out = []
                for s in range(2):
                    logits_t = lax.dot_general(k_of(k, s), qs[s], nt_dims, preferred_element_type=F32)
                    p32 = jnp.exp2(logits_t)
                    out.append(carry[s] + jnp.dot(vt_ref[c, :LANES, :], p32.astype(BF16), preferred_element_type=F32))
                    out.append(carry[2 + s] + jnp.sum(p32, axis=0, keepdims=True))
                return (out[0], out[2], out[1], out[3])

            zero_acc = jnp.zeros((LANES, tq), F32)
            zero_den = jnp.zeros((1, tq), F32)
            carry = lax.fori_loop(0, seq // TK_MAX_FREE, body, (zero_acc, zero_acc, zero_den, zero_den))
            pad = jnp.zeros((ONES_ROWS - 1, tq), F32)
            acc = [jnp.concatenate([carry[s], carry[2 + s], pad], axis=0) for s in range(2)]
            vct = vct_ref[...]
        else:
            vct = with_ones_rows(vc_ref[...])
        p_ctx = [probs_t(kc_ref[...], qs[s], s, None) for s in range(2)]
        if main in ("window", "na"):
            windows = [local_keys(t) for t in range(n_sub)]
            p_loc = [[probs_t(k, q_of(s, t), s, fn) for s in range(2)] for t, (k, _, fn) in enumerate(windows)]
            vts = [with_ones_rows(v) for _, v, _ in windows]
        outs = []
        for s in range(2):
            acc_s = jnp.dot(vct, p_ctx[s], preferred_element_type=F32)
            if main == "full":
                acc_s = acc_s + acc[s]
            elif main is not None:
                acc_s = acc_s + jnp.concatenate(
                    [jnp.dot(vts[t], p_loc[t][s], preferred_element_type=F32) for t in range(n_sub)], axis=1)
            den = acc_s[LANES:LANES + 1]
            if has_sink:
                den = den + jnp.exp2(jnp.full((1, tq), sinks[s], F32))
            outs.append((acc_s[:LANES] / den).T)
        finish(*outs)

    def online():
        parts = [online_sub(t) for t in range(n_sub)]
        finish(*[jnp.concatenate([p[s] for p in parts], axis=0) if n_sub > 1 else parts[0][s]
                 for s in range(2)])

    safe = safe_ref[0] == 1
    pl.when(safe)(max_free)
    pl.when(jnp.logical_not(safe))(online)


def _attention(p, *, name, mode, split, main, n_pairs, q_blk, k_blk, v_blk, out_cols, ctx_queries,
               nbatch, seq, ctx_len, safe, bias=None, sink=None, lam=None, subln=None, lam_init=0.0):
    n_lat = nbatch * seq
    qw = LANES if split == "mask" else 2 * LANES
    if ctx_queries:
        tq, nq = ctx_len, 1
        q_row = lambda b, j, i: n_lat // ctx_len + b
        out_rows = nbatch * ctx_len
        out_row = lambda b, j, i: b
    else:
        tq = TQ_FULL if main == "full" else TQ
        nq = seq // tq
        q_row = lambda b, j, i: b * nq + i
        out_rows = n_lat
        out_row = q_row
    ctx_row = lambda b, j, i: n_lat // ctx_len + b

    in_specs = [pl.BlockSpec((tq, qw), lambda b, j, i: (q_row(b, j, i), q_blk(j)))]
    args = [p]
    if main is not None:
        in_specs += [pl.BlockSpec((seq, qw), lambda b, j, i: (b, k_blk(j))),
                     pl.BlockSpec((seq, LANES), lambda b, j, i: (b, v_blk(j)))]
        args += [p, p]
    in_specs += [pl.BlockSpec((ctx_len, qw), lambda b, j, i: (ctx_row(b, j, i), k_blk(j))),
                 pl.BlockSpec((ctx_len, LANES), lambda b, j, i: (ctx_row(b, j, i), v_blk(j)))]
    args += [p, p]
    tq_sub = TQ_SUB if main in ("window", "na") else tq
    n_sub = tq // tq_sub
    if main == "na":
        span = (tq_sub // GRID_W + NA_ROWS) * GRID_W
        last = nq * n_sub - 1
        cls = lambda g: jnp.where(g == 0, 0, jnp.where(g == last, 2, 1))
        for t in range(n_sub):
            in_specs.append(pl.BlockSpec((1, 2, span, tq_sub),
                                         lambda b, j, i, t=t: (cls(i * n_sub + t), j, 0, 0)))
            args.append(bias)
    if sink is not None:
        in_specs.append(pl.BlockSpec(memory_space=pltpu.SMEM))
        args.append(sink)
    in_specs.append(pl.BlockSpec(memory_space=pltpu.SMEM))
    args.append(safe)
    if mode == "diff":
        in_specs += [_whole(lam.shape), _whole(subln.shape)]
        args += [lam, subln]

    kern = functools.partial(_attn_kernel, mode=mode, split=split, main=main, seq=seq,
                             has_sink=sink is not None, lam_init=lam_init, tq=tq, tq_sub=tq_sub)
    scratch = []
    if main == "full":
        scratch = [pltpu.VMEM((seq // TK_MAX_FREE, LANES + ONES_ROWS, TK_MAX_FREE), BF16),
                   pltpu.VMEM((LANES + ONES_ROWS, ctx_len), BF16)]
    return pl.pallas_call(
        kern,
        grid=(nbatch, n_pairs, nq),
        in_specs=in_specs,
        out_specs=pl.BlockSpec((tq, LANES), lambda b, j, i: (out_row(b, j, i), j)),
        out_shape=jax.ShapeDtypeStruct((out_rows, out_cols), BF16),
        scratch_shapes=scratch,
        compiler_params=_params("parallel", "parallel", "arbitrary"),
        name=name,
    )(*args)


def _na_bias_kernel(rpb_ref, o_ref, *, n_rows, rows_q):
    h = pl.program_id(0)
    n_dr = 2 * NA_ROWS - 1
    n_dc = 2 * NA_COLS - 1
    kc = lax.broadcasted_iota(jnp.int32, (GRID_W, LANES), 0)
    cq = lax.broadcasted_iota(jnp.int32, (GRID_W, LANES), 1) & (GRID_W - 1)
    dc_idx = jnp.clip(kc - cq, -(NA_COLS - 1), NA_COLS - 1) + NA_COLS - 1
    cs = jnp.clip(cq - NA_COLS // 2, 0, GRID_W - NA_COLS)
    col_ok = (kc >= cs) & (kc < cs + NA_COLS)
    tiles = [jnp.zeros((GRID_W, LANES), F32) for _ in range(n_dr)]
    for dc in range(n_dc):
        hit = dc_idx == dc
        for dr in range(n_dr):
            tiles[dr] = jnp.where(hit, rpb_ref[(h * n_dr + dr) * n_dc + dc] * LOG2E, tiles[dr])
    tiles = [jnp.where(col_ok, t, NEG) for t in tiles]
    masked = jnp.full((GRID_W, LANES), NEG, F32)
    low = lax.broadcasted_iota(jnp.int32, (GRID_W, LANES), 1) < HALF

    span_rows = rows_q + NA_ROWS
    first_q_row = (0, span_rows, n_rows - rows_q)
    for c in range(3):
        r0 = first_q_row[c]
        w0 = min(max(r0 - NA_ROWS // 2, 0), n_rows - span_rows)
        for kr in range(span_rows):
            k_abs = w0 + kr
            for m in range(rows_q // 2):
                halves = []
                for r in (r0 + 2 * m, r0 + 2 * m + 1):
                    rs = min(max(r - NA_ROWS // 2, 0), n_rows - NA_ROWS)
                    halves.append(tiles[k_abs - r + NA_ROWS - 1] if rs <= k_abs < rs + NA_ROWS else masked)
                o_ref[c, 0, kr * GRID_W:(kr + 1) * GRID_W, m * LANES:(m + 1) * LANES] = (
                    jnp.where(low, halves[0], halves[1]))


def _na_bias(rpb, seq):
    n_rows = seq // GRID_W
    rows_q = TQ_SUB // GRID_W
    span = (rows_q + NA_ROWS) * GRID_W
    kern = functools.partial(_na_bias_kernel, n_rows=n_rows, rows_q=rows_q)
    return pl.pallas_call(
        kern,
        grid=(NA_HEADS,),
        in_specs=[pl.BlockSpec(memory_space=pltpu.SMEM)],
        out_specs=pl.BlockSpec((3, 1, span, TQ_SUB), lambda h: (0, h, 0, 0)),
        out_shape=jax.ShapeDtypeStruct((3, NA_HEADS, span, TQ_SUB), F32),
        compiler_params=_params("parallel"),
        name="na_bias",
    )(rpb.reshape(-1))


def _post_kernel(*refs, n_x, n_o, bpb, nbatch):
    n_lat_blocks = bpb * nbatch
    x_refs, oa_refs, ob_refs = refs[:n_x], refs[n_x:n_x + n_o], refs[n_x + n_o:n_x + 2 * n_o]
    wo_ref, g1_ref, n2_ref, sh_ref, sc_ref, g2_ref, w1_ref, w2_ref, o_ref = refs[n_x + 2 * n_o:]
    oa = _stream_block(oa_refs, n_lat_blocks)
    ob = _stream_block(ob_refs, n_lat_blocks)
    half = oa.shape[1]
    y = (jnp.dot(oa, wo_ref[:half, :], preferred_element_type=F32)
         + jnp.dot(ob, wo_ref[half:, :], preferred_element_type=F32))
    x1 = _stream_block(x_refs, n_lat_blocks) + _mod_row(g1_ref, bpb, nbatch) * y
    h = _modulated(x1, n2_ref, _mod_row(sh_ref, bpb, nbatch), _mod_row(sc_ref, bpb, nbatch)).astype(BF16)
    a = jnp.dot(h, w1_ref[...], preferred_element_type=F32)
    a = jnp.square(jnp.maximum(a, 0.0)).astype(BF16)
    o_ref[...] = x1 + _mod_row(g2_ref, bpb, nbatch) * jnp.dot(a, w2_ref[...], preferred_element_type=F32)


def _post(x_src, oa_src, ob_src, wo, mod, n2, w1, w2, layer, n_blocks, bpb, nbatch):
    d = x_src[0].shape[1]
    half = oa_src[0].shape[1]
    n_lat_blocks = bpb * nbatch
    g1_spec, sh_spec, sc_spec, g2_spec = _mod_specs(mod, layer, (GATE1, SHIFT2, SCALE2, GATE2))
    resident = lambda a: pl.BlockSpec((None,) + a.shape[1:], lambda i: (layer, 0, 0),
                                      pipeline_mode=pl.Buffered(1))
    kern = functools.partial(_post_kernel, n_x=len(x_src), n_o=len(oa_src), bpb=bpb, nbatch=nbatch)
    return pl.pallas_call(
        kern,
        grid=(n_blocks,),
        in_specs=(_stream_specs(x_src, d, n_lat_blocks) + _stream_specs(oa_src, half, n_lat_blocks)
                  + _stream_specs(ob_src, half, n_lat_blocks)
                  + [resident(wo), g1_spec, _whole(n2.shape), sh_spec, sc_spec, g2_spec, resident(w1), resident(w2)]),
        out_specs=pl.BlockSpec((TM, d), lambda i: (i, 0)),
        out_shape=jax.ShapeDtypeStruct((n_blocks * TM, d), F32),
        compiler_params=_params("parallel"),
        name="post",
    )(*x_src, *oa_src, *ob_src, wo, mod, n2, mod, mod, mod, w1, w2)


def _block_diag_ones(group):
    idx = np.arange(CHUNK) // group
    return jnp.asarray(idx[:, None] == idx[None, :], dtype=BF16)


def _rope_tables(seq, rot_dim, lane0, pad_rows):
    t = jnp.arange(seq, dtype=jnp.int32)
    row = (t // GRID_W).astype(F32)
    col = (t % GRID_W).astype(F32)
    n_freq = rot_dim // 4
    freqs = jnp.power(ROPE_BASE, -jnp.arange(n_freq, dtype=F32) / n_freq)
    ar = row[:, None] * freqs[None, :]
    ac = col[:, None] * freqs[None, :]
    ang = jnp.concatenate([ar, ar, ac, ac], axis=-1)
    sign = jnp.asarray(np.tile(np.repeat([-1.0, 1.0], n_freq), 2), F32)
    cos, sin = jnp.cos(ang), jnp.sin(ang) * sign[None, :]
    if lane0 is None:
        reps = LANES // rot_dim
        cos, sin = jnp.tile(cos, (1, reps)), jnp.tile(sin, (1, reps))
    else:
        pad = ((0, 0), (lane0, LANES - lane0 - rot_dim))
        cos = jnp.pad(cos, pad, constant_values=1.0)
        sin = jnp.pad(sin, pad)
    cos = jnp.concatenate([cos, jnp.ones((pad_rows, LANES), F32)], axis=0)
    sin = jnp.concatenate([sin, jnp.zeros((pad_rows, LANES), F32)], axis=0)
    return cos, sin


def _logits_bounded(q_gain, k_gain, dim, bias=None, sink=None):
    bound = jnp.max(jnp.abs(q_gain)) * jnp.max(jnp.abs(k_gain)) * (dim ** 0.5) * BF16_ROUNDING_MARGIN
    if bias is not None:
        bound = bound + jnp.max(jnp.abs(bias))
    if sink is not None:
        bound = jnp.maximum(bound, jnp.max(jnp.abs(sink)))
    return (bound <= MAX_FREE_LOGIT_BOUND).astype(jnp.int32).reshape(1)


def _tile_gain(g, reps, scale=1.0):
    return (jnp.tile(g.astype(F32), reps) * scale)[None, :]


def _even_weights(w, q_g, k_g, sq_g, sk_g):
    nq = DIFF_HEADS * 2 * HEAD_DIM
    qa, ka, va = w[:, :nq], w[:, nq:2 * nq], w[:, 2 * nq:3 * nq]
    o = 3 * nq
    qb = w[:, o:o + SWA_Q_HEADS * HEAD_DIM]
    o += SWA_Q_HEADS * HEAD_DIM
    kb = [w[:, o + i * HEAD_DIM:o + (i + 1) * HEAD_DIM] for i in range(SWA_KV_HEADS)]
    o += SWA_KV_HEADS * HEAD_DIM
    vb = [w[:, o + i * HEAD_DIM:o + (i + 1) * HEAD_DIM] for i in range(SWA_KV_HEADS)]
    dup = lambda parts: [p for p in parts for _ in range(2)]
    w_new = jnp.concatenate([qa, ka, va, qb] + dup(kb) + dup(vb), axis=1).astype(BF16)
    scale = HEAD_DIM ** -0.5 * LOG2E
    ones = lambda n: jnp.ones((1, n), F32)
    gain = jnp.concatenate([
        _tile_gain(q_g, 2 * DIFF_HEADS, scale), _tile_gain(k_g, 2 * DIFF_HEADS), ones(nq),
        _tile_gain(sq_g, SWA_Q_HEADS, scale), _tile_gain(sk_g, 2 * SWA_KV_HEADS),
        ones(2 * SWA_KV_HEADS * HEAD_DIM)], axis=1)
    return w_new, gain


def _odd_weights(w, wq_up, wkv_up, qa_g, kva_g, mq_g, mk_g, nq_g, nk_g):
    d = w.shape[0]
    o = 0
    parts = []
    for n in (MLA_Q_LORA, MLA_KV_LORA, MLA_ROPE, NA_HEADS * HEAD_DIM, NA_HEADS * HEAD_DIM, NA_HEADS * HEAD_DIM):
        parts.append(w[:, o:o + n])
        o += n
    q_a, kv_a, k_r, nq, nk, nv = parts
    slot = jnp.concatenate([jnp.zeros((d, MLA_NOPE), w.dtype), k_r,
                            jnp.zeros((d, LANES - MLA_QK), w.dtype)], axis=1)
    w_new = jnp.concatenate([q_a, kv_a, nq, nk, nv, slot], axis=1).astype(BF16)
    wq = jnp.pad(wq_up.reshape(MLA_Q_LORA, MLA_HEADS, MLA_QK), ((0, 0), (0, 0), (0, MLA_PAD - MLA_QK)))
    wq = wq.reshape(MLA_Q_LORA, MLA_HEADS * MLA_PAD).astype(BF16)
    wkv = wkv_up.reshape(MLA_KV_LORA, MLA_HEADS, MLA_NOPE + MLA_V)
    wk = jnp.pad(wkv[..., :MLA_NOPE], ((0, 0), (0, 0), (0, MLA_PAD - MLA_NOPE))).reshape(MLA_KV_LORA, -1)
    wv = wkv[..., MLA_NOPE:].reshape(MLA_KV_LORA, MLA_HEADS * MLA_V)
    wkv_new = jnp.concatenate([wk, wv], axis=1).astype(BF16)
    pad_gain = lambda g, scale: _tile_gain(jnp.pad(g.astype(F32), (0, MLA_PAD - MLA_QK)), CHUNK // MLA_PAD, scale)
    gains = (qa_g.astype(F32)[None, :], kva_g.astype(F32)[None, :],
             pad_gain(mq_g, MLA_QK ** -0.5 * LOG2E), pad_gain(mk_g, 1.0),
             _tile_gain(nq_g, CHUNK // HEAD_DIM, HEAD_DIM ** -0.5 * LOG2E), _tile_gain(nk_g, CHUNK // HEAD_DIM))
    return w_new, wq, wkv_new, gains


def kernel(x, c, ctx, c_ctx, ada_w, ada_b, norm1_g, norm2_g, w_out, mlp_w1, mlp_w2, ev_w_in, diff_q_g,
           diff_k_g, diff_lam, diff_subln_g, swa_q_g, swa_k_g, swa_sink, od_w_in, mla_qa_g, mla_kva_g,
           mla_wq_up, mla_wkv_up, mla_q_g, mla_k_g, na_q_g, na_k_g, na_rpb):
    nbatch, seq, d = x.shape
    ctx_len = ctx.shape[1]
    depth = ada_w.shape[0]
    n_lat = nbatch * seq
    n_ctx = nbatch * ctx_len
    assert seq % TM == 0 and n_ctx == TM and seq % TQ == 0 and seq % GRID_W == 0
    assert seq % TQ_FULL == 0 and seq % TK_MAX_FREE == 0 and seq % TK_FULL == 0
    assert seq // GRID_W >= 2 * (NA_ROWS + TQ_SUB // GRID_W) and seq >= TQ_SUB + 2 * WINDOW
    bpb = seq // TM

    mod_rows = 8
    cvec = jnp.concatenate([c, c_ctx[None, :], jnp.zeros((mod_rows - nbatch - 1, d), F32)], axis=0)
    mod = _modulation(cvec, ada_w, ada_b)

    cos64, sin64 = _rope_tables(seq, HEAD_DIM, None, n_ctx)
    cos32, sin32 = _rope_tables(seq, MLA_ROPE, MLA_NOPE, n_ctx)
    g64 = _block_diag_ones(HEAD_DIM)
    g128 = _block_diag_ones(LANES)

    x_src = (x.reshape(n_lat, d), ctx.reshape(n_ctx, d))
    wo_all, w1_all, w2_all = w_out.astype(BF16), mlp_w1.astype(BF16), mlp_w2.astype(BF16)
    common = dict(nbatch=nbatch, seq=seq, ctx_len=ctx_len)

    for layer in range(depth):
        need_ctx = layer < depth - 1
        n_blocks = bpb * nbatch + (1 if need_ctx else 0)
        n1 = norm1_g[layer][None, :]
        n2 = norm2_g[layer][None, :]
        i = layer // 2
        if layer % 2 == 0:
            lam_init = 0.8 - 0.6 * math.exp(-0.3 * layer)
            w_in, gain = _even_weights(ev_w_in[i], diff_q_g[i], diff_k_g[i], swa_q_g[i], swa_k_g[i])
            p = _proj_even(x_src, n1, mod, layer, w_in, gain, g64, cos64, sin64, bpb, nbatch)
            lam = diff_lam[i].astype(F32)
            subln = diff_subln_g[i].astype(F32)[None, :]
            sink = swa_sink[i].astype(F32)
            diff_kw = dict(mode="diff", split="mask", n_pairs=DIFF_HEADS, q_blk=lambda j: EVEN_QA // LANES + j,
                           k_blk=lambda j: EVEN_KA // LANES + j, v_blk=lambda j: EVEN_VA // LANES + j,
                           out_cols=DIFF_HEADS * LANES,
                           lam=lam, subln=subln, lam_init=lam_init,
                           safe=_logits_bounded(diff_q_g[i], diff_k_g[i], HEAD_DIM), **common)
            swa_kw = dict(mode="pair", split="mask", n_pairs=SWA_Q_HEADS // 2,
                          q_blk=lambda j: EVEN_QB // LANES + j, k_blk=lambda j: EVEN_KB // LANES + j // 2,
                          v_blk=lambda j: EVEN_VB // LANES + j // 2,
                          out_cols=SWA_Q_HEADS * HEAD_DIM, sink=sink,
                          safe=_logits_bounded(swa_q_g[i], swa_k_g[i], HEAD_DIM, sink=sink), **common)
            oa = _attention(p, name="diff_attn", main="full", ctx_queries=False, **diff_kw)
            ob = _attention(p, name="window_attn", main="window", ctx_queries=False, **swa_kw)
            if need_ctx:
                oa_c = _attention(p, name="diff_attn_ctx", main=None, ctx_queries=True, **diff_kw)
                ob_c = _attention(p, name="window_attn_ctx", main=None, ctx_queries=True, **swa_kw)
        else:
            w_in, wq, wkv, gains = _odd_weights(od_w_in[i], mla_wq_up[i], mla_wkv_up[i], mla_qa_g[i],
                                                mla_kva_g[i], mla_q_g[i], mla_k_g[i], na_q_g[i], na_k_g[i])
            p = _proj_odd(x_src[0], n1, mod, layer, w_in, wq, wkv, gains, g64, g128, cos32, sin32, bpb, nbatch)
            bias = _na_bias(na_rpb[i].astype(F32), seq)
            mla_kw = dict(mode="pair", split="slice", n_pairs=MLA_HEADS // 2, q_blk=lambda j: j,
                          k_blk=lambda j: ODD_K // CHUNK + j, v_blk=lambda j: ODD_V // LANES + j,
                          out_cols=MLA_HEADS * MLA_V,
                          safe=_logits_bounded(mla_q_g[i], mla_k_g[i], MLA_QK), **common)
            na_kw = dict(mode="pair", split="mask", n_pairs=NA_HEADS // 2,
                         q_blk=lambda j: ODD_NQ // LANES + j, k_blk=lambda j: ODD_NK // LANES + j,
                         v_blk=lambda j: ODD_NV // LANES + j, out_cols=NA_HEADS * HEAD_DIM,
                         safe=_logits_bounded(na_q_g[i], na_k_g[i], HEAD_DIM, bias=na_rpb[i]), **common)
            oa = _attention(p, name="mla_attn", main="full", ctx_queries=False, **mla_kw)
            ob = _attention(p, name="na_attn", main="na", ctx_queries=False, bias=bias, **na_kw)
            if need_ctx:
                oa_c = _attention(p, name="mla_attn_ctx", main=None, ctx_queries=True, **mla_kw)
                ob_c = _attention(p, name="na_attn_ctx", main=None, ctx_queries=True, **na_kw)
        oa_src, ob_src = ((oa, oa_c), (ob, ob_c)) if need_ctx else ((oa,), (ob,))
        x_src = (_post(x_src, oa_src, ob_src, wo_all, mod, n2, w1_all, w2_all, layer, n_blocks, bpb, nbatch),)
    return x_src[0].reshape(nbatch, seq, d)
```

```python
import functools
import math

import numpy as np
import jax
import jax.numpy as jnp
from jax import lax
from jax.experimental import pallas as pl
from jax.experimental.pallas import tpu as pltpu

F32 = jnp.float32
BF16 = jnp.bfloat16

LANES = 128
HALF = LANES // 2
CHUNK = 2 * LANES
GRID_W = 64
HEAD_DIM = 64
EPS = 1e-6
ROPE_BASE = 10000.0
DIFF_HEADS = 4
SWA_Q_HEADS = 8
SWA_KV_HEADS = 2
WINDOW = 128
MLA_HEADS = 8
MLA_Q_LORA = 512
MLA_KV_LORA = 256
MLA_NOPE = 64
MLA_ROPE = 32
MLA_QK = MLA_NOPE + MLA_ROPE
MLA_V = 64
NA_HEADS = 8
NA_ROWS = 8
NA_COLS = 16
NEG = -1e30
LOG2E = math.log2(math.e)
MAX_FREE_LOGIT_BOUND = 50.0
BF16_ROUNDING_MARGIN = 1.02

TM = 512
TQ = 2048
TQ_SUB = 256
TQ_FULL = 1024
TK_FULL = 512
TK_MAX_FREE = 4096
MOD_TN = 1536
VMEM_LIMIT = 56 * 1024 * 1024


def _params(*sem):
    return pltpu.CompilerParams(dimension_semantics=sem, vmem_limit_bytes=VMEM_LIMIT)


def _mod_kernel(c_ref, w_ref, b_ref, o_ref):
    c = c_ref[...]
    a = (c * jax.nn.sigmoid(c)).astype(BF16)
    w = w_ref[0].astype(BF16)
    o_ref[0] = jnp.dot(a, w, preferred_element_type=F32) + b_ref[0]


def _modulation(cvec, ada_w, ada_b):
    depth, d, n = ada_w.shape
    rows = cvec.shape[0]
    return pl.pallas_call(
        _mod_kernel,
        grid=(depth, n // MOD_TN),
        in_specs=[
            pl.BlockSpec((rows, d), lambda l, j: (0, 0)),
            pl.BlockSpec((1, d, MOD_TN), lambda l, j: (l, 0, j)),
            pl.BlockSpec((1, 1, MOD_TN), lambda l, j: (l, 0, j)),
        ],
        out_specs=pl.BlockSpec((1, rows, MOD_TN), lambda l, j: (l, 0, j)),
        out_shape=jax.ShapeDtypeStruct((depth, rows, n), F32),
        compiler_params=_params("parallel", "parallel"),
        name="modulation",
    )(cvec, ada_w, ada_b.reshape(depth, 1, n))


def _modulated(x, g_ref, shift, scale):
    ms = jnp.mean(x * x, axis=-1, keepdims=True)
    y = x * lax.rsqrt(ms + EPS) * g_ref[...]
    return y * (1.0 + scale) + shift


def _group_meansq(chunks, g_ref, inv_n):
    rows = chunks[0].shape[0]
    sq = jnp.concatenate([(c * c).astype(BF16) for c in chunks], axis=0)
    ss = jnp.dot(sq, g_ref[...], preferred_element_type=F32) * inv_n
    return [ss[i * rows:(i + 1) * rows] for i in range(len(chunks))]


def _swap_quarters(x, quarter):
    lane = lax.broadcasted_iota(jnp.int32, x.shape, 1)
    odd = (lane & quarter) != 0
    from_lower = pltpu.roll(x, quarter, 1)
    from_upper = pltpu.roll(x, LANES - quarter, 1)
    return jnp.where(odd, from_lower, from_upper)


def _rope(x, cos, sin_signed, quarter):
    return x * cos + _swap_quarters(x, quarter) * sin_signed


def _norm_rope_chunk(acc, meansq, gain, cos, sin, quarter):
    y = acc * lax.rsqrt(meansq + EPS) * gain
    if quarter is None:
        return y
    return jnp.concatenate([_rope(y[:, :LANES], cos, sin, quarter),
                            _rope(y[:, LANES:], cos, sin, quarter)], axis=1)


def _stream_specs(src, cols, n_lat_blocks):
    if len(src) == 1:
        return [pl.BlockSpec((TM, cols), lambda i: (i, 0))]
    return [pl.BlockSpec((TM, cols), lambda i: (jnp.minimum(i, n_lat_blocks - 1), 0)),
            pl.BlockSpec((TM, cols), lambda i: (0, 0))]


def _stream_block(refs, n_lat_blocks):
    if len(refs) == 1:
        return refs[0][...]
    return jnp.where(pl.program_id(0) < n_lat_blocks, refs[0][...], refs[1][...])


SHIFT1, SCALE1, GATE1, SHIFT2, SCALE2, GATE2 = range(6)


def _mod_specs(mod, layer, chunks):
    d = mod.shape[2] // 6
    return [pl.BlockSpec((None, mod.shape[1], d), lambda i, k=k: (layer, 0, k)) for k in chunks]


def _mod_row(ref, bpb, nbatch):
    row = jnp.minimum(pl.program_id(0) // bpb, nbatch)
    return ref[pl.ds(row, 1), :]


def _rope_spec(bpb, nbatch):
    return pl.BlockSpec((TM, LANES), lambda i: (jnp.where(i < bpb * nbatch, i % bpb, bpb), 0))


def _whole(shape):
    return pl.BlockSpec(shape, lambda *_: (0,) * len(shape))


EVEN_QA = 0
EVEN_KA = EVEN_QA + DIFF_HEADS * 2 * HEAD_DIM
EVEN_VA = EVEN_KA + DIFF_HEADS * 2 * HEAD_DIM
EVEN_QB = EVEN_VA + DIFF_HEADS * 2 * HEAD_DIM
EVEN_KB = EVEN_QB + SWA_Q_HEADS * HEAD_DIM
EVEN_VB = EVEN_KB + 2 * SWA_KV_HEADS * HEAD_DIM
EVEN_COLS = EVEN_VB + 2 * SWA_KV_HEADS * HEAD_DIM
EVEN_NORM_CHUNKS = tuple(c for c in range(EVEN_COLS // CHUNK)
                         if c * CHUNK < EVEN_VA or EVEN_QB <= c * CHUNK < EVEN_VB)
EVEN_DOT_RANGES = ((EVEN_QA // CHUNK, EVEN_VA // CHUNK), (EVEN_QB // CHUNK, EVEN_VB // CHUNK),
                   (EVEN_VA // CHUNK, EVEN_QB // CHUNK), (EVEN_VB // CHUNK, EVEN_COLS // CHUNK))


def _proj_even_kernel(*refs, n_x, bpb, nbatch):
    n_lat_blocks = bpb * nbatch
    x_refs = refs[:n_x]
    g_ref, sh_ref, sc_ref, w_ref, gain_ref, g64_ref, cos_ref, sin_ref, o_ref = refs[n_x:]
    h = _modulated(_stream_block(x_refs, n_lat_blocks), g_ref, _mod_row(sh_ref, bpb, nbatch),
                   _mod_row(sc_ref, bpb, nbatch)).astype(BF16)
    cos = cos_ref[...]
    sin = sin_ref[...]
    for first, last in EVEN_DOT_RANGES:
        full = jnp.dot(h, w_ref[:, first * CHUNK:last * CHUNK], preferred_element_type=F32)
        chunk = lambda c: full[:, (c - first) * CHUNK:(c - first + 1) * CHUNK]
        normed = [c for c in range(first, last) if c in EVEN_NORM_CHUNKS]
        meansq = {}
        if normed:
            meansq = dict(zip(normed, _group_meansq([chunk(c) for c in normed], g64_ref, 1.0 / HEAD_DIM)))
        for c in range(first, last):
            cols = slice(c * CHUNK, (c + 1) * CHUNK)
            acc = chunk(c)
            if c in EVEN_NORM_CHUNKS:
                acc = _norm_rope_chunk(acc, meansq[c], gain_ref[:, cols], cos, sin, HEAD_DIM // 4)
            o_ref[:, cols] = acc.astype(BF16)


def _proj_even(x_src, g, mod, layer, w, gain, g64, cos, sin, bpb, nbatch):
    d = x_src[0].shape[1]
    n_lat_blocks = bpb * nbatch
    kern = functools.partial(_proj_even_kernel, n_x=len(x_src), bpb=bpb, nbatch=nbatch)
    return pl.pallas_call(
        kern,
        grid=(n_lat_blocks + 1,),
        in_specs=_stream_specs(x_src, d, n_lat_blocks) + [_whole(g.shape)] + _mod_specs(mod, layer, (SHIFT1, SCALE1)) + [
            _whole(w.shape), _whole(gain.shape), _whole(g64.shape),
            _rope_spec(bpb, nbatch), _rope_spec(bpb, nbatch),
        ],
        out_specs=pl.BlockSpec((TM, EVEN_COLS), lambda i: (i, 0)),
        out_shape=jax.ShapeDtypeStruct(((n_lat_blocks + 1) * TM, EVEN_COLS), BF16),
        compiler_params=_params("parallel"),
        name="proj_even",
    )(*x_src, g, mod, mod, w, gain, g64, cos, sin)


NA_WIDTH = NA_HEADS * HEAD_DIM
MLA_PAD = LANES
MLA_WIDTH = MLA_HEADS * MLA_PAD
ODD_W_KVA = MLA_Q_LORA
ODD_W_NQ = ODD_W_KVA + MLA_KV_LORA
ODD_W_NV = ODD_W_NQ + 2 * NA_WIDTH
ODD_W_SLOT = ODD_W_NV + NA_WIDTH
ODD_W_COLS = ODD_W_SLOT + LANES
ODD_K = MLA_WIDTH
ODD_V = ODD_K + MLA_WIDTH
ODD_NQ = ODD_V + MLA_HEADS * MLA_V
ODD_NK = ODD_NQ + NA_WIDTH
ODD_NV = ODD_NK + NA_WIDTH
ODD_COLS = ODD_NV + NA_WIDTH


def _proj_odd_kernel(x_ref, g_ref, sh_ref, sc_ref, w_ref, wq_ref, wkv_ref, qa_g_ref, kva_g_ref,
                     mq_g_ref, mk_g_ref, nq_g_ref, nk_g_ref, g64_ref, g128_ref, cos_ref, sin_ref, o_ref,
                     *, bpb, nbatch):
    h = _modulated(x_ref[...], g_ref, _mod_row(sh_ref, bpb, nbatch), _mod_row(sc_ref, bpb, nbatch)).astype(BF16)
    cos = cos_ref[...]
    sin = sin_ref[...]
    quarter = MLA_ROPE // 4

    lora = jnp.dot(h, w_ref[:, :ODD_W_NQ], preferred_element_type=F32)
    rest = jnp.dot(h, w_ref[:, ODD_W_NQ:], preferred_element_type=F32)

    def low_rank(a, gain_ref):
        ms = jnp.mean(a * a, axis=-1, keepdims=True)
        return (a * lax.rsqrt(ms + EPS) * gain_ref[...]).astype(BF16)

    n_mla = MLA_WIDTH // CHUNK
    chunks_of = lambda a, first, n: [a[:, first + c * CHUNK:first + (c + 1) * CHUNK] for c in range(n)]
    qa = low_rank(lora[:, :MLA_Q_LORA], qa_g_ref)
    qf = jnp.dot(qa, wq_ref[...], preferred_element_type=F32)
    q_chunks = chunks_of(qf, 0, n_mla)
    ms_q = _group_meansq(q_chunks, g128_ref, 1.0 / MLA_QK)
    kva = low_rank(lora[:, ODD_W_KVA:], kva_g_ref)
    kvf = jnp.dot(kva, wkv_ref[...], preferred_element_type=F32)
    kslot = rest[:, ODD_W_SLOT - ODD_W_NQ:]
    kslot2 = jnp.concatenate([kslot, kslot], axis=1)
    k_rot = _swap_quarters(kslot * mk_g_ref[:, :LANES], quarter)
    k_rot2 = jnp.concatenate([k_rot, k_rot], axis=1)
    cos2 = jnp.concatenate([cos, cos], axis=1)
    sin2 = jnp.concatenate([sin, sin], axis=1)
    k_chunks = [kc + kslot2 for kc in chunks_of(kvf, 0, n_mla)]
    ms_k = _group_meansq(k_chunks, g128_ref, 1.0 / MLA_QK)
    for c in range(n_mla):
        y = _norm_rope_chunk(q_chunks[c], ms_q[c], mq_g_ref[...], cos, sin, quarter)
        o_ref[:, c * CHUNK:(c + 1) * CHUNK] = y.astype(BF16)
    for c in range(n_mla):
        inv_rms = lax.rsqrt(ms_k[c] + EPS)
        y = (k_chunks[c] * inv_rms * mk_g_ref[...]) * cos2 + (inv_rms * k_rot2) * sin2
        o_ref[:, ODD_K + c * CHUNK:ODD_K + (c + 1) * CHUNK] = y.astype(BF16)
    o_ref[:, ODD_V:ODD_NQ] = kvf[:, MLA_WIDTH:].astype(BF16)

    n_na = NA_WIDTH // CHUNK
    n_chunks = chunks_of(rest, 0, 2 * n_na)
    ms64 = _group_meansq(n_chunks, g64_ref, 1.0 / HEAD_DIM)
    for c in range(2 * n_na):
        gain = nq_g_ref[...] if c < n_na else nk_g_ref[...]
        y = _norm_rope_chunk(n_chunks[c], ms64[c], gain, None, None, None)
        o_ref[:, ODD_NQ + c * CHUNK:ODD_NQ + (c + 1) * CHUNK] = y.astype(BF16)
    o_ref[:, ODD_NV:ODD_COLS] = rest[:, ODD_W_NV - ODD_W_NQ:ODD_W_SLOT - ODD_W_NQ].astype(BF16)


def _proj_odd(xa, g, mod, layer, w, wq, wkv, gains, g64, g128, cos, sin, bpb, nbatch):
    na, d = xa.shape
    tok_specs = _stream_specs((xa,), d, bpb * nbatch) + [_whole(g.shape)] + _mod_specs(mod, layer, (SHIFT1, SCALE1))
    consts = (w, wq, wkv) + tuple(gains) + (g64, g128)
    return pl.pallas_call(
        functools.partial(_proj_odd_kernel, bpb=bpb, nbatch=nbatch),
        grid=(na // TM,),
        in_specs=tok_specs + [_whole(a.shape) for a in consts] + [_rope_spec(bpb, nbatch)] * 2,
        out_specs=pl.BlockSpec((TM, ODD_COLS), lambda i: (i, 0)),
        out_shape=jax.ShapeDtypeStruct((na, ODD_COLS), BF16),
        compiler_params=_params("parallel"),
        name="proj_odd",
    )(xa, g, mod, mod, *consts, cos, sin)


def _attn_kernel(*refs, mode, split, main, seq, has_sink, lam_init, tq, tq_sub):
    n_sub = tq // tq_sub
    it = iter(refs)
    q_ref = next(it)
    k_ref = v_ref = sink_ref = lam_ref = subln_ref = None
    bias_refs = ()
    if main is not None:
        k_ref, v_ref = next(it), next(it)
    kc_ref, vc_ref = next(it), next(it)
    if main == "na":
        bias_refs = tuple(next(it) for _ in range(n_sub))
    if has_sink:
        sink_ref = next(it)
    safe_ref = next(it)
    if mode == "diff":
        lam_ref, subln_ref = next(it), next(it)
    o_ref = next(it)
    vt_ref, vct_ref = (next(it), next(it)) if main == "full" else (None, None)

    pair = pl.program_id(1)
    qi = pl.program_id(2)
    q = q_ref[...]
    low = lax.broadcasted_iota(jnp.int32, (1, LANES), 1) < HALF
    if split == "mask":
        zero = jnp.zeros_like(q)
        qs = (jnp.where(low, q, zero), jnp.where(low, zero, q))
        k_of = lambda k, s: k
    else:
        qs = (q[:, :LANES], q[:, LANES:])
        k_of = lambda k, s: k[:, s * LANES:(s + 1) * LANES]
    q_of = lambda s, t: qs[s][t * tq_sub:(t + 1) * tq_sub]
    sinks = [sink_ref[2 * pair + s] * LOG2E for s in range(2)] if has_sink else None
    nt_dims = (((1,), (1,)), ((), ()))

    def finish(o_lo, o_hi):
        if mode == "diff":
            lv = lam_ref[...]
            lam = (jnp.exp(jnp.sum(lv[0:1] * lv[1:2], axis=-1, keepdims=True))
                   - jnp.exp(jnp.sum(lv[2:3] * lv[3:4], axis=-1, keepdims=True)) + lam_init)
            o = o_lo - lam * o_hi
            ms = jnp.mean(o * o, axis=-1, keepdims=True)
            o = o * lax.rsqrt(ms + EPS) * subln_ref[...] * (1.0 - lam_init)
        else:
            o = jnp.where(low, o_lo, o_hi)
        o_ref[...] = o.astype(BF16)

    def local_keys(t):
        if main == "window":
            span = tq_sub + 2 * WINDOW
            q0 = qi * tq + t * tq_sub
            w0 = jnp.clip(q0 - WINDOW, 0, seq - span)
            off = pl.multiple_of(w0, WINDOW)
            key = lax.broadcasted_iota(jnp.int32, (span, tq_sub), 0)
            qry = lax.broadcasted_iota(jnp.int32, (span, tq_sub), 1)
            inside = jnp.abs(key - qry + (w0 - q0)) <= WINDOW
            logit_fn = lambda lg_t, s: jnp.where(inside, lg_t, NEG)
        else:
            rows_q = tq_sub // GRID_W
            span_rows = rows_q + NA_ROWS
            w0 = jnp.clip(qi * (tq // GRID_W) + t * rows_q - NA_ROWS // 2, 0, seq // GRID_W - span_rows)
            off = pl.multiple_of(w0 * GRID_W, GRID_W)
            span = span_rows * GRID_W
            logit_fn = lambda lg_t, s: lg_t + bias_refs[t][0, s]
        return k_ref[pl.ds(off, span), :], v_ref[pl.ds(off, span), :], logit_fn

    def online_sub(t):
        def init(s):
            if has_sink:
                m0 = jnp.full((tq_sub, 1), sinks[s], F32)
                l0 = jnp.ones((tq_sub, 1), F32)
            else:
                m0 = jnp.full((tq_sub, 1), NEG, F32)
                l0 = jnp.zeros((tq_sub, 1), F32)
            return m0, l0, jnp.zeros((tq_sub, LANES), F32)

        def segment(state, k, v, logit_fn):
            out = []
            for s in range(2):
                m, l, acc = state[s]
                logits = lax.dot_general(q_of(s, t), k_of(k, s), nt_dims, preferred_element_type=F32)
                if logit_fn is not None:
                    logits = logit_fn(logits.T, s).T
                m_new = jnp.maximum(m, jnp.max(logits, axis=-1, keepdims=True))
                alpha = jnp.exp2(m - m_new)
                p = jnp.exp2(logits - m_new)
                l = alpha * l + jnp.sum(p, axis=-1, keepdims=True)
                acc = alpha * acc + jnp.dot(p.astype(BF16), v, preferred_element_type=F32)
                out.append((m_new, l, acc))
            return tuple(out)

        state = (init(0), init(1))
        if main == "full":
            def body(c, st):
                off = pl.multiple_of(c * TK_FULL, TK_FULL)
                return segment(st, k_ref[pl.ds(off, TK_FULL), :], v_ref[pl.ds(off, TK_FULL), :], None)
            state = lax.fori_loop(0, seq // TK_FULL, body, state)
        elif main is not None:
            state = segment(state, *local_keys(t))
        (_, l_lo, acc_lo), (_, l_hi, acc_hi) = segment(state, kc_ref[...], vc_ref[...], None)
        return acc_lo / l_lo, acc_hi / l_hi

    def probs_t(k, q_rows, s, logit_fn):
        logits_t = lax.dot_general(k_of(k, s), q_rows, nt_dims, preferred_element_type=F32)
        if logit_fn is not None:
            logits_t = logit_fn(logits_t, s)
        p = jnp.exp2(logits_t)
        return p.astype(BF16), jnp.sum(p, axis=0, keepdims=True)

    def max_free():
        loop = None
        if main == "full":
            @pl.when(qi == 0)
            def _():
                for c in range(seq // TK_MAX_FREE):
                    vt_ref[c] = v_ref[c * TK_MAX_FREE:(c + 1) * TK_MAX_FREE, :].T
                vct_ref[...] = vc_ref[...].T

            def body(c, carry):
                off = pl.multiple_of(c * TK_MAX_FREE, TK_MAX_FREE)
                k = k_ref[pl.ds(off, TK_MAX_FREE), :]
                acc, den = [], []
                for s in range(2):
                    p, p_sum = probs_t(k, qs[s], s, None)
                    acc.append(carry[s] + jnp.dot(vt_ref[c], p, preferred_element_type=F32))
                    den.append(carry[2 + s] + p_sum)
                return tuple(acc + den)

            zero_acc = jnp.zeros((LANES, tq), F32)
            zero_den = jnp.zeros((1, tq), F32)
            loop = lax.fori_loop(0, seq // TK_MAX_FREE, body, (zero_acc, zero_acc, zero_den, zero_den))
            vct = vct_ref[...]
        else:
            vct = vc_ref[...].T
        p_ctx = [probs_t(kc_ref[...], qs[s], s, None) for s in range(2)]
        if main in ("window", "na"):
            windows = [local_keys(t) for t in range(n_sub)]
            p_loc = [[probs_t(k, q_of(s, t), s, fn) for s in range(2)] for t, (k, _, fn) in enumerate(windows)]
            vts = [v.T for _, v, _ in windows]
        outs = []
        for s in range(2):
            acc = jnp.dot(vct, p_ctx[s][0], preferred_element_type=F32)
            den = p_ctx[s][1]
            if main == "full":
                acc, den = acc + loop[s], den + loop[2 + s]
            elif main is not None:
                acc = acc + jnp.concatenate(
                    [jnp.dot(vts[t], p_loc[t][s][0], preferred_element_type=F32) for t in range(n_sub)], axis=1)
                den = den + jnp.concatenate([p_loc[t][s][1] for t in range(n_sub)], axis=1)
            if has_sink:
                den = den + jnp.exp2(jnp.full((1, tq), sinks[s], F32))
            outs.append((acc / den).T)
        finish(*outs)

    def online():
        parts = [online_sub(t) for t in range(n_sub)]
        finish(*[jnp.concatenate([p[s] for p in parts], axis=0) if n_sub > 1 else parts[0][s]
                 for s in range(2)])

    safe = safe_ref[0] == 1
    pl.when(safe)(max_free)
    pl.when(jnp.logical_not(safe))(online)


def _attention(p, *, name, mode, split, main, n_pairs, q_blk, k_blk, v_blk, out_cols, ctx_queries,
               nbatch, seq, ctx_len, safe, bias=None, sink=None, lam=None, subln=None, lam_init=0.0):
    n_lat = nbatch * seq
    qw = LANES if split == "mask" else 2 * LANES
    if ctx_queries:
        tq, nq = ctx_len, 1
        q_row = lambda b, j, i: n_lat // ctx_len + b
        out_rows = nbatch * ctx_len
        out_row = lambda b, j, i: b
    else:
        tq = TQ_FULL if main == "full" else TQ
        nq = seq // tq
        q_row = lambda b, j, i: b * nq + i
        out_rows = n_lat
        out_row = q_row
    ctx_row = lambda b, j, i: n_lat // ctx_len + b

    in_specs = [pl.BlockSpec((tq, qw), lambda b, j, i: (q_row(b, j, i), q_blk(j)))]
    args = [p]
    if main is not None:
        in_specs += [pl.BlockSpec((seq, qw), lambda b, j, i: (b, k_blk(j))),
                     pl.BlockSpec((seq, LANES), lambda b, j, i: (b, v_blk(j)))]
        args += [p, p]
    in_specs += [pl.BlockSpec((ctx_len, qw), lambda b, j, i: (ctx_row(b, j, i), k_blk(j))),
                 pl.BlockSpec((ctx_len, LANES), lambda b, j, i: (ctx_row(b, j, i), v_blk(j)))]
    args += [p, p]
    tq_sub = TQ_SUB if main in ("window", "na") else tq
    n_sub = tq // tq_sub
    if main == "na":
        span = (tq_sub // GRID_W + NA_ROWS) * GRID_W
        last = nq * n_sub - 1
        cls = lambda g: jnp.where(g == 0, 0, jnp.where(g == last, 2, 1))
        for t in range(n_sub):
            in_specs.append(pl.BlockSpec((1, 2, span, tq_sub),
                                         lambda b, j, i, t=t: (cls(i * n_sub + t), j, 0, 0)))
            args.append(bias)
    if sink is not None:
        in_specs.append(pl.BlockSpec(memory_space=pltpu.SMEM))
        args.append(sink)
    in_specs.append(pl.BlockSpec(memory_space=pltpu.SMEM))
    args.append(safe)
    if mode == "diff":
        in_specs += [_whole(lam.shape), _whole(subln.shape)]
        args += [lam, subln]

    kern = functools.partial(_attn_kernel, mode=mode, split=split, main=main, seq=seq,
                             has_sink=sink is not None, lam_init=lam_init, tq=tq, tq_sub=tq_sub)
    scratch = []
    if main == "full":
        scratch = [pltpu.VMEM((seq // TK_MAX_FREE, LANES, TK_MAX_FREE), BF16), pltpu.VMEM((LANES, ctx_len), BF16)]
    return pl.pallas_call(
        kern,
        grid=(nbatch, n_pairs, nq),
        in_specs=in_specs,
        out_specs=pl.BlockSpec((tq, LANES), lambda b, j, i: (out_row(b, j, i), j)),
        out_shape=jax.ShapeDtypeStruct((out_rows, out_cols), BF16),
        scratch_shapes=scratch,
        compiler_params=_params("parallel", "parallel", "arbitrary"),
        name=name,
    )(*args)


def _na_bias_kernel(rpb_ref, o_ref, *, n_rows, rows_q):
    h = pl.program_id(0)
    n_dr = 2 * NA_ROWS - 1
    n_dc = 2 * NA_COLS - 1
    kc = lax.broadcasted_iota(jnp.int32, (GRID_W, LANES), 0)
    cq = lax.broadcasted_iota(jnp.int32, (GRID_W, LANES), 1) & (GRID_W - 1)
    dc_idx = jnp.clip(kc - cq, -(NA_COLS - 1), NA_COLS - 1) + NA_COLS - 1
    cs = jnp.clip(cq - NA_COLS // 2, 0, GRID_W - NA_COLS)
    col_ok = (kc >= cs) & (kc < cs + NA_COLS)
    tiles = [jnp.zeros((GRID_W, LANES), F32) for _ in range(n_dr)]
    for dc in range(n_dc):
        hit = dc_idx == dc
        for dr in range(n_dr):
            tiles[dr] = jnp.where(hit, rpb_ref[(h * n_dr + dr) * n_dc + dc] * LOG2E, tiles[dr])
    tiles = [jnp.where(col_ok, t, NEG) for t in tiles]
    masked = jnp.full((GRID_W, LANES), NEG, F32)
    low = lax.broadcasted_iota(jnp.int32, (GRID_W, LANES), 1) < HALF

    span_rows = rows_q + NA_ROWS
    first_q_row = (0, span_rows, n_rows - rows_q)
    for c in range(3):
        r0 = first_q_row[c]
        w0 = min(max(r0 - NA_ROWS // 2, 0), n_rows - span_rows)
        for kr in range(span_rows):
            k_abs = w0 + kr
            for m in range(rows_q // 2):
                halves = []
                for r in (r0 + 2 * m, r0 + 2 * m + 1):
                    rs = min(max(r - NA_ROWS // 2, 0), n_rows - NA_ROWS)
                    halves.append(tiles[k_abs - r + NA_ROWS - 1] if rs <= k_abs < rs + NA_ROWS else masked)
                o_ref[c, 0, kr * GRID_W:(kr + 1) * GRID_W, m * LANES:(m + 1) * LANES] = (
                    jnp.where(low, halves[0], halves[1]))


def _na_bias(rpb, seq):
    n_rows = seq // GRID_W
    rows_q = TQ_SUB // GRID_W
    span = (rows_q + NA_ROWS) * GRID_W
    kern = functools.partial(_na_bias_kernel, n_rows=n_rows, rows_q=rows_q)
    return pl.pallas_call(
        kern,
        grid=(NA_HEADS,),
        in_specs=[pl.BlockSpec(memory_space=pltpu.SMEM)],
        out_specs=pl.BlockSpec((3, 1, span, TQ_SUB), lambda h: (0, h, 0, 0)),
        out_shape=jax.ShapeDtypeStruct((3, NA_HEADS, span, TQ_SUB), F32),
        compiler_params=_params("parallel"),
        name="na_bias",
    )(rpb.reshape(-1))


def _post_kernel(*refs, n_x, n_o, bpb, nbatch):
    n_lat_blocks = bpb * nbatch
    x_refs, oa_refs, ob_refs = refs[:n_x], refs[n_x:n_x + n_o], refs[n_x + n_o:n_x + 2 * n_o]
    wo_ref, g1_ref, n2_ref, sh_ref, sc_ref, g2_ref, w1_ref, w2_ref, o_ref = refs[n_x + 2 * n_o:]
    oa = _stream_block(oa_refs, n_lat_blocks)
    ob = _stream_block(ob_refs, n_lat_blocks)
    half = oa.shape[1]
    y = (jnp.dot(oa, wo_ref[:half, :], preferred_element_type=F32)
         + jnp.dot(ob, wo_ref[half:, :], preferred_element_type=F32))
    x1 = _stream_block(x_refs, n_lat_blocks) + _mod_row(g1_ref, bpb, nbatch) * y
    h = _modulated(x1, n2_ref, _mod_row(sh_ref, bpb, nbatch), _mod_row(sc_ref, bpb, nbatch)).astype(BF16)
    a = jnp.dot(h, w1_ref[...], preferred_element_type=F32)
    a = jnp.square(jnp.maximum(a, 0.0)).astype(BF16)
    o_ref[...] = x1 + _mod_row(g2_ref, bpb, nbatch) * jnp.dot(a, w2_ref[...], preferred_element_type=F32)


def _post(x_src, oa_src, ob_src, wo, mod, n2, w1, w2, layer, n_blocks, bpb, nbatch):
    d = x_src[0].shape[1]
    half = oa_src[0].shape[1]
    n_lat_blocks = bpb * nbatch
    g1_spec, sh_spec, sc_spec, g2_spec = _mod_specs(mod, layer, (GATE1, SHIFT2, SCALE2, GATE2))
    resident = lambda a: pl.BlockSpec((None,) + a.shape[1:], lambda i: (layer, 0, 0),
                                      pipeline_mode=pl.Buffered(1))
    kern = functools.partial(_post_kernel, n_x=len(x_src), n_o=len(oa_src), bpb=bpb, nbatch=nbatch)
    return pl.pallas_call(
        kern,
        grid=(n_blocks,),
        in_specs=(_stream_specs(x_src, d, n_lat_blocks) + _stream_specs(oa_src, half, n_lat_blocks)
                  + _stream_specs(ob_src, half, n_lat_blocks)
                  + [resident(wo), g1_spec, _whole(n2.shape), sh_spec, sc_spec, g2_spec, resident(w1), resident(w2)]),
        out_specs=pl.BlockSpec((TM, d), lambda i: (i, 0)),
        out_shape=jax.ShapeDtypeStruct((n_blocks * TM, d), F32),
        compiler_params=_params("parallel"),
        name="post",
    )(*x_src, *oa_src, *ob_src, wo, mod, n2, mod, mod, mod, w1, w2)


def _block_diag_ones(group):
    idx = np.arange(CHUNK) // group
    return jnp.asarray(idx[:, None] == idx[None, :], dtype=BF16)


def _rope_tables(seq, rot_dim, lane0, pad_rows):
    t = jnp.arange(seq, dtype=jnp.int32)
    row = (t // GRID_W).astype(F32)
    col = (t % GRID_W).astype(F32)
    n_freq = rot_dim // 4
    freqs = jnp.power(ROPE_BASE, -jnp.arange(n_freq, dtype=F32) / n_freq)
    ar = row[:, None] * freqs[None, :]
    ac = col[:, None] * freqs[None, :]
    ang = jnp.concatenate([ar, ar, ac, ac], axis=-1)
    sign = jnp.asarray(np.tile(np.repeat([-1.0, 1.0], n_freq), 2), F32)
    cos, sin = jnp.cos(ang), jnp.sin(ang) * sign[None, :]
    if lane0 is None:
        reps = LANES // rot_dim
        cos, sin = jnp.tile(cos, (1, reps)), jnp.tile(sin, (1, reps))
    else:
        pad = ((0, 0), (lane0, LANES - lane0 - rot_dim))
        cos = jnp.pad(cos, pad, constant_values=1.0)
        sin = jnp.pad(sin, pad)
    cos = jnp.concatenate([cos, jnp.ones((pad_rows, LANES), F32)], axis=0)
    sin = jnp.concatenate([sin, jnp.zeros((pad_rows, LANES), F32)], axis=0)
    return cos, sin


def _logits_bounded(q_gain, k_gain, dim, bias=None, sink=None):
    bound = jnp.max(jnp.abs(q_gain)) * jnp.max(jnp.abs(k_gain)) * (dim ** 0.5) * BF16_ROUNDING_MARGIN
    if bias is not None:
        bound = bound + jnp.max(jnp.abs(bias))
    if sink is not None:
        bound = jnp.maximum(bound, jnp.max(jnp.abs(sink)))
    return (bound <= MAX_FREE_LOGIT_BOUND).astype(jnp.int32).reshape(1)


def _tile_gain(g, reps, scale=1.0):
    return (jnp.tile(g.astype(F32), reps) * scale)[None, :]


def _even_weights(w, q_g, k_g, sq_g, sk_g):
    nq = DIFF_HEADS * 2 * HEAD_DIM
    qa, ka, va = w[:, :nq], w[:, nq:2 * nq], w[:, 2 * nq:3 * nq]
    o = 3 * nq
    qb = w[:, o:o + SWA_Q_HEADS * HEAD_DIM]
    o += SWA_Q_HEADS * HEAD_DIM
    kb = [w[:, o + i * HEAD_DIM:o + (i + 1) * HEAD_DIM] for i in range(SWA_KV_HEADS)]
    o += SWA_KV_HEADS * HEAD_DIM
    vb = [w[:, o + i * HEAD_DIM:o + (i + 1) * HEAD_DIM] for i in range(SWA_KV_HEADS)]
    dup = lambda parts: [p for p in parts for _ in range(2)]
    w_new = jnp.concatenate([qa, ka, va, qb] + dup(kb) + dup(vb), axis=1).astype(BF16)
    scale = HEAD_DIM ** -0.5 * LOG2E
    ones = lambda n: jnp.ones((1, n), F32)
    gain = jnp.concatenate([
        _tile_gain(q_g, 2 * DIFF_HEADS, scale), _tile_gain(k_g, 2 * DIFF_HEADS), ones(nq),
        _tile_gain(sq_g, SWA_Q_HEADS, scale), _tile_gain(sk_g, 2 * SWA_KV_HEADS),
        ones(2 * SWA_KV_HEADS * HEAD_DIM)], axis=1)
    return w_new, gain


def _odd_weights(w, wq_up, wkv_up, qa_g, kva_g, mq_g, mk_g, nq_g, nk_g):
    d = w.shape[0]
    o = 0
    parts = []
    for n in (MLA_Q_LORA, MLA_KV_LORA, MLA_ROPE, NA_HEADS * HEAD_DIM, NA_HEADS * HEAD_DIM, NA_HEADS * HEAD_DIM):
        parts.append(w[:, o:o + n])
        o += n
    q_a, kv_a, k_r, nq, nk, nv = parts
    slot = jnp.concatenate([jnp.zeros((d, MLA_NOPE), w.dtype), k_r,
                            jnp.zeros((d, LANES - MLA_QK), w.dtype)], axis=1)
    w_new = jnp.concatenate([q_a, kv_a, nq, nk, nv, slot], axis=1).astype(BF16)
    wq = jnp.pad(wq_up.reshape(MLA_Q_LORA, MLA_HEADS, MLA_QK), ((0, 0), (0, 0), (0, MLA_PAD - MLA_QK)))
    wq = wq.reshape(MLA_Q_LORA, MLA_HEADS * MLA_PAD).astype(BF16)
    wkv = wkv_up.reshape(MLA_KV_LORA, MLA_HEADS, MLA_NOPE + MLA_V)
    wk = jnp.pad(wkv[..., :MLA_NOPE], ((0, 0), (0, 0), (0, MLA_PAD - MLA_NOPE))).reshape(MLA_KV_LORA, -1)
    wv = wkv[..., MLA_NOPE:].reshape(MLA_KV_LORA, MLA_HEADS * MLA_V)
    wkv_new = jnp.concatenate([wk, wv], axis=1).astype(BF16)
    pad_gain = lambda g, scale: _tile_gain(jnp.pad(g.astype(F32), (0, MLA_PAD - MLA_QK)), CHUNK // MLA_PAD, scale)
    gains = (qa_g.astype(F32)[None, :], kva_g.astype(F32)[None, :],
             pad_gain(mq_g, MLA_QK ** -0.5 * LOG2E), pad_gain(mk_g, 1.0),
             _tile_gain(nq_g, CHUNK // HEAD_DIM, HEAD_DIM ** -0.5 * LOG2E), _tile_gain(nk_g, CHUNK // HEAD_DIM))
    return w_new, wq, wkv_new, gains


def kernel(x, c, ctx, c_ctx, ada_w, ada_b, norm1_g, norm2_g, w_out, mlp_w1, mlp_w2, ev_w_in, diff_q_g,
           diff_k_g, diff_lam, diff_subln_g, swa_q_g, swa_k_g, swa_sink, od_w_in, mla_qa_g, mla_kva_g,
           mla_wq_up, mla_wkv_up, mla_q_g, mla_k_g, na_q_g, na_k_g, na_rpb):
    nbatch, seq, d = x.shape
    ctx_len = ctx.shape[1]
    depth = ada_w.shape[0]
    n_lat = nbatch * seq
    n_ctx = nbatch * ctx_len
    assert seq % TM == 0 and n_ctx == TM and seq % TQ == 0 and seq % GRID_W == 0
    assert seq % TQ_FULL == 0 and seq % TK_MAX_FREE == 0 and seq % TK_FULL == 0
    assert seq // GRID_W >= 2 * (NA_ROWS + TQ_SUB // GRID_W) and seq >= TQ_SUB + 2 * WINDOW
    bpb = seq // TM

    mod_rows = 8
    cvec = jnp.concatenate([c, c_ctx[None, :], jnp.zeros((mod_rows - nbatch - 1, d), F32)], axis=0)
    mod = _modulation(cvec, ada_w, ada_b)

    cos64, sin64 = _rope_tables(seq, HEAD_DIM, None, n_ctx)
    cos32, sin32 = _rope_tables(seq, MLA_ROPE, MLA_NOPE, n_ctx)
    g64 = _block_diag_ones(HEAD_DIM)
    g128 = _block_diag_ones(LANES)

    x_src = (x.reshape(n_lat, d), ctx.reshape(n_ctx, d))
    wo_all, w1_all, w2_all = w_out.astype(BF16), mlp_w1.astype(BF16), mlp_w2.astype(BF16)
    common = dict(nbatch=nbatch, seq=seq, ctx_len=ctx_len)

    for layer in range(depth):
        need_ctx = layer < depth - 1
        n_blocks = bpb * nbatch + (1 if need_ctx else 0)
        n1 = norm1_g[layer][None, :]
        n2 = norm2_g[layer][None, :]
        i = layer // 2
        if layer % 2 == 0:
            lam_init = 0.8 - 0.6 * math.exp(-0.3 * layer)
            w_in, gain = _even_weights(ev_w_in[i], diff_q_g[i], diff_k_g[i], swa_q_g[i], swa_k_g[i])
            p = _proj_even(x_src, n1, mod, layer, w_in, gain, g64, cos64, sin64, bpb, nbatch)
            lam = diff_lam[i].astype(F32)
            subln = diff_subln_g[i].astype(F32)[None, :]
            sink = swa_sink[i].astype(F32)
            diff_kw = dict(mode="diff", split="mask", n_pairs=DIFF_HEADS, q_blk=lambda j: EVEN_QA // LANES + j,
                           k_blk=lambda j: EVEN_KA // LANES + j, v_blk=lambda j: EVEN_VA // LANES + j,
                           out_cols=DIFF_HEADS * LANES,
                           lam=lam, subln=subln, lam_init=lam_init,
                           safe=_logits_bounded(diff_q_g[i], diff_k_g[i], HEAD_DIM), **common)
            swa_kw = dict(mode="pair", split="mask", n_pairs=SWA_Q_HEADS // 2,
                          q_blk=lambda j: EVEN_QB // LANES + j, k_blk=lambda j: EVEN_KB // LANES + j // 2,
                          v_blk=lambda j: EVEN_VB // LANES + j // 2,
                          out_cols=SWA_Q_HEADS * HEAD_DIM, sink=sink,
                          safe=_logits_bounded(swa_q_g[i], swa_k_g[i], HEAD_DIM, sink=sink), **common)
            oa = _attention(p, name="diff_attn", main="full", ctx_queries=False, **diff_kw)
            ob = _attention(p, name="window_attn", main="window", ctx_queries=False, **swa_kw)
            if need_ctx:
                oa_c = _attention(p, name="diff_attn_ctx", main=None, ctx_queries=True, **diff_kw)
                ob_c = _attention(p, name="window_attn_ctx", main=None, ctx_queries=True, **swa_kw)
        else:
            w_in, wq, wkv, gains = _odd_weights(od_w_in[i], mla_wq_up[i], mla_wkv_up[i], mla_qa_g[i],
                                                mla_kva_g[i], mla_q_g[i], mla_k_g[i], na_q_g[i], na_k_g[i])
            p = _proj_odd(x_src[0], n1, mod, layer, w_in, wq, wkv, gains, g64, g128, cos32, sin32, bpb, nbatch)
            bias = _na_bias(na_rpb[i].astype(F32), seq)
            mla_kw = dict(mode="pair", split="slice", n_pairs=MLA_HEADS // 2, q_blk=lambda j: j,
                          k_blk=lambda j: ODD_K // CHUNK + j, v_blk=lambda j: ODD_V // LANES + j,
                          out_cols=MLA_HEADS * MLA_V,
                          safe=_logits_bounded(mla_q_g[i], mla_k_g[i], MLA_QK), **common)
            na_kw = dict(mode="pair", split="mask", n_pairs=NA_HEADS // 2,
                         q_blk=lambda j: ODD_NQ // LANES + j, k_blk=lambda j: ODD_NK // LANES + j,
                         v_blk=lambda j: ODD_NV // LANES + j, out_cols=NA_HEADS * HEAD_DIM,
                         safe=_logits_bounded(na_q_g[i], na_k_g[i], HEAD_DIM, bias=na_rpb[i]), **common)
            oa = _attention(p, name="mla_attn", main="full", ctx_queries=False, **mla_kw)
            ob = _attention(p, name="na_attn", main="na", ctx_queries=False, bias=bias, **na_kw)
            if need_ctx:
                oa_c = _attention(p, name="mla_attn_ctx", main=None, ctx_queries=True, **mla_kw)
                ob_c = _attention(p, name="na_attn_ctx", main=None, ctx_queries=True, **na_kw)
        oa_src, ob_src = ((oa, oa_c), (ob, ob_c)) if need_ctx else ((oa,), (ob,))
        x_src = (_post(x_src, oa_src, ob_src, wo_all, mod, n2, w1_all, w2_all, layer, n_blocks, bpb, nbatch),)
    return x_src[0].reshape(nbatch, seq, d)
```

```python
import functools
import math

import numpy as np
import jax
import jax.numpy as jnp
from jax import lax
from jax.experimental import pallas as pl
from jax.experimental.pallas import tpu as pltpu

F32 = jnp.float32
BF16 = jnp.bfloat16

LANES = 128
HALF = LANES // 2
CHUNK = 2 * LANES
GRID_W = 64
HEAD_DIM = 64
EPS = 1e-6
ROPE_BASE = 10000.0
DIFF_HEADS = 4
SWA_Q_HEADS = 8
SWA_KV_HEADS = 2
WINDOW = 128
MLA_HEADS = 8
MLA_Q_LORA = 512
MLA_KV_LORA = 256
MLA_NOPE = 64
MLA_ROPE = 32
MLA_QK = MLA_NOPE + MLA_ROPE
MLA_V = 64
NA_HEADS = 8
NA_ROWS = 8
NA_COLS = 16
NEG = -1e30
LOG2E = math.log2(math.e)
MAX_FREE_LOGIT_BOUND = 50.0
BF16_ROUNDING_MARGIN = 1.02

TM = 512
TQ = 2048
TQ_SUB = 256
TQ_FULL = 1024
TK_FULL = 512
TK_MAX_FREE = 2048
MOD_TN = 1536
VMEM_LIMIT = 56 * 1024 * 1024


def _params(*sem):
    return pltpu.CompilerParams(dimension_semantics=sem, vmem_limit_bytes=VMEM_LIMIT)


def _mod_kernel(c_ref, w_ref, b_ref, o_ref):
    c = c_ref[...]
    a = (c * jax.nn.sigmoid(c)).astype(BF16)
    w = w_ref[0].astype(BF16)
    o_ref[0] = jnp.dot(a, w, preferred_element_type=F32) + b_ref[0]


def _modulation(cvec, ada_w, ada_b):
    depth, d, n = ada_w.shape
    rows = cvec.shape[0]
    return pl.pallas_call(
        _mod_kernel,
        grid=(depth, n // MOD_TN),
        in_specs=[
            pl.BlockSpec((rows, d), lambda l, j: (0, 0)),
            pl.BlockSpec((1, d, MOD_TN), lambda l, j: (l, 0, j)),
            pl.BlockSpec((1, 1, MOD_TN), lambda l, j: (l, 0, j)),
        ],
        out_specs=pl.BlockSpec((1, rows, MOD_TN), lambda l, j: (l, 0, j)),
        out_shape=jax.ShapeDtypeStruct((depth, rows, n), F32),
        compiler_params=_params("parallel", "parallel"),
        name="modulation",
    )(cvec, ada_w, ada_b.reshape(depth, 1, n))


def _modulated(x, g_ref, shift, scale):
    ms = jnp.mean(x * x, axis=-1, keepdims=True)
    y = x * lax.rsqrt(ms + EPS) * g_ref[...]
    return y * (1.0 + scale) + shift


def _group_meansq(chunks, g_ref, inv_n):
    rows = chunks[0].shape[0]
    sq = jnp.concatenate([(c * c).astype(BF16) for c in chunks], axis=0)
    ss = jnp.dot(sq, g_ref[...], preferred_element_type=F32) * inv_n
    return [ss[i * rows:(i + 1) * rows] for i in range(len(chunks))]


def _swap_quarters(x, quarter):
    lane = lax.broadcasted_iota(jnp.int32, x.shape, 1)
    odd = (lane & quarter) != 0
    from_lower = pltpu.roll(x, quarter, 1)
    from_upper = pltpu.roll(x, LANES - quarter, 1)
    return jnp.where(odd, from_lower, from_upper)


def _rope(x, cos, sin_signed, quarter):
    return x * cos + _swap_quarters(x, quarter) * sin_signed


def _norm_rope_chunk(acc, meansq, gain, cos, sin, quarter):
    y = acc * lax.rsqrt(meansq + EPS) * gain
    if quarter is None:
        return y
    return jnp.concatenate([_rope(y[:, :LANES], cos, sin, quarter),
                            _rope(y[:, LANES:], cos, sin, quarter)], axis=1)


def _stream_specs(src, cols, n_lat_blocks):
    if len(src) == 1:
        return [pl.BlockSpec((TM, cols), lambda i: (i, 0))]
    return [pl.BlockSpec((TM, cols), lambda i: (jnp.minimum(i, n_lat_blocks - 1), 0)),
            pl.BlockSpec((TM, cols), lambda i: (0, 0))]


def _stream_block(refs, n_lat_blocks):
    if len(refs) == 1:
        return refs[0][...]
    return jnp.where(pl.program_id(0) < n_lat_blocks, refs[0][...], refs[1][...])


SHIFT1, SCALE1, GATE1, SHIFT2, SCALE2, GATE2 = range(6)


def _mod_specs(mod, layer, chunks):
    d = mod.shape[2] // 6
    return [pl.BlockSpec((None, mod.shape[1], d), lambda i, k=k: (layer, 0, k)) for k in chunks]


def _mod_row(ref, bpb, nbatch):
    row = jnp.minimum(pl.program_id(0) // bpb, nbatch)
    return ref[pl.ds(row, 1), :]


def _rope_spec(bpb, nbatch):
    return pl.BlockSpec((TM, LANES), lambda i: (jnp.where(i < bpb * nbatch, i % bpb, bpb), 0))


def _whole(shape):
    return pl.BlockSpec(shape, lambda *_: (0,) * len(shape))


EVEN_QA = 0
EVEN_KA = EVEN_QA + DIFF_HEADS * 2 * HEAD_DIM
EVEN_VA = EVEN_KA + DIFF_HEADS * 2 * HEAD_DIM
EVEN_QB = EVEN_VA + DIFF_HEADS * 2 * HEAD_DIM
EVEN_KB = EVEN_QB + SWA_Q_HEADS * HEAD_DIM
EVEN_VB = EVEN_KB + 2 * SWA_KV_HEADS * HEAD_DIM
EVEN_COLS = EVEN_VB + 2 * SWA_KV_HEADS * HEAD_DIM
EVEN_NORM_CHUNKS = tuple(c for c in range(EVEN_COLS // CHUNK)
                         if c * CHUNK < EVEN_VA or EVEN_QB <= c * CHUNK < EVEN_VB)
EVEN_DOT_RANGES = ((EVEN_QA // CHUNK, EVEN_VA // CHUNK), (EVEN_QB // CHUNK, EVEN_VB // CHUNK),
                   (EVEN_VA // CHUNK, EVEN_QB // CHUNK), (EVEN_VB // CHUNK, EVEN_COLS // CHUNK))


def _proj_even_kernel(*refs, n_x, bpb, nbatch):
    n_lat_blocks = bpb * nbatch
    x_refs = refs[:n_x]
    g_ref, sh_ref, sc_ref, w_ref, gain_ref, g64_ref, cos_ref, sin_ref, o_ref = refs[n_x:]
    h = _modulated(_stream_block(x_refs, n_lat_blocks), g_ref, _mod_row(sh_ref, bpb, nbatch),
                   _mod_row(sc_ref, bpb, nbatch)).astype(BF16)
    cos = cos_ref[...]
    sin = sin_ref[...]
    for first, last in EVEN_DOT_RANGES:
        full = jnp.dot(h, w_ref[:, first * CHUNK:last * CHUNK], preferred_element_type=F32)
        chunk = lambda c: full[:, (c - first) * CHUNK:(c - first + 1) * CHUNK]
        normed = [c for c in range(first, last) if c in EVEN_NORM_CHUNKS]
        meansq = {}
        if normed:
            meansq = dict(zip(normed, _group_meansq([chunk(c) for c in normed], g64_ref, 1.0 / HEAD_DIM)))
        for c in range(first, last):
            cols = slice(c * CHUNK, (c + 1) * CHUNK)
            acc = chunk(c)
            if c in EVEN_NORM_CHUNKS:
                acc = _norm_rope_chunk(acc, meansq[c], gain_ref[:, cols], cos, sin, HEAD_DIM // 4)
            o_ref[:, cols] = acc.astype(BF16)


def _proj_even(x_src, g, mod, layer, w, gain, g64, cos, sin, bpb, nbatch):
    d = x_src[0].shape[1]
    n_lat_blocks = bpb * nbatch
    kern = functools.partial(_proj_even_kernel, n_x=len(x_src), bpb=bpb, nbatch=nbatch)
    return pl.pallas_call(
        kern,
        grid=(n_lat_blocks + 1,),
        in_specs=_stream_specs(x_src, d, n_lat_blocks) + [_whole(g.shape)] + _mod_specs(mod, layer, (SHIFT1, SCALE1)) + [
            _whole(w.shape), _whole(gain.shape), _whole(g64.shape),
            _rope_spec(bpb, nbatch), _rope_spec(bpb, nbatch),
        ],
        out_specs=pl.BlockSpec((TM, EVEN_COLS), lambda i: (i, 0)),
        out_shape=jax.ShapeDtypeStruct(((n_lat_blocks + 1) * TM, EVEN_COLS), BF16),
        compiler_params=_params("parallel"),
        name="proj_even",
    )(*x_src, g, mod, mod, w, gain, g64, cos, sin)


NA_WIDTH = NA_HEADS * HEAD_DIM
MLA_PAD = LANES
MLA_WIDTH = MLA_HEADS * MLA_PAD
ODD_W_KVA = MLA_Q_LORA
ODD_W_NQ = ODD_W_KVA + MLA_KV_LORA
ODD_W_NV = ODD_W_NQ + 2 * NA_WIDTH
ODD_W_SLOT = ODD_W_NV + NA_WIDTH
ODD_W_COLS = ODD_W_SLOT + LANES
ODD_K = MLA_WIDTH
ODD_V = ODD_K + MLA_WIDTH
ODD_NQ = ODD_V + MLA_HEADS * MLA_V
ODD_NK = ODD_NQ + NA_WIDTH
ODD_NV = ODD_NK + NA_WIDTH
ODD_COLS = ODD_NV + NA_WIDTH


def _proj_odd_kernel(x_ref, g_ref, sh_ref, sc_ref, w_ref, wq_ref, wkv_ref, qa_g_ref, kva_g_ref,
                     mq_g_ref, mk_g_ref, nq_g_ref, nk_g_ref, g64_ref, g128_ref, cos_ref, sin_ref, o_ref,
                     *, bpb, nbatch):
    h = _modulated(x_ref[...], g_ref, _mod_row(sh_ref, bpb, nbatch), _mod_row(sc_ref, bpb, nbatch)).astype(BF16)
    cos = cos_ref[...]
    sin = sin_ref[...]
    quarter = MLA_ROPE // 4

    lora = jnp.dot(h, w_ref[:, :ODD_W_NQ], preferred_element_type=F32)
    rest = jnp.dot(h, w_ref[:, ODD_W_NQ:], preferred_element_type=F32)

    def low_rank(a, gain_ref):
        ms = jnp.mean(a * a, axis=-1, keepdims=True)
        return (a * lax.rsqrt(ms + EPS) * gain_ref[...]).astype(BF16)

    n_mla = MLA_WIDTH // CHUNK
    chunks_of = lambda a, first, n: [a[:, first + c * CHUNK:first + (c + 1) * CHUNK] for c in range(n)]
    qa = low_rank(lora[:, :MLA_Q_LORA], qa_g_ref)
    qf = jnp.dot(qa, wq_ref[...], preferred_element_type=F32)
    q_chunks = chunks_of(qf, 0, n_mla)
    ms_q = _group_meansq(q_chunks, g128_ref, 1.0 / MLA_QK)
    kva = low_rank(lora[:, ODD_W_KVA:], kva_g_ref)
    kvf = jnp.dot(kva, wkv_ref[...], preferred_element_type=F32)
    kslot = rest[:, ODD_W_SLOT - ODD_W_NQ:]
    kslot2 = jnp.concatenate([kslot, kslot], axis=1)
    k_rot = _swap_quarters(kslot * mk_g_ref[:, :LANES], quarter)
    k_rot2 = jnp.concatenate([k_rot, k_rot], axis=1)
    cos2 = jnp.concatenate([cos, cos], axis=1)
    sin2 = jnp.concatenate([sin, sin], axis=1)
    k_chunks = [kc + kslot2 for kc in chunks_of(kvf, 0, n_mla)]
    ms_k = _group_meansq(k_chunks, g128_ref, 1.0 / MLA_QK)
    for c in range(n_mla):
        y = _norm_rope_chunk(q_chunks[c], ms_q[c], mq_g_ref[...], cos, sin, quarter)
        o_ref[:, c * CHUNK:(c + 1) * CHUNK] = y.astype(BF16)
    for c in range(n_mla):
        inv_rms = lax.rsqrt(ms_k[c] + EPS)
        y = (k_chunks[c] * inv_rms * mk_g_ref[...]) * cos2 + (inv_rms * k_rot2) * sin2
        o_ref[:, ODD_K + c * CHUNK:ODD_K + (c + 1) * CHUNK] = y.astype(BF16)
    o_ref[:, ODD_V:ODD_NQ] = kvf[:, MLA_WIDTH:].astype(BF16)

    n_na = NA_WIDTH // CHUNK
    n_chunks = chunks_of(rest, 0, 2 * n_na)
    ms64 = _group_meansq(n_chunks, g64_ref, 1.0 / HEAD_DIM)
    for c in range(2 * n_na):
        gain = nq_g_ref[...] if c < n_na else nk_g_ref[...]
        y = _norm_rope_chunk(n_chunks[c], ms64[c], gain, None, None, None)
        o_ref[:, ODD_NQ + c * CHUNK:ODD_NQ + (c + 1) * CHUNK] = y.astype(BF16)
    o_ref[:, ODD_NV:ODD_COLS] = rest[:, ODD_W_NV - ODD_W_NQ:ODD_W_SLOT - ODD_W_NQ].astype(BF16)


def _proj_odd(xa, g, mod, layer, w, wq, wkv, gains, g64, g128, cos, sin, bpb, nbatch):
    na, d = xa.shape
    tok_specs = _stream_specs((xa,), d, bpb * nbatch) + [_whole(g.shape)] + _mod_specs(mod, layer, (SHIFT1, SCALE1))
    consts = (w, wq, wkv) + tuple(gains) + (g64, g128)
    return pl.pallas_call(
        functools.partial(_proj_odd_kernel, bpb=bpb, nbatch=nbatch),
        grid=(na // TM,),
        in_specs=tok_specs + [_whole(a.shape) for a in consts] + [_rope_spec(bpb, nbatch)] * 2,
        out_specs=pl.BlockSpec((TM, ODD_COLS), lambda i: (i, 0)),
        out_shape=jax.ShapeDtypeStruct((na, ODD_COLS), BF16),
        compiler_params=_params("parallel"),
        name="proj_odd",
    )(xa, g, mod, mod, *consts, cos, sin)


def _attn_kernel(*refs, mode, split, main, seq, has_sink, lam_init, tq, tq_sub):
    n_sub = tq // tq_sub
    it = iter(refs)
    q_ref = next(it)
    k_ref = v_ref = sink_ref = lam_ref = subln_ref = None
    bias_refs = ()
    if main is not None:
        k_ref, v_ref = next(it), next(it)
    kc_ref, vc_ref = next(it), next(it)
    if main == "na":
        bias_refs = tuple(next(it) for _ in range(n_sub))
    if has_sink:
        sink_ref = next(it)
    safe_ref = next(it)
    if mode == "diff":
        lam_ref, subln_ref = next(it), next(it)
    o_ref = next(it)
    vt_ref, vct_ref = (next(it), next(it)) if main == "full" else (None, None)

    pair = pl.program_id(1)
    qi = pl.program_id(2)
    q = q_ref[...]
    low = lax.broadcasted_iota(jnp.int32, (1, LANES), 1) < HALF
    if split == "mask":
        zero = jnp.zeros_like(q)
        qs = (jnp.where(low, q, zero), jnp.where(low, zero, q))
        k_of = lambda k, s: k
    else:
        qs = (q[:, :LANES], q[:, LANES:])
        k_of = lambda k, s: k[:, s * LANES:(s + 1) * LANES]
    q_of = lambda s, t: qs[s][t * tq_sub:(t + 1) * tq_sub]
    sinks = [sink_ref[2 * pair + s] * LOG2E for s in range(2)] if has_sink else None
    nt_dims = (((1,), (1,)), ((), ()))

    def finish(o_lo, o_hi):
        if mode == "diff":
            lv = lam_ref[...]
            lam = (jnp.exp(jnp.sum(lv[0:1] * lv[1:2], axis=-1, keepdims=True))
                   - jnp.exp(jnp.sum(lv[2:3] * lv[3:4], axis=-1, keepdims=True)) + lam_init)
            o = o_lo - lam * o_hi
            ms = jnp.mean(o * o, axis=-1, keepdims=True)
            o = o * lax.rsqrt(ms + EPS) * subln_ref[...] * (1.0 - lam_init)
        else:
            o = jnp.where(low, o_lo, o_hi)
        o_ref[...] = o.astype(BF16)

    def local_keys(t):
        if main == "window":
            span = tq_sub + 2 * WINDOW
            q0 = qi * tq + t * tq_sub
            w0 = jnp.clip(q0 - WINDOW, 0, seq - span)
            off = pl.multiple_of(w0, WINDOW)
            key = lax.broadcasted_iota(jnp.int32, (span, tq_sub), 0)
            qry = lax.broadcasted_iota(jnp.int32, (span, tq_sub), 1)
            inside = jnp.abs(key - qry + (w0 - q0)) <= WINDOW
            logit_fn = lambda lg_t, s: jnp.where(inside, lg_t, NEG)
        else:
            rows_q = tq_sub // GRID_W
            span_rows = rows_q + NA_ROWS
            w0 = jnp.clip(qi * (tq // GRID_W) + t * rows_q - NA_ROWS // 2, 0, seq // GRID_W - span_rows)
            off = pl.multiple_of(w0 * GRID_W, GRID_W)
            span = span_rows * GRID_W
            logit_fn = lambda lg_t, s: lg_t + bias_refs[t][0, s]
        return k_ref[pl.ds(off, span), :], v_ref[pl.ds(off, span), :], logit_fn

    def online_sub(t):
        def init(s):
            if has_sink:
                m0 = jnp.full((tq_sub, 1), sinks[s], F32)
                l0 = jnp.ones((tq_sub, 1), F32)
            else:
                m0 = jnp.full((tq_sub, 1), NEG, F32)
                l0 = jnp.zeros((tq_sub, 1), F32)
            return m0, l0, jnp.zeros((tq_sub, LANES), F32)

        def segment(state, k, v, logit_fn):
            out = []
            for s in range(2):
                m, l, acc = state[s]
                logits = lax.dot_general(q_of(s, t), k_of(k, s), nt_dims, preferred_element_type=F32)
                if logit_fn is not None:
                    logits = logit_fn(logits.T, s).T
                m_new = jnp.maximum(m, jnp.max(logits, axis=-1, keepdims=True))
                alpha = jnp.exp2(m - m_new)
                p = jnp.exp2(logits - m_new)
                l = alpha * l + jnp.sum(p, axis=-1, keepdims=True)
                acc = alpha * acc + jnp.dot(p.astype(BF16), v, preferred_element_type=F32)
                out.append((m_new, l, acc))
            return tuple(out)

        state = (init(0), init(1))
        if main == "full":
            def body(c, st):
                off = pl.multiple_of(c * TK_FULL, TK_FULL)
                return segment(st, k_ref[pl.ds(off, TK_FULL), :], v_ref[pl.ds(off, TK_FULL), :], None)
            state = lax.fori_loop(0, seq // TK_FULL, body, state)
        elif main is not None:
            state = segment(state, *local_keys(t))
        (_, l_lo, acc_lo), (_, l_hi, acc_hi) = segment(state, kc_ref[...], vc_ref[...], None)
        return acc_lo / l_lo, acc_hi / l_hi

    def probs_t(k, q_rows, s, logit_fn):
        logits_t = lax.dot_general(k_of(k, s), q_rows, nt_dims, preferred_element_type=F32)
        if logit_fn is not None:
            logits_t = logit_fn(logits_t, s)
        p = jnp.exp2(logits_t)
        return p.astype(BF16), jnp.sum(p, axis=0, keepdims=True)

    def max_free():
        if main == "full":
            @pl.when(qi == 0)
            def _():
                for c in range(seq // TK_MAX_FREE):
                    vt_ref[c] = v_ref[c * TK_MAX_FREE:(c + 1) * TK_MAX_FREE, :].T
                vct_ref[...] = vc_ref[...].T

            acc_lat = [jnp.zeros((LANES, tq), F32)] * 2
            den_lat = [jnp.zeros((1, tq), F32)] * 2
            for c in range(seq // TK_MAX_FREE):
                k = k_ref[c * TK_MAX_FREE:(c + 1) * TK_MAX_FREE, :]
                for s in range(2):
                    p, p_sum = probs_t(k, qs[s], s, None)
                    acc_lat[s] = acc_lat[s] + jnp.dot(vt_ref[c], p, preferred_element_type=F32)
                    den_lat[s] = den_lat[s] + p_sum
            vct = vct_ref[...]
        else:
            vct = vc_ref[...].T
        p_ctx = [probs_t(kc_ref[...], qs[s], s, None) for s in range(2)]
        if main in ("window", "na"):
            windows = [local_keys(t) for t in range(n_sub)]
            p_loc = [[probs_t(k, q_of(s, t), s, fn) for s in range(2)] for t, (k, _, fn) in enumerate(windows)]
            vts = [v.T for _, v, _ in windows]
        outs = []
        for s in range(2):
            acc = jnp.dot(vct, p_ctx[s][0], preferred_element_type=F32)
            den = p_ctx[s][1]
            if main == "full":
                acc, den = acc + acc_lat[s], den + den_lat[s]
            elif main is not None:
                acc = acc + jnp.concatenate(
                    [jnp.dot(vts[t], p_loc[t][s][0], preferred_element_type=F32) for t in range(n_sub)], axis=1)
                den = den + jnp.concatenate([p_loc[t][s][1] for t in range(n_sub)], axis=1)
            if has_sink:
                den = den + jnp.exp2(jnp.full((1, tq), sinks[s], F32))
            outs.append((acc / den).T)
        finish(*outs)

    def online():
        parts = [online_sub(t) for t in range(n_sub)]
        finish(*[jnp.concatenate([p[s] for p in parts], axis=0) if n_sub > 1 else parts[0][s]
                 for s in range(2)])

    safe = safe_ref[0] == 1
    pl.when(safe)(max_free)
    pl.when(jnp.logical_not(safe))(online)


def _attention(p, *, name, mode, split, main, n_pairs, q_blk, k_blk, v_blk, out_cols, ctx_queries,
               nbatch, seq, ctx_len, safe, bias=None, sink=None, lam=None, subln=None, lam_init=0.0):
    n_lat = nbatch * seq
    qw = LANES if split == "mask" else 2 * LANES
    if ctx_queries:
        tq, nq = ctx_len, 1
        q_row = lambda b, j, i: n_lat // ctx_len + b
        out_rows = nbatch * ctx_len
        out_row = lambda b, j, i: b
    else:
        tq = TQ_FULL if main == "full" else TQ
        nq = seq // tq
        q_row = lambda b, j, i: b * nq + i
        out_rows = n_lat
        out_row = q_row
    ctx_row = lambda b, j, i: n_lat // ctx_len + b

    in_specs = [pl.BlockSpec((tq, qw), lambda b, j, i: (q_row(b, j, i), q_blk(j)))]
    args = [p]
    if main is not None:
        in_specs += [pl.BlockSpec((seq, qw), lambda b, j, i: (b, k_blk(j))),
                     pl.BlockSpec((seq, LANES), lambda b, j, i: (b, v_blk(j)))]
        args += [p, p]
    in_specs += [pl.BlockSpec((ctx_len, qw), lambda b, j, i: (ctx_row(b, j, i), k_blk(j))),
                 pl.BlockSpec((ctx_len, LANES), lambda b, j, i: (ctx_row(b, j, i), v_blk(j)))]
    args += [p, p]
    tq_sub = TQ_SUB if main in ("window", "na") else tq
    n_sub = tq // tq_sub
    if main == "na":
        span = (tq_sub // GRID_W + NA_ROWS) * GRID_W
        last = nq * n_sub - 1
        cls = lambda g: jnp.where(g == 0, 0, jnp.where(g == last, 2, 1))
        for t in range(n_sub):
            in_specs.append(pl.BlockSpec((1, 2, span, tq_sub),
                                         lambda b, j, i, t=t: (cls(i * n_sub + t), j, 0, 0)))
            args.append(bias)
    if sink is not None:
        in_specs.append(pl.BlockSpec(memory_space=pltpu.SMEM))
        args.append(sink)
    in_specs.append(pl.BlockSpec(memory_space=pltpu.SMEM))
    args.append(safe)
    if mode == "diff":
        in_specs += [_whole(lam.shape), _whole(subln.shape)]
        args += [lam, subln]

    kern = functools.partial(_attn_kernel, mode=mode, split=split, main=main, seq=seq,
                             has_sink=sink is not None, lam_init=lam_init, tq=tq, tq_sub=tq_sub)
    scratch = []
    if main == "full":
        scratch = [pltpu.VMEM((seq // TK_MAX_FREE, LANES, TK_MAX_FREE), BF16), pltpu.VMEM((LANES, ctx_len), BF16)]
    return pl.pallas_call(
        kern,
        grid=(nbatch, n_pairs, nq),
        in_specs=in_specs,
        out_specs=pl.BlockSpec((tq, LANES), lambda b, j, i: (out_row(b, j, i), j)),
        out_shape=jax.ShapeDtypeStruct((out_rows, out_cols), BF16),
        scratch_shapes=scratch,
        compiler_params=_params("parallel", "parallel", "arbitrary"),
        name=name,
    )(*args)


def _na_bias_kernel(rpb_ref, o_ref, *, n_rows, rows_q):
    h = pl.program_id(0)
    n_dr = 2 * NA_ROWS - 1
    n_dc = 2 * NA_COLS - 1
    kc = lax.broadcasted_iota(jnp.int32, (GRID_W, LANES), 0)
    cq = lax.broadcasted_iota(jnp.int32, (GRID_W, LANES), 1) & (GRID_W - 1)
    dc_idx = jnp.clip(kc - cq, -(NA_COLS - 1), NA_COLS - 1) + NA_COLS - 1
    cs = jnp.clip(cq - NA_COLS // 2, 0, GRID_W - NA_COLS)
    col_ok = (kc >= cs) & (kc < cs + NA_COLS)
    tiles = [jnp.zeros((GRID_W, LANES), F32) for _ in range(n_dr)]
    for dc in range(n_dc):
        hit = dc_idx == dc
        for dr in range(n_dr):
            tiles[dr] = jnp.where(hit, rpb_ref[(h * n_dr + dr) * n_dc + dc] * LOG2E, tiles[dr])
    tiles = [jnp.where(col_ok, t, NEG) for t in tiles]
    masked = jnp.full((GRID_W, LANES), NEG, F32)
    low = lax.broadcasted_iota(jnp.int32, (GRID_W, LANES), 1) < HALF

    span_rows = rows_q + NA_ROWS
    first_q_row = (0, span_rows, n_rows - rows_q)
    for c in range(3):
        r0 = first_q_row[c]
        w0 = min(max(r0 - NA_ROWS // 2, 0), n_rows - span_rows)
        for kr in range(span_rows):
            k_abs = w0 + kr
            for m in range(rows_q // 2):
                halves = []
                for r in (r0 + 2 * m, r0 + 2 * m + 1):
                    rs = min(max(r - NA_ROWS // 2, 0), n_rows - NA_ROWS)
                    halves.append(tiles[k_abs - r + NA_ROWS - 1] if rs <= k_abs < rs + NA_ROWS else masked)
                o_ref[c, 0, kr * GRID_W:(kr + 1) * GRID_W, m * LANES:(m + 1) * LANES] = (
                    jnp.where(low, halves[0], halves[1]))


def _na_bias(rpb, seq):
    n_rows = seq // GRID_W
    rows_q = TQ_SUB // GRID_W
    span = (rows_q + NA_ROWS) * GRID_W
    kern = functools.partial(_na_bias_kernel, n_rows=n_rows, rows_q=rows_q)
    return pl.pallas_call(
        kern,
        grid=(NA_HEADS,),
        in_specs=[pl.BlockSpec(memory_space=pltpu.SMEM)],
        out_specs=pl.BlockSpec((3, 1, span, TQ_SUB), lambda h: (0, h, 0, 0)),
        out_shape=jax.ShapeDtypeStruct((3, NA_HEADS, span, TQ_SUB), F32),
        compiler_params=_params("parallel"),
        name="na_bias",
    )(rpb.reshape(-1))


def _post_kernel(*refs, n_x, n_o, bpb, nbatch):
    n_lat_blocks = bpb * nbatch
    x_refs, oa_refs, ob_refs = refs[:n_x], refs[n_x:n_x + n_o], refs[n_x + n_o:n_x + 2 * n_o]
    wo_ref, g1_ref, n2_ref, sh_ref, sc_ref, g2_ref, w1_ref, w2_ref, o_ref = refs[n_x + 2 * n_o:]
    oa = _stream_block(oa_refs, n_lat_blocks)
    ob = _stream_block(ob_refs, n_lat_blocks)
    half = oa.shape[1]
    y = (jnp.dot(oa, wo_ref[:half, :], preferred_element_type=F32)
         + jnp.dot(ob, wo_ref[half:, :], preferred_element_type=F32))
    x1 = _stream_block(x_refs, n_lat_blocks) + _mod_row(g1_ref, bpb, nbatch) * y
    h = _modulated(x1, n2_ref, _mod_row(sh_ref, bpb, nbatch), _mod_row(sc_ref, bpb, nbatch)).astype(BF16)
    a = jnp.dot(h, w1_ref[...], preferred_element_type=F32)
    a = jnp.square(jnp.maximum(a, 0.0)).astype(BF16)
    o_ref[...] = x1 + _mod_row(g2_ref, bpb, nbatch) * jnp.dot(a, w2_ref[...], preferred_element_type=F32)


def _post(x_src, oa_src, ob_src, wo, mod, n2, w1, w2, layer, n_blocks, bpb, nbatch):
    d = x_src[0].shape[1]
    half = oa_src[0].shape[1]
    n_lat_blocks = bpb * nbatch
    g1_spec, sh_spec, sc_spec, g2_spec = _mod_specs(mod, layer, (GATE1, SHIFT2, SCALE2, GATE2))
    resident = lambda a: pl.BlockSpec((None,) + a.shape[1:], lambda i: (layer, 0, 0),
                                      pipeline_mode=pl.Buffered(1))
    kern = functools.partial(_post_kernel, n_x=len(x_src), n_o=len(oa_src), bpb=bpb, nbatch=nbatch)
    return pl.pallas_call(
        kern,
        grid=(n_blocks,),
        in_specs=(_stream_specs(x_src, d, n_lat_blocks) + _stream_specs(oa_src, half, n_lat_blocks)
                  + _stream_specs(ob_src, half, n_lat_blocks)
                  + [resident(wo), g1_spec, _whole(n2.shape), sh_spec, sc_spec, g2_spec, resident(w1), resident(w2)]),
        out_specs=pl.BlockSpec((TM, d), lambda i: (i, 0)),
        out_shape=jax.ShapeDtypeStruct((n_blocks * TM, d), F32),
        compiler_params=_params("parallel"),
        name="post",
    )(*x_src, *oa_src, *ob_src, wo, mod, n2, mod, mod, mod, w1, w2)


def _block_diag_ones(group):
    idx = np.arange(CHUNK) // group
    return jnp.asarray(idx[:, None] == idx[None, :], dtype=BF16)


def _rope_tables(seq, rot_dim, lane0, pad_rows):
    t = jnp.arange(seq, dtype=jnp.int32)
    row = (t // GRID_W).astype(F32)
    col = (t % GRID_W).astype(F32)
    n_freq = rot_dim // 4
    freqs = jnp.power(ROPE_BASE, -jnp.arange(n_freq, dtype=F32) / n_freq)
    ar = row[:, None] * freqs[None, :]
    ac = col[:, None] * freqs[None, :]
    ang = jnp.concatenate([ar, ar, ac, ac], axis=-1)
    sign = jnp.asarray(np.tile(np.repeat([-1.0, 1.0], n_freq), 2), F32)
    cos, sin = jnp.cos(ang), jnp.sin(ang) * sign[None, :]
    if lane0 is None:
        reps = LANES // rot_dim
        cos, sin = jnp.tile(cos, (1, reps)), jnp.tile(sin, (1, reps))
    else:
        pad = ((0, 0), (lane0, LANES - lane0 - rot_dim))
        cos = jnp.pad(cos, pad, constant_values=1.0)
        sin = jnp.pad(sin, pad)
    cos = jnp.concatenate([cos, jnp.ones((pad_rows, LANES), F32)], axis=0)
    sin = jnp.concatenate([sin, jnp.zeros((pad_rows, LANES), F32)], axis=0)
    return cos, sin


def _logits_bounded(q_gain, k_gain, dim, bias=None, sink=None):
    bound = jnp.max(jnp.abs(q_gain)) * jnp.max(jnp.abs(k_gain)) * (dim ** 0.5) * BF16_ROUNDING_MARGIN
    if bias is not None:
        bound = bound + jnp.max(jnp.abs(bias))
    if sink is not None:
        bound = jnp.maximum(bound, jnp.max(jnp.abs(sink)))
    return (bound <= MAX_FREE_LOGIT_BOUND).astype(jnp.int32).reshape(1)


def _tile_gain(g, reps, scale=1.0):
    return (jnp.tile(g.astype(F32), reps) * scale)[None, :]


def _even_weights(w, q_g, k_g, sq_g, sk_g):
    nq = DIFF_HEADS * 2 * HEAD_DIM
    qa, ka, va = w[:, :nq], w[:, nq:2 * nq], w[:, 2 * nq:3 * nq]
    o = 3 * nq
    qb = w[:, o:o + SWA_Q_HEADS * HEAD_DIM]
    o += SWA_Q_HEADS * HEAD_DIM
    kb = [w[:, o + i * HEAD_DIM:o + (i + 1) * HEAD_DIM] for i in range(SWA_KV_HEADS)]
    o += SWA_KV_HEADS * HEAD_DIM
    vb = [w[:, o + i * HEAD_DIM:o + (i + 1) * HEAD_DIM] for i in range(SWA_KV_HEADS)]
    dup = lambda parts: [p for p in parts for _ in range(2)]
    w_new = jnp.concatenate([qa, ka, va, qb] + dup(kb) + dup(vb), axis=1).astype(BF16)
    scale = HEAD_DIM ** -0.5 * LOG2E
    ones = lambda n: jnp.ones((1, n), F32)
    gain = jnp.concatenate([
        _tile_gain(q_g, 2 * DIFF_HEADS, scale), _tile_gain(k_g, 2 * DIFF_HEADS), ones(nq),
        _tile_gain(sq_g, SWA_Q_HEADS, scale), _tile_gain(sk_g, 2 * SWA_KV_HEADS),
        ones(2 * SWA_KV_HEADS * HEAD_DIM)], axis=1)
    return w_new, gain


def _odd_weights(w, wq_up, wkv_up, qa_g, kva_g, mq_g, mk_g, nq_g, nk_g):
    d = w.shape[0]
    o = 0
    parts = []
    for n in (MLA_Q_LORA, MLA_KV_LORA, MLA_ROPE, NA_HEADS * HEAD_DIM, NA_HEADS * HEAD_DIM, NA_HEADS * HEAD_DIM):
        parts.append(w[:, o:o + n])
        o += n
    q_a, kv_a, k_r, nq, nk, nv = parts
    slot = jnp.concatenate([jnp.zeros((d, MLA_NOPE), w.dtype), k_r,
                            jnp.zeros((d, LANES - MLA_QK), w.dtype)], axis=1)
    w_new = jnp.concatenate([q_a, kv_a, nq, nk, nv, slot], axis=1).astype(BF16)
    wq = jnp.pad(wq_up.reshape(MLA_Q_LORA, MLA_HEADS, MLA_QK), ((0, 0), (0, 0), (0, MLA_PAD - MLA_QK)))
    wq = wq.reshape(MLA_Q_LORA, MLA_HEADS * MLA_PAD).astype(BF16)
    wkv = wkv_up.reshape(MLA_KV_LORA, MLA_HEADS, MLA_NOPE + MLA_V)
    wk = jnp.pad(wkv[..., :MLA_NOPE], ((0, 0), (0, 0), (0, MLA_PAD - MLA_NOPE))).reshape(MLA_KV_LORA, -1)
    wv = wkv[..., MLA_NOPE:].reshape(MLA_KV_LORA, MLA_HEADS * MLA_V)
    wkv_new = jnp.concatenate([wk, wv], axis=1).astype(BF16)
    pad_gain = lambda g, scale: _tile_gain(jnp.pad(g.astype(F32), (0, MLA_PAD - MLA_QK)), CHUNK // MLA_PAD, scale)
    gains = (qa_g.astype(F32)[None, :], kva_g.astype(F32)[None, :],
             pad_gain(mq_g, MLA_QK ** -0.5 * LOG2E), pad_gain(mk_g, 1.0),
             _tile_gain(nq_g, CHUNK // HEAD_DIM, HEAD_DIM ** -0.5 * LOG2E), _tile_gain(nk_g, CHUNK // HEAD_DIM))
    return w_new, wq, wkv_new, gains


def kernel(x, c, ctx, c_ctx, ada_w, ada_b, norm1_g, norm2_g, w_out, mlp_w1, mlp_w2, ev_w_in, diff_q_g,
           diff_k_g, diff_lam, diff_subln_g, swa_q_g, swa_k_g, swa_sink, od_w_in, mla_qa_g, mla_kva_g,
           mla_wq_up, mla_wkv_up, mla_q_g, mla_k_g, na_q_g, na_k_g, na_rpb):
    nbatch, seq, d = x.shape
    ctx_len = ctx.shape[1]
    depth = ada_w.shape[0]
    n_lat = nbatch * seq
    n_ctx = nbatch * ctx_len
    assert seq % TM == 0 and n_ctx == TM and seq % TQ == 0 and seq % GRID_W == 0
    assert seq % TQ_FULL == 0 and seq % TK_MAX_FREE == 0 and seq % TK_FULL == 0
    assert seq // GRID_W >= 2 * (NA_ROWS + TQ_SUB // GRID_W) and seq >= TQ_SUB + 2 * WINDOW
    bpb = seq // TM

    mod_rows = 8
    cvec = jnp.concatenate([c, c_ctx[None, :], jnp.zeros((mod_rows - nbatch - 1, d), F32)], axis=0)
    mod = _modulation(cvec, ada_w, ada_b)

    cos64, sin64 = _rope_tables(seq, HEAD_DIM, None, n_ctx)
    cos32, sin32 = _rope_tables(seq, MLA_ROPE, MLA_NOPE, n_ctx)
    g64 = _block_diag_ones(HEAD_DIM)
    g128 = _block_diag_ones(LANES)

    x_src = (x.reshape(n_lat, d), ctx.reshape(n_ctx, d))
    wo_all, w1_all, w2_all = w_out.astype(BF16), mlp_w1.astype(BF16), mlp_w2.astype(BF16)
    common = dict(nbatch=nbatch, seq=seq, ctx_len=ctx_len)

    for layer in range(depth):
        need_ctx = layer < depth - 1
        n_blocks = bpb * nbatch + (1 if need_ctx else 0)
        n1 = norm1_g[layer][None, :]
        n2 = norm2_g[layer][None, :]
        i = layer // 2
        if layer % 2 == 0:
            lam_init = 0.8 - 0.6 * math.exp(-0.3 * layer)
            w_in, gain = _even_weights(ev_w_in[i], diff_q_g[i], diff_k_g[i], swa_q_g[i], swa_k_g[i])
            p = _proj_even(x_src, n1, mod, layer, w_in, gain, g64, cos64, sin64, bpb, nbatch)
            lam = diff_lam[i].astype(F32)
            subln = diff_subln_g[i].astype(F32)[None, :]
            sink = swa_sink[i].astype(F32)
            diff_kw = dict(mode="diff", split="mask", n_pairs=DIFF_HEADS, q_blk=lambda j: EVEN_QA // LANES + j,
                           k_blk=lambda j: EVEN_KA // LANES + j, v_blk=lambda j: EVEN_VA // LANES + j,
                           out_cols=DIFF_HEADS * LANES,
                           lam=lam, subln=subln, lam_init=lam_init,
                           safe=_logits_bounded(diff_q_g[i], diff_k_g[i], HEAD_DIM), **common)
            swa_kw = dict(mode="pair", split="mask", n_pairs=SWA_Q_HEADS // 2,
                          q_blk=lambda j: EVEN_QB // LANES + j, k_blk=lambda j: EVEN_KB // LANES + j // 2,
                          v_blk=lambda j: EVEN_VB // LANES + j // 2,
                          out_cols=SWA_Q_HEADS * HEAD_DIM, sink=sink,
                          safe=_logits_bounded(swa_q_g[i], swa_k_g[i], HEAD_DIM, sink=sink), **common)
            oa = _attention(p, name="diff_attn", main="full", ctx_queries=False, **diff_kw)
            ob = _attention(p, name="window_attn", main="window", ctx_queries=False, **swa_kw)
            if need_ctx:
                oa_c = _attention(p, name="diff_attn_ctx", main=None, ctx_queries=True, **diff_kw)
                ob_c = _attention(p, name="window_attn_ctx", main=None, ctx_queries=True, **swa_kw)
        else:
            w_in, wq, wkv, gains = _odd_weights(od_w_in[i], mla_wq_up[i], mla_wkv_up[i], mla_qa_g[i],
                                                mla_kva_g[i], mla_q_g[i], mla_k_g[i], na_q_g[i], na_k_g[i])
            p = _proj_odd(x_src[0], n1, mod, layer, w_in, wq, wkv, gains, g64, g128, cos32, sin32, bpb, nbatch)
            bias = _na_bias(na_rpb[i].astype(F32), seq)
            mla_kw = dict(mode="pair", split="slice", n_pairs=MLA_HEADS // 2, q_blk=lambda j: j,
                          k_blk=lambda j: ODD_K // CHUNK + j, v_blk=lambda j: ODD_V // LANES + j,
                          out_cols=MLA_HEADS * MLA_V,
                          safe=_logits_bounded(mla_q_g[i], mla_k_g[i], MLA_QK), **common)
            na_kw = dict(mode="pair", split="mask", n_pairs=NA_HEADS // 2,
                         q_blk=lambda j: ODD_NQ // LANES + j, k_blk=lambda j: ODD_NK // LANES + j,
                         v_blk=lambda j: ODD_NV // LANES + j, out_cols=NA_HEADS * HEAD_DIM,
                         safe=_logits_bounded(na_q_g[i], na_k_g[i], HEAD_DIM, bias=na_rpb[i]), **common)
            oa = _attention(p, name="mla_attn", main="full", ctx_queries=False, **mla_kw)
            ob = _attention(p, name="na_attn", main="na", ctx_queries=False, bias=bias, **na_kw)
            if need_ctx:
                oa_c = _attention(p, name="mla_attn_ctx", main=None, ctx_queries=True, **mla_kw)
                ob_c = _attention(p, name="na_attn_ctx", main=None, ctx_queries=True, **na_kw)
        oa_src, ob_src = ((oa, oa_c), (ob, ob_c)) if need_ctx else ((oa,), (ob,))
        x_src = (_post(x_src, oa_src, ob_src, wo_all, mod, n2, w1_all, w2_all, layer, n_blocks, bpb, nbatch),)
    return x_src[0].reshape(nbatch, seq, d)
```

```python
import functools
import math

import numpy as np
import jax
import jax.numpy as jnp
from jax import lax
from jax.experimental import pallas as pl
from jax.experimental.pallas import tpu as pltpu

F32 = jnp.float32
BF16 = jnp.bfloat16

LANES = 128
HALF = LANES // 2
CHUNK = 2 * LANES
GRID_W = 64
HEAD_DIM = 64
EPS = 1e-6
ROPE_BASE = 10000.0
DIFF_HEADS = 4
SWA_Q_HEADS = 8
SWA_KV_HEADS = 2
WINDOW = 128
MLA_HEADS = 8
MLA_Q_LORA = 512
MLA_KV_LORA = 256
MLA_NOPE = 64
MLA_ROPE = 32
MLA_QK = MLA_NOPE + MLA_ROPE
MLA_V = 64
NA_HEADS = 8
NA_ROWS = 8
NA_COLS = 16
NEG = -1e30
LOG2E = math.log2(math.e)
MAX_FREE_LOGIT_BOUND = 50.0
BF16_ROUNDING_MARGIN = 1.02

TM = 512
TQ = 2048
TQ_SUB = 256
TQ_FULL = 1024
TK_FULL = 512
TK_MAX_FREE = 2048
MOD_TN = 1536
VMEM_LIMIT = 56 * 1024 * 1024


def _params(*sem):
    return pltpu.CompilerParams(dimension_semantics=sem, vmem_limit_bytes=VMEM_LIMIT)


def _mod_kernel(c_ref, w_ref, b_ref, o_ref):
    c = c_ref[...]
    a = (c * jax.nn.sigmoid(c)).astype(BF16)
    w = w_ref[0].astype(BF16)
    o_ref[0] = jnp.dot(a, w, preferred_element_type=F32) + b_ref[0]


def _modulation(cvec, ada_w, ada_b):
    depth, d, n = ada_w.shape
    rows = cvec.shape[0]
    return pl.pallas_call(
        _mod_kernel,
        grid=(depth, n // MOD_TN),
        in_specs=[
            pl.BlockSpec((rows, d), lambda l, j: (0, 0)),
            pl.BlockSpec((1, d, MOD_TN), lambda l, j: (l, 0, j)),
            pl.BlockSpec((1, 1, MOD_TN), lambda l, j: (l, 0, j)),
        ],
        out_specs=pl.BlockSpec((1, rows, MOD_TN), lambda l, j: (l, 0, j)),
        out_shape=jax.ShapeDtypeStruct((depth, rows, n), F32),
        compiler_params=_params("parallel", "parallel"),
        name="modulation",
    )(cvec, ada_w, ada_b.reshape(depth, 1, n))


def _modulated(x, g_ref, shift, scale):
    ms = jnp.mean(x * x, axis=-1, keepdims=True)
    y = x * lax.rsqrt(ms + EPS) * g_ref[...]
    return y * (1.0 + scale) + shift


def _group_meansq(chunks, g_ref, inv_n):
    rows = chunks[0].shape[0]
    sq = jnp.concatenate([(c * c).astype(BF16) for c in chunks], axis=0)
    ss = jnp.dot(sq, g_ref[...], preferred_element_type=F32) * inv_n
    return [ss[i * rows:(i + 1) * rows] for i in range(len(chunks))]


def _swap_quarters(x, quarter):
    lane = lax.broadcasted_iota(jnp.int32, x.shape, 1)
    odd = (lane & quarter) != 0
    from_lower = pltpu.roll(x, quarter, 1)
    from_upper = pltpu.roll(x, LANES - quarter, 1)
    return jnp.where(odd, from_lower, from_upper)


def _rope(x, cos, sin_signed, quarter):
    return x * cos + _swap_quarters(x, quarter) * sin_signed


def _norm_rope_chunk(acc, meansq, gain, cos, sin, quarter):
    y = acc * lax.rsqrt(meansq + EPS) * gain
    if quarter is None:
        return y
    return jnp.concatenate([_rope(y[:, :LANES], cos, sin, quarter),
                            _rope(y[:, LANES:], cos, sin, quarter)], axis=1)


def _stream_specs(src, cols, n_lat_blocks):
    if len(src) == 1:
        return [pl.BlockSpec((TM, cols), lambda i: (i, 0))]
    return [pl.BlockSpec((TM, cols), lambda i: (jnp.minimum(i, n_lat_blocks - 1), 0)),
            pl.BlockSpec((TM, cols), lambda i: (0, 0))]


def _stream_block(refs, n_lat_blocks):
    if len(refs) == 1:
        return refs[0][...]
    return jnp.where(pl.program_id(0) < n_lat_blocks, refs[0][...], refs[1][...])


SHIFT1, SCALE1, GATE1, SHIFT2, SCALE2, GATE2 = range(6)


def _mod_specs(mod, layer, chunks):
    d = mod.shape[2] // 6
    return [pl.BlockSpec((None, mod.shape[1], d), lambda i, k=k: (layer, 0, k)) for k in chunks]


def _mod_row(ref, bpb, nbatch):
    row = jnp.minimum(pl.program_id(0) // bpb, nbatch)
    return ref[pl.ds(row, 1), :]


def _rope_spec(bpb, nbatch):
    return pl.BlockSpec((TM, LANES), lambda i: (jnp.where(i < bpb * nbatch, i % bpb, bpb), 0))


def _whole(shape):
    return pl.BlockSpec(shape, lambda *_: (0,) * len(shape))


EVEN_QA = 0
EVEN_KA = EVEN_QA + DIFF_HEADS * 2 * HEAD_DIM
EVEN_VA = EVEN_KA + DIFF_HEADS * 2 * HEAD_DIM
EVEN_QB = EVEN_VA + DIFF_HEADS * 2 * HEAD_DIM
EVEN_KB = EVEN_QB + SWA_Q_HEADS * HEAD_DIM
EVEN_VB = EVEN_KB + 2 * SWA_KV_HEADS * HEAD_DIM
EVEN_COLS = EVEN_VB + 2 * SWA_KV_HEADS * HEAD_DIM
EVEN_NORM_CHUNKS = tuple(c for c in range(EVEN_COLS // CHUNK)
                         if c * CHUNK < EVEN_VA or EVEN_QB <= c * CHUNK < EVEN_VB)
EVEN_DOT_RANGES = ((EVEN_QA // CHUNK, EVEN_VA // CHUNK), (EVEN_QB // CHUNK, EVEN_VB // CHUNK),
                   (EVEN_VA // CHUNK, EVEN_QB // CHUNK), (EVEN_VB // CHUNK, EVEN_COLS // CHUNK))


def _proj_even_kernel(*refs, n_x, bpb, nbatch):
    n_lat_blocks = bpb * nbatch
    x_refs = refs[:n_x]
    g_ref, sh_ref, sc_ref, w_ref, gain_ref, g64_ref, cos_ref, sin_ref, o_ref = refs[n_x:]
    h = _modulated(_stream_block(x_refs, n_lat_blocks), g_ref, _mod_row(sh_ref, bpb, nbatch),
                   _mod_row(sc_ref, bpb, nbatch)).astype(BF16)
    cos = cos_ref[...]
    sin = sin_ref[...]
    for first, last in EVEN_DOT_RANGES:
        full = jnp.dot(h, w_ref[:, first * CHUNK:last * CHUNK], preferred_element_type=F32)
        chunk = lambda c: full[:, (c - first) * CHUNK:(c - first + 1) * CHUNK]
        normed = [c for c in range(first, last) if c in EVEN_NORM_CHUNKS]
        meansq = {}
        if normed:
            meansq = dict(zip(normed, _group_meansq([chunk(c) for c in normed], g64_ref, 1.0 / HEAD_DIM)))
        for c in range(first, last):
            cols = slice(c * CHUNK, (c + 1) * CHUNK)
            acc = chunk(c)
            if c in EVEN_NORM_CHUNKS:
                acc = _norm_rope_chunk(acc, meansq[c], gain_ref[:, cols], cos, sin, HEAD_DIM // 4)
            o_ref[:, cols] = acc.astype(BF16)


def _proj_even(x_src, g, mod, layer, w, gain, g64, cos, sin, bpb, nbatch):
    d = x_src[0].shape[1]
    n_lat_blocks = bpb * nbatch
    kern = functools.partial(_proj_even_kernel, n_x=len(x_src), bpb=bpb, nbatch=nbatch)
    return pl.pallas_call(
        kern,
        grid=(n_lat_blocks + 1,),
        in_specs=_stream_specs(x_src, d, n_lat_blocks) + [_whole(g.shape)] + _mod_specs(mod, layer, (SHIFT1, SCALE1)) + [
            _whole(w.shape), _whole(gain.shape), _whole(g64.shape),
            _rope_spec(bpb, nbatch), _rope_spec(bpb, nbatch),
        ],
        out_specs=pl.BlockSpec((TM, EVEN_COLS), lambda i: (i, 0)),
        out_shape=jax.ShapeDtypeStruct(((n_lat_blocks + 1) * TM, EVEN_COLS), BF16),
        compiler_params=_params("parallel"),
        name="proj_even",
    )(*x_src, g, mod, mod, w, gain, g64, cos, sin)


NA_WIDTH = NA_HEADS * HEAD_DIM
MLA_PAD = LANES
MLA_WIDTH = MLA_HEADS * MLA_PAD
ODD_W_KVA = MLA_Q_LORA
ODD_W_NQ = ODD_W_KVA + MLA_KV_LORA
ODD_W_NV = ODD_W_NQ + 2 * NA_WIDTH
ODD_W_SLOT = ODD_W_NV + NA_WIDTH
ODD_W_COLS = ODD_W_SLOT + LANES
ODD_K = MLA_WIDTH
ODD_V = ODD_K + MLA_WIDTH
ODD_NQ = ODD_V + MLA_HEADS * MLA_V
ODD_NK = ODD_NQ + NA_WIDTH
ODD_NV = ODD_NK + NA_WIDTH
ODD_COLS = ODD_NV + NA_WIDTH


def _proj_odd_kernel(x_ref, g_ref, sh_ref, sc_ref, w_ref, wq_ref, wkv_ref, qa_g_ref, kva_g_ref,
                     mq_g_ref, mk_g_ref, nq_g_ref, nk_g_ref, g64_ref, g128_ref, cos_ref, sin_ref, o_ref,
                     *, bpb, nbatch):
    h = _modulated(x_ref[...], g_ref, _mod_row(sh_ref, bpb, nbatch), _mod_row(sc_ref, bpb, nbatch)).astype(BF16)
    cos = cos_ref[...]
    sin = sin_ref[...]
    quarter = MLA_ROPE // 4

    lora = jnp.dot(h, w_ref[:, :ODD_W_NQ], preferred_element_type=F32)
    rest = jnp.dot(h, w_ref[:, ODD_W_NQ:], preferred_element_type=F32)

    def low_rank(a, gain_ref):
        ms = jnp.mean(a * a, axis=-1, keepdims=True)
        return (a * lax.rsqrt(ms + EPS) * gain_ref[...]).astype(BF16)

    n_mla = MLA_WIDTH // CHUNK
    chunks_of = lambda a, first, n: [a[:, first + c * CHUNK:first + (c + 1) * CHUNK] for c in range(n)]
    qa = low_rank(lora[:, :MLA_Q_LORA], qa_g_ref)
    qf = jnp.dot(qa, wq_ref[...], preferred_element_type=F32)
    q_chunks = chunks_of(qf, 0, n_mla)
    ms_q = _group_meansq(q_chunks, g128_ref, 1.0 / MLA_QK)
    kva = low_rank(lora[:, ODD_W_KVA:], kva_g_ref)
    kvf = jnp.dot(kva, wkv_ref[...], preferred_element_type=F32)
    kslot = rest[:, ODD_W_SLOT - ODD_W_NQ:]
    kslot2 = jnp.concatenate([kslot, kslot], axis=1)
    k_rot = _swap_quarters(kslot * mk_g_ref[:, :LANES], quarter)
    k_rot2 = jnp.concatenate([k_rot, k_rot], axis=1)
    cos2 = jnp.concatenate([cos, cos], axis=1)
    sin2 = jnp.concatenate([sin, sin], axis=1)
    k_chunks = [kc + kslot2 for kc in chunks_of(kvf, 0, n_mla)]
    ms_k = _group_meansq(k_chunks, g128_ref, 1.0 / MLA_QK)
    for c in range(n_mla):
        y = _norm_rope_chunk(q_chunks[c], ms_q[c], mq_g_ref[...], cos, sin, quarter)
        o_ref[:, c * CHUNK:(c + 1) * CHUNK] = y.astype(BF16)
    for c in range(n_mla):
        inv_rms = lax.rsqrt(ms_k[c] + EPS)
        y = (k_chunks[c] * inv_rms * mk_g_ref[...]) * cos2 + (inv_rms * k_rot2) * sin2
        o_ref[:, ODD_K + c * CHUNK:ODD_K + (c + 1) * CHUNK] = y.astype(BF16)
    o_ref[:, ODD_V:ODD_NQ] = kvf[:, MLA_WIDTH:].astype(BF16)

    n_na = NA_WIDTH // CHUNK
    n_chunks = chunks_of(rest, 0, 2 * n_na)
    ms64 = _group_meansq(n_chunks, g64_ref, 1.0 / HEAD_DIM)
    for c in range(2 * n_na):
        gain = nq_g_ref[...] if c < n_na else nk_g_ref[...]
        y = _norm_rope_chunk(n_chunks[c], ms64[c], gain, None, None, None)
        o_ref[:, ODD_NQ + c * CHUNK:ODD_NQ + (c + 1) * CHUNK] = y.astype(BF16)
    o_ref[:, ODD_NV:ODD_COLS] = rest[:, ODD_W_NV - ODD_W_NQ:ODD_W_SLOT - ODD_W_NQ].astype(BF16)


def _proj_odd(xa, g, mod, layer, w, wq, wkv, gains, g64, g128, cos, sin, bpb, nbatch):
    na, d = xa.shape
    tok_specs = _stream_specs((xa,), d, bpb * nbatch) + [_whole(g.shape)] + _mod_specs(mod, layer, (SHIFT1, SCALE1))
    consts = (w, wq, wkv) + tuple(gains) + (g64, g128)
    return pl.pallas_call(
        functools.partial(_proj_odd_kernel, bpb=bpb, nbatch=nbatch),
        grid=(na // TM,),
        in_specs=tok_specs + [_whole(a.shape) for a in consts] + [_rope_spec(bpb, nbatch)] * 2,
        out_specs=pl.BlockSpec((TM, ODD_COLS), lambda i: (i, 0)),
        out_shape=jax.ShapeDtypeStruct((na, ODD_COLS), BF16),
        compiler_params=_params("parallel"),
        name="proj_odd",
    )(xa, g, mod, mod, *consts, cos, sin)


def _attn_kernel(*refs, mode, split, main, seq, has_sink, lam_init, tq, tq_sub):
    n_sub = tq // tq_sub
    it = iter(refs)
    q_ref = next(it)
    k_ref = v_ref = sink_ref = lam_ref = subln_ref = None
    bias_refs = ()
    if main is not None:
        k_ref, v_ref = next(it), next(it)
    kc_ref, vc_ref = next(it), next(it)
    if main == "na":
        bias_refs = tuple(next(it) for _ in range(n_sub))
    if has_sink:
        sink_ref = next(it)
    safe_ref = next(it)
    if mode == "diff":
        lam_ref, subln_ref = next(it), next(it)
    o_ref = next(it)
    vt_ref, vct_ref = (next(it), next(it)) if main == "full" else (None, None)

    pair = pl.program_id(1)
    qi = pl.program_id(2)
    q = q_ref[...]
    low = lax.broadcasted_iota(jnp.int32, (1, LANES), 1) < HALF
    if split == "mask":
        zero = jnp.zeros_like(q)
        qs = (jnp.where(low, q, zero), jnp.where(low, zero, q))
        k_of = lambda k, s: k
    else:
        qs = (q[:, :LANES], q[:, LANES:])
        k_of = lambda k, s: k[:, s * LANES:(s + 1) * LANES]
    q_of = lambda s, t: qs[s][t * tq_sub:(t + 1) * tq_sub]
    sinks = [sink_ref[2 * pair + s] * LOG2E for s in range(2)] if has_sink else None
    nt_dims = (((1,), (1,)), ((), ()))

    def finish(o_lo, o_hi):
        if mode == "diff":
            lv = lam_ref[...]
            lam = (jnp.exp(jnp.sum(lv[0:1] * lv[1:2], axis=-1, keepdims=True))
                   - jnp.exp(jnp.sum(lv[2:3] * lv[3:4], axis=-1, keepdims=True)) + lam_init)
            o = o_lo - lam * o_hi
            ms = jnp.mean(o * o, axis=-1, keepdims=True)
            o = o * lax.rsqrt(ms + EPS) * subln_ref[...] * (1.0 - lam_init)
        else:
            o = jnp.where(low, o_lo, o_hi)
        o_ref[...] = o.astype(BF16)

    window_masks = {}

    def local_keys(t):
        if main == "window":
            span = tq_sub + 2 * WINDOW
            q0 = qi * tq + t * tq_sub
            at_edge = t in (0, n_sub - 1)
            w0 = jnp.clip(q0 - WINDOW, 0, seq - span) if at_edge else q0 - WINDOW
            off = pl.multiple_of(w0, WINDOW)
            if at_edge or "inner" not in window_masks:
                key = lax.broadcasted_iota(jnp.int32, (span, tq_sub), 0)
                qry = lax.broadcasted_iota(jnp.int32, (span, tq_sub), 1)
                inside = jnp.abs(key - qry + ((w0 - q0) if at_edge else -WINDOW)) <= WINDOW
                if not at_edge:
                    window_masks["inner"] = inside
            else:
                inside = window_masks["inner"]
            logit_fn = lambda lg_t, s: jnp.where(inside, lg_t, NEG)
        else:
            rows_q = tq_sub // GRID_W
            span_rows = rows_q + NA_ROWS
            w0 = jnp.clip(qi * (tq // GRID_W) + t * rows_q - NA_ROWS // 2, 0, seq // GRID_W - span_rows)
            off = pl.multiple_of(w0 * GRID_W, GRID_W)
            span = span_rows * GRID_W
            logit_fn = lambda lg_t, s: lg_t + bias_refs[t][0, s]
        return k_ref[pl.ds(off, span), :], v_ref[pl.ds(off, span), :], logit_fn

    def online_sub(t):
        def init(s):
            if has_sink:
                m0 = jnp.full((tq_sub, 1), sinks[s], F32)
                l0 = jnp.ones((tq_sub, 1), F32)
            else:
                m0 = jnp.full((tq_sub, 1), NEG, F32)
                l0 = jnp.zeros((tq_sub, 1), F32)
            return m0, l0, jnp.zeros((tq_sub, LANES), F32)

        def segment(state, k, v, logit_fn):
            out = []
            for s in range(2):
                m, l, acc = state[s]
                logits = lax.dot_general(q_of(s, t), k_of(k, s), nt_dims, preferred_element_type=F32)
                if logit_fn is not None:
                    logits = logit_fn(logits.T, s).T
                m_new = jnp.maximum(m, jnp.max(logits, axis=-1, keepdims=True))
                alpha = jnp.exp2(m - m_new)
                p = jnp.exp2(logits - m_new)
                l = alpha * l + jnp.sum(p, axis=-1, keepdims=True)
                acc = alpha * acc + jnp.dot(p.astype(BF16), v, preferred_element_type=F32)
                out.append((m_new, l, acc))
            return tuple(out)

        state = (init(0), init(1))
        if main == "full":
            def body(c, st):
                off = pl.multiple_of(c * TK_FULL, TK_FULL)
                return segment(st, k_ref[pl.ds(off, TK_FULL), :], v_ref[pl.ds(off, TK_FULL), :], None)
            state = lax.fori_loop(0, seq // TK_FULL, body, state)
        elif main is not None:
            state = segment(state, *local_keys(t))
        (_, l_lo, acc_lo), (_, l_hi, acc_hi) = segment(state, kc_ref[...], vc_ref[...], None)
        return acc_lo / l_lo, acc_hi / l_hi

    def probs_t(k, q_rows, s, logit_fn):
        logits_t = lax.dot_general(k_of(k, s), q_rows, nt_dims, preferred_element_type=F32)
        if logit_fn is not None:
            logits_t = logit_fn(logits_t, s)
        p = jnp.exp2(logits_t)
        return p.astype(BF16), jnp.sum(p, axis=0, keepdims=True)

    def max_free():
        window_masks.clear()
        if main == "full":
            @pl.when(qi == 0)
            def _():
                for c in range(seq // TK_MAX_FREE):
                    vt_ref[c] = v_ref[c * TK_MAX_FREE:(c + 1) * TK_MAX_FREE, :].T
                vct_ref[...] = vc_ref[...].T

            acc_lat = [jnp.zeros((LANES, tq), F32)] * 2
            den_lat = [jnp.zeros((1, tq), F32)] * 2
            for c in range(seq // TK_MAX_FREE):
                k = k_ref[c * TK_MAX_FREE:(c + 1) * TK_MAX_FREE, :]
                for s in range(2):
                    p, p_sum = probs_t(k, qs[s], s, None)
                    acc_lat[s] = acc_lat[s] + jnp.dot(vt_ref[c], p, preferred_element_type=F32)
                    den_lat[s] = den_lat[s] + p_sum
            vct = vct_ref[...]
        else:
            vct = vc_ref[...].T
        p_ctx = [probs_t(kc_ref[...], qs[s], s, None) for s in range(2)]
        if main in ("window", "na"):
            windows = [local_keys(t) for t in range(n_sub)]
            p_loc = [[probs_t(k, q_of(s, t), s, fn) for s in range(2)] for t, (k, _, fn) in enumerate(windows)]
            vts = [v.T for _, v, _ in windows]
        outs = []
        for s in range(2):
            acc = jnp.dot(vct, p_ctx[s][0], preferred_element_type=F32)
            den = p_ctx[s][1]
            if main == "full":
                acc, den = acc + acc_lat[s], den + den_lat[s]
            elif main is not None:
                acc = acc + jnp.concatenate(
                    [jnp.dot(vts[t], p_loc[t][s][0], preferred_element_type=F32) for t in range(n_sub)], axis=1)
                den = den + jnp.concatenate([p_loc[t][s][1] for t in range(n_sub)], axis=1)
            if has_sink:
                den = den + jnp.exp2(jnp.full((1, tq), sinks[s], F32))
            outs.append((acc / den).T)
        finish(*outs)

    def online():
        window_masks.clear()
        parts = [online_sub(t) for t in range(n_sub)]
        finish(*[jnp.concatenate([p[s] for p in parts], axis=0) if n_sub > 1 else parts[0][s]
                 for s in range(2)])

    safe = safe_ref[0] == 1
    pl.when(safe)(max_free)
    pl.when(jnp.logical_not(safe))(online)


def _attention(p, *, name, mode, split, main, n_pairs, q_blk, k_blk, v_blk, out_cols, ctx_queries,
               nbatch, seq, ctx_len, safe, bias=None, sink=None, lam=None, subln=None, lam_init=0.0):
    n_lat = nbatch * seq
    qw = LANES if split == "mask" else 2 * LANES
    if ctx_queries:
        tq, nq = ctx_len, 1
        q_row = lambda b, j, i: n_lat // ctx_len + b
        out_rows = nbatch * ctx_len
        out_row = lambda b, j, i: b
    else:
        tq = TQ_FULL if main == "full" else TQ
        nq = seq // tq
        q_row = lambda b, j, i: b * nq + i
        out_rows = n_lat
        out_row = q_row
    ctx_row = lambda b, j, i: n_lat // ctx_len + b

    in_specs = [pl.BlockSpec((tq, qw), lambda b, j, i: (q_row(b, j, i), q_blk(j)))]
    args = [p]
    if main is not None:
        in_specs += [pl.BlockSpec((seq, qw), lambda b, j, i: (b, k_blk(j))),
                     pl.BlockSpec((seq, LANES), lambda b, j, i: (b, v_blk(j)))]
        args += [p, p]
    in_specs += [pl.BlockSpec((ctx_len, qw), lambda b, j, i: (ctx_row(b, j, i), k_blk(j))),
                 pl.BlockSpec((ctx_len, LANES), lambda b, j, i: (ctx_row(b, j, i), v_blk(j)))]
    args += [p, p]
    tq_sub = TQ_SUB if main in ("window", "na") else tq
    n_sub = tq // tq_sub
    if main == "na":
        span = (tq_sub // GRID_W + NA_ROWS) * GRID_W
        last = nq * n_sub - 1
        cls = lambda g: jnp.where(g == 0, 0, jnp.where(g == last, 2, 1))
        for t in range(n_sub):
            in_specs.append(pl.BlockSpec((1, 2, span, tq_sub),
                                         lambda b, j, i, t=t: (cls(i * n_sub + t), j, 0, 0)))
            args.append(bias)
    if sink is not None:
        in_specs.append(pl.BlockSpec(memory_space=pltpu.SMEM))
        args.append(sink)
    in_specs.append(pl.BlockSpec(memory_space=pltpu.SMEM))
    args.append(safe)
    if mode == "diff":
        in_specs += [_whole(lam.shape), _whole(subln.shape)]
        args += [lam, subln]

    kern = functools.partial(_attn_kernel, mode=mode, split=split, main=main, seq=seq,
                             has_sink=sink is not None, lam_init=lam_init, tq=tq, tq_sub=tq_sub)
    scratch = []
    if main == "full":
        scratch = [pltpu.VMEM((seq // TK_MAX_FREE, LANES, TK_MAX_FREE), BF16), pltpu.VMEM((LANES, ctx_len), BF16)]
    return pl.pallas_call(
        kern,
        grid=(nbatch, n_pairs, nq),
        in_specs=in_specs,
        out_specs=pl.BlockSpec((tq, LANES), lambda b, j, i: (out_row(b, j, i), j)),
        out_shape=jax.ShapeDtypeStruct((out_rows, out_cols), BF16),
        scratch_shapes=scratch,
        compiler_params=_params("parallel", "parallel", "arbitrary"),
        name=name,
    )(*args)


def _na_bias_kernel(rpb_ref, o_ref, *, n_rows, rows_q):
    h = pl.program_id(0)
    n_dr = 2 * NA_ROWS - 1
    n_dc = 2 * NA_COLS - 1
    kc = lax.broadcasted_iota(jnp.int32, (GRID_W, LANES), 0)
    cq = lax.broadcasted_iota(jnp.int32, (GRID_W, LANES), 1) & (GRID_W - 1)
    dc_idx = jnp.clip(kc - cq, -(NA_COLS - 1), NA_COLS - 1) + NA_COLS - 1
    cs = jnp.clip(cq - NA_COLS // 2, 0, GRID_W - NA_COLS)
    col_ok = (kc >= cs) & (kc < cs + NA_COLS)
    tiles = [jnp.zeros((GRID_W, LANES), F32) for _ in range(n_dr)]
    for dc in range(n_dc):
        hit = dc_idx == dc
        for dr in range(n_dr):
            tiles[dr] = jnp.where(hit, rpb_ref[(h * n_dr + dr) * n_dc + dc] * LOG2E, tiles[dr])
    tiles = [jnp.where(col_ok, t, NEG) for t in tiles]
    masked = jnp.full((GRID_W, LANES), NEG, F32)
    low = lax.broadcasted_iota(jnp.int32, (GRID_W, LANES), 1) < HALF

    span_rows = rows_q + NA_ROWS
    first_q_row = (0, span_rows, n_rows - rows_q)
    for c in range(3):
        r0 = first_q_row[c]
        w0 = min(max(r0 - NA_ROWS // 2, 0), n_rows - span_rows)
        for kr in range(span_rows):
            k_abs = w0 + kr
            for m in range(rows_q // 2):
                halves = []
                for r in (r0 + 2 * m, r0 + 2 * m + 1):
                    rs = min(max(r - NA_ROWS // 2, 0), n_rows - NA_ROWS)
                    halves.append(tiles[k_abs - r + NA_ROWS - 1] if rs <= k_abs < rs + NA_ROWS else masked)
                o_ref[c, 0, kr * GRID_W:(kr + 1) * GRID_W, m * LANES:(m + 1) * LANES] = (
                    jnp.where(low, halves[0], halves[1]))


def _na_bias(rpb, seq):
    n_rows = seq // GRID_W
    rows_q = TQ_SUB // GRID_W
    span = (rows_q + NA_ROWS) * GRID_W
    kern = functools.partial(_na_bias_kernel, n_rows=n_rows, rows_q=rows_q)
    return pl.pallas_call(
        kern,
        grid=(NA_HEADS,),
        in_specs=[pl.BlockSpec(memory_space=pltpu.SMEM)],
        out_specs=pl.BlockSpec((3, 1, span, TQ_SUB), lambda h: (0, h, 0, 0)),
        out_shape=jax.ShapeDtypeStruct((3, NA_HEADS, span, TQ_SUB), F32),
        compiler_params=_params("parallel"),
        name="na_bias",
    )(rpb.reshape(-1))


def _post_kernel(*refs, n_x, n_o, bpb, nbatch):
    n_lat_blocks = bpb * nbatch
    x_refs, oa_refs, ob_refs = refs[:n_x], refs[n_x:n_x + n_o], refs[n_x + n_o:n_x + 2 * n_o]
    wo_ref, g1_ref, n2_ref, sh_ref, sc_ref, g2_ref, w1_ref, w2_ref, o_ref = refs[n_x + 2 * n_o:]
    oa = _stream_block(oa_refs, n_lat_blocks)
    ob = _stream_block(ob_refs, n_lat_blocks)
    half = oa.shape[1]
    y = (jnp.dot(oa, wo_ref[:half, :], preferred_element_type=F32)
         + jnp.dot(ob, wo_ref[half:, :], preferred_element_type=F32))
    x1 = _stream_block(x_refs, n_lat_blocks) + _mod_row(g1_ref, bpb, nbatch) * y
    h = _modulated(x1, n2_ref, _mod_row(sh_ref, bpb, nbatch), _mod_row(sc_ref, bpb, nbatch)).astype(BF16)
    a = jnp.dot(h, w1_ref[...], preferred_element_type=F32)
    a = jnp.square(jnp.maximum(a, 0.0)).astype(BF16)
    o_ref[...] = x1 + _mod_row(g2_ref, bpb, nbatch) * jnp.dot(a, w2_ref[...], preferred_element_type=F32)


def _post(x_src, oa_src, ob_src, wo, mod, n2, w1, w2, layer, n_blocks, bpb, nbatch):
    d = x_src[0].shape[1]
    half = oa_src[0].shape[1]
    n_lat_blocks = bpb * nbatch
    g1_spec, sh_spec, sc_spec, g2_spec = _mod_specs(mod, layer, (GATE1, SHIFT2, SCALE2, GATE2))
    resident = lambda a: pl.BlockSpec((None,) + a.shape[1:], lambda i: (layer, 0, 0),
                                      pipeline_mode=pl.Buffered(1))
    kern = functools.partial(_post_kernel, n_x=len(x_src), n_o=len(oa_src), bpb=bpb, nbatch=nbatch)
    return pl.pallas_call(
        kern,
        grid=(n_blocks,),
        in_specs=(_stream_specs(x_src, d, n_lat_blocks) + _stream_specs(oa_src, half, n_lat_blocks)
                  + _stream_specs(ob_src, half, n_lat_blocks)
                  + [resident(wo), g1_spec, _whole(n2.shape), sh_spec, sc_spec, g2_spec, resident(w1), resident(w2)]),
        out_specs=pl.BlockSpec((TM, d), lambda i: (i, 0)),
        out_shape=jax.ShapeDtypeStruct((n_blocks * TM, d), F32),
        compiler_params=_params("parallel"),
        name="post",
    )(*x_src, *oa_src, *ob_src, wo, mod, n2, mod, mod, mod, w1, w2)


def _block_diag_ones(group):
    idx = np.arange(CHUNK) // group
    return jnp.asarray(idx[:, None] == idx[None, :], dtype=BF16)


def _rope_tables(seq, rot_dim, lane0, pad_rows):
    t = jnp.arange(seq, dtype=jnp.int32)
    row = (t // GRID_W).astype(F32)
    col = (t % GRID_W).astype(F32)
    n_freq = rot_dim // 4
    freqs = jnp.power(ROPE_BASE, -jnp.arange(n_freq, dtype=F32) / n_freq)
    ar = row[:, None] * freqs[None, :]
    ac = col[:, None] * freqs[None, :]
    ang = jnp.concatenate([ar, ar, ac, ac], axis=-1)
    sign = jnp.asarray(np.tile(np.repeat([-1.0, 1.0], n_freq), 2), F32)
    cos, sin = jnp.cos(ang), jnp.sin(ang) * sign[None, :]
    if lane0 is None:
        reps = LANES // rot_dim
        cos, sin = jnp.tile(cos, (1, reps)), jnp.tile(sin, (1, reps))
    else:
        pad = ((0, 0), (lane0, LANES - lane0 - rot_dim))
        cos = jnp.pad(cos, pad, constant_values=1.0)
        sin = jnp.pad(sin, pad)
    cos = jnp.concatenate([cos, jnp.ones((pad_rows, LANES), F32)], axis=0)
    sin = jnp.concatenate([sin, jnp.zeros((pad_rows, LANES), F32)], axis=0)
    return cos, sin


def _logits_bounded(q_gain, k_gain, dim, bias=None, sink=None):
    bound = jnp.max(jnp.abs(q_gain)) * jnp.max(jnp.abs(k_gain)) * (dim ** 0.5) * BF16_ROUNDING_MARGIN
    if bias is not None:
        bound = bound + jnp.max(jnp.abs(bias))
    if sink is not None:
        bound = jnp.maximum(bound, jnp.max(jnp.abs(sink)))
    return (bound <= MAX_FREE_LOGIT_BOUND).astype(jnp.int32).reshape(1)


def _tile_gain(g, reps, scale=1.0):
    return (jnp.tile(g.astype(F32), reps) * scale)[None, :]


def _even_weights(w, q_g, k_g, sq_g, sk_g):
    nq = DIFF_HEADS * 2 * HEAD_DIM
    qa, ka, va = w[:, :nq], w[:, nq:2 * nq], w[:, 2 * nq:3 * nq]
    o = 3 * nq
    qb = w[:, o:o + SWA_Q_HEADS * HEAD_DIM]
    o += SWA_Q_HEADS * HEAD_DIM
    kb = [w[:, o + i * HEAD_DIM:o + (i + 1) * HEAD_DIM] for i in range(SWA_KV_HEADS)]
    o += SWA_KV_HEADS * HEAD_DIM
    vb = [w[:, o + i * HEAD_DIM:o + (i + 1) * HEAD_DIM] for i in range(SWA_KV_HEADS)]
    dup = lambda parts: [p for p in parts for _ in range(2)]
    w_new = jnp.concatenate([qa, ka, va, qb] + dup(kb) + dup(vb), axis=1).astype(BF16)
    scale = HEAD_DIM ** -0.5 * LOG2E
    ones = lambda n: jnp.ones((1, n), F32)
    gain = jnp.concatenate([
        _tile_gain(q_g, 2 * DIFF_HEADS, scale), _tile_gain(k_g, 2 * DIFF_HEADS), ones(nq),
        _tile_gain(sq_g, SWA_Q_HEADS, scale), _tile_gain(sk_g, 2 * SWA_KV_HEADS),
        ones(2 * SWA_KV_HEADS * HEAD_DIM)], axis=1)
    return w_new, gain


def _odd_weights(w, wq_up, wkv_up, qa_g, kva_g, mq_g, mk_g, nq_g, nk_g):
    d = w.shape[0]
    o = 0
    parts = []
    for n in (MLA_Q_LORA, MLA_KV_LORA, MLA_ROPE, NA_HEADS * HEAD_DIM, NA_HEADS * HEAD_DIM, NA_HEADS * HEAD_DIM):
        parts.append(w[:, o:o + n])
        o += n
    q_a, kv_a, k_r, nq, nk, nv = parts
    slot = jnp.concatenate([jnp.zeros((d, MLA_NOPE), w.dtype), k_r,
                            jnp.zeros((d, LANES - MLA_QK), w.dtype)], axis=1)
    w_new = jnp.concatenate([q_a, kv_a, nq, nk, nv, slot], axis=1).astype(BF16)
    wq = jnp.pad(wq_up.reshape(MLA_Q_LORA, MLA_HEADS, MLA_QK), ((0, 0), (0, 0), (0, MLA_PAD - MLA_QK)))
    wq = wq.reshape(MLA_Q_LORA, MLA_HEADS * MLA_PAD).astype(BF16)
    wkv = wkv_up.reshape(MLA_KV_LORA, MLA_HEADS, MLA_NOPE + MLA_V)
    wk = jnp.pad(wkv[..., :MLA_NOPE], ((0, 0), (0, 0), (0, MLA_PAD - MLA_NOPE))).reshape(MLA_KV_LORA, -1)
    wv = wkv[..., MLA_NOPE:].reshape(MLA_KV_LORA, MLA_HEADS * MLA_V)
    wkv_new = jnp.concatenate([wk, wv], axis=1).astype(BF16)
    pad_gain = lambda g, scale: _tile_gain(jnp.pad(g.astype(F32), (0, MLA_PAD - MLA_QK)), CHUNK // MLA_PAD, scale)
    gains = (qa_g.astype(F32)[None, :], kva_g.astype(F32)[None, :],
             pad_gain(mq_g, MLA_QK ** -0.5 * LOG2E), pad_gain(mk_g, 1.0),
             _tile_gain(nq_g, CHUNK // HEAD_DIM, HEAD_DIM ** -0.5 * LOG2E), _tile_gain(nk_g, CHUNK // HEAD_DIM))
    return w_new, wq, wkv_new, gains


def kernel(x, c, ctx, c_ctx, ada_w, ada_b, norm1_g, norm2_g, w_out, mlp_w1, mlp_w2, ev_w_in, diff_q_g,
           diff_k_g, diff_lam, diff_subln_g, swa_q_g, swa_k_g, swa_sink, od_w_in, mla_qa_g, mla_kva_g,
           mla_wq_up, mla_wkv_up, mla_q_g, mla_k_g, na_q_g, na_k_g, na_rpb):
    nbatch, seq, d = x.shape
    ctx_len = ctx.shape[1]
    depth = ada_w.shape[0]
    n_lat = nbatch * seq
    n_ctx = nbatch * ctx_len
    assert seq % TM == 0 and n_ctx == TM and seq % TQ == 0 and seq % GRID_W == 0
    assert seq % TQ_FULL == 0 and seq % TK_MAX_FREE == 0 and seq % TK_FULL == 0
    assert seq // GRID_W >= 2 * (NA_ROWS + TQ_SUB // GRID_W) and seq >= TQ_SUB + 2 * WINDOW
    bpb = seq // TM

    mod_rows = 8
    cvec = jnp.concatenate([c, c_ctx[None, :], jnp.zeros((mod_rows - nbatch - 1, d), F32)], axis=0)
    mod = _modulation(cvec, ada_w, ada_b)

    cos64, sin64 = _rope_tables(seq, HEAD_DIM, None, n_ctx)
    cos32, sin32 = _rope_tables(seq, MLA_ROPE, MLA_NOPE, n_ctx)
    g64 = _block_diag_ones(HEAD_DIM)
    g128 = _block_diag_ones(LANES)

    x_src = (x.reshape(n_lat, d), ctx.reshape(n_ctx, d))
    wo_all, w1_all, w2_all = w_out.astype(BF16), mlp_w1.astype(BF16), mlp_w2.astype(BF16)
    common = dict(nbatch=nbatch, seq=seq, ctx_len=ctx_len)

    for layer in range(depth):
        need_ctx = layer < depth - 1
        n_blocks = bpb * nbatch + (1 if need_ctx else 0)
        n1 = norm1_g[layer][None, :]
        n2 = norm2_g[layer][None, :]
        i = layer // 2
        if layer % 2 == 0:
            lam_init = 0.8 - 0.6 * math.exp(-0.3 * layer)
            w_in, gain = _even_weights(ev_w_in[i], diff_q_g[i], diff_k_g[i], swa_q_g[i], swa_k_g[i])
            p = _proj_even(x_src, n1, mod, layer, w_in, gain, g64, cos64, sin64, bpb, nbatch)
            lam = diff_lam[i].astype(F32)
            subln = diff_subln_g[i].astype(F32)[None, :]
            sink = swa_sink[i].astype(F32)
            diff_kw = dict(mode="diff", split="mask", n_pairs=DIFF_HEADS, q_blk=lambda j: EVEN_QA // LANES + j,
                           k_blk=lambda j: EVEN_KA // LANES + j, v_blk=lambda j: EVEN_VA // LANES + j,
                           out_cols=DIFF_HEADS * LANES,
                           lam=lam, subln=subln, lam_init=lam_init,
                           safe=_logits_bounded(diff_q_g[i], diff_k_g[i], HEAD_DIM), **common)
            swa_kw = dict(mode="pair", split="mask", n_pairs=SWA_Q_HEADS // 2,
                          q_blk=lambda j: EVEN_QB // LANES + j, k_blk=lambda j: EVEN_KB // LANES + j // 2,
                          v_blk=lambda j: EVEN_VB // LANES + j // 2,
                          out_cols=SWA_Q_HEADS * HEAD_DIM, sink=sink,
                          safe=_logits_bounded(swa_q_g[i], swa_k_g[i], HEAD_DIM, sink=sink), **common)
            oa = _attention(p, name="diff_attn", main="full", ctx_queries=False, **diff_kw)
            ob = _attention(p, name="window_attn", main="window", ctx_queries=False, **swa_kw)
            if need_ctx:
                oa_c = _attention(p, name="diff_attn_ctx", main=None, ctx_queries=True, **diff_kw)
                ob_c = _attention(p, name="window_attn_ctx", main=None, ctx_queries=True, **swa_kw)
        else:
            w_in, wq, wkv, gains = _odd_weights(od_w_in[i], mla_wq_up[i], mla_wkv_up[i], mla_qa_g[i],
                                                mla_kva_g[i], mla_q_g[i], mla_k_g[i], na_q_g[i], na_k_g[i])
            p = _proj_odd(x_src[0], n1, mod, layer, w_in, wq, wkv, gains, g64, g128, cos32, sin32, bpb, nbatch)
            bias = _na_bias(na_rpb[i].astype(F32), seq)
            mla_kw = dict(mode="pair", split="slice", n_pairs=MLA_HEADS // 2, q_blk=lambda j: j,
                          k_blk=lambda j: ODD_K // CHUNK + j, v_blk=lambda j: ODD_V // LANES + j,
                          out_cols=MLA_HEADS * MLA_V,
                          safe=_logits_bounded(mla_q_g[i], mla_k_g[i], MLA_QK), **common)
            na_kw = dict(mode="pair", split="mask", n_pairs=NA_HEADS // 2,
                         q_blk=lambda j: ODD_NQ // LANES + j, k_blk=lambda j: ODD_NK // LANES + j,
                         v_blk=lambda j: ODD_NV // LANES + j, out_cols=NA_HEADS * HEAD_DIM,
                         safe=_logits_bounded(na_q_g[i], na_k_g[i], HEAD_DIM, bias=na_rpb[i]), **common)
            oa = _attention(p, name="mla_attn", main="full", ctx_queries=False, **mla_kw)
            ob = _attention(p, name="na_attn", main="na", ctx_queries=False, bias=bias, **na_kw)
            if need_ctx:
                oa_c = _attention(p, name="mla_attn_ctx", main=None, ctx_queries=True, **mla_kw)
                ob_c = _attention(p, name="na_attn_ctx", main=None, ctx_queries=True, **na_kw)
        oa_src, ob_src = ((oa, oa_c), (ob, ob_c)) if need_ctx else ((oa,), (ob,))
        x_src = (_post(x_src, oa_src, ob_src, wo_all, mod, n2, w1_all, w2_all, layer, n_blocks, bpb, nbatch),)
    return x_src[0].reshape(nbatch, seq, d)
```

```python
import functools
import math

import numpy as np
import jax
import jax.numpy as jnp
from jax import lax
from jax.experimental import pallas as pl
from jax.experimental.pallas import tpu as pltpu

F32 = jnp.float32
BF16 = jnp.bfloat16

LANES = 128
HALF = LANES // 2
CHUNK = 2 * LANES
GRID_W = 64
HEAD_DIM = 64
EPS = 1e-6
ROPE_BASE = 10000.0
DIFF_HEADS = 4
SWA_Q_HEADS = 8
SWA_KV_HEADS = 2
WINDOW = 128
MLA_HEADS = 8
MLA_Q_LORA = 512
MLA_KV_LORA = 256
MLA_NOPE = 64
MLA_ROPE = 32
MLA_QK = MLA_NOPE + MLA_ROPE
MLA_V = 64
NA_HEADS = 8
NA_ROWS = 8
NA_COLS = 16
NEG = -1e30
LOG2E = math.log2(math.e)
MAX_FREE_LOGIT_BOUND = 50.0
BF16_ROUNDING_MARGIN = 1.02

TM = 512
TQ = 2048
TQ_SUB = 256
TQ_FULL = 1024
TK_FULL = 512
TK_MAX_FREE = 2048
MOD_TN = 1536
VMEM_LIMIT = 56 * 1024 * 1024


def _params(*sem):
    return pltpu.CompilerParams(dimension_semantics=sem, vmem_limit_bytes=VMEM_LIMIT)


def _mod_kernel(c_ref, w_ref, b_ref, o_ref):
    c = c_ref[...]
    a = (c * jax.nn.sigmoid(c)).astype(BF16)
    w = w_ref[0].astype(BF16)
    o_ref[0] = jnp.dot(a, w, preferred_element_type=F32) + b_ref[0]


def _modulation(cvec, ada_w, ada_b):
    depth, d, n = ada_w.shape
    rows = cvec.shape[0]
    return pl.pallas_call(
        _mod_kernel,
        grid=(depth, n // MOD_TN),
        in_specs=[
            pl.BlockSpec((rows, d), lambda l, j: (0, 0)),
            pl.BlockSpec((1, d, MOD_TN), lambda l, j: (l, 0, j)),
            pl.BlockSpec((1, 1, MOD_TN), lambda l, j: (l, 0, j)),
        ],
        out_specs=pl.BlockSpec((1, rows, MOD_TN), lambda l, j: (l, 0, j)),
        out_shape=jax.ShapeDtypeStruct((depth, rows, n), F32),
        compiler_params=_params("parallel", "parallel"),
        name="modulation",
    )(cvec, ada_w, ada_b.reshape(depth, 1, n))


def _modulated(x, g_ref, shift, scale):
    ms = jnp.mean(x * x, axis=-1, keepdims=True)
    y = x * lax.rsqrt(ms + EPS) * g_ref[...]
    return y * (1.0 + scale) + shift


def _group_meansq(chunks, g_ref, inv_n):
    rows = chunks[0].shape[0]
    sq = jnp.concatenate([(c * c).astype(BF16) for c in chunks], axis=0)
    ss = jnp.dot(sq, g_ref[...], preferred_element_type=F32) * inv_n
    return [ss[i * rows:(i + 1) * rows] for i in range(len(chunks))]


def _swap_quarters(x, quarter):
    lane = lax.broadcasted_iota(jnp.int32, x.shape, 1)
    odd = (lane & quarter) != 0
    from_lower = pltpu.roll(x, quarter, 1)
    from_upper = pltpu.roll(x, LANES - quarter, 1)
    return jnp.where(odd, from_lower, from_upper)


def _rope(x, cos, sin_signed, quarter):
    return x * cos + _swap_quarters(x, quarter) * sin_signed


def _norm_rope_chunk(acc, meansq, gain, cos, sin, quarter):
    y = acc * lax.rsqrt(meansq + EPS) * gain
    if quarter is None:
        return y
    return jnp.concatenate([_rope(y[:, :LANES], cos, sin, quarter),
                            _rope(y[:, LANES:], cos, sin, quarter)], axis=1)


def _stream_specs(src, cols, n_lat_blocks):
    if len(src) == 1:
        return [pl.BlockSpec((TM, cols), lambda i: (i, 0))]
    return [pl.BlockSpec((TM, cols), lambda i: (jnp.minimum(i, n_lat_blocks - 1), 0)),
            pl.BlockSpec((TM, cols), lambda i: (0, 0))]


def _stream_block(refs, n_lat_blocks):
    if len(refs) == 1:
        return refs[0][...]
    return jnp.where(pl.program_id(0) < n_lat_blocks, refs[0][...], refs[1][...])


SHIFT1, SCALE1, GATE1, SHIFT2, SCALE2, GATE2 = range(6)


def _mod_specs(mod, layer, chunks):
    d = mod.shape[2] // 6
    return [pl.BlockSpec((None, mod.shape[1], d), lambda i, k=k: (layer, 0, k)) for k in chunks]


def _mod_row(ref, bpb, nbatch):
    row = jnp.minimum(pl.program_id(0) // bpb, nbatch)
    return ref[pl.ds(row, 1), :]


def _rope_spec(bpb, nbatch):
    return pl.BlockSpec((TM, LANES), lambda i: (jnp.where(i < bpb * nbatch, i % bpb, bpb), 0))


def _whole(shape):
    return pl.BlockSpec(shape, lambda *_: (0,) * len(shape))


EVEN_QA = 0
EVEN_KA = EVEN_QA + DIFF_HEADS * 2 * HEAD_DIM
EVEN_VA = EVEN_KA + DIFF_HEADS * 2 * HEAD_DIM
EVEN_QB = EVEN_VA + DIFF_HEADS * 2 * HEAD_DIM
EVEN_KB = EVEN_QB + SWA_Q_HEADS * HEAD_DIM
EVEN_VB = EVEN_KB + 2 * SWA_KV_HEADS * HEAD_DIM
EVEN_COLS = EVEN_VB + 2 * SWA_KV_HEADS * HEAD_DIM
EVEN_NORM_CHUNKS = tuple(c for c in range(EVEN_COLS // CHUNK)
                         if c * CHUNK < EVEN_VA or EVEN_QB <= c * CHUNK < EVEN_VB)
EVEN_DOT_RANGES = ((EVEN_QA // CHUNK, EVEN_VA // CHUNK), (EVEN_QB // CHUNK, EVEN_VB // CHUNK),
                   (EVEN_VA // CHUNK, EVEN_QB // CHUNK), (EVEN_VB // CHUNK, EVEN_COLS // CHUNK))


def _proj_even_kernel(*refs, n_x, bpb, nbatch):
    n_lat_blocks = bpb * nbatch
    x_refs = refs[:n_x]
    g_ref, sh_ref, sc_ref, w_ref, gain_ref, g64_ref, cos_ref, sin_ref, o_ref = refs[n_x:]
    h = _modulated(_stream_block(x_refs, n_lat_blocks), g_ref, _mod_row(sh_ref, bpb, nbatch),
                   _mod_row(sc_ref, bpb, nbatch)).astype(BF16)
    cos = cos_ref[...]
    sin = sin_ref[...]
    for first, last in EVEN_DOT_RANGES:
        full = jnp.dot(h, w_ref[:, first * CHUNK:last * CHUNK], preferred_element_type=F32)
        chunk = lambda c: full[:, (c - first) * CHUNK:(c - first + 1) * CHUNK]
        normed = [c for c in range(first, last) if c in EVEN_NORM_CHUNKS]
        meansq = {}
        if normed:
            meansq = dict(zip(normed, _group_meansq([chunk(c) for c in normed], g64_ref, 1.0 / HEAD_DIM)))
        for c in range(first, last):
            cols = slice(c * CHUNK, (c + 1) * CHUNK)
            acc = chunk(c)
            if c in EVEN_NORM_CHUNKS:
                acc = _norm_rope_chunk(acc, meansq[c], gain_ref[:, cols], cos, sin, HEAD_DIM // 4)
            o_ref[:, cols] = acc.astype(BF16)


def _proj_even(x_src, g, mod, layer, w, gain, g64, cos, sin, bpb, nbatch):
    d = x_src[0].shape[1]
    n_lat_blocks = bpb * nbatch
    kern = functools.partial(_proj_even_kernel, n_x=len(x_src), bpb=bpb, nbatch=nbatch)
    return pl.pallas_call(
        kern,
        grid=(n_lat_blocks + 1,),
        in_specs=(_stream_specs(x_src, d, n_lat_blocks) + [_whole(g.shape)]
                  + _mod_specs(mod, layer, (SHIFT1, SCALE1))
                  + [_whole(w.shape), _whole(gain.shape), _whole(g64.shape),
                     _rope_spec(bpb, nbatch), _rope_spec(bpb, nbatch)]),
        out_specs=pl.BlockSpec((TM, EVEN_COLS), lambda i: (i, 0)),
        out_shape=jax.ShapeDtypeStruct(((n_lat_blocks + 1) * TM, EVEN_COLS), BF16),
        compiler_params=_params("parallel"),
        name="proj_even",
    )(*x_src, g, mod, mod, w, gain, g64, cos, sin)


NA_WIDTH = NA_HEADS * HEAD_DIM
MLA_PAD = LANES
MLA_WIDTH = MLA_HEADS * MLA_PAD
ODD_W_KVA = MLA_Q_LORA
ODD_W_NQ = ODD_W_KVA + MLA_KV_LORA
ODD_W_NV = ODD_W_NQ + 2 * NA_WIDTH
ODD_W_SLOT = ODD_W_NV + NA_WIDTH
ODD_W_COLS = ODD_W_SLOT + LANES
ODD_K = MLA_WIDTH
ODD_V = ODD_K + MLA_WIDTH
ODD_NQ = ODD_V + MLA_HEADS * MLA_V
ODD_NK = ODD_NQ + NA_WIDTH
ODD_NV = ODD_NK + NA_WIDTH
ODD_COLS = ODD_NV + NA_WIDTH


def _proj_odd_kernel(x_ref, g_ref, sh_ref, sc_ref, w_ref, wq_ref, wkv_ref, qa_g_ref, kva_g_ref,
                     mq_g_ref, mk_g_ref, nq_g_ref, nk_g_ref, g64_ref, g128_ref, cos_ref, sin_ref, o_ref,
                     *, bpb, nbatch):
    h = _modulated(x_ref[...], g_ref, _mod_row(sh_ref, bpb, nbatch), _mod_row(sc_ref, bpb, nbatch)).astype(BF16)
    cos = cos_ref[...]
    sin = sin_ref[...]
    quarter = MLA_ROPE // 4

    lora = jnp.dot(h, w_ref[:, :ODD_W_NQ], preferred_element_type=F32)
    rest = jnp.dot(h, w_ref[:, ODD_W_NQ:], preferred_element_type=F32)

    def low_rank(a, gain_ref):
        ms = jnp.mean(a * a, axis=-1, keepdims=True)
        return (a * lax.rsqrt(ms + EPS) * gain_ref[...]).astype(BF16)

    n_mla = MLA_WIDTH // CHUNK
    chunks_of = lambda a, first, n: [a[:, first + c * CHUNK:first + (c + 1) * CHUNK] for c in range(n)]
    qa = low_rank(lora[:, :MLA_Q_LORA], qa_g_ref)
    qf = jnp.dot(qa, wq_ref[...], preferred_element_type=F32)
    q_chunks = chunks_of(qf, 0, n_mla)
    ms_q = _group_meansq(q_chunks, g128_ref, 1.0 / MLA_QK)
    kva = low_rank(lora[:, ODD_W_KVA:], kva_g_ref)
    kvf = jnp.dot(kva, wkv_ref[...], preferred_element_type=F32)
    kslot = rest[:, ODD_W_SLOT - ODD_W_NQ:]
    kslot2 = jnp.concatenate([kslot, kslot], axis=1)
    k_rot = _swap_quarters(kslot * mk_g_ref[:, :LANES], quarter)
    k_rot2 = jnp.concatenate([k_rot, k_rot], axis=1)
    cos2 = jnp.concatenate([cos, cos], axis=1)
    sin2 = jnp.concatenate([sin, sin], axis=1)
    k_chunks = [kc + kslot2 for kc in chunks_of(kvf, 0, n_mla)]
    ms_k = _group_meansq(k_chunks, g128_ref, 1.0 / MLA_QK)
    for c in range(n_mla):
        y = _norm_rope_chunk(q_chunks[c], ms_q[c], mq_g_ref[...], cos, sin, quarter)
        o_ref[:, c * CHUNK:(c + 1) * CHUNK] = y.astype(BF16)
    for c in range(n_mla):
        inv_rms = lax.rsqrt(ms_k[c] + EPS)
        y = (k_chunks[c] * inv_rms * mk_g_ref[...]) * cos2 + (inv_rms * k_rot2) * sin2
        o_ref[:, ODD_K + c * CHUNK:ODD_K + (c + 1) * CHUNK] = y.astype(BF16)
    o_ref[:, ODD_V:ODD_NQ] = kvf[:, MLA_WIDTH:].astype(BF16)

    n_na = NA_WIDTH // CHUNK
    n_chunks = chunks_of(rest, 0, 2 * n_na)
    ms64 = _group_meansq(n_chunks, g64_ref, 1.0 / HEAD_DIM)
    for c in range(2 * n_na):
        gain = nq_g_ref[...] if c < n_na else nk_g_ref[...]
        y = _norm_rope_chunk(n_chunks[c], ms64[c], gain, None, None, None)
        o_ref[:, ODD_NQ + c * CHUNK:ODD_NQ + (c + 1) * CHUNK] = y.astype(BF16)
    o_ref[:, ODD_NV:ODD_COLS] = rest[:, ODD_W_NV - ODD_W_NQ:ODD_W_SLOT - ODD_W_NQ].astype(BF16)


def _proj_odd(xa, g, mod, layer, w, wq, wkv, gains, g64, g128, cos, sin, bpb, nbatch):
    na, d = xa.shape
    tok_specs = _stream_specs((xa,), d, bpb * nbatch) + [_whole(g.shape)] + _mod_specs(mod, layer, (SHIFT1, SCALE1))
    consts = (w, wq, wkv) + tuple(gains) + (g64, g128)
    return pl.pallas_call(
        functools.partial(_proj_odd_kernel, bpb=bpb, nbatch=nbatch),
        grid=(na // TM,),
        in_specs=tok_specs + [_whole(a.shape) for a in consts] + [_rope_spec(bpb, nbatch)] * 2,
        out_specs=pl.BlockSpec((TM, ODD_COLS), lambda i: (i, 0)),
        out_shape=jax.ShapeDtypeStruct((na, ODD_COLS), BF16),
        compiler_params=_params("parallel"),
        name="proj_odd",
    )(xa, g, mod, mod, *consts, cos, sin)


def _attn_kernel(*refs, mode, split, main, seq, has_sink, lam_init, tq, tq_sub):
    n_sub = tq // tq_sub
    it = iter(refs)
    q_ref = next(it)
    k_ref = v_ref = sink_ref = lam_ref = subln_ref = None
    bias_refs = ()
    if main is not None:
        k_ref, v_ref = next(it), next(it)
    kc_ref, vc_ref = next(it), next(it)
    if main == "na":
        bias_refs = tuple(next(it) for _ in range(n_sub))
    if has_sink:
        sink_ref = next(it)
    safe_ref = next(it)
    if mode == "diff":
        lam_ref, subln_ref = next(it), next(it)
    o_ref = next(it)
    vt_ref, vct_ref = (next(it), next(it)) if main == "full" else (None, None)

    pair = pl.program_id(1)
    qi = pl.program_id(2)
    q = q_ref[...]
    low = lax.broadcasted_iota(jnp.int32, (1, LANES), 1) < HALF
    if split == "mask":
        zero = jnp.zeros_like(q)
        qs = (jnp.where(low, q, zero), jnp.where(low, zero, q))
        k_of = lambda k, s: k
    else:
        qs = (q[:, :LANES], q[:, LANES:])
        k_of = lambda k, s: k[:, s * LANES:(s + 1) * LANES]
    q_of = lambda s, t: qs[s][t * tq_sub:(t + 1) * tq_sub]
    sinks = [sink_ref[2 * pair + s] * LOG2E for s in range(2)] if has_sink else None
    nt_dims = (((1,), (1,)), ((), ()))

    def finish(o_lo, o_hi):
        if mode == "diff":
            lv = lam_ref[...]
            lam = (jnp.exp(jnp.sum(lv[0:1] * lv[1:2], axis=-1, keepdims=True))
                   - jnp.exp(jnp.sum(lv[2:3] * lv[3:4], axis=-1, keepdims=True)) + lam_init)
            o = o_lo - lam * o_hi
            ms = jnp.mean(o * o, axis=-1, keepdims=True)
            o = o * lax.rsqrt(ms + EPS) * subln_ref[...] * (1.0 - lam_init)
        else:
            o = jnp.where(low, o_lo, o_hi)
        o_ref[...] = o.astype(BF16)

    window_masks = {}

    def local_keys(t):
        if main == "window":
            span = tq_sub + 2 * WINDOW
            q0 = qi * tq + t * tq_sub
            at_edge = t in (0, n_sub - 1)
            w0 = jnp.clip(q0 - WINDOW, 0, seq - span) if at_edge else q0 - WINDOW
            off = pl.multiple_of(w0, WINDOW)
            if at_edge or "inner" not in window_masks:
                key = lax.broadcasted_iota(jnp.int32, (span, tq_sub), 0)
                qry = lax.broadcasted_iota(jnp.int32, (span, tq_sub), 1)
                inside = jnp.abs(key - qry + ((w0 - q0) if at_edge else -WINDOW)) <= WINDOW
                if not at_edge:
                    window_masks["inner"] = inside
            else:
                inside = window_masks["inner"]
            logit_fn = lambda lg_t, s: jnp.where(inside, lg_t, NEG)
        else:
            rows_q = tq_sub // GRID_W
            span_rows = rows_q + NA_ROWS
            w0 = jnp.clip(qi * (tq // GRID_W) + t * rows_q - NA_ROWS // 2, 0, seq // GRID_W - span_rows)
            off = pl.multiple_of(w0 * GRID_W, GRID_W)
            span = span_rows * GRID_W
            logit_fn = lambda lg_t, s: lg_t + bias_refs[t][0, s]
        return k_ref[pl.ds(off, span), :], v_ref[pl.ds(off, span), :], logit_fn

    def online_sub(t):
        def init(s):
            if has_sink:
                m0 = jnp.full((tq_sub, 1), sinks[s], F32)
                l0 = jnp.ones((tq_sub, 1), F32)
            else:
                m0 = jnp.full((tq_sub, 1), NEG, F32)
                l0 = jnp.zeros((tq_sub, 1), F32)
            return m0, l0, jnp.zeros((tq_sub, LANES), F32)

        def segment(state, k, v, logit_fn):
            out = []
            for s in range(2):
                m, l, acc = state[s]
                logits = lax.dot_general(q_of(s, t), k_of(k, s), nt_dims, preferred_element_type=F32)
                if logit_fn is not None:
                    logits = logit_fn(logits.T, s).T
                m_new = jnp.maximum(m, jnp.max(logits, axis=-1, keepdims=True))
                alpha = jnp.exp2(m - m_new)
                p = jnp.exp2(logits - m_new)
                l = alpha * l + jnp.sum(p, axis=-1, keepdims=True)
                acc = alpha * acc + jnp.dot(p.astype(BF16), v, preferred_element_type=F32)
                out.append((m_new, l, acc))
            return tuple(out)

        state = (init(0), init(1))
        if main == "full":
            def body(c, st):
                off = pl.multiple_of(c * TK_FULL, TK_FULL)
                return segment(st, k_ref[pl.ds(off, TK_FULL), :], v_ref[pl.ds(off, TK_FULL), :], None)
            state = lax.fori_loop(0, seq // TK_FULL, body, state)
        elif main is not None:
            state = segment(state, *local_keys(t))
        (_, l_lo, acc_lo), (_, l_hi, acc_hi) = segment(state, kc_ref[...], vc_ref[...], None)
        return acc_lo / l_lo, acc_hi / l_hi

    def probs_t(k, q_rows, s, logit_fn):
        logits_t = lax.dot_general(k_of(k, s), q_rows, nt_dims, preferred_element_type=F32)
        if logit_fn is not None:
            logits_t = logit_fn(logits_t, s)
        p = jnp.exp2(logits_t)
        return p.astype(BF16), jnp.sum(p, axis=0, keepdims=True)

    def max_free():
        window_masks.clear()
        if main == "full":
            @pl.when(qi == 0)
            def _():
                for c in range(seq // TK_MAX_FREE):
                    vt_ref[c] = v_ref[c * TK_MAX_FREE:(c + 1) * TK_MAX_FREE, :].T
                vct_ref[...] = vc_ref[...].T

            acc_lat = [jnp.zeros((LANES, tq), F32)] * 2
            den_lat = [jnp.zeros((1, tq), F32)] * 2
            for c in range(seq // TK_MAX_FREE):
                k = k_ref[c * TK_MAX_FREE:(c + 1) * TK_MAX_FREE, :]
                for s in range(2):
                    p, p_sum = probs_t(k, qs[s], s, None)
                    acc_lat[s] = acc_lat[s] + jnp.dot(vt_ref[c], p, preferred_element_type=F32)
                    den_lat[s] = den_lat[s] + p_sum
            vct = vct_ref[...]
        else:
            vct = vc_ref[...].T
        p_ctx = [probs_t(kc_ref[...], qs[s], s, None) for s in range(2)]
        if main in ("window", "na"):
            windows = [local_keys(t) for t in range(n_sub)]
            p_loc = [[probs_t(k, q_of(s, t), s, fn) for s in range(2)] for t, (k, _, fn) in enumerate(windows)]
            vts = [v.T for _, v, _ in windows]
        outs = []
        for s in range(2):
            acc = jnp.dot(vct, p_ctx[s][0], preferred_element_type=F32)
            den = p_ctx[s][1]
            if main == "full":
                acc, den = acc + acc_lat[s], den + den_lat[s]
            elif main is not None:
                acc = acc + jnp.concatenate(
                    [jnp.dot(vts[t], p_loc[t][s][0], preferred_element_type=F32) for t in range(n_sub)], axis=1)
                den = den + jnp.concatenate([p_loc[t][s][1] for t in range(n_sub)], axis=1)
            if has_sink:
                den = den + jnp.exp2(jnp.full((1, tq), sinks[s], F32))
            outs.append((acc / den).T)
        finish(*outs)

    def online():
        window_masks.clear()
        parts = [online_sub(t) for t in range(n_sub)]
        finish(*[jnp.concatenate([p[s] for p in parts], axis=0) if n_sub > 1 else parts[0][s]
                 for s in range(2)])

    safe = safe_ref[0] == 1
    pl.when(safe)(max_free)
    pl.when(jnp.logical_not(safe))(online)


def _attention(p, *, name, mode, split, main, n_pairs, q_blk, k_blk, v_blk, out_cols, ctx_queries,
               nbatch, seq, ctx_len, safe, bias=None, sink=None, lam=None, subln=None, lam_init=0.0):
    n_lat = nbatch * seq
    qw = LANES if split == "mask" else 2 * LANES
    if ctx_queries:
        tq, nq = ctx_len, 1
        q_row = lambda b, j, i: n_lat // ctx_len + b
        out_rows = nbatch * ctx_len
        out_row = lambda b, j, i: b
    else:
        tq = TQ_FULL if main == "full" else TQ
        nq = seq // tq
        q_row = lambda b, j, i: b * nq + i
        out_rows = n_lat
        out_row = q_row
    ctx_row = lambda b, j, i: n_lat // ctx_len + b

    in_specs = [pl.BlockSpec((tq, qw), lambda b, j, i: (q_row(b, j, i), q_blk(j)))]
    args = [p]
    if main is not None:
        in_specs += [pl.BlockSpec((seq, qw), lambda b, j, i: (b, k_blk(j))),
                     pl.BlockSpec((seq, LANES), lambda b, j, i: (b, v_blk(j)))]
        args += [p, p]
    in_specs += [pl.BlockSpec((ctx_len, qw), lambda b, j, i: (ctx_row(b, j, i), k_blk(j))),
                 pl.BlockSpec((ctx_len, LANES), lambda b, j, i: (ctx_row(b, j, i), v_blk(j)))]
    args += [p, p]
    tq_sub = TQ_SUB if main in ("window", "na") else tq
    n_sub = tq // tq_sub
    if main == "na":
        span = (tq_sub // GRID_W + NA_ROWS) * GRID_W
        last = nq * n_sub - 1
        cls = lambda g: jnp.where(g == 0, 0, jnp.where(g == last, 2, 1))
        for t in range(n_sub):
            in_specs.append(pl.BlockSpec((1, 2, span, tq_sub),
                                         lambda b, j, i, t=t: (cls(i * n_sub + t), j, 0, 0)))
            args.append(bias)
    if sink is not None:
        in_specs.append(pl.BlockSpec(memory_space=pltpu.SMEM))
        args.append(sink)
    in_specs.append(pl.BlockSpec(memory_space=pltpu.SMEM))
    args.append(safe)
    if mode == "diff":
        in_specs += [_whole(lam.shape), _whole(subln.shape)]
        args += [lam, subln]

    kern = functools.partial(_attn_kernel, mode=mode, split=split, main=main, seq=seq,
                             has_sink=sink is not None, lam_init=lam_init, tq=tq, tq_sub=tq_sub)
    scratch = []
    if main == "full":
        scratch = [pltpu.VMEM((seq // TK_MAX_FREE, LANES, TK_MAX_FREE), BF16), pltpu.VMEM((LANES, ctx_len), BF16)]
    return pl.pallas_call(
        kern,
        grid=(nbatch, n_pairs, nq),
        in_specs=in_specs,
        out_specs=pl.BlockSpec((tq, LANES), lambda b, j, i: (out_row(b, j, i), j)),
        out_shape=jax.ShapeDtypeStruct((out_rows, out_cols), BF16),
        scratch_shapes=scratch,
        compiler_params=_params("parallel", "parallel", "arbitrary"),
        name=name,
    )(*args)


def _na_bias_kernel(rpb_ref, o_ref, *, n_rows, rows_q):
    h = pl.program_id(0)
    n_dr = 2 * NA_ROWS - 1
    n_dc = 2 * NA_COLS - 1
    kc = lax.broadcasted_iota(jnp.int32, (GRID_W, LANES), 0)
    cq = lax.broadcasted_iota(jnp.int32, (GRID_W, LANES), 1) & (GRID_W - 1)
    dc_idx = jnp.clip(kc - cq, -(NA_COLS - 1), NA_COLS - 1) + NA_COLS - 1
    cs = jnp.clip(cq - NA_COLS // 2, 0, GRID_W - NA_COLS)
    col_ok = (kc >= cs) & (kc < cs + NA_COLS)
    tiles = [jnp.zeros((GRID_W, LANES), F32) for _ in range(n_dr)]
    for dc in range(n_dc):
        hit = dc_idx == dc
        for dr in range(n_dr):
            tiles[dr] = jnp.where(hit, rpb_ref[(h * n_dr + dr) * n_dc + dc] * LOG2E, tiles[dr])
    tiles = [jnp.where(col_ok, t, NEG) for t in tiles]
    masked = jnp.full((GRID_W, LANES), NEG, F32)
    low = lax.broadcasted_iota(jnp.int32, (GRID_W, LANES), 1) < HALF

    span_rows = rows_q + NA_ROWS
    first_q_row = (0, span_rows, n_rows - rows_q)
    for c in range(3):
        r0 = first_q_row[c]
        w0 = min(max(r0 - NA_ROWS // 2, 0), n_rows - span_rows)
        for kr in range(span_rows):
            k_abs = w0 + kr
            for m in range(rows_q // 2):
                halves = []
                for r in (r0 + 2 * m, r0 + 2 * m + 1):
                    rs = min(max(r - NA_ROWS // 2, 0), n_rows - NA_ROWS)
                    halves.append(tiles[k_abs - r + NA_ROWS - 1] if rs <= k_abs < rs + NA_ROWS else masked)
                o_ref[c, 0, kr * GRID_W:(kr + 1) * GRID_W, m * LANES:(m + 1) * LANES] = (
                    jnp.where(low, halves[0], halves[1]))


def _na_bias(rpb, seq):
    n_rows = seq // GRID_W
    rows_q = TQ_SUB // GRID_W
    span = (rows_q + NA_ROWS) * GRID_W
    kern = functools.partial(_na_bias_kernel, n_rows=n_rows, rows_q=rows_q)
    return pl.pallas_call(
        kern,
        grid=(NA_HEADS,),
        in_specs=[pl.BlockSpec(memory_space=pltpu.SMEM)],
        out_specs=pl.BlockSpec((3, 1, span, TQ_SUB), lambda h: (0, h, 0, 0)),
        out_shape=jax.ShapeDtypeStruct((3, NA_HEADS, span, TQ_SUB), F32),
        compiler_params=_params("parallel"),
        name="na_bias",
    )(rpb.reshape(-1))


def _post_kernel(*refs, n_x, n_o, bpb, nbatch):
    n_lat_blocks = bpb * nbatch
    x_refs, oa_refs, ob_refs = refs[:n_x], refs[n_x:n_x + n_o], refs[n_x + n_o:n_x + 2 * n_o]
    wo_ref, g1_ref, n2_ref, sh_ref, sc_ref, g2_ref, w1_ref, w2_ref, o_ref = refs[n_x + 2 * n_o:]
    oa = _stream_block(oa_refs, n_lat_blocks)
    ob = _stream_block(ob_refs, n_lat_blocks)
    half = oa.shape[1]
    y = (jnp.dot(oa, wo_ref[:half, :], preferred_element_type=F32)
         + jnp.dot(ob, wo_ref[half:, :], preferred_element_type=F32))
    x1 = _stream_block(x_refs, n_lat_blocks) + _mod_row(g1_ref, bpb, nbatch) * y
    h = _modulated(x1, n2_ref, _mod_row(sh_ref, bpb, nbatch), _mod_row(sc_ref, bpb, nbatch)).astype(BF16)
    a = jnp.dot(h, w1_ref[...], preferred_element_type=F32)
    a = jnp.square(jnp.maximum(a, 0.0)).astype(BF16)
    o_ref[...] = x1 + _mod_row(g2_ref, bpb, nbatch) * jnp.dot(a, w2_ref[...], preferred_element_type=F32)


def _post(x_src, oa_src, ob_src, wo, mod, n2, w1, w2, layer, n_blocks, bpb, nbatch):
    d = x_src[0].shape[1]
    half = oa_src[0].shape[1]
    n_lat_blocks = bpb * nbatch
    g1_spec, sh_spec, sc_spec, g2_spec = _mod_specs(mod, layer, (GATE1, SHIFT2, SCALE2, GATE2))
    resident = lambda a: pl.BlockSpec((None,) + a.shape[1:], lambda i: (layer, 0, 0),
                                      pipeline_mode=pl.Buffered(1))
    kern = functools.partial(_post_kernel, n_x=len(x_src), n_o=len(oa_src), bpb=bpb, nbatch=nbatch)
    return pl.pallas_call(
        kern,
        grid=(n_blocks,),
        in_specs=(_stream_specs(x_src, d, n_lat_blocks) + _stream_specs(oa_src, half, n_lat_blocks)
                  + _stream_specs(ob_src, half, n_lat_blocks)
                  + [resident(wo), g1_spec, _whole(n2.shape), sh_spec, sc_spec, g2_spec, resident(w1), resident(w2)]),
        out_specs=pl.BlockSpec((TM, d), lambda i: (i, 0)),
        out_shape=jax.ShapeDtypeStruct((n_blocks * TM, d), F32),
        compiler_params=_params("parallel"),
        name="post",
    )(*x_src, *oa_src, *ob_src, wo, mod, n2, mod, mod, mod, w1, w2)


def _block_diag_ones(group):
    idx = np.arange(CHUNK) // group
    return jnp.asarray(idx[:, None] == idx[None, :], dtype=BF16)


def _rope_tables(seq, rot_dim, lane0, pad_rows):
    t = jnp.arange(seq, dtype=jnp.int32)
    row = (t // GRID_W).astype(F32)
    col = (t % GRID_W).astype(F32)
    n_freq = rot_dim // 4
    freqs = jnp.power(ROPE_BASE, -jnp.arange(n_freq, dtype=F32) / n_freq)
    ar = row[:, None] * freqs[None, :]
    ac = col[:, None] * freqs[None, :]
    ang = jnp.concatenate([ar, ar, ac, ac], axis=-1)
    sign = jnp.asarray(np.tile(np.repeat([-1.0, 1.0], n_freq), 2), F32)
    cos, sin = jnp.cos(ang), jnp.sin(ang) * sign[None, :]
    if lane0 is None:
        reps = LANES // rot_dim
        cos, sin = jnp.tile(cos, (1, reps)), jnp.tile(sin, (1, reps))
    else:
        pad = ((0, 0), (lane0, LANES - lane0 - rot_dim))
        cos = jnp.pad(cos, pad, constant_values=1.0)
        sin = jnp.pad(sin, pad)
    cos = jnp.concatenate([cos, jnp.ones((pad_rows, LANES), F32)], axis=0)
    sin = jnp.concatenate([sin, jnp.zeros((pad_rows, LANES), F32)], axis=0)
    return cos, sin


def _logits_bounded(q_gain, k_gain, dim, bias=None, sink=None):
    bound = jnp.max(jnp.abs(q_gain)) * jnp.max(jnp.abs(k_gain)) * (dim ** 0.5) * BF16_ROUNDING_MARGIN
    if bias is not None:
        bound = bound + jnp.max(jnp.abs(bias))
    if sink is not None:
        bound = jnp.maximum(bound, jnp.max(jnp.abs(sink)))
    return (bound <= MAX_FREE_LOGIT_BOUND).astype(jnp.int32).reshape(1)


def _tile_gain(g, reps, scale=1.0):
    return (jnp.tile(g.astype(F32), reps) * scale)[None, :]


def _even_weights(w, q_g, k_g, sq_g, sk_g):
    nq = DIFF_HEADS * 2 * HEAD_DIM
    qa, ka, va = w[:, :nq], w[:, nq:2 * nq], w[:, 2 * nq:3 * nq]
    o = 3 * nq
    qb = w[:, o:o + SWA_Q_HEADS * HEAD_DIM]
    o += SWA_Q_HEADS * HEAD_DIM
    kb = [w[:, o + i * HEAD_DIM:o + (i + 1) * HEAD_DIM] for i in range(SWA_KV_HEADS)]
    o += SWA_KV_HEADS * HEAD_DIM
    vb = [w[:, o + i * HEAD_DIM:o + (i + 1) * HEAD_DIM] for i in range(SWA_KV_HEADS)]
    dup = lambda parts: [p for p in parts for _ in range(2)]
    w_new = jnp.concatenate([qa, ka, va, qb] + dup(kb) + dup(vb), axis=1).astype(BF16)
    scale = HEAD_DIM ** -0.5 * LOG2E
    ones = lambda n: jnp.ones((1, n), F32)
    gain = jnp.concatenate([
        _tile_gain(q_g, 2 * DIFF_HEADS, scale), _tile_gain(k_g, 2 * DIFF_HEADS), ones(nq),
        _tile_gain(sq_g, SWA_Q_HEADS, scale), _tile_gain(sk_g, 2 * SWA_KV_HEADS),
        ones(2 * SWA_KV_HEADS * HEAD_DIM)], axis=1)
    assert w_new.shape[1] == gain.shape[1] == EVEN_COLS
    return w_new, gain


def _odd_weights(w, wq_up, wkv_up, qa_g, kva_g, mq_g, mk_g, nq_g, nk_g):
    d = w.shape[0]
    o = 0
    parts = []
    for n in (MLA_Q_LORA, MLA_KV_LORA, MLA_ROPE, NA_HEADS * HEAD_DIM, NA_HEADS * HEAD_DIM, NA_HEADS * HEAD_DIM):
        parts.append(w[:, o:o + n])
        o += n
    q_a, kv_a, k_r, nq, nk, nv = parts
    slot = jnp.concatenate([jnp.zeros((d, MLA_NOPE), w.dtype), k_r,
                            jnp.zeros((d, LANES - MLA_QK), w.dtype)], axis=1)
    w_new = jnp.concatenate([q_a, kv_a, nq, nk, nv, slot], axis=1).astype(BF16)
    assert w_new.shape[1] == ODD_W_COLS
    wq = jnp.pad(wq_up.reshape(MLA_Q_LORA, MLA_HEADS, MLA_QK), ((0, 0), (0, 0), (0, MLA_PAD - MLA_QK)))
    wq = wq.reshape(MLA_Q_LORA, MLA_HEADS * MLA_PAD).astype(BF16)
    wkv = wkv_up.reshape(MLA_KV_LORA, MLA_HEADS, MLA_NOPE + MLA_V)
    wk = jnp.pad(wkv[..., :MLA_NOPE], ((0, 0), (0, 0), (0, MLA_PAD - MLA_NOPE))).reshape(MLA_KV_LORA, -1)
    wv = wkv[..., MLA_NOPE:].reshape(MLA_KV_LORA, MLA_HEADS * MLA_V)
    wkv_new = jnp.concatenate([wk, wv], axis=1).astype(BF16)
    pad_gain = lambda g, scale: _tile_gain(jnp.pad(g.astype(F32), (0, MLA_PAD - MLA_QK)), CHUNK // MLA_PAD, scale)
    gains = (qa_g.astype(F32)[None, :], kva_g.astype(F32)[None, :],
             pad_gain(mq_g, MLA_QK ** -0.5 * LOG2E), pad_gain(mk_g, 1.0),
             _tile_gain(nq_g, CHUNK // HEAD_DIM, HEAD_DIM ** -0.5 * LOG2E), _tile_gain(nk_g, CHUNK // HEAD_DIM))
    return w_new, wq, wkv_new, gains


def kernel(x, c, ctx, c_ctx, ada_w, ada_b, norm1_g, norm2_g, w_out, mlp_w1, mlp_w2, ev_w_in, diff_q_g,
           diff_k_g, diff_lam, diff_subln_g, swa_q_g, swa_k_g, swa_sink, od_w_in, mla_qa_g, mla_kva_g,
           mla_wq_up, mla_wkv_up, mla_q_g, mla_k_g, na_q_g, na_k_g, na_rpb):
    nbatch, seq, d = x.shape
    ctx_len = ctx.shape[1]
    depth = ada_w.shape[0]
    n_lat = nbatch * seq
    n_ctx = nbatch * ctx_len
    assert seq % TM == 0 and n_ctx == TM and seq % TQ == 0 and seq % GRID_W == 0
    assert seq % TQ_FULL == 0 and seq % TK_MAX_FREE == 0 and seq % TK_FULL == 0
    assert seq // GRID_W >= 2 * (NA_ROWS + TQ_SUB // GRID_W) and seq >= TQ_SUB + 2 * WINDOW
    bpb = seq // TM

    mod_rows = 8
    cvec = jnp.concatenate([c, c_ctx[None, :], jnp.zeros((mod_rows - nbatch - 1, d), F32)], axis=0)
    mod = _modulation(cvec, ada_w, ada_b)

    cos64, sin64 = _rope_tables(seq, HEAD_DIM, None, n_ctx)
    cos32, sin32 = _rope_tables(seq, MLA_ROPE, MLA_NOPE, n_ctx)
    g64 = _block_diag_ones(HEAD_DIM)
    g128 = _block_diag_ones(LANES)

    x_src = (x.reshape(n_lat, d), ctx.reshape(n_ctx, d))
    wo_all, w1_all, w2_all = w_out.astype(BF16), mlp_w1.astype(BF16), mlp_w2.astype(BF16)
    common = dict(nbatch=nbatch, seq=seq, ctx_len=ctx_len)

    for layer in range(depth):
        need_ctx = layer < depth - 1
        n_blocks = bpb * nbatch + (1 if need_ctx else 0)
        n1 = norm1_g[layer][None, :]
        n2 = norm2_g[layer][None, :]
        i = layer // 2
        if layer % 2 == 0:
            lam_init = 0.8 - 0.6 * math.exp(-0.3 * layer)
            w_in, gain = _even_weights(ev_w_in[i], diff_q_g[i], diff_k_g[i], swa_q_g[i], swa_k_g[i])
            p = _proj_even(x_src, n1, mod, layer, w_in, gain, g64, cos64, sin64, bpb, nbatch)
            lam = diff_lam[i].astype(F32)
            subln = diff_subln_g[i].astype(F32)[None, :]
            sink = swa_sink[i].astype(F32)
            diff_kw = dict(mode="diff", split="mask", n_pairs=DIFF_HEADS, q_blk=lambda j: EVEN_QA // LANES + j,
                           k_blk=lambda j: EVEN_KA // LANES + j, v_blk=lambda j: EVEN_VA // LANES + j,
                           out_cols=DIFF_HEADS * LANES,
                           lam=lam, subln=subln, lam_init=lam_init,
                           safe=_logits_bounded(diff_q_g[i], diff_k_g[i], HEAD_DIM), **common)
            swa_kw = dict(mode="pair", split="mask", n_pairs=SWA_Q_HEADS // 2,
                          q_blk=lambda j: EVEN_QB // LANES + j, k_blk=lambda j: EVEN_KB // LANES + j // 2,
                          v_blk=lambda j: EVEN_VB // LANES + j // 2,
                          out_cols=SWA_Q_HEADS * HEAD_DIM, sink=sink,
                          safe=_logits_bounded(swa_q_g[i], swa_k_g[i], HEAD_DIM, sink=sink), **common)
            oa = _attention(p, name="diff_attn", main="full", ctx_queries=False, **diff_kw)
            ob = _attention(p, name="window_attn", main="window", ctx_queries=False, **swa_kw)
            if need_ctx:
                oa_c = _attention(p, name="diff_attn_ctx", main=None, ctx_queries=True, **diff_kw)
                ob_c = _attention(p, name="window_attn_ctx", main=None, ctx_queries=True, **swa_kw)
        else:
            w_in, wq, wkv, gains = _odd_weights(od_w_in[i], mla_wq_up[i], mla_wkv_up[i], mla_qa_g[i],
                                                mla_kva_g[i], mla_q_g[i], mla_k_g[i], na_q_g[i], na_k_g[i])
            p = _proj_odd(x_src[0], n1, mod, layer, w_in, wq, wkv, gains, g64, g128, cos32, sin32, bpb, nbatch)
            bias = _na_bias(na_rpb[i].astype(F32), seq)
            mla_kw = dict(mode="pair", split="slice", n_pairs=MLA_HEADS // 2, q_blk=lambda j: j,
                          k_blk=lambda j: ODD_K // CHUNK + j, v_blk=lambda j: ODD_V // LANES + j,
                          out_cols=MLA_HEADS * MLA_V,
                          safe=_logits_bounded(mla_q_g[i], mla_k_g[i], MLA_QK), **common)
            na_kw = dict(mode="pair", split="mask", n_pairs=NA_HEADS // 2,
                         q_blk=lambda j: ODD_NQ // LANES + j, k_blk=lambda j: ODD_NK // LANES + j,
                         v_blk=lambda j: ODD_NV // LANES + j, out_cols=NA_HEADS * HEAD_DIM,
                         safe=_logits_bounded(na_q_g[i], na_k_g[i], HEAD_DIM, bias=na_rpb[i]), **common)
            oa = _attention(p, name="mla_attn", main="full", ctx_queries=False, **mla_kw)
            ob = _attention(p, name="na_attn", main="na", ctx_queries=False, bias=bias, **na_kw)
            if need_ctx:
                oa_c = _attention(p, name="mla_attn_ctx", main=None, ctx_queries=True, **mla_kw)
                ob_c = _attention(p, name="na_attn_ctx", main=None, ctx_queries=True, **na_kw)
        oa_src, ob_src = ((oa, oa_c), (ob, ob_c)) if need_ctx else ((oa,), (ob,))
        x_src = (_post(x_src, oa_src, ob_src, wo_all, mod, n2, w1_all, w2_all, layer, n_blocks, bpb, nbatch),)
    return x_src[0].reshape(nbatch, seq, d)
```

```python
import functools
import math

import numpy as np
import jax
import jax.numpy as jnp
from jax import lax
from jax.experimental import pallas as pl
from jax.experimental.pallas import tpu as pltpu

F32 = jnp.float32
BF16 = jnp.bfloat16

LANES = 128
HALF = LANES // 2
CHUNK = 2 * LANES
GRID_W = 64
HEAD_DIM = 64
EPS = 1e-6
ROPE_BASE = 10000.0
DIFF_HEADS = 4
SWA_Q_HEADS = 8
SWA_KV_HEADS = 2
WINDOW = 128
MLA_HEADS = 8
MLA_Q_LORA = 512
MLA_KV_LORA = 256
MLA_NOPE = 64
MLA_ROPE = 32
MLA_QK = MLA_NOPE + MLA_ROPE
MLA_V = 64
NA_HEADS = 8
NA_ROWS = 8
NA_COLS = 16
NEG = -1e30
LOG2E = math.log2(math.e)
MAX_FREE_LOGIT_BOUND = 50.0
BF16_ROUNDING_MARGIN = 1.02

TM = 512
TQ = 2048
TQ_SUB = 256
TQ_FULL = 1024
TK_FULL = 512
TK_MAX_FREE = 2048
MOD_TN = 1536
VMEM_LIMIT = 56 * 1024 * 1024


def _params(*sem):
    return pltpu.CompilerParams(dimension_semantics=sem, vmem_limit_bytes=VMEM_LIMIT)


def _mod_kernel(c_ref, w_ref, b_ref, o_ref):
    c = c_ref[...]
    a = (c * jax.nn.sigmoid(c)).astype(BF16)
    w = w_ref[0].astype(BF16)
    o_ref[0] = jnp.dot(a, w, preferred_element_type=F32) + b_ref[0]


def _modulation(cvec, ada_w, ada_b):
    depth, d, n = ada_w.shape
    rows = cvec.shape[0]
    return pl.pallas_call(
        _mod_kernel,
        grid=(depth, n // MOD_TN),
        in_specs=[
            pl.BlockSpec((rows, d), lambda l, j: (0, 0)),
            pl.BlockSpec((1, d, MOD_TN), lambda l, j: (l, 0, j)),
            pl.BlockSpec((1, 1, MOD_TN), lambda l, j: (l, 0, j)),
        ],
        out_specs=pl.BlockSpec((1, rows, MOD_TN), lambda l, j: (l, 0, j)),
        out_shape=jax.ShapeDtypeStruct((depth, rows, n), F32),
        compiler_params=_params("parallel", "parallel"),
        name="modulation",
    )(cvec, ada_w, ada_b.reshape(depth, 1, n))


def _modulated(x, g_ref, shift, scale):
    ms = jnp.mean(x * x, axis=-1, keepdims=True)
    y = x * lax.rsqrt(ms + EPS) * g_ref[...]
    return y * (1.0 + scale) + shift


def _group_meansq(chunks, g_ref, inv_n):
    rows = chunks[0].shape[0]
    sq = jnp.concatenate([(c * c).astype(BF16) for c in chunks], axis=0)
    ss = jnp.dot(sq, g_ref[...], preferred_element_type=F32) * inv_n
    return [ss[i * rows:(i + 1) * rows] for i in range(len(chunks))]


def _swap_quarters(x, quarter):
    lane = lax.broadcasted_iota(jnp.int32, x.shape, 1)
    odd = (lane & quarter) != 0
    from_lower = pltpu.roll(x, quarter, 1)
    from_upper = pltpu.roll(x, LANES - quarter, 1)
    return jnp.where(odd, from_lower, from_upper)


def _rope(x, cos, sin_signed, quarter):
    return x * cos + _swap_quarters(x, quarter) * sin_signed


def _norm_rope_chunk(acc, meansq, gain, cos, sin, quarter):
    y = acc * lax.rsqrt(meansq + EPS) * gain
    if quarter is None:
        return y
    return jnp.concatenate([_rope(y[:, :LANES], cos, sin, quarter),
                            _rope(y[:, LANES:], cos, sin, quarter)], axis=1)


def _stream_specs(src, cols, n_lat_blocks):
    if len(src) == 1:
        return [pl.BlockSpec((TM, cols), lambda i: (i, 0))]
    return [pl.BlockSpec((TM, cols), lambda i: (jnp.minimum(i, n_lat_blocks - 1), 0)),
            pl.BlockSpec((TM, cols), lambda i: (0, 0))]


def _stream_block(refs, n_lat_blocks):
    if len(refs) == 1:
        return refs[0][...]
    return jnp.where(pl.program_id(0) < n_lat_blocks, refs[0][...], refs[1][...])


SHIFT1, SCALE1, GATE1, SHIFT2, SCALE2, GATE2 = range(6)


def _mod_specs(mod, layer, chunks):
    d = mod.shape[2] // 6
    return [pl.BlockSpec((None, mod.shape[1], d), lambda i, k=k: (layer, 0, k)) for k in chunks]


def _mod_row(ref, bpb, nbatch):
    row = jnp.minimum(pl.program_id(0) // bpb, nbatch)
    return ref[pl.ds(row, 1), :]


def _rope_spec(bpb, nbatch):
    return pl.BlockSpec((TM, LANES), lambda i: (jnp.where(i < bpb * nbatch, i % bpb, bpb), 0))


def _whole(shape):
    return pl.BlockSpec(shape, lambda *_: (0,) * len(shape))


EVEN_QA = 0
EVEN_KA = EVEN_QA + DIFF_HEADS * 2 * HEAD_DIM
EVEN_VA = EVEN_KA + DIFF_HEADS * 2 * HEAD_DIM
EVEN_QB = EVEN_VA + DIFF_HEADS * 2 * HEAD_DIM
EVEN_KB = EVEN_QB + SWA_Q_HEADS * HEAD_DIM
EVEN_VB = EVEN_KB + 2 * SWA_KV_HEADS * HEAD_DIM
EVEN_COLS = EVEN_VB + 2 * SWA_KV_HEADS * HEAD_DIM
EVEN_NORM_CHUNKS = tuple(c for c in range(EVEN_COLS // CHUNK)
                         if c * CHUNK < EVEN_VA or EVEN_QB <= c * CHUNK < EVEN_VB)
EVEN_DOT_RANGES = ((EVEN_QA // CHUNK, EVEN_VA // CHUNK), (EVEN_QB // CHUNK, EVEN_VB // CHUNK),
                   (EVEN_VA // CHUNK, EVEN_QB // CHUNK), (EVEN_VB // CHUNK, EVEN_COLS // CHUNK))


def _proj_even_kernel(*refs, n_x, bpb, nbatch):
    n_lat_blocks = bpb * nbatch
    x_refs = refs[:n_x]
    g_ref, sh_ref, sc_ref, w_ref, gain_ref, g64_ref, cos_ref, sin_ref, o_ref = refs[n_x:]
    h = _modulated(_stream_block(x_refs, n_lat_blocks), g_ref, _mod_row(sh_ref, bpb, nbatch),
                   _mod_row(sc_ref, bpb, nbatch)).astype(BF16)
    cos = cos_ref[...]
    sin = sin_ref[...]
    for first, last in EVEN_DOT_RANGES:
        full = jnp.dot(h, w_ref[:, first * CHUNK:last * CHUNK], preferred_element_type=F32)
        chunk = lambda c: full[:, (c - first) * CHUNK:(c - first + 1) * CHUNK]
        normed = [c for c in range(first, last) if c in EVEN_NORM_CHUNKS]
        meansq = {}
        if normed:
            meansq = dict(zip(normed, _group_meansq([chunk(c) for c in normed], g64_ref, 1.0 / HEAD_DIM)))
        for c in range(first, last):
            cols = slice(c * CHUNK, (c + 1) * CHUNK)
            acc = chunk(c)
            if c in EVEN_NORM_CHUNKS:
                acc = _norm_rope_chunk(acc, meansq[c], gain_ref[:, cols], cos, sin, HEAD_DIM // 4)
            o_ref[:, cols] = acc.astype(BF16)


def _proj_even(x_src, g, mod, layer, w, gain, g64, cos, sin, bpb, nbatch):
    d = x_src[0].shape[1]
    n_lat_blocks = bpb * nbatch
    kern = functools.partial(_proj_even_kernel, n_x=len(x_src), bpb=bpb, nbatch=nbatch)
    return pl.pallas_call(
        kern,
        grid=(n_lat_blocks + 1,),
        in_specs=(_stream_specs(x_src, d, n_lat_blocks) + [_whole(g.shape)]
                  + _mod_specs(mod, layer, (SHIFT1, SCALE1))
                  + [_whole(w.shape), _whole(gain.shape), _whole(g64.shape),
                     _rope_spec(bpb, nbatch), _rope_spec(bpb, nbatch)]),
        out_specs=pl.BlockSpec((TM, EVEN_COLS), lambda i: (i, 0)),
        out_shape=jax.ShapeDtypeStruct(((n_lat_blocks + 1) * TM, EVEN_COLS), BF16),
        compiler_params=_params("parallel"),
        name="proj_even",
    )(*x_src, g, mod, mod, w, gain, g64, cos, sin)


NA_WIDTH = NA_HEADS * HEAD_DIM
MLA_PAD = LANES
MLA_WIDTH = MLA_HEADS * MLA_PAD
ODD_W_KVA = MLA_Q_LORA
ODD_W_NQ = ODD_W_KVA + MLA_KV_LORA
ODD_W_NV = ODD_W_NQ + 2 * NA_WIDTH
ODD_W_SLOT = ODD_W_NV + NA_WIDTH
ODD_W_COLS = ODD_W_SLOT + LANES
ODD_K = MLA_WIDTH
ODD_V = ODD_K + MLA_WIDTH
ODD_NQ = ODD_V + MLA_HEADS * MLA_V
ODD_NK = ODD_NQ + NA_WIDTH
ODD_NV = ODD_NK + NA_WIDTH
ODD_COLS = ODD_NV + NA_WIDTH


def _proj_odd_kernel(x_ref, g_ref, sh_ref, sc_ref, w_ref, wq_ref, wkv_ref, qa_g_ref, kva_g_ref,
                     mq_g_ref, mk_g_ref, nq_g_ref, nk_g_ref, g64_ref, g128_ref, cos_ref, sin_ref, o_ref,
                     *, bpb, nbatch):
    h = _modulated(x_ref[...], g_ref, _mod_row(sh_ref, bpb, nbatch), _mod_row(sc_ref, bpb, nbatch)).astype(BF16)
    cos = cos_ref[...]
    sin = sin_ref[...]
    quarter = MLA_ROPE // 4

    lora = jnp.dot(h, w_ref[:, :ODD_W_NQ], preferred_element_type=F32)
    rest = jnp.dot(h, w_ref[:, ODD_W_NQ:], preferred_element_type=F32)

    def low_rank(a, gain_ref):
        ms = jnp.mean(a * a, axis=-1, keepdims=True)
        return (a * lax.rsqrt(ms + EPS) * gain_ref[...]).astype(BF16)

    n_mla = MLA_WIDTH // CHUNK
    chunks_of = lambda a, first, n: [a[:, first + c * CHUNK:first + (c + 1) * CHUNK] for c in range(n)]
    qa = low_rank(lora[:, :MLA_Q_LORA], qa_g_ref)
    qf = jnp.dot(qa, wq_ref[...], preferred_element_type=F32)
    q_chunks = chunks_of(qf, 0, n_mla)
    ms_q = _group_meansq(q_chunks, g128_ref, 1.0 / MLA_QK)
    kva = low_rank(lora[:, ODD_W_KVA:], kva_g_ref)
    kvf = jnp.dot(kva, wkv_ref[...], preferred_element_type=F32)
    kslot = rest[:, ODD_W_SLOT - ODD_W_NQ:]
    kslot2 = jnp.concatenate([kslot, kslot], axis=1)
    k_rot = _swap_quarters(kslot * mk_g_ref[:, :LANES], quarter)
    k_rot2 = jnp.concatenate([k_rot, k_rot], axis=1)
    cos2 = jnp.concatenate([cos, cos], axis=1)
    sin2 = jnp.concatenate([sin, sin], axis=1)
    k_chunks = [kc + kslot2 for kc in chunks_of(kvf, 0, n_mla)]
    ms_k = _group_meansq(k_chunks, g128_ref, 1.0 / MLA_QK)
    for c in range(n_mla):
        y = _norm_rope_chunk(q_chunks[c], ms_q[c], mq_g_ref[...], cos, sin, quarter)
        o_ref[:, c * CHUNK:(c + 1) * CHUNK] = y.astype(BF16)
    for c in range(n_mla):
        inv_rms = lax.rsqrt(ms_k[c] + EPS)
        y = (k_chunks[c] * inv_rms * mk_g_ref[...]) * cos2 + (inv_rms * k_rot2) * sin2
        o_ref[:, ODD_K + c * CHUNK:ODD_K + (c + 1) * CHUNK] = y.astype(BF16)
    o_ref[:, ODD_V:ODD_NQ] = kvf[:, MLA_WIDTH:].astype(BF16)

    n_na = NA_WIDTH // CHUNK
    n_chunks = chunks_of(rest, 0, 2 * n_na)
    ms64 = _group_meansq(n_chunks, g64_ref, 1.0 / HEAD_DIM)
    for c in range(2 * n_na):
        gain = nq_g_ref[...] if c < n_na else nk_g_ref[...]
        y = _norm_rope_chunk(n_chunks[c], ms64[c], gain, None, None, None)
        o_ref[:, ODD_NQ + c * CHUNK:ODD_NQ + (c + 1) * CHUNK] = y.astype(BF16)
    o_ref[:, ODD_NV:ODD_COLS] = rest[:, ODD_W_NV - ODD_W_NQ:ODD_W_SLOT - ODD_W_NQ].astype(BF16)


def _proj_odd(xa, g, mod, layer, w, wq, wkv, gains, g64, g128, cos, sin, bpb, nbatch):
    na, d = xa.shape
    tok_specs = _stream_specs((xa,), d, bpb * nbatch) + [_whole(g.shape)] + _mod_specs(mod, layer, (SHIFT1, SCALE1))
    consts = (w, wq, wkv) + tuple(gains) + (g64, g128)
    return pl.pallas_call(
        functools.partial(_proj_odd_kernel, bpb=bpb, nbatch=nbatch),
        grid=(na // TM,),
        in_specs=tok_specs + [_whole(a.shape) for a in consts] + [_rope_spec(bpb, nbatch)] * 2,
        out_specs=pl.BlockSpec((TM, ODD_COLS), lambda i: (i, 0)),
        out_shape=jax.ShapeDtypeStruct((na, ODD_COLS), BF16),
        compiler_params=_params("parallel"),
        name="proj_odd",
    )(xa, g, mod, mod, *consts, cos, sin)


def _attn_kernel(*refs, mode, split, main, seq, has_sink, lam_init, tq, tq_sub, ctx_out):
    n_sub = tq // tq_sub
    it = iter(refs)
    q_ref = next(it)
    qc_ref = next(it) if ctx_out else None
    sink_ref = lam_ref = subln_ref = None
    bias_refs = ()
    k_ref, v_ref, kc_ref, vc_ref = next(it), next(it), next(it), next(it)
    if main == "na":
        bias_refs = tuple(next(it) for _ in range(n_sub))
    if has_sink:
        sink_ref = next(it)
    safe_ref = next(it)
    if mode == "diff":
        lam_ref, subln_ref = next(it), next(it)
    o_ref = next(it)
    oc_ref = next(it) if ctx_out else None
    vt_ref, vct_ref = (next(it), next(it)) if main == "full" else (None, None)

    pair = pl.program_id(1)
    qi = pl.program_id(2)
    low = lax.broadcasted_iota(jnp.int32, (1, LANES), 1) < HALF
    if split == "mask":
        sub_heads = lambda q: (jnp.where(low, q, jnp.zeros_like(q)), jnp.where(low, jnp.zeros_like(q), q))
        k_of = lambda k, s: k
    else:
        sub_heads = lambda q: (q[:, :LANES], q[:, LANES:])
        k_of = lambda k, s: k[:, s * LANES:(s + 1) * LANES]
    qs = sub_heads(q_ref[...])
    q_of = lambda s, t: qs[s][t * tq_sub:(t + 1) * tq_sub]
    qcs = sub_heads(qc_ref[...]) if ctx_out else None
    sinks = [sink_ref[2 * pair + s] * LOG2E for s in range(2)] if has_sink else None
    nt_dims = (((1,), (1,)), ((), ()))

    def finish(o_lo, o_hi, out_ref):
        if mode == "diff":
            lv = lam_ref[...]
            lam = (jnp.exp(jnp.sum(lv[0:1] * lv[1:2], axis=-1, keepdims=True))
                   - jnp.exp(jnp.sum(lv[2:3] * lv[3:4], axis=-1, keepdims=True)) + lam_init)
            o = o_lo - lam * o_hi
            ms = jnp.mean(o * o, axis=-1, keepdims=True)
            o = o * lax.rsqrt(ms + EPS) * subln_ref[...] * (1.0 - lam_init)
        else:
            o = jnp.where(low, o_lo, o_hi)
        out_ref[...] = o.astype(BF16)

    window_masks = {}

    def local_keys(t):
        if main == "window":
            span = tq_sub + 2 * WINDOW
            q0 = qi * tq + t * tq_sub
            at_edge = t in (0, n_sub - 1)
            w0 = jnp.clip(q0 - WINDOW, 0, seq - span) if at_edge else q0 - WINDOW
            off = pl.multiple_of(w0, WINDOW)
            if at_edge or "inner" not in window_masks:
                key = lax.broadcasted_iota(jnp.int32, (span, tq_sub), 0)
                qry = lax.broadcasted_iota(jnp.int32, (span, tq_sub), 1)
                inside = jnp.abs(key - qry + ((w0 - q0) if at_edge else -WINDOW)) <= WINDOW
                if not at_edge:
                    window_masks["inner"] = inside
            else:
                inside = window_masks["inner"]
            logit_fn = lambda lg_t, s: jnp.where(inside, lg_t, NEG)
        else:
            rows_q = tq_sub // GRID_W
            span_rows = rows_q + NA_ROWS
            w0 = jnp.clip(qi * (tq // GRID_W) + t * rows_q - NA_ROWS // 2, 0, seq // GRID_W - span_rows)
            off = pl.multiple_of(w0 * GRID_W, GRID_W)
            span = span_rows * GRID_W
            logit_fn = lambda lg_t, s: lg_t + bias_refs[t][0, s]
        return k_ref[pl.ds(off, span), :], v_ref[pl.ds(off, span), :], logit_fn

    def online_block(q_pair, t):
        rows = q_pair[0].shape[0]

        def init(s):
            if has_sink:
                m0 = jnp.full((rows, 1), sinks[s], F32)
                l0 = jnp.ones((rows, 1), F32)
            else:
                m0 = jnp.full((rows, 1), NEG, F32)
                l0 = jnp.zeros((rows, 1), F32)
            return m0, l0, jnp.zeros((rows, LANES), F32)

        def segment(state, k, v, logit_fn):
            out = []
            for s in range(2):
                m, l, acc = state[s]
                logits = lax.dot_general(q_pair[s], k_of(k, s), nt_dims, preferred_element_type=F32)
                if logit_fn is not None:
                    logits = logit_fn(logits.T, s).T
                m_new = jnp.maximum(m, jnp.max(logits, axis=-1, keepdims=True))
                alpha = jnp.exp2(m - m_new)
                p = jnp.exp2(logits - m_new)
                l = alpha * l + jnp.sum(p, axis=-1, keepdims=True)
                acc = alpha * acc + jnp.dot(p.astype(BF16), v, preferred_element_type=F32)
                out.append((m_new, l, acc))
            return tuple(out)

        state = (init(0), init(1))
        if t is None:
            pass
        elif main == "full":
            def body(c, st):
                off = pl.multiple_of(c * TK_FULL, TK_FULL)
                return segment(st, k_ref[pl.ds(off, TK_FULL), :], v_ref[pl.ds(off, TK_FULL), :], None)
            state = lax.fori_loop(0, seq // TK_FULL, body, state)
        else:
            state = segment(state, *local_keys(t))
        (_, l_lo, acc_lo), (_, l_hi, acc_hi) = segment(state, kc_ref[...], vc_ref[...], None)
        return acc_lo / l_lo, acc_hi / l_hi

    def probs_t(k, q_rows, s, logit_fn):
        logits_t = lax.dot_general(k_of(k, s), q_rows, nt_dims, preferred_element_type=F32)
        if logit_fn is not None:
            logits_t = logit_fn(logits_t, s)
        p = jnp.exp2(logits_t)
        return p.astype(BF16), jnp.sum(p, axis=0, keepdims=True)

    def max_free():
        window_masks.clear()
        if main == "full":
            @pl.when(qi == 0)
            def _():
                for c in range(seq // TK_MAX_FREE):
                    vt_ref[c] = v_ref[c * TK_MAX_FREE:(c + 1) * TK_MAX_FREE, :].T
                vct_ref[...] = vc_ref[...].T

            acc_lat = [jnp.zeros((LANES, tq), F32)] * 2
            den_lat = [jnp.zeros((1, tq), F32)] * 2
            for c in range(seq // TK_MAX_FREE):
                k = k_ref[c * TK_MAX_FREE:(c + 1) * TK_MAX_FREE, :]
                for s in range(2):
                    p, p_sum = probs_t(k, qs[s], s, None)
                    acc_lat[s] = acc_lat[s] + jnp.dot(vt_ref[c], p, preferred_element_type=F32)
                    den_lat[s] = den_lat[s] + p_sum
            vct = vct_ref[...]
        else:
            vct = vc_ref[...].T
        p_ctx = [probs_t(kc_ref[...], qs[s], s, None) for s in range(2)]
        if main in ("window", "na"):
            windows = [local_keys(t) for t in range(n_sub)]
            p_loc = [[probs_t(k, q_of(s, t), s, fn) for s in range(2)] for t, (k, _, fn) in enumerate(windows)]
            vts = [v.T for _, v, _ in windows]
        outs = []
        for s in range(2):
            acc = jnp.dot(vct, p_ctx[s][0], preferred_element_type=F32)
            den = p_ctx[s][1]
            if main == "full":
                acc, den = acc + acc_lat[s], den + den_lat[s]
            else:
                acc = acc + jnp.concatenate(
                    [jnp.dot(vts[t], p_loc[t][s][0], preferred_element_type=F32) for t in range(n_sub)], axis=1)
                den = den + jnp.concatenate([p_loc[t][s][1] for t in range(n_sub)], axis=1)
            if has_sink:
                den = den + jnp.exp2(jnp.full((1, tq), sinks[s], F32))
            outs.append((acc / den).T)
        finish(*outs, o_ref)

        if ctx_out:
            @pl.when(qi == 0)
            def _():
                outs = []
                for s in range(2):
                    p, den = probs_t(kc_ref[...], qcs[s], s, None)
                    if has_sink:
                        den = den + jnp.exp2(jnp.full(den.shape, sinks[s], F32))
                    outs.append((jnp.dot(vct, p, preferred_element_type=F32) / den).T)
                finish(*outs, oc_ref)

    def online():
        window_masks.clear()
        parts = [online_block((q_of(0, t), q_of(1, t)), t) for t in range(n_sub)]
        finish(*[jnp.concatenate([p[s] for p in parts], axis=0) if n_sub > 1 else parts[0][s]
                 for s in range(2)], o_ref)
        if ctx_out:
            pl.when(qi == 0)(lambda: finish(*online_block(qcs, None), oc_ref))

    safe = safe_ref[0] == 1
    pl.when(safe)(max_free)
    pl.when(jnp.logical_not(safe))(online)


def _attention(p, *, name, mode, split, main, n_pairs, q_blk, k_blk, v_blk, out_cols, ctx_out,
               nbatch, seq, ctx_len, safe, bias=None, sink=None, lam=None, subln=None, lam_init=0.0):
    n_lat = nbatch * seq
    qw = LANES if split == "mask" else 2 * LANES
    tq = TQ_FULL if main == "full" else TQ
    nq = seq // tq
    q_row = lambda b, j, i: b * nq + i
    ctx_row = lambda b, j, i: n_lat // ctx_len + b

    in_specs = [pl.BlockSpec((tq, qw), lambda b, j, i: (q_row(b, j, i), q_blk(j)))]
    args = [p]
    if ctx_out:
        in_specs.append(pl.BlockSpec((ctx_len, qw), lambda b, j, i: (ctx_row(b, j, i), q_blk(j))))
        args.append(p)
    in_specs += [pl.BlockSpec((seq, qw), lambda b, j, i: (b, k_blk(j))),
                 pl.BlockSpec((seq, LANES), lambda b, j, i: (b, v_blk(j))),
                 pl.BlockSpec((ctx_len, qw), lambda b, j, i: (ctx_row(b, j, i), k_blk(j))),
                 pl.BlockSpec((ctx_len, LANES), lambda b, j, i: (ctx_row(b, j, i), v_blk(j)))]
    args += [p, p, p, p]
    tq_sub = TQ_SUB if main in ("window", "na") else tq
    n_sub = tq // tq_sub
    if main == "na":
        span = (tq_sub // GRID_W + NA_ROWS) * GRID_W
        last = nq * n_sub - 1
        cls = lambda g: jnp.where(g == 0, 0, jnp.where(g == last, 2, 1))
        for t in range(n_sub):
            in_specs.append(pl.BlockSpec((1, 2, span, tq_sub),
                                         lambda b, j, i, t=t: (cls(i * n_sub + t), j, 0, 0)))
            args.append(bias)
    if sink is not None:
        in_specs.append(pl.BlockSpec(memory_space=pltpu.SMEM))
        args.append(sink)
    in_specs.append(pl.BlockSpec(memory_space=pltpu.SMEM))
    args.append(safe)
    if mode == "diff":
        in_specs += [_whole(lam.shape), _whole(subln.shape)]
        args += [lam, subln]

    kern = functools.partial(_attn_kernel, mode=mode, split=split, main=main, seq=seq, has_sink=sink is not None,
                             lam_init=lam_init, tq=tq, tq_sub=tq_sub, ctx_out=ctx_out)
    out_specs = [pl.BlockSpec((tq, LANES), lambda b, j, i: (q_row(b, j, i), j))]
    out_shape = [jax.ShapeDtypeStruct((n_lat, out_cols), BF16)]
    if ctx_out:
        out_specs.append(pl.BlockSpec((ctx_len, LANES), lambda b, j, i: (b, j)))
        out_shape.append(jax.ShapeDtypeStruct((nbatch * ctx_len, out_cols), BF16))
    scratch = []
    if main == "full":
        scratch = [pltpu.VMEM((seq // TK_MAX_FREE, LANES, TK_MAX_FREE), BF16), pltpu.VMEM((LANES, ctx_len), BF16)]
    return pl.pallas_call(
        kern,
        grid=(nbatch, n_pairs, nq),
        in_specs=in_specs,
        out_specs=out_specs,
        out_shape=out_shape,
        scratch_shapes=scratch,
        compiler_params=_params("parallel", "parallel", "arbitrary"),
        name=name,
    )(*args)


def _na_bias_kernel(rpb_ref, o_ref, *, n_rows, rows_q):
    h = pl.program_id(0)
    n_dr = 2 * NA_ROWS - 1
    n_dc = 2 * NA_COLS - 1
    kc = lax.broadcasted_iota(jnp.int32, (GRID_W, LANES), 0)
    cq = lax.broadcasted_iota(jnp.int32, (GRID_W, LANES), 1) & (GRID_W - 1)
    dc_idx = jnp.clip(kc - cq, -(NA_COLS - 1), NA_COLS - 1) + NA_COLS - 1
    cs = jnp.clip(cq - NA_COLS // 2, 0, GRID_W - NA_COLS)
    col_ok = (kc >= cs) & (kc < cs + NA_COLS)
    tiles = [jnp.zeros((GRID_W, LANES), F32) for _ in range(n_dr)]
    for dc in range(n_dc):
        hit = dc_idx == dc
        for dr in range(n_dr):
            tiles[dr] = jnp.where(hit, rpb_ref[(h * n_dr + dr) * n_dc + dc] * LOG2E, tiles[dr])
    tiles = [jnp.where(col_ok, t, NEG) for t in tiles]
    masked = jnp.full((GRID_W, LANES), NEG, F32)
    low = lax.broadcasted_iota(jnp.int32, (GRID_W, LANES), 1) < HALF

    span_rows = rows_q + NA_ROWS
    first_q_row = (0, span_rows, n_rows - rows_q)
    for c in range(3):
        r0 = first_q_row[c]
        w0 = min(max(r0 - NA_ROWS // 2, 0), n_rows - span_rows)
        for kr in range(span_rows):
            k_abs = w0 + kr
            for m in range(rows_q // 2):
                halves = []
                for r in (r0 + 2 * m, r0 + 2 * m + 1):
                    rs = min(max(r - NA_ROWS // 2, 0), n_rows - NA_ROWS)
                    halves.append(tiles[k_abs - r + NA_ROWS - 1] if rs <= k_abs < rs + NA_ROWS else masked)
                o_ref[c, 0, kr * GRID_W:(kr + 1) * GRID_W, m * LANES:(m + 1) * LANES] = (
                    jnp.where(low, halves[0], halves[1]))


def _na_bias(rpb, seq):
    n_rows = seq // GRID_W
    rows_q = TQ_SUB // GRID_W
    span = (rows_q + NA_ROWS) * GRID_W
    kern = functools.partial(_na_bias_kernel, n_rows=n_rows, rows_q=rows_q)
    return pl.pallas_call(
        kern,
        grid=(NA_HEADS,),
        in_specs=[pl.BlockSpec(memory_space=pltpu.SMEM)],
        out_specs=pl.BlockSpec((3, 1, span, TQ_SUB), lambda h: (0, h, 0, 0)),
        out_shape=jax.ShapeDtypeStruct((3, NA_HEADS, span, TQ_SUB), F32),
        compiler_params=_params("parallel"),
        name="na_bias",
    )(rpb.reshape(-1))


def _post_kernel(*refs, n_x, n_o, bpb, nbatch):
    n_lat_blocks = bpb * nbatch
    x_refs, oa_refs, ob_refs = refs[:n_x], refs[n_x:n_x + n_o], refs[n_x + n_o:n_x + 2 * n_o]
    wo_ref, g1_ref, n2_ref, sh_ref, sc_ref, g2_ref, w1_ref, w2_ref, o_ref = refs[n_x + 2 * n_o:]
    oa = _stream_block(oa_refs, n_lat_blocks)
    ob = _stream_block(ob_refs, n_lat_blocks)
    half = oa.shape[1]
    y = (jnp.dot(oa, wo_ref[:half, :], preferred_element_type=F32)
         + jnp.dot(ob, wo_ref[half:, :], preferred_element_type=F32))
    x1 = _stream_block(x_refs, n_lat_blocks) + _mod_row(g1_ref, bpb, nbatch) * y
    h = _modulated(x1, n2_ref, _mod_row(sh_ref, bpb, nbatch), _mod_row(sc_ref, bpb, nbatch)).astype(BF16)
    a = jnp.dot(h, w1_ref[...], preferred_element_type=F32)
    a = jnp.square(jnp.maximum(a, 0.0)).astype(BF16)
    o_ref[...] = x1 + _mod_row(g2_ref, bpb, nbatch) * jnp.dot(a, w2_ref[...], preferred_element_type=F32)


def _post(x_src, oa_src, ob_src, wo, mod, n2, w1, w2, layer, n_blocks, bpb, nbatch):
    d = x_src[0].shape[1]
    half = oa_src[0].shape[1]
    n_lat_blocks = bpb * nbatch
    g1_spec, sh_spec, sc_spec, g2_spec = _mod_specs(mod, layer, (GATE1, SHIFT2, SCALE2, GATE2))
    resident = lambda a: pl.BlockSpec((None,) + a.shape[1:], lambda i: (layer, 0, 0),
                                      pipeline_mode=pl.Buffered(1))
    kern = functools.partial(_post_kernel, n_x=len(x_src), n_o=len(oa_src), bpb=bpb, nbatch=nbatch)
    return pl.pallas_call(
        kern,
        grid=(n_blocks,),
        in_specs=(_stream_specs(x_src, d, n_lat_blocks) + _stream_specs(oa_src, half, n_lat_blocks)
                  + _stream_specs(ob_src, half, n_lat_blocks)
                  + [resident(wo), g1_spec, _whole(n2.shape), sh_spec, sc_spec, g2_spec, resident(w1), resident(w2)]),
        out_specs=pl.BlockSpec((TM, d), lambda i: (i, 0)),
        out_shape=jax.ShapeDtypeStruct((n_blocks * TM, d), F32),
        compiler_params=_params("parallel"),
        name="post",
    )(*x_src, *oa_src, *ob_src, wo, mod, n2, mod, mod, mod, w1, w2)


def _block_diag_ones(group):
    idx = np.arange(CHUNK) // group
    return jnp.asarray(idx[:, None] == idx[None, :], dtype=BF16)


def _rope_tables(seq, rot_dim, lane0, pad_rows):
    t = jnp.arange(seq, dtype=jnp.int32)
    row = (t // GRID_W).astype(F32)
    col = (t % GRID_W).astype(F32)
    n_freq = rot_dim // 4
    freqs = jnp.power(ROPE_BASE, -jnp.arange(n_freq, dtype=F32) / n_freq)
    ar = row[:, None] * freqs[None, :]
    ac = col[:, None] * freqs[None, :]
    ang = jnp.concatenate([ar, ar, ac, ac], axis=-1)
    sign = jnp.asarray(np.tile(np.repeat([-1.0, 1.0], n_freq), 2), F32)
    cos, sin = jnp.cos(ang), jnp.sin(ang) * sign[None, :]
    if lane0 is None:
        reps = LANES // rot_dim
        cos, sin = jnp.tile(cos, (1, reps)), jnp.tile(sin, (1, reps))
    else:
        pad = ((0, 0), (lane0, LANES - lane0 - rot_dim))
        cos = jnp.pad(cos, pad, constant_values=1.0)
        sin = jnp.pad(sin, pad)
    cos = jnp.concatenate([cos, jnp.ones((pad_rows, LANES), F32)], axis=0)
    sin = jnp.concatenate([sin, jnp.zeros((pad_rows, LANES), F32)], axis=0)
    return cos, sin


def _logits_bounded(q_gain, k_gain, dim, bias=None, sink=None):
    bound = jnp.max(jnp.abs(q_gain)) * jnp.max(jnp.abs(k_gain)) * (dim ** 0.5) * BF16_ROUNDING_MARGIN
    if bias is not None:
        bound = bound + jnp.max(jnp.abs(bias))
    if sink is not None:
        bound = jnp.maximum(bound, jnp.max(jnp.abs(sink)))
    return (bound <= MAX_FREE_LOGIT_BOUND).astype(jnp.int32).reshape(1)


def _tile_gain(g, reps, scale=1.0):
    return (jnp.tile(g.astype(F32), reps) * scale)[None, :]


def _even_weights(w, q_g, k_g, sq_g, sk_g):
    nq = DIFF_HEADS * 2 * HEAD_DIM
    qa, ka, va = w[:, :nq], w[:, nq:2 * nq], w[:, 2 * nq:3 * nq]
    o = 3 * nq
    qb = w[:, o:o + SWA_Q_HEADS * HEAD_DIM]
    o += SWA_Q_HEADS * HEAD_DIM
    kb = [w[:, o + i * HEAD_DIM:o + (i + 1) * HEAD_DIM] for i in range(SWA_KV_HEADS)]
    o += SWA_KV_HEADS * HEAD_DIM
    vb = [w[:, o + i * HEAD_DIM:o + (i + 1) * HEAD_DIM] for i in range(SWA_KV_HEADS)]
    dup = lambda parts: [p for p in parts for _ in range(2)]
    w_new = jnp.concatenate([qa, ka, va, qb] + dup(kb) + dup(vb), axis=1).astype(BF16)
    scale = HEAD_DIM ** -0.5 * LOG2E
    ones = lambda n: jnp.ones((1, n), F32)
    gain = jnp.concatenate([
        _tile_gain(q_g, 2 * DIFF_HEADS, scale), _tile_gain(k_g, 2 * DIFF_HEADS), ones(nq),
        _tile_gain(sq_g, SWA_Q_HEADS, scale), _tile_gain(sk_g, 2 * SWA_KV_HEADS),
        ones(2 * SWA_KV_HEADS * HEAD_DIM)], axis=1)
    assert w_new.shape[1] == gain.shape[1] == EVEN_COLS
    return w_new, gain


def _odd_weights(w, wq_up, wkv_up, qa_g, kva_g, mq_g, mk_g, nq_g, nk_g):
    d = w.shape[0]
    o = 0
    parts = []
    for n in (MLA_Q_LORA, MLA_KV_LORA, MLA_ROPE, NA_HEADS * HEAD_DIM, NA_HEADS * HEAD_DIM, NA_HEADS * HEAD_DIM):
        parts.append(w[:, o:o + n])
        o += n
    q_a, kv_a, k_r, nq, nk, nv = parts
    slot = jnp.concatenate([jnp.zeros((d, MLA_NOPE), w.dtype), k_r,
                            jnp.zeros((d, LANES - MLA_QK), w.dtype)], axis=1)
    w_new = jnp.concatenate([q_a, kv_a, nq, nk, nv, slot], axis=1).astype(BF16)
    assert w_new.shape[1] == ODD_W_COLS
    wq = jnp.pad(wq_up.reshape(MLA_Q_LORA, MLA_HEADS, MLA_QK), ((0, 0), (0, 0), (0, MLA_PAD - MLA_QK)))
    wq = wq.reshape(MLA_Q_LORA, MLA_HEADS * MLA_PAD).astype(BF16)
    wkv = wkv_up.reshape(MLA_KV_LORA, MLA_HEADS, MLA_NOPE + MLA_V)
    wk = jnp.pad(wkv[..., :MLA_NOPE], ((0, 0), (0, 0), (0, MLA_PAD - MLA_NOPE))).reshape(MLA_KV_LORA, -1)
    wv = wkv[..., MLA_NOPE:].reshape(MLA_KV_LORA, MLA_HEADS * MLA_V)
    wkv_new = jnp.concatenate([wk, wv], axis=1).astype(BF16)
    pad_gain = lambda g, scale: _tile_gain(jnp.pad(g.astype(F32), (0, MLA_PAD - MLA_QK)), CHUNK // MLA_PAD, scale)
    gains = (qa_g.astype(F32)[None, :], kva_g.astype(F32)[None, :],
             pad_gain(mq_g, MLA_QK ** -0.5 * LOG2E), pad_gain(mk_g, 1.0),
             _tile_gain(nq_g, CHUNK // HEAD_DIM, HEAD_DIM ** -0.5 * LOG2E), _tile_gain(nk_g, CHUNK // HEAD_DIM))
    return w_new, wq, wkv_new, gains


def kernel(x, c, ctx, c_ctx, ada_w, ada_b, norm1_g, norm2_g, w_out, mlp_w1, mlp_w2, ev_w_in, diff_q_g,
           diff_k_g, diff_lam, diff_subln_g, swa_q_g, swa_k_g, swa_sink, od_w_in, mla_qa_g, mla_kva_g,
           mla_wq_up, mla_wkv_up, mla_q_g, mla_k_g, na_q_g, na_k_g, na_rpb):
    nbatch, seq, d = x.shape
    ctx_len = ctx.shape[1]
    depth = ada_w.shape[0]
    n_lat = nbatch * seq
    n_ctx = nbatch * ctx_len
    assert seq % TM == 0 and n_ctx == TM and seq % TQ == 0 and seq % GRID_W == 0
    assert seq % TQ_FULL == 0 and seq % TK_MAX_FREE == 0 and seq % TK_FULL == 0
    assert seq // GRID_W >= 2 * (NA_ROWS + TQ_SUB // GRID_W) and seq >= TQ_SUB + 2 * WINDOW
    bpb = seq // TM

    mod_rows = 8
    cvec = jnp.concatenate([c, c_ctx[None, :], jnp.zeros((mod_rows - nbatch - 1, d), F32)], axis=0)
    mod = _modulation(cvec, ada_w, ada_b)

    cos64, sin64 = _rope_tables(seq, HEAD_DIM, None, n_ctx)
    cos32, sin32 = _rope_tables(seq, MLA_ROPE, MLA_NOPE, n_ctx)
    g64 = _block_diag_ones(HEAD_DIM)
    g128 = _block_diag_ones(LANES)

    x_src = (x.reshape(n_lat, d), ctx.reshape(n_ctx, d))
    wo_all, w1_all, w2_all = w_out.astype(BF16), mlp_w1.astype(BF16), mlp_w2.astype(BF16)
    common = dict(nbatch=nbatch, seq=seq, ctx_len=ctx_len)

    for layer in range(depth):
        need_ctx = layer < depth - 1
        n_blocks = bpb * nbatch + (1 if need_ctx else 0)
        n1 = norm1_g[layer][None, :]
        n2 = norm2_g[layer][None, :]
        i = layer // 2
        if layer % 2 == 0:
            lam_init = 0.8 - 0.6 * math.exp(-0.3 * layer)
            w_in, gain = _even_weights(ev_w_in[i], diff_q_g[i], diff_k_g[i], swa_q_g[i], swa_k_g[i])
            p = _proj_even(x_src, n1, mod, layer, w_in, gain, g64, cos64, sin64, bpb, nbatch)
            lam = diff_lam[i].astype(F32)
            subln = diff_subln_g[i].astype(F32)[None, :]
            sink = swa_sink[i].astype(F32)
            diff_kw = dict(mode="diff", split="mask", n_pairs=DIFF_HEADS, q_blk=lambda j: EVEN_QA // LANES + j,
                           k_blk=lambda j: EVEN_KA // LANES + j, v_blk=lambda j: EVEN_VA // LANES + j,
                           out_cols=DIFF_HEADS * LANES,
                           lam=lam, subln=subln, lam_init=lam_init,
                           safe=_logits_bounded(diff_q_g[i], diff_k_g[i], HEAD_DIM), **common)
            swa_kw = dict(mode="pair", split="mask", n_pairs=SWA_Q_HEADS // 2,
                          q_blk=lambda j: EVEN_QB // LANES + j, k_blk=lambda j: EVEN_KB // LANES + j // 2,
                          v_blk=lambda j: EVEN_VB // LANES + j // 2,
                          out_cols=SWA_Q_HEADS * HEAD_DIM, sink=sink,
                          safe=_logits_bounded(swa_q_g[i], swa_k_g[i], HEAD_DIM, sink=sink), **common)
            oa_src = _attention(p, name="diff_attn", main="full", ctx_out=need_ctx, **diff_kw)
            ob_src = _attention(p, name="window_attn", main="window", ctx_out=need_ctx, **swa_kw)
        else:
            w_in, wq, wkv, gains = _odd_weights(od_w_in[i], mla_wq_up[i], mla_wkv_up[i], mla_qa_g[i],
                                                mla_kva_g[i], mla_q_g[i], mla_k_g[i], na_q_g[i], na_k_g[i])
            p = _proj_odd(x_src[0], n1, mod, layer, w_in, wq, wkv, gains, g64, g128, cos32, sin32, bpb, nbatch)
            bias = _na_bias(na_rpb[i].astype(F32), seq)
            mla_kw = dict(mode="pair", split="slice", n_pairs=MLA_HEADS // 2, q_blk=lambda j: j,
                          k_blk=lambda j: ODD_K // CHUNK + j, v_blk=lambda j: ODD_V // LANES + j,
                          out_cols=MLA_HEADS * MLA_V,
                          safe=_logits_bounded(mla_q_g[i], mla_k_g[i], MLA_QK), **common)
            na_kw = dict(mode="pair", split="mask", n_pairs=NA_HEADS // 2,
                         q_blk=lambda j: ODD_NQ // LANES + j, k_blk=lambda j: ODD_NK // LANES + j,
                         v_blk=lambda j: ODD_NV // LANES + j, out_cols=NA_HEADS * HEAD_DIM,
                         safe=_logits_bounded(na_q_g[i], na_k_g[i], HEAD_DIM, bias=na_rpb[i]), **common)
            oa_src = _attention(p, name="mla_attn", main="full", ctx_out=need_ctx, **mla_kw)
            ob_src = _attention(p, name="na_attn", main="na", ctx_out=need_ctx, bias=bias, **na_kw)
        x_src = (_post(x_src, oa_src, ob_src, wo_all, mod, n2, w1_all, w2_all, layer, n_blocks, bpb, nbatch),)
    return x_src[0].reshape(nbatch, seq, d)
```

```python
import functools
import math

import numpy as np
import jax
import jax.numpy as jnp
from jax import lax
from jax.experimental import pallas as pl
from jax.experimental.pallas import tpu as pltpu

F32 = jnp.float32
BF16 = jnp.bfloat16

LANES = 128
HALF = LANES // 2
CHUNK = 2 * LANES
GRID_W = 64
HEAD_DIM = 64
EPS = 1e-6
ROPE_BASE = 10000.0
DIFF_HEADS = 4
SWA_Q_HEADS = 8
SWA_KV_HEADS = 2
WINDOW = 128
MLA_HEADS = 8
MLA_Q_LORA = 512
MLA_KV_LORA = 256
MLA_NOPE = 64
MLA_ROPE = 32
MLA_QK = MLA_NOPE + MLA_ROPE
MLA_V = 64
NA_HEADS = 8
NA_ROWS = 8
NA_COLS = 16
NEG = -1e30
LOG2E = math.log2(math.e)
MAX_FREE_LOGIT_BOUND = 50.0
BF16_ROUNDING_MARGIN = 1.02

TM = 512
TQ = 2048
TQ_SUB = 256
TQ_FULL = 1024
TK_FULL = 512
TK_MAX_FREE = 2048
MOD_TN = 1536
VMEM_LIMIT = 56 * 1024 * 1024


def _params(*sem):
    return pltpu.CompilerParams(dimension_semantics=sem, vmem_limit_bytes=VMEM_LIMIT)


def _mod_kernel(c_ref, w_ref, b_ref, o_ref):
    c = c_ref[...]
    a = (c * jax.nn.sigmoid(c)).astype(BF16)
    w = w_ref[0].astype(BF16)
    o_ref[0] = jnp.dot(a, w, preferred_element_type=F32) + b_ref[0]


def _modulation(cvec, ada_w, ada_b):
    depth, d, n = ada_w.shape
    rows = cvec.shape[0]
    return pl.pallas_call(
        _mod_kernel,
        grid=(depth, n // MOD_TN),
        in_specs=[
            pl.BlockSpec((rows, d), lambda l, j: (0, 0)),
            pl.BlockSpec((1, d, MOD_TN), lambda l, j: (l, 0, j)),
            pl.BlockSpec((1, 1, MOD_TN), lambda l, j: (l, 0, j)),
        ],
        out_specs=pl.BlockSpec((1, rows, MOD_TN), lambda l, j: (l, 0, j)),
        out_shape=jax.ShapeDtypeStruct((depth, rows, n), F32),
        compiler_params=_params("parallel", "parallel"),
        name="modulation",
    )(cvec, ada_w, ada_b.reshape(depth, 1, n))


def _modulated(x, g_ref, shift, scale):
    ms = jnp.mean(x * x, axis=-1, keepdims=True)
    y = x * lax.rsqrt(ms + EPS) * g_ref[...]
    return y * (1.0 + scale) + shift


def _group_meansq(chunks, g_ref, inv_n):
    rows = chunks[0].shape[0]
    sq = jnp.concatenate([(c * c).astype(BF16) for c in chunks], axis=0)
    ss = jnp.dot(sq, g_ref[...], preferred_element_type=F32) * inv_n
    return [ss[i * rows:(i + 1) * rows] for i in range(len(chunks))]


def _swap_quarters(x, quarter):
    lane = lax.broadcasted_iota(jnp.int32, x.shape, 1)
    odd = (lane & quarter) != 0
    from_lower = pltpu.roll(x, quarter, 1)
    from_upper = pltpu.roll(x, LANES - quarter, 1)
    return jnp.where(odd, from_lower, from_upper)


def _rope(x, cos, sin_signed, quarter):
    return x * cos + _swap_quarters(x, quarter) * sin_signed


def _norm_rope_chunk(acc, meansq, gain, cos, sin, quarter):
    y = acc * lax.rsqrt(meansq + EPS) * gain
    if quarter is None:
        return y
    return jnp.concatenate([_rope(y[:, :LANES], cos, sin, quarter),
                            _rope(y[:, LANES:], cos, sin, quarter)], axis=1)


def _stream_specs(src, cols, n_lat_blocks):
    if len(src) == 1:
        return [pl.BlockSpec((TM, cols), lambda i: (i, 0))]
    return [pl.BlockSpec((TM, cols), lambda i: (jnp.minimum(i, n_lat_blocks - 1), 0)),
            pl.BlockSpec((TM, cols), lambda i: (0, 0))]


def _stream_block(refs, n_lat_blocks):
    if len(refs) == 1:
        return refs[0][...]
    return jnp.where(pl.program_id(0) < n_lat_blocks, refs[0][...], refs[1][...])


SHIFT1, SCALE1, GATE1, SHIFT2, SCALE2, GATE2 = range(6)


def _mod_specs(mod, layer, chunks):
    d = mod.shape[2] // 6
    return [pl.BlockSpec((None, mod.shape[1], d), lambda i, k=k: (layer, 0, k)) for k in chunks]


def _mod_row(ref, bpb, nbatch):
    row = jnp.minimum(pl.program_id(0) // bpb, nbatch)
    return ref[pl.ds(row, 1), :]


def _rope_spec(bpb, nbatch):
    return pl.BlockSpec((TM, LANES), lambda i: (jnp.where(i < bpb * nbatch, i % bpb, bpb), 0))


def _whole(shape):
    return pl.BlockSpec(shape, lambda *_: (0,) * len(shape))


EVEN_QA = 0
EVEN_KA = EVEN_QA + DIFF_HEADS * 2 * HEAD_DIM
EVEN_VA = EVEN_KA + DIFF_HEADS * 2 * HEAD_DIM
EVEN_QB = EVEN_VA + DIFF_HEADS * 2 * HEAD_DIM
EVEN_KB = EVEN_QB + SWA_Q_HEADS * HEAD_DIM
EVEN_VB = EVEN_KB + 2 * SWA_KV_HEADS * HEAD_DIM
EVEN_COLS = EVEN_VB + 2 * SWA_KV_HEADS * HEAD_DIM
EVEN_NORM_CHUNKS = tuple(c for c in range(EVEN_COLS // CHUNK)
                         if c * CHUNK < EVEN_VA or EVEN_QB <= c * CHUNK < EVEN_VB)
EVEN_DOT_RANGES = ((EVEN_QA // CHUNK, EVEN_VA // CHUNK), (EVEN_QB // CHUNK, EVEN_VB // CHUNK),
                   (EVEN_VA // CHUNK, EVEN_QB // CHUNK), (EVEN_VB // CHUNK, EVEN_COLS // CHUNK))


def _proj_even_kernel(*refs, n_x, bpb, nbatch):
    n_lat_blocks = bpb * nbatch
    x_refs = refs[:n_x]
    g_ref, sh_ref, sc_ref, w_ref, gain_ref, g64_ref, cos_ref, sin_ref, o_ref = refs[n_x:]
    h = _modulated(_stream_block(x_refs, n_lat_blocks), g_ref, _mod_row(sh_ref, bpb, nbatch),
                   _mod_row(sc_ref, bpb, nbatch)).astype(BF16)
    cos = cos_ref[...]
    sin = sin_ref[...]
    for first, last in EVEN_DOT_RANGES:
        full = jnp.dot(h, w_ref[:, first * CHUNK:last * CHUNK], preferred_element_type=F32)
        chunk = lambda c: full[:, (c - first) * CHUNK:(c - first + 1) * CHUNK]
        normed = [c for c in range(first, last) if c in EVEN_NORM_CHUNKS]
        meansq = {}
        if normed:
            meansq = dict(zip(normed, _group_meansq([chunk(c) for c in normed], g64_ref, 1.0 / HEAD_DIM)))
        for c in range(first, last):
            cols = slice(c * CHUNK, (c + 1) * CHUNK)
            acc = chunk(c)
            if c in EVEN_NORM_CHUNKS:
                acc = _norm_rope_chunk(acc, meansq[c], gain_ref[:, cols], cos, sin, HEAD_DIM // 4)
            o_ref[:, cols] = acc.astype(BF16)


def _proj_even(x_src, g, mod, layer, w, w_index, gain, g64, cos, sin, bpb, nbatch):
    d = x_src[0].shape[1]
    n_lat_blocks = bpb * nbatch
    kern = functools.partial(_proj_even_kernel, n_x=len(x_src), bpb=bpb, nbatch=nbatch)
    return pl.pallas_call(
        kern,
        grid=(n_lat_blocks + 1,),
        in_specs=(_stream_specs(x_src, d, n_lat_blocks) + [_whole(g.shape)]
                  + _mod_specs(mod, layer, (SHIFT1, SCALE1))
                  + [pl.BlockSpec((None,) + w.shape[1:], lambda i: (w_index, 0, 0)), _whole(gain.shape),
                     _whole(g64.shape), _rope_spec(bpb, nbatch), _rope_spec(bpb, nbatch)]),
        out_specs=pl.BlockSpec((TM, EVEN_COLS), lambda i: (i, 0)),
        out_shape=jax.ShapeDtypeStruct(((n_lat_blocks + 1) * TM, EVEN_COLS), BF16),
        compiler_params=_params("parallel"),
        name="proj_even",
    )(*x_src, g, mod, mod, w, gain, g64, cos, sin)


NA_WIDTH = NA_HEADS * HEAD_DIM
MLA_PAD = LANES
MLA_WIDTH = MLA_HEADS * MLA_PAD
ODD_W_KVA = MLA_Q_LORA
ODD_W_NQ = ODD_W_KVA + MLA_KV_LORA
ODD_W_NV = ODD_W_NQ + 2 * NA_WIDTH
ODD_W_SLOT = ODD_W_NV + NA_WIDTH
ODD_W_COLS = ODD_W_SLOT + LANES
ODD_K = MLA_WIDTH
ODD_V = ODD_K + MLA_WIDTH
ODD_NQ = ODD_V + MLA_HEADS * MLA_V
ODD_NK = ODD_NQ + NA_WIDTH
ODD_NV = ODD_NK + NA_WIDTH
ODD_COLS = ODD_NV + NA_WIDTH


def _proj_odd_kernel(x_ref, g_ref, sh_ref, sc_ref, w_ref, wq_ref, wkv_ref, qa_g_ref, kva_g_ref,
                     mq_g_ref, mk_g_ref, nq_g_ref, nk_g_ref, g64_ref, g128_ref, cos_ref, sin_ref, o_ref,
                     *, bpb, nbatch):
    h = _modulated(x_ref[...], g_ref, _mod_row(sh_ref, bpb, nbatch), _mod_row(sc_ref, bpb, nbatch)).astype(BF16)
    cos = cos_ref[...]
    sin = sin_ref[...]
    quarter = MLA_ROPE // 4

    lora = jnp.dot(h, w_ref[:, :ODD_W_NQ], preferred_element_type=F32)
    rest = jnp.dot(h, w_ref[:, ODD_W_NQ:], preferred_element_type=F32)

    def low_rank(a, gain_ref):
        ms = jnp.mean(a * a, axis=-1, keepdims=True)
        return (a * lax.rsqrt(ms + EPS) * gain_ref[...]).astype(BF16)

    n_mla = MLA_WIDTH // CHUNK
    chunks_of = lambda a, first, n: [a[:, first + c * CHUNK:first + (c + 1) * CHUNK] for c in range(n)]
    qa = low_rank(lora[:, :MLA_Q_LORA], qa_g_ref)
    qf = jnp.dot(qa, wq_ref[...], preferred_element_type=F32)
    q_chunks = chunks_of(qf, 0, n_mla)
    ms_q = _group_meansq(q_chunks, g128_ref, 1.0 / MLA_QK)
    kva = low_rank(lora[:, ODD_W_KVA:], kva_g_ref)
    kvf = jnp.dot(kva, wkv_ref[...], preferred_element_type=F32)
    kslot = rest[:, ODD_W_SLOT - ODD_W_NQ:]
    kslot2 = jnp.concatenate([kslot, kslot], axis=1)
    k_rot = _swap_quarters(kslot * mk_g_ref[:, :LANES], quarter)
    k_rot2 = jnp.concatenate([k_rot, k_rot], axis=1)
    cos2 = jnp.concatenate([cos, cos], axis=1)
    sin2 = jnp.concatenate([sin, sin], axis=1)
    k_chunks = [kc + kslot2 for kc in chunks_of(kvf, 0, n_mla)]
    ms_k = _group_meansq(k_chunks, g128_ref, 1.0 / MLA_QK)
    for c in range(n_mla):
        y = _norm_rope_chunk(q_chunks[c], ms_q[c], mq_g_ref[...], cos, sin, quarter)
        o_ref[:, c * CHUNK:(c + 1) * CHUNK] = y.astype(BF16)
    for c in range(n_mla):
        inv_rms = lax.rsqrt(ms_k[c] + EPS)
        y = (k_chunks[c] * inv_rms * mk_g_ref[...]) * cos2 + (inv_rms * k_rot2) * sin2
        o_ref[:, ODD_K + c * CHUNK:ODD_K + (c + 1) * CHUNK] = y.astype(BF16)
    o_ref[:, ODD_V:ODD_NQ] = kvf[:, MLA_WIDTH:].astype(BF16)

    n_na = NA_WIDTH // CHUNK
    n_chunks = chunks_of(rest, 0, 2 * n_na)
    ms64 = _group_meansq(n_chunks, g64_ref, 1.0 / HEAD_DIM)
    for c in range(2 * n_na):
        gain = nq_g_ref[...] if c < n_na else nk_g_ref[...]
        y = _norm_rope_chunk(n_chunks[c], ms64[c], gain, None, None, None)
        o_ref[:, ODD_NQ + c * CHUNK:ODD_NQ + (c + 1) * CHUNK] = y.astype(BF16)
    o_ref[:, ODD_NV:ODD_COLS] = rest[:, ODD_W_NV - ODD_W_NQ:ODD_W_SLOT - ODD_W_NQ].astype(BF16)


def _proj_odd(xa, g, mod, layer, w, w_index, wq, wkv, gains, g64, g128, cos, sin, bpb, nbatch):
    na, d = xa.shape
    tok_specs = _stream_specs((xa,), d, bpb * nbatch) + [_whole(g.shape)] + _mod_specs(mod, layer, (SHIFT1, SCALE1))
    consts = (wq, wkv) + tuple(gains) + (g64, g128)
    return pl.pallas_call(
        functools.partial(_proj_odd_kernel, bpb=bpb, nbatch=nbatch),
        grid=(na // TM,),
        in_specs=(tok_specs + [pl.BlockSpec((None,) + w.shape[1:], lambda i: (w_index, 0, 0))]
                  + [_whole(a.shape) for a in consts] + [_rope_spec(bpb, nbatch)] * 2),
        out_specs=pl.BlockSpec((TM, ODD_COLS), lambda i: (i, 0)),
        out_shape=jax.ShapeDtypeStruct((na, ODD_COLS), BF16),
        compiler_params=_params("parallel"),
        name="proj_odd",
    )(xa, g, mod, mod, w, *consts, cos, sin)


def _attn_kernel(*refs, mode, split, main, seq, has_sink, lam_init, tq, tq_sub):
    n_sub = tq // tq_sub
    it = iter(refs)
    q_ref = next(it)
    k_ref = v_ref = sink_ref = lam_ref = subln_ref = None
    bias_refs = ()
    if main is not None:
        k_ref, v_ref = next(it), next(it)
    kc_ref, vc_ref = next(it), next(it)
    if main == "na":
        bias_refs = tuple(next(it) for _ in range(n_sub))
    if has_sink:
        sink_ref = next(it)
    safe_ref = next(it)
    if mode == "diff":
        lam_ref, subln_ref = next(it), next(it)
    o_ref = next(it)
    vt_ref, vct_ref = (next(it), next(it)) if main == "full" else (None, None)

    pair = pl.program_id(1)
    qi = pl.program_id(2)
    q = q_ref[...]
    low = lax.broadcasted_iota(jnp.int32, (1, LANES), 1) < HALF
    if split == "mask":
        zero = jnp.zeros_like(q)
        qs = (jnp.where(low, q, zero), jnp.where(low, zero, q))
        k_of = lambda k, s: k
    else:
        qs = (q[:, :LANES], q[:, LANES:])
        k_of = lambda k, s: k[:, s * LANES:(s + 1) * LANES]
    q_of = lambda s, t: qs[s][t * tq_sub:(t + 1) * tq_sub]
    sinks = [sink_ref[2 * pair + s] * LOG2E for s in range(2)] if has_sink else None
    nt_dims = (((1,), (1,)), ((), ()))

    def finish(o_lo, o_hi):
        if mode == "diff":
            lv = lam_ref[...]
            lam = (jnp.exp(jnp.sum(lv[0:1] * lv[1:2], axis=-1, keepdims=True))
                   - jnp.exp(jnp.sum(lv[2:3] * lv[3:4], axis=-1, keepdims=True)) + lam_init)
            o = o_lo - lam * o_hi
            ms = jnp.mean(o * o, axis=-1, keepdims=True)
            o = o * lax.rsqrt(ms + EPS) * subln_ref[...] * (1.0 - lam_init)
        else:
            o = jnp.where(low, o_lo, o_hi)
        o_ref[...] = o.astype(BF16)

    window_masks = {}

    def local_keys(t):
        if main == "window":
            span = tq_sub + 2 * WINDOW
            q0 = qi * tq + t * tq_sub
            at_edge = t in (0, n_sub - 1)
            w0 = jnp.clip(q0 - WINDOW, 0, seq - span) if at_edge else q0 - WINDOW
            off = pl.multiple_of(w0, WINDOW)
            if at_edge or "inner" not in window_masks:
                key = lax.broadcasted_iota(jnp.int32, (span, tq_sub), 0)
                qry = lax.broadcasted_iota(jnp.int32, (span, tq_sub), 1)
                inside = jnp.abs(key - qry + ((w0 - q0) if at_edge else -WINDOW)) <= WINDOW
                if not at_edge:
                    window_masks["inner"] = inside
            else:
                inside = window_masks["inner"]
            logit_fn = lambda lg_t, s: jnp.where(inside, lg_t, NEG)
        else:
            rows_q = tq_sub // GRID_W
            span_rows = rows_q + NA_ROWS
            w0 = jnp.clip(qi * (tq // GRID_W) + t * rows_q - NA_ROWS // 2, 0, seq // GRID_W - span_rows)
            off = pl.multiple_of(w0 * GRID_W, GRID_W)
            span = span_rows * GRID_W
            logit_fn = lambda lg_t, s: lg_t + bias_refs[t][0, s]
        return k_ref[pl.ds(off, span), :], v_ref[pl.ds(off, span), :], logit_fn

    def online_sub(t):
        def init(s):
            if has_sink:
                m0 = jnp.full((tq_sub, 1), sinks[s], F32)
                l0 = jnp.ones((tq_sub, 1), F32)
            else:
                m0 = jnp.full((tq_sub, 1), NEG, F32)
                l0 = jnp.zeros((tq_sub, 1), F32)
            return m0, l0, jnp.zeros((tq_sub, LANES), F32)

        def segment(state, k, v, logit_fn):
            out = []
            for s in range(2):
                m, l, acc = state[s]
                logits = lax.dot_general(q_of(s, t), k_of(k, s), nt_dims, preferred_element_type=F32)
                if logit_fn is not None:
                    logits = logit_fn(logits.T, s).T
                m_new = jnp.maximum(m, jnp.max(logits, axis=-1, keepdims=True))
                alpha = jnp.exp2(m - m_new)
                p = jnp.exp2(logits - m_new)
                l = alpha * l + jnp.sum(p, axis=-1, keepdims=True)
                acc = alpha * acc + jnp.dot(p.astype(BF16), v, preferred_element_type=F32)
                out.append((m_new, l, acc))
            return tuple(out)

        state = (init(0), init(1))
        if main == "full":
            def body(c, st):
                off = pl.multiple_of(c * TK_FULL, TK_FULL)
                return segment(st, k_ref[pl.ds(off, TK_FULL), :], v_ref[pl.ds(off, TK_FULL), :], None)
            state = lax.fori_loop(0, seq // TK_FULL, body, state)
        elif main is not None:
            state = segment(state, *local_keys(t))
        (_, l_lo, acc_lo), (_, l_hi, acc_hi) = segment(state, kc_ref[...], vc_ref[...], None)
        return acc_lo / l_lo, acc_hi / l_hi

    def probs_t(k, q_rows, s, logit_fn):
        logits_t = lax.dot_general(k_of(k, s), q_rows, nt_dims, preferred_element_type=F32)
        if logit_fn is not None:
            logits_t = logit_fn(logits_t, s)
        p = jnp.exp2(logits_t)
        return p.astype(BF16), jnp.sum(p, axis=0, keepdims=True)

    def max_free():
        window_masks.clear()
        if main == "full":
            @pl.when(qi == 0)
            def _():
                for c in range(seq // TK_MAX_FREE):
                    vt_ref[c] = v_ref[c * TK_MAX_FREE:(c + 1) * TK_MAX_FREE, :].T
                vct_ref[...] = vc_ref[...].T

            acc_lat = [jnp.zeros((LANES, tq), F32)] * 2
            den_lat = [jnp.zeros((1, tq), F32)] * 2
            for c in range(seq // TK_MAX_FREE):
                k = k_ref[c * TK_MAX_FREE:(c + 1) * TK_MAX_FREE, :]
                for s in range(2):
                    p, p_sum = probs_t(k, qs[s], s, None)
                    acc_lat[s] = acc_lat[s] + jnp.dot(vt_ref[c], p, preferred_element_type=F32)
                    den_lat[s] = den_lat[s] + p_sum
            vct = vct_ref[...]
        else:
            vct = vc_ref[...].T
        p_ctx = [probs_t(kc_ref[...], qs[s], s, None) for s in range(2)]
        if main in ("window", "na"):
            windows = [local_keys(t) for t in range(n_sub)]
            p_loc = [[probs_t(k, q_of(s, t), s, fn) for s in range(2)] for t, (k, _, fn) in enumerate(windows)]
            vts = [v.T for _, v, _ in windows]
        outs = []
        for s in range(2):
            acc = jnp.dot(vct, p_ctx[s][0], preferred_element_type=F32)
            den = p_ctx[s][1]
            if main == "full":
                acc, den = acc + acc_lat[s], den + den_lat[s]
            elif main is not None:
                acc = acc + jnp.concatenate(
                    [jnp.dot(vts[t], p_loc[t][s][0], preferred_element_type=F32) for t in range(n_sub)], axis=1)
                den = den + jnp.concatenate([p_loc[t][s][1] for t in range(n_sub)], axis=1)
            if has_sink:
                den = den + jnp.exp2(jnp.full((1, tq), sinks[s], F32))
            outs.append((acc / den).T)
        finish(*outs)

    def online():
        window_masks.clear()
        parts = [online_sub(t) for t in range(n_sub)]
        finish(*[jnp.concatenate([p[s] for p in parts], axis=0) if n_sub > 1 else parts[0][s]
                 for s in range(2)])

    safe = safe_ref[0] == 1
    pl.when(safe)(max_free)
    pl.when(jnp.logical_not(safe))(online)


def _attention(p, *, name, mode, split, main, n_pairs, q_blk, k_blk, v_blk, out_cols, ctx_queries,
               nbatch, seq, ctx_len, safe, bias=None, sink=None, lam=None, subln=None, lam_init=0.0):
    n_lat = nbatch * seq
    qw = LANES if split == "mask" else 2 * LANES
    if ctx_queries:
        tq, nq = ctx_len, 1
        q_row = lambda b, j, i: n_lat // ctx_len + b
        out_rows = nbatch * ctx_len
        out_row = lambda b, j, i: b
    else:
        tq = TQ_FULL if main == "full" else TQ
        nq = seq // tq
        q_row = lambda b, j, i: b * nq + i
        out_rows = n_lat
        out_row = q_row
    ctx_row = lambda b, j, i: n_lat // ctx_len + b

    in_specs = [pl.BlockSpec((tq, qw), lambda b, j, i: (q_row(b, j, i), q_blk(j)))]
    args = [p]
    if main is not None:
        in_specs += [pl.BlockSpec((seq, qw), lambda b, j, i: (b, k_blk(j))),
                     pl.BlockSpec((seq, LANES), lambda b, j, i: (b, v_blk(j)))]
        args += [p, p]
    in_specs += [pl.BlockSpec((ctx_len, qw), lambda b, j, i: (ctx_row(b, j, i), k_blk(j))),
                 pl.BlockSpec((ctx_len, LANES), lambda b, j, i: (ctx_row(b, j, i), v_blk(j)))]
    args += [p, p]
    tq_sub = TQ_SUB if main in ("window", "na") else tq
    n_sub = tq // tq_sub
    if main == "na":
        span = (tq_sub // GRID_W + NA_ROWS) * GRID_W
        last = nq * n_sub - 1
        cls = lambda g: jnp.where(g == 0, 0, jnp.where(g == last, 2, 1))
        for t in range(n_sub):
            in_specs.append(pl.BlockSpec((1, 2, span, tq_sub),
                                         lambda b, j, i, t=t: (cls(i * n_sub + t), j, 0, 0)))
            args.append(bias)
    if sink is not None:
        in_specs.append(pl.BlockSpec(memory_space=pltpu.SMEM))
        args.append(sink)
    in_specs.append(pl.BlockSpec(memory_space=pltpu.SMEM))
    args.append(safe)
    if mode == "diff":
        in_specs += [_whole(lam.shape), _whole(subln.shape)]
        args += [lam, subln]

    kern = functools.partial(_attn_kernel, mode=mode, split=split, main=main, seq=seq,
                             has_sink=sink is not None, lam_init=lam_init, tq=tq, tq_sub=tq_sub)
    scratch = []
    if main == "full":
        scratch = [pltpu.VMEM((seq // TK_MAX_FREE, LANES, TK_MAX_FREE), BF16), pltpu.VMEM((LANES, ctx_len), BF16)]
    return pl.pallas_call(
        kern,
        grid=(nbatch, n_pairs, nq),
        in_specs=in_specs,
        out_specs=pl.BlockSpec((tq, LANES), lambda b, j, i: (out_row(b, j, i), j)),
        out_shape=jax.ShapeDtypeStruct((out_rows, out_cols), BF16),
        scratch_shapes=scratch,
        compiler_params=_params("parallel", "parallel", "arbitrary"),
        name=name,
    )(*args)


def _na_bias_kernel(rpb_ref, o_ref, *, n_rows, rows_q):
    h = pl.program_id(0)
    n_dr = 2 * NA_ROWS - 1
    n_dc = 2 * NA_COLS - 1
    kc = lax.broadcasted_iota(jnp.int32, (GRID_W, LANES), 0)
    cq = lax.broadcasted_iota(jnp.int32, (GRID_W, LANES), 1) & (GRID_W - 1)
    dc_idx = jnp.clip(kc - cq, -(NA_COLS - 1), NA_COLS - 1) + NA_COLS - 1
    cs = jnp.clip(cq - NA_COLS // 2, 0, GRID_W - NA_COLS)
    col_ok = (kc >= cs) & (kc < cs + NA_COLS)
    tiles = [jnp.zeros((GRID_W, LANES), F32) for _ in range(n_dr)]
    for dc in range(n_dc):
        hit = dc_idx == dc
        for dr in range(n_dr):
            tiles[dr] = jnp.where(hit, rpb_ref[(h * n_dr + dr) * n_dc + dc] * LOG2E, tiles[dr])
    tiles = [jnp.where(col_ok, t, NEG) for t in tiles]
    masked = jnp.full((GRID_W, LANES), NEG, F32)
    low = lax.broadcasted_iota(jnp.int32, (GRID_W, LANES), 1) < HALF

    span_rows = rows_q + NA_ROWS
    first_q_row = (0, span_rows, n_rows - rows_q)
    for c in range(3):
        r0 = first_q_row[c]
        w0 = min(max(r0 - NA_ROWS // 2, 0), n_rows - span_rows)
        for kr in range(span_rows):
            k_abs = w0 + kr
            for m in range(rows_q // 2):
                halves = []
                for r in (r0 + 2 * m, r0 + 2 * m + 1):
                    rs = min(max(r - NA_ROWS // 2, 0), n_rows - NA_ROWS)
                    halves.append(tiles[k_abs - r + NA_ROWS - 1] if rs <= k_abs < rs + NA_ROWS else masked)
                o_ref[c, 0, kr * GRID_W:(kr + 1) * GRID_W, m * LANES:(m + 1) * LANES] = (
                    jnp.where(low, halves[0], halves[1]))


def _na_bias(rpb, seq):
    n_rows = seq // GRID_W
    rows_q = TQ_SUB // GRID_W
    span = (rows_q + NA_ROWS) * GRID_W
    kern = functools.partial(_na_bias_kernel, n_rows=n_rows, rows_q=rows_q)
    return pl.pallas_call(
        kern,
        grid=(NA_HEADS,),
        in_specs=[pl.BlockSpec(memory_space=pltpu.SMEM)],
        out_specs=pl.BlockSpec((3, 1, span, TQ_SUB), lambda h: (0, h, 0, 0)),
        out_shape=jax.ShapeDtypeStruct((3, NA_HEADS, span, TQ_SUB), F32),
        compiler_params=_params("parallel"),
        name="na_bias",
    )(rpb.reshape(-1))


def _post_kernel(*refs, n_x, n_o, bpb, nbatch):
    n_lat_blocks = bpb * nbatch
    x_refs, oa_refs, ob_refs = refs[:n_x], refs[n_x:n_x + n_o], refs[n_x + n_o:n_x + 2 * n_o]
    wo_ref, g1_ref, n2_ref, sh_ref, sc_ref, g2_ref, w1_ref, w2_ref, o_ref = refs[n_x + 2 * n_o:]
    oa = _stream_block(oa_refs, n_lat_blocks)
    ob = _stream_block(ob_refs, n_lat_blocks)
    half = oa.shape[1]
    y = (jnp.dot(oa, wo_ref[:half, :], preferred_element_type=F32)
         + jnp.dot(ob, wo_ref[half:, :], preferred_element_type=F32))
    x1 = _stream_block(x_refs, n_lat_blocks) + _mod_row(g1_ref, bpb, nbatch) * y
    h = _modulated(x1, n2_ref, _mod_row(sh_ref, bpb, nbatch), _mod_row(sc_ref, bpb, nbatch)).astype(BF16)
    a = jnp.dot(h, w1_ref[...], preferred_element_type=F32)
    a = jnp.square(jnp.maximum(a, 0.0)).astype(BF16)
    o_ref[...] = x1 + _mod_row(g2_ref, bpb, nbatch) * jnp.dot(a, w2_ref[...], preferred_element_type=F32)


def _post(x_src, oa_src, ob_src, wo, mod, n2, w1, w2, layer, n_blocks, bpb, nbatch):
    d = x_src[0].shape[1]
    half = oa_src[0].shape[1]
    n_lat_blocks = bpb * nbatch
    g1_spec, sh_spec, sc_spec, g2_spec = _mod_specs(mod, layer, (GATE1, SHIFT2, SCALE2, GATE2))
    resident = lambda a: pl.BlockSpec((None,) + a.shape[1:], lambda i: (layer, 0, 0),
                                      pipeline_mode=pl.Buffered(1))
    kern = functools.partial(_post_kernel, n_x=len(x_src), n_o=len(oa_src), bpb=bpb, nbatch=nbatch)
    return pl.pallas_call(
        kern,
        grid=(n_blocks,),
        in_specs=(_stream_specs(x_src, d, n_lat_blocks) + _stream_specs(oa_src, half, n_lat_blocks)
                  + _stream_specs(ob_src, half, n_lat_blocks)
                  + [resident(wo), g1_spec, _whole(n2.shape), sh_spec, sc_spec, g2_spec, resident(w1), resident(w2)]),
        out_specs=pl.BlockSpec((TM, d), lambda i: (i, 0)),
        out_shape=jax.ShapeDtypeStruct((n_blocks * TM, d), F32),
        compiler_params=_params("parallel"),
        name="post",
    )(*x_src, *oa_src, *ob_src, wo, mod, n2, mod, mod, mod, w1, w2)


def _block_diag_ones(group):
    idx = np.arange(CHUNK) // group
    return jnp.asarray(idx[:, None] == idx[None, :], dtype=BF16)


def _rope_tables(seq, rot_dim, lane0, pad_rows):
    t = jnp.arange(seq, dtype=jnp.int32)
    row = (t // GRID_W).astype(F32)
    col = (t % GRID_W).astype(F32)
    n_freq = rot_dim // 4
    freqs = jnp.power(ROPE_BASE, -jnp.arange(n_freq, dtype=F32) / n_freq)
    ar = row[:, None] * freqs[None, :]
    ac = col[:, None] * freqs[None, :]
    ang = jnp.concatenate([ar, ar, ac, ac], axis=-1)
    sign = jnp.asarray(np.tile(np.repeat([-1.0, 1.0], n_freq), 2), F32)
    cos, sin = jnp.cos(ang), jnp.sin(ang) * sign[None, :]
    if lane0 is None:
        reps = LANES // rot_dim
        cos, sin = jnp.tile(cos, (1, reps)), jnp.tile(sin, (1, reps))
    else:
        pad = ((0, 0), (lane0, LANES - lane0 - rot_dim))
        cos = jnp.pad(cos, pad, constant_values=1.0)
        sin = jnp.pad(sin, pad)
    cos = jnp.concatenate([cos, jnp.ones((pad_rows, LANES), F32)], axis=0)
    sin = jnp.concatenate([sin, jnp.zeros((pad_rows, LANES), F32)], axis=0)
    return cos, sin


def _logits_bounded(q_gain, k_gain, dim, bias=None, sink=None):
    bound = jnp.max(jnp.abs(q_gain)) * jnp.max(jnp.abs(k_gain)) * (dim ** 0.5) * BF16_ROUNDING_MARGIN
    if bias is not None:
        bound = bound + jnp.max(jnp.abs(bias))
    if sink is not None:
        bound = jnp.maximum(bound, jnp.max(jnp.abs(sink)))
    return (bound <= MAX_FREE_LOGIT_BOUND).astype(jnp.int32).reshape(1)


def _tile_gain(g, reps, scale=1.0):
    return (jnp.tile(g.astype(F32), reps) * scale)[None, :]


def _even_in_proj(w):
    nq = DIFF_HEADS * 2 * HEAD_DIM
    o = 3 * nq + SWA_Q_HEADS * HEAD_DIM
    kb = [w[..., o + i * HEAD_DIM:o + (i + 1) * HEAD_DIM] for i in range(SWA_KV_HEADS)]
    o += SWA_KV_HEADS * HEAD_DIM
    vb = [w[..., o + i * HEAD_DIM:o + (i + 1) * HEAD_DIM] for i in range(SWA_KV_HEADS)]
    dup = lambda parts: [p for p in parts for _ in range(2)]
    w_new = jnp.concatenate([w[..., :3 * nq + SWA_Q_HEADS * HEAD_DIM]] + dup(kb) + dup(vb), axis=-1).astype(BF16)
    assert w_new.shape[-1] == EVEN_COLS
    return w_new


def _even_gain(q_g, k_g, sq_g, sk_g):
    nq = DIFF_HEADS * 2 * HEAD_DIM
    scale = HEAD_DIM ** -0.5 * LOG2E
    ones = lambda n: jnp.ones((1, n), F32)
    gain = jnp.concatenate([
        _tile_gain(q_g, 2 * DIFF_HEADS, scale), _tile_gain(k_g, 2 * DIFF_HEADS), ones(nq),
        _tile_gain(sq_g, SWA_Q_HEADS, scale), _tile_gain(sk_g, 2 * SWA_KV_HEADS),
        ones(2 * SWA_KV_HEADS * HEAD_DIM)], axis=1)
    assert gain.shape[1] == EVEN_COLS
    return gain


def _odd_in_proj(w):
    lora = MLA_Q_LORA + MLA_KV_LORA
    k_r = w[..., lora:lora + MLA_ROPE]
    zeros = lambda n: jnp.zeros(w.shape[:-1] + (n,), w.dtype)
    w_new = jnp.concatenate([w[..., :lora], w[..., lora + MLA_ROPE:], zeros(MLA_NOPE), k_r, zeros(LANES - MLA_QK)],
                            axis=-1).astype(BF16)
    assert w_new.shape[-1] == ODD_W_COLS
    return w_new


def _odd_up_weights(wq_up, wkv_up, qa_g, kva_g, mq_g, mk_g, nq_g, nk_g):
    wq =jnp.pad(wq_up.reshape(MLA_Q_LORA, MLA_HEADS, MLA_QK), ((0, 0), (0, 0), (0, MLA_PAD - MLA_QK)))
    wq = wq.reshape(MLA_Q_LORA, MLA_HEADS * MLA_PAD).astype(BF16)
    wkv = wkv_up.reshape(MLA_KV_LORA, MLA_HEADS, MLA_NOPE + MLA_V)
    wk = jnp.pad(wkv[..., :MLA_NOPE], ((0, 0), (0, 0), (0, MLA_PAD - MLA_NOPE))).reshape(MLA_KV_LORA, -1)
    wv = wkv[..., MLA_NOPE:].reshape(MLA_KV_LORA, MLA_HEADS * MLA_V)
    wkv_new = jnp.concatenate([wk, wv], axis=1).astype(BF16)
    pad_gain = lambda g, scale: _tile_gain(jnp.pad(g.astype(F32), (0, MLA_PAD - MLA_QK)), CHUNK // MLA_PAD, scale)
    gains = (qa_g.astype(F32)[None, :], kva_g.astype(F32)[None, :],
             pad_gain(mq_g, MLA_QK ** -0.5 * LOG2E), pad_gain(mk_g, 1.0),
             _tile_gain(nq_g, CHUNK // HEAD_DIM, HEAD_DIM ** -0.5 * LOG2E), _tile_gain(nk_g, CHUNK // HEAD_DIM))
    return wq, wkv_new, gains


def kernel(x, c, ctx, c_ctx, ada_w, ada_b, norm1_g, norm2_g, w_out, mlp_w1, mlp_w2, ev_w_in, diff_q_g,
           diff_k_g, diff_lam, diff_subln_g, swa_q_g, swa_k_g, swa_sink, od_w_in, mla_qa_g, mla_kva_g,
           mla_wq_up, mla_wkv_up, mla_q_g, mla_k_g, na_q_g, na_k_g, na_rpb):
    nbatch, seq, d = x.shape
    ctx_len = ctx.shape[1]
    depth = ada_w.shape[0]
    n_lat = nbatch * seq
    n_ctx = nbatch * ctx_len
    assert seq % TM == 0 and n_ctx == TM and seq % TQ == 0 and seq % GRID_W == 0
    assert seq % TQ_FULL == 0 and seq % TK_MAX_FREE == 0 and seq % TK_FULL == 0
    assert seq // GRID_W >= 2 * (NA_ROWS + TQ_SUB // GRID_W) and seq >= TQ_SUB + 2 * WINDOW
    bpb = seq // TM

    mod_rows = 8
    cvec = jnp.concatenate([c, c_ctx[None, :], jnp.zeros((mod_rows - nbatch - 1, d), F32)], axis=0)
    mod = _modulation(cvec, ada_w, ada_b)

    cos64, sin64 = _rope_tables(seq, HEAD_DIM, None, n_ctx)
    cos32, sin32 = _rope_tables(seq, MLA_ROPE, MLA_NOPE, n_ctx)
    g64 = _block_diag_ones(HEAD_DIM)
    g128 = _block_diag_ones(LANES)

    x_src = (x.reshape(n_lat, d), ctx.reshape(n_ctx, d))
    wo_all, w1_all, w2_all = w_out.astype(BF16), mlp_w1.astype(BF16), mlp_w2.astype(BF16)
    ev_w_all, od_w_all = _even_in_proj(ev_w_in), _odd_in_proj(od_w_in)
    common = dict(nbatch=nbatch, seq=seq, ctx_len=ctx_len)

    for layer in range(depth):
        need_ctx = layer < depth - 1
        n_blocks = bpb * nbatch + (1 if need_ctx else 0)
        n1 = norm1_g[layer][None, :]
        n2 = norm2_g[layer][None, :]
        i = layer // 2
        if layer % 2 == 0:
            lam_init = 0.8 - 0.6 * math.exp(-0.3 * layer)
            gain = _even_gain(diff_q_g[i], diff_k_g[i], swa_q_g[i], swa_k_g[i])
            p = _proj_even(x_src, n1, mod, layer, ev_w_all, i, gain, g64, cos64, sin64, bpb, nbatch)
            lam = diff_lam[i].astype(F32)
            subln = diff_subln_g[i].astype(F32)[None, :]
            sink = swa_sink[i].astype(F32)
            diff_kw = dict(mode="diff", split="mask", n_pairs=DIFF_HEADS, q_blk=lambda j: EVEN_QA // LANES + j,
                           k_blk=lambda j: EVEN_KA // LANES + j, v_blk=lambda j: EVEN_VA // LANES + j,
                           out_cols=DIFF_HEADS * LANES,
                           lam=lam, subln=subln, lam_init=lam_init,
                           safe=_logits_bounded(diff_q_g[i], diff_k_g[i], HEAD_DIM), **common)
            swa_kw = dict(mode="pair", split="mask", n_pairs=SWA_Q_HEADS // 2,
                          q_blk=lambda j: EVEN_QB // LANES + j, k_blk=lambda j: EVEN_KB // LANES + j // 2,
                          v_blk=lambda j: EVEN_VB // LANES + j // 2,
                          out_cols=SWA_Q_HEADS * HEAD_DIM, sink=sink,
                          safe=_logits_bounded(swa_q_g[i], swa_k_g[i], HEAD_DIM, sink=sink), **common)
            oa = _attention(p, name="diff_attn", main="full", ctx_queries=False, **diff_kw)
            ob = _attention(p, name="window_attn", main="window", ctx_queries=False, **swa_kw)
            if need_ctx:
                oa_c = _attention(p, name="diff_attn_ctx", main=None, ctx_queries=True, **diff_kw)
                ob_c = _attention(p, name="window_attn_ctx", main=None, ctx_queries=True, **swa_kw)
        else:
            wq, wkv, gains = _odd_up_weights(mla_wq_up[i], mla_wkv_up[i], mla_qa_g[i], mla_kva_g[i],
                                             mla_q_g[i], mla_k_g[i], na_q_g[i], na_k_g[i])
            p = _proj_odd(x_src[0], n1, mod, layer, od_w_all, i, wq, wkv, gains, g64, g128, cos32, sin32, bpb, nbatch)
            bias = _na_bias(na_rpb[i].astype(F32), seq)
            mla_kw = dict(mode="pair", split="slice", n_pairs=MLA_HEADS // 2, q_blk=lambda j: j,
                          k_blk=lambda j: ODD_K // CHUNK + j, v_blk=lambda j: ODD_V // LANES + j,
                          out_cols=MLA_HEADS * MLA_V,
                          safe=_logits_bounded(mla_q_g[i], mla_k_g[i], MLA_QK), **common)
            na_kw = dict(mode="pair", split="mask", n_pairs=NA_HEADS // 2,
                         q_blk=lambda j: ODD_NQ // LANES + j, k_blk=lambda j: ODD_NK // LANES + j,
                         v_blk=lambda j: ODD_NV // LANES + j, out_cols=NA_HEADS * HEAD_DIM,
                         safe=_logits_bounded(na_q_g[i], na_k_g[i], HEAD_DIM, bias=na_rpb[i]), **common)
            oa = _attention(p, name="mla_attn", main="full", ctx_queries=False, **mla_kw)
            ob = _attention(p, name="na_attn", main="na", ctx_queries=False, bias=bias, **na_kw)
            if need_ctx:
                oa_c = _attention(p, name="mla_attn_ctx", main=None, ctx_queries=True, **mla_kw)
                ob_c = _attention(p, name="na_attn_ctx", main=None, ctx_queries=True, **na_kw)
        oa_src, ob_src = ((oa, oa_c), (ob, ob_c)) if need_ctx else ((oa,), (ob,))
        x_src = (_post(x_src, oa_src, ob_src, wo_all, mod, n2, w1_all, w2_all, layer, n_blocks, bpb, nbatch),)
    return x_src[0].reshape(nbatch, seq, d)
```

```python
import functools
import math

import numpy as np
import jax
import jax.numpy as jnp
from jax import lax
from jax.experimental import pallas as pl
from jax.experimental.pallas import tpu as pltpu

F32 = jnp.float32
BF16 = jnp.bfloat16

LANES = 128
HALF = LANES // 2
CHUNK = 2 * LANES
GRID_W = 64
HEAD_DIM = 64
EPS = 1e-6
ROPE_BASE = 10000.0
DIFF_HEADS = 4
SWA_Q_HEADS = 8
SWA_KV_HEADS = 2
WINDOW = 128
MLA_HEADS = 8
MLA_Q_LORA = 512
MLA_KV_LORA = 256
MLA_NOPE = 64
MLA_ROPE = 32
MLA_QK = MLA_NOPE + MLA_ROPE
MLA_V = 64
NA_HEADS = 8
NA_ROWS = 8
NA_COLS = 16
NEG = -1e30
LOG2E = math.log2(math.e)
MAX_FREE_LOGIT_BOUND = 50.0
BF16_ROUNDING_MARGIN = 1.02

TM = 512
TQ = 2048
TQ_SUB = 256
TQ_FULL = 1024
TK_FULL = 512
TK_MAX_FREE = 2048
MOD_TN = 1536
VMEM_LIMIT = 56 * 1024 * 1024


def _params(*sem):
    return pltpu.CompilerParams(dimension_semantics=sem, vmem_limit_bytes=VMEM_LIMIT)


def _mod_kernel(c_ref, w_ref, b_ref, o_ref):
    c = c_ref[...]
    a = (c * jax.nn.sigmoid(c)).astype(BF16)
    w = w_ref[0].astype(BF16)
    o_ref[0] = jnp.dot(a, w, preferred_element_type=F32) + b_ref[0]


def _modulation(cvec, ada_w, ada_b):
    depth, d, n = ada_w.shape
    rows = cvec.shape[0]
    return pl.pallas_call(
        _mod_kernel,
        grid=(depth, n // MOD_TN),
        in_specs=[
            pl.BlockSpec((rows, d), lambda l, j: (0, 0)),
            pl.BlockSpec((1, d, MOD_TN), lambda l, j: (l, 0, j)),
            pl.BlockSpec((1, 1, MOD_TN), lambda l, j: (l, 0, j)),
        ],
        out_specs=pl.BlockSpec((1, rows, MOD_TN), lambda l, j: (l, 0, j)),
        out_shape=jax.ShapeDtypeStruct((depth, rows, n), F32),
        compiler_params=_params("parallel", "parallel"),
        name="modulation",
    )(cvec, ada_w, ada_b.reshape(depth, 1, n))


def _modulated(x, g_ref, shift, scale):
    ms = jnp.mean(x * x, axis=-1, keepdims=True)
    y = x * lax.rsqrt(ms + EPS) * g_ref[...]
    return y * (1.0 + scale) + shift


def _group_meansq(chunks, g_ref, inv_n):
    rows = chunks[0].shape[0]
    sq = jnp.concatenate([(c * c).astype(BF16) for c in chunks], axis=0)
    ss = jnp.dot(sq, g_ref[...], preferred_element_type=F32) * inv_n
    return [ss[i * rows:(i + 1) * rows] for i in range(len(chunks))]


def _swap_quarters(x, quarter):
    lane = lax.broadcasted_iota(jnp.int32, x.shape, 1)
    odd = (lane & quarter) != 0
    from_lower = pltpu.roll(x, quarter, 1)
    from_upper = pltpu.roll(x, LANES - quarter, 1)
    return jnp.where(odd, from_lower, from_upper)


def _rope(x, cos, sin_signed, quarter):
    return x * cos + _swap_quarters(x, quarter) * sin_signed


def _norm_rope_chunk(acc, meansq, gain, cos, sin, quarter):
    y = acc * lax.rsqrt(meansq + EPS) * gain
    if quarter is None:
        return y
    return jnp.concatenate([_rope(y[:, :LANES], cos, sin, quarter),
                            _rope(y[:, LANES:], cos, sin, quarter)], axis=1)


def _stream_specs(src, cols, n_lat_blocks):
    if len(src) == 1:
        return [pl.BlockSpec((TM, cols), lambda i: (i, 0))]
    return [pl.BlockSpec((TM, cols), lambda i: (jnp.minimum(i, n_lat_blocks - 1), 0)),
            pl.BlockSpec((TM, cols), lambda i: (0, 0))]


def _stream_block(refs, n_lat_blocks):
    if len(refs) == 1:
        return refs[0][...]
    return jnp.where(pl.program_id(0) < n_lat_blocks, refs[0][...], refs[1][...])


SHIFT1, SCALE1, GATE1, SHIFT2, SCALE2, GATE2 = range(6)


def _mod_specs(mod, layer, chunks):
    d = mod.shape[2] // 6
    return [pl.BlockSpec((None, mod.shape[1], d), lambda i, k=k: (layer, 0, k)) for k in chunks]


def _mod_row(ref, bpb, nbatch):
    row = jnp.minimum(pl.program_id(0) // bpb, nbatch)
    return ref[pl.ds(row, 1), :]


def _rope_spec(bpb, nbatch):
    return pl.BlockSpec((TM, LANES), lambda i: (jnp.where(i < bpb * nbatch, i % bpb, bpb), 0))


def _whole(shape):
    return pl.BlockSpec(shape, lambda *_: (0,) * len(shape))


EVEN_QA = 0
EVEN_KA = EVEN_QA + DIFF_HEADS * 2 * HEAD_DIM
EVEN_VA = EVEN_KA + DIFF_HEADS * 2 * HEAD_DIM
EVEN_QB = EVEN_VA + DIFF_HEADS * 2 * HEAD_DIM
EVEN_KB = EVEN_QB + SWA_Q_HEADS * HEAD_DIM
EVEN_VB = EVEN_KB + 2 * SWA_KV_HEADS * HEAD_DIM
EVEN_COLS = EVEN_VB + 2 * SWA_KV_HEADS * HEAD_DIM
EVEN_NORM_CHUNKS = tuple(c for c in range(EVEN_COLS // CHUNK)
                         if c * CHUNK < EVEN_VA or EVEN_QB <= c * CHUNK < EVEN_VB)
EVEN_DOT_RANGES = ((EVEN_QA // CHUNK, EVEN_VA // CHUNK), (EVEN_QB // CHUNK, EVEN_VB // CHUNK),
                   (EVEN_VA // CHUNK, EVEN_QB // CHUNK), (EVEN_VB // CHUNK, EVEN_COLS // CHUNK))


def _proj_even_kernel(*refs, n_x, bpb, nbatch):
    n_lat_blocks = bpb * nbatch
    x_refs = refs[:n_x]
    g_ref, sh_ref, sc_ref, w_ref, gain_ref, g64_ref, cos_ref, sin_ref, o_ref = refs[n_x:]
    h = _modulated(_stream_block(x_refs, n_lat_blocks), g_ref, _mod_row(sh_ref, bpb, nbatch),
                   _mod_row(sc_ref, bpb, nbatch)).astype(BF16)
    cos = cos_ref[...]
    sin = sin_ref[...]
    for first, last in EVEN_DOT_RANGES:
        full = jnp.dot(h, w_ref[:, first * CHUNK:last * CHUNK], preferred_element_type=F32)
        chunk = lambda c: full[:, (c - first) * CHUNK:(c - first + 1) * CHUNK]
        normed = [c for c in range(first, last) if c in EVEN_NORM_CHUNKS]
        meansq = {}
        if normed:
            meansq = dict(zip(normed, _group_meansq([chunk(c) for c in normed], g64_ref, 1.0 / HEAD_DIM)))
        for c in range(first, last):
            cols = slice(c * CHUNK, (c + 1) * CHUNK)
            acc = chunk(c)
            if c in EVEN_NORM_CHUNKS:
                acc = _norm_rope_chunk(acc, meansq[c], gain_ref[:, cols], cos, sin, HEAD_DIM // 4)
            o_ref[:, cols] = acc.astype(BF16)


def _proj_even(x_src, g, mod, layer, w, gain, g64, cos, sin, bpb, nbatch):
    d = x_src[0].shape[1]
    n_lat_blocks = bpb * nbatch
    kern = functools.partial(_proj_even_kernel, n_x=len(x_src), bpb=bpb, nbatch=nbatch)
    return pl.pallas_call(
        kern,
        grid=(n_lat_blocks + 1,),
        in_specs=(_stream_specs(x_src, d, n_lat_blocks) + [_whole(g.shape)]
                  + _mod_specs(mod, layer, (SHIFT1, SCALE1))
                  + [_whole(w.shape), _whole(gain.shape), _whole(g64.shape),
                     _rope_spec(bpb, nbatch), _rope_spec(bpb, nbatch)]),
        out_specs=pl.BlockSpec((TM, EVEN_COLS), lambda i: (i, 0)),
        out_shape=jax.ShapeDtypeStruct(((n_lat_blocks + 1) * TM, EVEN_COLS), BF16),
        compiler_params=_params("parallel"),
        name="proj_even",
    )(*x_src, g, mod, mod, w, gain, g64, cos, sin)


NA_WIDTH = NA_HEADS * HEAD_DIM
MLA_PAD = LANES
MLA_WIDTH = MLA_HEADS * MLA_PAD
ODD_W_KVA = MLA_Q_LORA
ODD_W_NQ = ODD_W_KVA + MLA_KV_LORA
ODD_W_NV = ODD_W_NQ + 2 * NA_WIDTH
ODD_W_SLOT = ODD_W_NV + NA_WIDTH
ODD_W_COLS = ODD_W_SLOT + LANES
ODD_K = MLA_WIDTH
ODD_V = ODD_K + MLA_WIDTH
ODD_NQ = ODD_V + MLA_HEADS * MLA_V
ODD_NK = ODD_NQ + NA_WIDTH
ODD_NV = ODD_NK + NA_WIDTH
ODD_COLS = ODD_NV + NA_WIDTH


def _proj_odd_kernel(x_ref, g_ref, sh_ref, sc_ref, w_ref, wq_ref, wkv_ref, qa_g_ref, kva_g_ref,
                     mq_g_ref, mk_g_ref, nq_g_ref, nk_g_ref, g64_ref, g128_ref, cos_ref, sin_ref, o_ref,
                     *, bpb, nbatch):
    h = _modulated(x_ref[...], g_ref, _mod_row(sh_ref, bpb, nbatch), _mod_row(sc_ref, bpb, nbatch)).astype(BF16)
    cos = cos_ref[...]
    sin = sin_ref[...]
    quarter = MLA_ROPE // 4

    lora = jnp.dot(h, w_ref[:, :ODD_W_NQ], preferred_element_type=F32)
    rest = jnp.dot(h, w_ref[:, ODD_W_NQ:], preferred_element_type=F32)

    def low_rank(a, gain_ref):
        ms = jnp.mean(a * a, axis=-1, keepdims=True)
        return (a * lax.rsqrt(ms + EPS) * gain_ref[...]).astype(BF16)

    n_mla = MLA_WIDTH // CHUNK
    chunks_of = lambda a, first, n: [a[:, first + c * CHUNK:first + (c + 1) * CHUNK] for c in range(n)]
    qa = low_rank(lora[:, :MLA_Q_LORA], qa_g_ref)
    qf = jnp.dot(qa, wq_ref[...], preferred_element_type=F32)
    q_chunks = chunks_of(qf, 0, n_mla)
    ms_q = _group_meansq(q_chunks, g128_ref, 1.0 / MLA_QK)
    kva = low_rank(lora[:, ODD_W_KVA:], kva_g_ref)
    kvf = jnp.dot(kva, wkv_ref[...], preferred_element_type=F32)
    kslot = rest[:, ODD_W_SLOT - ODD_W_NQ:]
    kslot2 = jnp.concatenate([kslot, kslot], axis=1)
    k_rot = _swap_quarters(kslot * mk_g_ref[:, :LANES], quarter)
    k_rot2 = jnp.concatenate([k_rot, k_rot], axis=1)
    cos2 = jnp.concatenate([cos, cos], axis=1)
    sin2 = jnp.concatenate([sin, sin], axis=1)
    k_chunks = [kc + kslot2 for kc in chunks_of(kvf, 0, n_mla)]
    ms_k = _group_meansq(k_chunks, g128_ref, 1.0 / MLA_QK)
    for c in range(n_mla):
        y = _norm_rope_chunk(q_chunks[c], ms_q[c], mq_g_ref[...], cos, sin, quarter)
        o_ref[:, c * CHUNK:(c + 1) * CHUNK] = y.astype(BF16)
    for c in range(n_mla):
        inv_rms = lax.rsqrt(ms_k[c] + EPS)
        y = (k_chunks[c] * inv_rms * mk_g_ref[...]) * cos2 + (inv_rms * k_rot2) * sin2
        o_ref[:, ODD_K + c * CHUNK:ODD_K + (c + 1) * CHUNK] = y.astype(BF16)
    o_ref[:, ODD_V:ODD_NQ] = kvf[:, MLA_WIDTH:].astype(BF16)

    n_na = NA_WIDTH // CHUNK
    n_chunks = chunks_of(rest, 0, 2 * n_na)
    ms64 = _group_meansq(n_chunks, g64_ref, 1.0 / HEAD_DIM)
    for c in range(2 * n_na):
        gain = nq_g_ref[...] if c < n_na else nk_g_ref[...]
        y = _norm_rope_chunk(n_chunks[c], ms64[c], gain, None, None, None)
        o_ref[:, ODD_NQ + c * CHUNK:ODD_NQ + (c + 1) * CHUNK] = y.astype(BF16)
    o_ref[:, ODD_NV:ODD_COLS] = rest[:, ODD_W_NV - ODD_W_NQ:ODD_W_SLOT - ODD_W_NQ].astype(BF16)


def _proj_odd(xa, g, mod, layer, w, wq, wkv, gains, g64, g128, cos, sin, bpb, nbatch):
    na, d = xa.shape
    tok_specs = _stream_specs((xa,), d, bpb * nbatch) + [_whole(g.shape)] + _mod_specs(mod, layer, (SHIFT1, SCALE1))
    consts = (w, wq, wkv) + tuple(gains) + (g64, g128)
    return pl.pallas_call(
        functools.partial(_proj_odd_kernel, bpb=bpb, nbatch=nbatch),
        grid=(na // TM,),
        in_specs=tok_specs + [_whole(a.shape) for a in consts] + [_rope_spec(bpb, nbatch)] * 2,
        out_specs=pl.BlockSpec((TM, ODD_COLS), lambda i: (i, 0)),
        out_shape=jax.ShapeDtypeStruct((na, ODD_COLS), BF16),
        compiler_params=_params("parallel"),
        name="proj_odd",
    )(xa, g, mod, mod, *consts, cos, sin)


def _attn_kernel(*refs, mode, split, main, seq, has_sink, lam_init, tq, tq_sub):
    n_sub = tq // tq_sub
    it = iter(refs)
    q_ref = next(it)
    k_ref = v_ref = bias_ref = sink_ref = lam_ref = subln_ref = None
    if main is not None:
        k_ref, v_ref = next(it), next(it)
    kc_ref, vc_ref = next(it), next(it)
    if main == "na":
        bias_ref = next(it)
    if has_sink:
        sink_ref = next(it)
    safe_ref = next(it)
    if mode == "diff":
        lam_ref, subln_ref = next(it), next(it)
    o_ref = next(it)
    vt_ref, vct_ref = (next(it), next(it)) if main == "full" else (None, None)

    pair = pl.program_id(1)
    qi = pl.program_id(2)
    q = q_ref[...]
    low = lax.broadcasted_iota(jnp.int32, (1, LANES), 1) < HALF
    if split == "mask":
        zero = jnp.zeros_like(q)
        qs = (jnp.where(low, q, zero), jnp.where(low, zero, q))
        k_of = lambda k, s: k
    else:
        qs = (q[:, :LANES], q[:, LANES:])
        k_of = lambda k, s: k[:, s * LANES:(s + 1) * LANES]
    q_of = lambda s, t: qs[s][t * tq_sub:(t + 1) * tq_sub]
    sinks = [sink_ref[2 * pair + s] * LOG2E for s in range(2)] if has_sink else None
    nt_dims = (((1,), (1,)), ((), ()))

    def finish(o_lo, o_hi):
        if mode == "diff":
            lv = lam_ref[...]
            lam = (jnp.exp(jnp.sum(lv[0:1] * lv[1:2], axis=-1, keepdims=True))
                   - jnp.exp(jnp.sum(lv[2:3] * lv[3:4], axis=-1, keepdims=True)) + lam_init)
            o = o_lo - lam * o_hi
            ms = jnp.mean(o * o, axis=-1, keepdims=True)
            o = o * lax.rsqrt(ms + EPS) * subln_ref[...] * (1.0 - lam_init)
        else:
            o = jnp.where(low, o_lo, o_hi)
        o_ref[...] = o.astype(BF16)

    window_masks = {}

    def local_keys(t):
        if main == "window":
            span = tq_sub + 2 * WINDOW
            q0 = qi * tq + t * tq_sub
            at_edge = t in (0, n_sub - 1)
            w0 = jnp.clip(q0 - WINDOW, 0, seq - span) if at_edge else q0 - WINDOW
            off = pl.multiple_of(w0, WINDOW)
            if at_edge or "inner" not in window_masks:
                key = lax.broadcasted_iota(jnp.int32, (span, tq_sub), 0)
                qry = lax.broadcasted_iota(jnp.int32, (span, tq_sub), 1)
                inside = jnp.abs(key - qry + ((w0 - q0) if at_edge else -WINDOW)) <= WINDOW
                if not at_edge:
                    window_masks["inner"] = inside
            else:
                inside = window_masks["inner"]
            logit_fn = lambda lg_t, s: jnp.where(inside, lg_t, NEG)
        else:
            rows_q = tq_sub // GRID_W
            span_rows = rows_q + NA_ROWS
            w0 = jnp.clip(qi * (tq // GRID_W) + t * rows_q - NA_ROWS // 2, 0, seq // GRID_W - span_rows)
            off = pl.multiple_of(w0 * GRID_W, GRID_W)
            span = span_rows * GRID_W
            cls = 1
            if t == 0:
                cls = jnp.where(qi == 0, 0, cls)
            if t == n_sub - 1:
                cls = jnp.where(qi == pl.num_programs(2) - 1, 2, cls)
            logit_fn = lambda lg_t, s: lg_t + bias_ref[cls, s]
        return k_ref[pl.ds(off, span), :], v_ref[pl.ds(off, span), :], logit_fn

    def online_sub(t):
        def init(s):
            if has_sink:
                m0 = jnp.full((tq_sub, 1), sinks[s], F32)
                l0 = jnp.ones((tq_sub, 1), F32)
            else:
                m0 = jnp.full((tq_sub, 1), NEG, F32)
                l0 = jnp.zeros((tq_sub, 1), F32)
            return m0, l0, jnp.zeros((tq_sub, LANES), F32)

        def segment(state, k, v, logit_fn):
            out = []
            for s in range(2):
                m, l, acc = state[s]
                logits = lax.dot_general(q_of(s, t), k_of(k, s), nt_dims, preferred_element_type=F32)
                if logit_fn is not None:
                    logits = logit_fn(logits.T, s).T
                m_new = jnp.maximum(m, jnp.max(logits, axis=-1, keepdims=True))
                alpha = jnp.exp2(m - m_new)
                p = jnp.exp2(logits - m_new)
                l = alpha * l + jnp.sum(p, axis=-1, keepdims=True)
                acc = alpha * acc + jnp.dot(p.astype(BF16), v, preferred_element_type=F32)
                out.append((m_new, l, acc))
            return tuple(out)

        state = (init(0), init(1))
        if main == "full":
            def body(c, st):
                off = pl.multiple_of(c * TK_FULL, TK_FULL)
                return segment(st, k_ref[pl.ds(off, TK_FULL), :], v_ref[pl.ds(off, TK_FULL), :], None)
            state = lax.fori_loop(0, seq // TK_FULL, body, state)
        elif main is not None:
            state = segment(state, *local_keys(t))
        (_, l_lo, acc_lo), (_, l_hi, acc_hi) = segment(state, kc_ref[...], vc_ref[...], None)
        return acc_lo / l_lo, acc_hi / l_hi

    def probs_t(k, q_rows, s, logit_fn):
        logits_t = lax.dot_general(k_of(k, s), q_rows, nt_dims, preferred_element_type=F32)
        if logit_fn is not None:
            logits_t = logit_fn(logits_t, s)
        p = jnp.exp2(logits_t)
        return p.astype(BF16), jnp.sum(p, axis=0, keepdims=True)

    def max_free():
        window_masks.clear()
        if main == "full":
            @pl.when(qi == 0)
            def _():
                for c in range(seq // TK_MAX_FREE):
                    vt_ref[c] = v_ref[c * TK_MAX_FREE:(c + 1) * TK_MAX_FREE, :].T
                vct_ref[...] = vc_ref[...].T

            acc_lat = [jnp.zeros((LANES, tq), F32)] * 2
            den_lat = [jnp.zeros((1, tq), F32)] * 2
            for c in range(seq // TK_MAX_FREE):
                k = k_ref[c * TK_MAX_FREE:(c + 1) * TK_MAX_FREE, :]
                for s in range(2):
                    p, p_sum = probs_t(k, qs[s], s, None)
                    acc_lat[s] = acc_lat[s] + jnp.dot(vt_ref[c], p, preferred_element_type=F32)
                    den_lat[s] = den_lat[s] + p_sum
            vct = vct_ref[...]
        else:
            vct = vc_ref[...].T
        p_ctx = [probs_t(kc_ref[...], qs[s], s, None) for s in range(2)]
        if main in ("window", "na"):
            windows = [local_keys(t) for t in range(n_sub)]
            p_loc = [[probs_t(k, q_of(s, t), s, fn) for s in range(2)] for t, (k, _, fn) in enumerate(windows)]
            vts = [v.T for _, v, _ in windows]
        outs = []
        for s in range(2):
            acc = jnp.dot(vct, p_ctx[s][0], preferred_element_type=F32)
            den = p_ctx[s][1]
            if main == "full":
                acc, den = acc + acc_lat[s], den + den_lat[s]
            elif main is not None:
                acc = acc + jnp.concatenate(
                    [jnp.dot(vts[t], p_loc[t][s][0], preferred_element_type=F32) for t in range(n_sub)], axis=1)
                den = den + jnp.concatenate([p_loc[t][s][1] for t in range(n_sub)], axis=1)
            if has_sink:
                den = den + jnp.exp2(jnp.full((1, tq), sinks[s], F32))
            outs.append((acc / den).T)
        finish(*outs)

    def online():
        window_masks.clear()
        parts = [online_sub(t) for t in range(n_sub)]
        finish(*[jnp.concatenate([p[s] for p in parts], axis=0) if n_sub > 1 else parts[0][s]
                 for s in range(2)])

    safe = safe_ref[0] == 1
    pl.when(safe)(max_free)
    pl.when(jnp.logical_not(safe))(online)


def _attention(p, *, name, mode, split, main, n_pairs, q_blk, k_blk, v_blk, out_cols, ctx_queries,
               nbatch, seq, ctx_len, safe, bias=None, sink=None, lam=None, subln=None, lam_init=0.0):
    n_lat = nbatch * seq
    qw = LANES if split == "mask" else 2 * LANES
    if ctx_queries:
        tq, nq = ctx_len, 1
        q_row = lambda b, j, i: n_lat // ctx_len + b
        out_rows = nbatch * ctx_len
        out_row = lambda b, j, i: b
    else:
        tq = TQ_FULL if main == "full" else TQ
        nq = seq // tq
        q_row = lambda b, j, i: b * nq + i
        out_rows = n_lat
        out_row = q_row
    ctx_row = lambda b, j, i: n_lat // ctx_len + b

    in_specs = [pl.BlockSpec((tq, qw), lambda b, j, i: (q_row(b, j, i), q_blk(j)))]
    args = [p]
    if main is not None:
        in_specs += [pl.BlockSpec((seq, qw), lambda b, j, i: (b, k_blk(j))),
                     pl.BlockSpec((seq, LANES), lambda b, j, i: (b, v_blk(j)))]
        args += [p, p]
    in_specs += [pl.BlockSpec((ctx_len, qw), lambda b, j, i: (ctx_row(b, j, i), k_blk(j))),
                 pl.BlockSpec((ctx_len, LANES), lambda b, j, i: (ctx_row(b, j, i), v_blk(j)))]
    args += [p, p]
    tq_sub = TQ_SUB if main in ("window", "na") else tq
    n_sub = tq // tq_sub
    if main == "na":
        span = (tq_sub // GRID_W + NA_ROWS) * GRID_W
        in_specs.append(pl.BlockSpec((3, 2, span, tq_sub), lambda b, j, i: (0, j, 0, 0)))
        args.append(bias)
    if sink is not None:
        in_specs.append(pl.BlockSpec(memory_space=pltpu.SMEM))
        args.append(sink)
    in_specs.append(pl.BlockSpec(memory_space=pltpu.SMEM))
    args.append(safe)
    if mode == "diff":
        in_specs += [_whole(lam.shape), _whole(subln.shape)]
        args += [lam, subln]

    kern = functools.partial(_attn_kernel, mode=mode, split=split, main=main, seq=seq,
                             has_sink=sink is not None, lam_init=lam_init, tq=tq, tq_sub=tq_sub)
    scratch = []
    if main == "full":
        scratch = [pltpu.VMEM((seq // TK_MAX_FREE, LANES, TK_MAX_FREE), BF16), pltpu.VMEM((LANES, ctx_len), BF16)]
    return pl.pallas_call(
        kern,
        grid=(nbatch, n_pairs, nq),
        in_specs=in_specs,
        out_specs=pl.BlockSpec((tq, LANES), lambda b, j, i: (out_row(b, j, i), j)),
        out_shape=jax.ShapeDtypeStruct((out_rows, out_cols), BF16),
        scratch_shapes=scratch,
        compiler_params=_params("parallel", "parallel", "arbitrary"),
        name=name,
    )(*args)


def _na_bias_kernel(rpb_ref, o_ref, *, n_rows, rows_q):
    h = pl.program_id(0)
    n_dr = 2 * NA_ROWS - 1
    n_dc = 2 * NA_COLS - 1
    kc = lax.broadcasted_iota(jnp.int32, (GRID_W, LANES), 0)
    cq = lax.broadcasted_iota(jnp.int32, (GRID_W, LANES), 1) & (GRID_W - 1)
    dc_idx = jnp.clip(kc - cq, -(NA_COLS - 1), NA_COLS - 1) + NA_COLS - 1
    cs = jnp.clip(cq - NA_COLS // 2, 0, GRID_W - NA_COLS)
    col_ok = (kc >= cs) & (kc < cs + NA_COLS)
    tiles = [jnp.zeros((GRID_W, LANES), F32) for _ in range(n_dr)]
    for dc in range(n_dc):
        hit = dc_idx == dc
        for dr in range(n_dr):
            tiles[dr] = jnp.where(hit, rpb_ref[(h * n_dr + dr) * n_dc + dc] * LOG2E, tiles[dr])
    tiles = [jnp.where(col_ok, t, NEG) for t in tiles]
    masked = jnp.full((GRID_W, LANES), NEG, F32)
    low = lax.broadcasted_iota(jnp.int32, (GRID_W, LANES), 1) < HALF

    span_rows = rows_q + NA_ROWS
    first_q_row = (0, span_rows, n_rows - rows_q)
    for c in range(3):
        r0 = first_q_row[c]
        w0 = min(max(r0 - NA_ROWS // 2, 0), n_rows - span_rows)
        for kr in range(span_rows):
            k_abs = w0 + kr
            for m in range(rows_q // 2):
                halves = []
                for r in (r0 + 2 * m, r0 + 2 * m + 1):
                    rs = min(max(r - NA_ROWS // 2, 0), n_rows - NA_ROWS)
                    halves.append(tiles[k_abs - r + NA_ROWS - 1] if rs <= k_abs < rs + NA_ROWS else masked)
                o_ref[c, 0, kr * GRID_W:(kr + 1) * GRID_W, m * LANES:(m + 1) * LANES] = (
                    jnp.where(low, halves[0], halves[1]))


def _na_bias(rpb, seq):
    n_rows = seq // GRID_W
    rows_q = TQ_SUB // GRID_W
    span = (rows_q + NA_ROWS) * GRID_W
    kern = functools.partial(_na_bias_kernel, n_rows=n_rows, rows_q=rows_q)
    return pl.pallas_call(
        kern,
        grid=(NA_HEADS,),
        in_specs=[pl.BlockSpec(memory_space=pltpu.SMEM)],
        out_specs=pl.BlockSpec((3, 1, span, TQ_SUB), lambda h: (0, h, 0, 0)),
        out_shape=jax.ShapeDtypeStruct((3, NA_HEADS, span, TQ_SUB), F32),
        compiler_params=_params("parallel"),
        name="na_bias",
    )(rpb.reshape(-1))


def _post_kernel(*refs, n_x, n_o, bpb, nbatch):
    n_lat_blocks = bpb * nbatch
    x_refs, oa_refs, ob_refs = refs[:n_x], refs[n_x:n_x + n_o], refs[n_x + n_o:n_x + 2 * n_o]
    wo_ref, g1_ref, n2_ref, sh_ref, sc_ref, g2_ref, w1_ref, w2_ref, o_ref = refs[n_x + 2 * n_o:]
    oa = _stream_block(oa_refs, n_lat_blocks)
    ob = _stream_block(ob_refs, n_lat_blocks)
    half = oa.shape[1]
    y = (jnp.dot(oa, wo_ref[:half, :], preferred_element_type=F32)
         + jnp.dot(ob, wo_ref[half:, :], preferred_element_type=F32))
    x1 = _stream_block(x_refs, n_lat_blocks) + _mod_row(g1_ref, bpb, nbatch) * y
    h = _modulated(x1, n2_ref, _mod_row(sh_ref, bpb, nbatch), _mod_row(sc_ref, bpb, nbatch)).astype(BF16)
    a = jnp.dot(h, w1_ref[...], preferred_element_type=F32)
    a = jnp.square(jnp.maximum(a, 0.0)).astype(BF16)
    o_ref[...] = x1 + _mod_row(g2_ref, bpb, nbatch) * jnp.dot(a, w2_ref[...], preferred_element_type=F32)


def _post(x_src, oa_src, ob_src, wo, mod, n2, w1, w2, layer, n_blocks, bpb, nbatch):
    d = x_src[0].shape[1]
    half = oa_src[0].shape[1]
    n_lat_blocks = bpb * nbatch
    g1_spec, sh_spec, sc_spec, g2_spec = _mod_specs(mod, layer, (GATE1, SHIFT2, SCALE2, GATE2))
    resident = lambda a: pl.BlockSpec((None,) + a.shape[1:], lambda i: (layer, 0, 0),
                                      pipeline_mode=pl.Buffered(1))
    kern = functools.partial(_post_kernel, n_x=len(x_src), n_o=len(oa_src), bpb=bpb, nbatch=nbatch)
    return pl.pallas_call(
        kern,
        grid=(n_blocks,),
        in_specs=(_stream_specs(x_src, d, n_lat_blocks) + _stream_specs(oa_src, half, n_lat_blocks)
                  + _stream_specs(ob_src, half, n_lat_blocks)
                  + [resident(wo), g1_spec, _whole(n2.shape), sh_spec, sc_spec, g2_spec, resident(w1), resident(w2)]),
        out_specs=pl.BlockSpec((TM, d), lambda i: (i, 0)),
        out_shape=jax.ShapeDtypeStruct((n_blocks * TM, d), F32),
        compiler_params=_params("parallel"),
        name="post",
    )(*x_src, *oa_src, *ob_src, wo, mod, n2, mod, mod, mod, w1, w2)


def _block_diag_ones(group):
    idx = np.arange(CHUNK) // group
    return jnp.asarray(idx[:, None] == idx[None, :], dtype=BF16)


def _rope_tables(seq, rot_dim, lane0, pad_rows):
    t = jnp.arange(seq, dtype=jnp.int32)
    row = (t // GRID_W).astype(F32)
    col = (t % GRID_W).astype(F32)
    n_freq = rot_dim // 4
    freqs = jnp.power(ROPE_BASE, -jnp.arange(n_freq, dtype=F32) / n_freq)
    ar = row[:, None] * freqs[None, :]
    ac = col[:, None] * freqs[None, :]
    ang = jnp.concatenate([ar, ar, ac, ac], axis=-1)
    sign = jnp.asarray(np.tile(np.repeat([-1.0, 1.0], n_freq), 2), F32)
    cos, sin = jnp.cos(ang), jnp.sin(ang) * sign[None, :]
    if lane0 is None:
        reps = LANES // rot_dim
        cos, sin = jnp.tile(cos, (1, reps)), jnp.tile(sin, (1, reps))
    else:
        pad = ((0, 0), (lane0, LANES - lane0 - rot_dim))
        cos = jnp.pad(cos, pad, constant_values=1.0)
        sin = jnp.pad(sin, pad)
    cos = jnp.concatenate([cos, jnp.ones((pad_rows, LANES), F32)], axis=0)
    sin = jnp.concatenate([sin, jnp.zeros((pad_rows, LANES), F32)], axis=0)
    return cos, sin


def _logits_bounded(q_gain, k_gain, dim, bias=None, sink=None):
    bound = jnp.max(jnp.abs(q_gain)) * jnp.max(jnp.abs(k_gain)) * (dim ** 0.5) * BF16_ROUNDING_MARGIN
    if bias is not None:
        bound = bound + jnp.max(jnp.abs(bias))
    if sink is not None:
        bound = jnp.maximum(bound, jnp.max(jnp.abs(sink)))
    return (bound <= MAX_FREE_LOGIT_BOUND).astype(jnp.int32).reshape(1)


def _tile_gain(g, reps, scale=1.0):
    return (jnp.tile(g.astype(F32), reps) * scale)[None, :]


def _even_weights(w, q_g, k_g, sq_g, sk_g):
    nq = DIFF_HEADS * 2 * HEAD_DIM
    qa, ka, va = w[:, :nq], w[:, nq:2 * nq], w[:, 2 * nq:3 * nq]
    o = 3 * nq
    qb = w[:, o:o + SWA_Q_HEADS * HEAD_DIM]
    o += SWA_Q_HEADS * HEAD_DIM
    kb = [w[:, o + i * HEAD_DIM:o + (i + 1) * HEAD_DIM] for i in range(SWA_KV_HEADS)]
    o += SWA_KV_HEADS * HEAD_DIM
    vb = [w[:, o + i * HEAD_DIM:o + (i + 1) * HEAD_DIM] for i in range(SWA_KV_HEADS)]
    dup = lambda parts: [p for p in parts for _ in range(2)]
    w_new = jnp.concatenate([qa, ka, va, qb] + dup(kb) + dup(vb), axis=1).astype(BF16)
    scale = HEAD_DIM ** -0.5 * LOG2E
    ones = lambda n: jnp.ones((1, n), F32)
    gain = jnp.concatenate([
        _tile_gain(q_g, 2 * DIFF_HEADS, scale), _tile_gain(k_g, 2 * DIFF_HEADS), ones(nq),
        _tile_gain(sq_g, SWA_Q_HEADS, scale), _tile_gain(sk_g, 2 * SWA_KV_HEADS),
        ones(2 * SWA_KV_HEADS * HEAD_DIM)], axis=1)
    assert w_new.shape[1] == gain.shape[1] == EVEN_COLS
    return w_new, gain


def _odd_weights(w, wq_up, wkv_up, qa_g, kva_g, mq_g, mk_g, nq_g, nk_g):
    d = w.shape[0]
    o = 0
    parts = []
    for n in (MLA_Q_LORA, MLA_KV_LORA, MLA_ROPE, NA_HEADS * HEAD_DIM, NA_HEADS * HEAD_DIM, NA_HEADS * HEAD_DIM):
        parts.append(w[:, o:o + n])
        o += n
    q_a, kv_a, k_r, nq, nk, nv = parts
    slot = jnp.concatenate([jnp.zeros((d, MLA_NOPE), w.dtype), k_r,
                            jnp.zeros((d, LANES - MLA_QK), w.dtype)], axis=1)
    w_new = jnp.concatenate([q_a, kv_a, nq, nk, nv, slot], axis=1).astype(BF16)
    assert w_new.shape[1] == ODD_W_COLS
    wq = jnp.pad(wq_up.reshape(MLA_Q_LORA, MLA_HEADS, MLA_QK), ((0, 0), (0, 0), (0, MLA_PAD - MLA_QK)))
    wq = wq.reshape(MLA_Q_LORA, MLA_HEADS * MLA_PAD).astype(BF16)
    wkv = wkv_up.reshape(MLA_KV_LORA, MLA_HEADS, MLA_NOPE + MLA_V)
    wk = jnp.pad(wkv[..., :MLA_NOPE], ((0, 0), (0, 0), (0, MLA_PAD - MLA_NOPE))).reshape(MLA_KV_LORA, -1)
    wv = wkv[..., MLA_NOPE:].reshape(MLA_KV_LORA, MLA_HEADS * MLA_V)
    wkv_new = jnp.concatenate([wk, wv], axis=1).astype(BF16)
    pad_gain = lambda g, scale: _tile_gain(jnp.pad(g.astype(F32), (0, MLA_PAD - MLA_QK)), CHUNK // MLA_PAD, scale)
    gains = (qa_g.astype(F32)[None, :], kva_g.astype(F32)[None, :],
             pad_gain(mq_g, MLA_QK ** -0.5 * LOG2E), pad_gain(mk_g, 1.0),
             _tile_gain(nq_g, CHUNK // HEAD_DIM, HEAD_DIM ** -0.5 * LOG2E), _tile_gain(nk_g, CHUNK // HEAD_DIM))
    return w_new, wq, wkv_new, gains


def kernel(x, c, ctx, c_ctx, ada_w, ada_b, norm1_g, norm2_g, w_out, mlp_w1, mlp_w2, ev_w_in, diff_q_g,
           diff_k_g, diff_lam, diff_subln_g, swa_q_g, swa_k_g, swa_sink, od_w_in, mla_qa_g, mla_kva_g,
           mla_wq_up, mla_wkv_up, mla_q_g, mla_k_g, na_q_g, na_k_g, na_rpb):
    nbatch, seq, d = x.shape
    ctx_len = ctx.shape[1]
    depth = ada_w.shape[0]
    n_lat = nbatch * seq
    n_ctx = nbatch * ctx_len
    assert seq % TM == 0 and n_ctx == TM and seq % TQ == 0 and seq % GRID_W == 0
    assert seq % TQ_FULL == 0 and seq % TK_MAX_FREE == 0 and seq % TK_FULL == 0
    assert seq // GRID_W >= 2 * (NA_ROWS + TQ_SUB // GRID_W) and seq >= TQ_SUB + 2 * WINDOW
    bpb = seq // TM

    mod_rows = 8
    cvec = jnp.concatenate([c, c_ctx[None, :], jnp.zeros((mod_rows - nbatch - 1, d), F32)], axis=0)
    mod = _modulation(cvec, ada_w, ada_b)

    cos64, sin64 = _rope_tables(seq, HEAD_DIM, None, n_ctx)
    cos32, sin32 = _rope_tables(seq, MLA_ROPE, MLA_NOPE, n_ctx)
    g64 = _block_diag_ones(HEAD_DIM)
    g128 = _block_diag_ones(LANES)

    x_src = (x.reshape(n_lat, d), ctx.reshape(n_ctx, d))
    wo_all, w1_all, w2_all = w_out.astype(BF16), mlp_w1.astype(BF16), mlp_w2.astype(BF16)
    common = dict(nbatch=nbatch, seq=seq, ctx_len=ctx_len)

    for layer in range(depth):
        need_ctx = layer < depth - 1
        n_blocks = bpb * nbatch + (1 if need_ctx else 0)
        n1 = norm1_g[layer][None, :]
        n2 = norm2_g[layer][None, :]
        i = layer // 2
        if layer % 2 == 0:
            lam_init = 0.8 - 0.6 * math.exp(-0.3 * layer)
            w_in, gain = _even_weights(ev_w_in[i], diff_q_g[i], diff_k_g[i], swa_q_g[i], swa_k_g[i])
            p = _proj_even(x_src, n1, mod, layer, w_in, gain, g64, cos64, sin64, bpb, nbatch)
            lam = diff_lam[i].astype(F32)
            subln = diff_subln_g[i].astype(F32)[None, :]
            sink = swa_sink[i].astype(F32)
            diff_kw = dict(mode="diff", split="mask", n_pairs=DIFF_HEADS, q_blk=lambda j: EVEN_QA // LANES + j,
                           k_blk=lambda j: EVEN_KA // LANES + j, v_blk=lambda j: EVEN_VA // LANES + j,
                           out_cols=DIFF_HEADS * LANES,
                           lam=lam, subln=subln, lam_init=lam_init,
                           safe=_logits_bounded(diff_q_g[i], diff_k_g[i], HEAD_DIM), **common)
            swa_kw = dict(mode="pair", split="mask", n_pairs=SWA_Q_HEADS // 2,
                          q_blk=lambda j: EVEN_QB // LANES + j, k_blk=lambda j: EVEN_KB // LANES + j // 2,
                          v_blk=lambda j: EVEN_VB // LANES + j // 2,
                          out_cols=SWA_Q_HEADS * HEAD_DIM, sink=sink,
                          safe=_logits_bounded(swa_q_g[i], swa_k_g[i], HEAD_DIM, sink=sink), **common)
            oa = _attention(p, name="diff_attn", main="full", ctx_queries=False, **diff_kw)
            ob = _attention(p, name="window_attn", main="window", ctx_queries=False, **swa_kw)
            if need_ctx:
                oa_c = _attention(p, name="diff_attn_ctx", main=None, ctx_queries=True, **diff_kw)
                ob_c = _attention(p, name="window_attn_ctx", main=None, ctx_queries=True, **swa_kw)
        else:
            w_in, wq, wkv, gains = _odd_weights(od_w_in[i], mla_wq_up[i], mla_wkv_up[i], mla_qa_g[i],
                                                mla_kva_g[i], mla_q_g[i], mla_k_g[i], na_q_g[i], na_k_g[i])
            p = _proj_odd(x_src[0], n1, mod, layer, w_in, wq, wkv, gains, g64, g128, cos32, sin32, bpb, nbatch)
            bias = _na_bias(na_rpb[i].astype(F32), seq)
            mla_kw = dict(mode="pair", split="slice", n_pairs=MLA_HEADS // 2, q_blk=lambda j: j,
                          k_blk=lambda j: ODD_K // CHUNK + j, v_blk=lambda j: ODD_V // LANES + j,
                          out_cols=MLA_HEADS * MLA_V,
                          safe=_logits_bounded(mla_q_g[i], mla_k_g[i], MLA_QK), **common)
            na_kw = dict(mode="pair", split="mask", n_pairs=NA_HEADS // 2,
                         q_blk=lambda j: ODD_NQ // LANES + j, k_blk=lambda j: ODD_NK // LANES + j,
                         v_blk=lambda j: ODD_NV // LANES + j, out_cols=NA_HEADS * HEAD_DIM,
                         safe=_logits_bounded(na_q_g[i], na_k_g[i], HEAD_DIM, bias=na_rpb[i]), **common)
            oa = _attention(p, name="mla_attn", main="full", ctx_queries=False, **mla_kw)
            ob = _attention(p, name="na_attn", main="na", ctx_queries=False, bias=bias, **na_kw)
            if need_ctx:
                oa_c = _attention(p, name="mla_attn_ctx", main=None, ctx_queries=True, **mla_kw)
                ob_c = _attention(p, name="na_attn_ctx", main=None, ctx_queries=True, **na_kw)
        oa_src, ob_src = ((oa, oa_c), (ob, ob_c)) if need_ctx else ((oa,), (ob,))
        x_src = (_post(x_src, oa_src, ob_src, wo_all, mod, n2, w1_all, w2_all, layer, n_blocks, bpb, nbatch),)
    return x_src[0].reshape(nbatch, seq, d)
```

```python
import functools
import math

import numpy as np
import jax
import jax.numpy as jnp
from jax import lax
from jax.experimental import pallas as pl
from jax.experimental.pallas import tpu as pltpu

F32 = jnp.float32
BF16 = jnp.bfloat16

LANES = 128
HALF = LANES // 2
CHUNK = 2 * LANES
GRID_W = 64
HEAD_DIM = 64
EPS = 1e-6
ROPE_BASE = 10000.0
DIFF_HEADS = 4
SWA_Q_HEADS = 8
SWA_KV_HEADS = 2
WINDOW = 128
MLA_HEADS = 8
MLA_Q_LORA = 512
MLA_KV_LORA = 256
MLA_NOPE = 64
MLA_ROPE = 32
MLA_QK = MLA_NOPE + MLA_ROPE
MLA_V = 64
NA_HEADS = 8
NA_ROWS = 8
NA_COLS = 16
NEG = -1e30
LOG2E = math.log2(math.e)
MAX_FREE_LOGIT_BOUND = 50.0
BF16_ROUNDING_MARGIN = 1.02

TM = 512
TQ = 2048
TQ_WINDOW = 4096
TQ_SUB = 256
TQ_FULL = 1024
TK_FULL = 512
TK_MAX_FREE = 2048
MOD_TN = 1536
VMEM_LIMIT = 56 * 1024 * 1024


def _params(*sem):
    return pltpu.CompilerParams(dimension_semantics=sem, vmem_limit_bytes=VMEM_LIMIT)


def _mod_kernel(c_ref, w_ref, b_ref, o_ref):
    c = c_ref[...]
    a = (c * jax.nn.sigmoid(c)).astype(BF16)
    w = w_ref[0].astype(BF16)
    o_ref[0] = jnp.dot(a, w, preferred_element_type=F32) + b_ref[0]


def _modulation(cvec, ada_w, ada_b):
    depth, d, n = ada_w.shape
    rows = cvec.shape[0]
    return pl.pallas_call(
        _mod_kernel,
        grid=(depth, n // MOD_TN),
        in_specs=[
            pl.BlockSpec((rows, d), lambda l, j: (0, 0)),
            pl.BlockSpec((1, d, MOD_TN), lambda l, j: (l, 0, j)),
            pl.BlockSpec((1, 1, MOD_TN), lambda l, j: (l, 0, j)),
        ],
        out_specs=pl.BlockSpec((1, rows, MOD_TN), lambda l, j: (l, 0, j)),
        out_shape=jax.ShapeDtypeStruct((depth, rows, n), F32),
        compiler_params=_params("parallel", "parallel"),
        name="modulation",
    )(cvec, ada_w, ada_b.reshape(depth, 1, n))


def _modulated(x, g_ref, shift, scale):
    ms = jnp.mean(x * x, axis=-1, keepdims=True)
    y = x * lax.rsqrt(ms + EPS) * g_ref[...]
    return y * (1.0 + scale) + shift


def _group_meansq(chunks, g_ref, inv_n):
    rows = chunks[0].shape[0]
    sq = jnp.concatenate([(c * c).astype(BF16) for c in chunks], axis=0)
    ss = jnp.dot(sq, g_ref[...], preferred_element_type=F32) * inv_n
    return [ss[i * rows:(i + 1) * rows] for i in range(len(chunks))]


def _swap_quarters(x, quarter):
    lane = lax.broadcasted_iota(jnp.int32, x.shape, 1)
    odd = (lane & quarter) != 0
    from_lower = pltpu.roll(x, quarter, 1)
    from_upper = pltpu.roll(x, LANES - quarter, 1)
    return jnp.where(odd, from_lower, from_upper)


def _rope(x, cos, sin_signed, quarter):
    return x * cos + _swap_quarters(x, quarter) * sin_signed


def _norm_rope_chunk(acc, meansq, gain, cos, sin, quarter):
    y = acc * lax.rsqrt(meansq + EPS) * gain
    if quarter is None:
        return y
    return jnp.concatenate([_rope(y[:, :LANES], cos, sin, quarter),
                            _rope(y[:, LANES:], cos, sin, quarter)], axis=1)


def _stream_specs(src, cols, n_lat_blocks):
    if len(src) == 1:
        return [pl.BlockSpec((TM, cols), lambda i: (i, 0))]
    return [pl.BlockSpec((TM, cols), lambda i: (jnp.minimum(i, n_lat_blocks - 1), 0)),
            pl.BlockSpec((TM, cols), lambda i: (0, 0))]


def _stream_block(refs, n_lat_blocks):
    if len(refs) == 1:
        return refs[0][...]
    return jnp.where(pl.program_id(0) < n_lat_blocks, refs[0][...], refs[1][...])


SHIFT1, SCALE1, GATE1, SHIFT2, SCALE2, GATE2 = range(6)


def _mod_specs(mod, layer, chunks):
    d = mod.shape[2] // 6
    return [pl.BlockSpec((None, mod.shape[1], d), lambda i, k=k: (layer, 0, k)) for k in chunks]


def _mod_row(ref, bpb, nbatch):
    row = jnp.minimum(pl.program_id(0) // bpb, nbatch)
    return ref[pl.ds(row, 1), :]


def _rope_spec(bpb, nbatch):
    return pl.BlockSpec((TM, LANES), lambda i: (jnp.where(i < bpb * nbatch, i % bpb, bpb), 0))


def _whole(shape):
    return pl.BlockSpec(shape, lambda *_: (0,) * len(shape))


EVEN_QA = 0
EVEN_KA = EVEN_QA + DIFF_HEADS * 2 * HEAD_DIM
EVEN_VA = EVEN_KA + DIFF_HEADS * 2 * HEAD_DIM
EVEN_QB = EVEN_VA + DIFF_HEADS * 2 * HEAD_DIM
EVEN_KB = EVEN_QB + SWA_Q_HEADS * HEAD_DIM
EVEN_VB = EVEN_KB + 2 * SWA_KV_HEADS * HEAD_DIM
EVEN_COLS = EVEN_VB + 2 * SWA_KV_HEADS * HEAD_DIM
EVEN_NORM_CHUNKS = tuple(c for c in range(EVEN_COLS // CHUNK)
                         if c * CHUNK < EVEN_VA or EVEN_QB <= c * CHUNK < EVEN_VB)
EVEN_DOT_RANGES = ((EVEN_QA // CHUNK, EVEN_VA // CHUNK), (EVEN_QB // CHUNK, EVEN_VB // CHUNK),
                   (EVEN_VA // CHUNK, EVEN_QB // CHUNK), (EVEN_VB // CHUNK, EVEN_COLS // CHUNK))


def _proj_even_kernel(*refs, n_x, bpb, nbatch):
    n_lat_blocks = bpb * nbatch
    x_refs = refs[:n_x]
    g_ref, sh_ref, sc_ref, w_ref, gain_ref, g64_ref, cos_ref, sin_ref, o_ref = refs[n_x:]
    h = _modulated(_stream_block(x_refs, n_lat_blocks), g_ref, _mod_row(sh_ref, bpb, nbatch),
                   _mod_row(sc_ref, bpb, nbatch)).astype(BF16)
    cos = cos_ref[...]
    sin = sin_ref[...]
    for first, last in EVEN_DOT_RANGES:
        full = jnp.dot(h, w_ref[:, first * CHUNK:last * CHUNK], preferred_element_type=F32)
        chunk = lambda c: full[:, (c - first) * CHUNK:(c - first + 1) * CHUNK]
        normed = [c for c in range(first, last) if c in EVEN_NORM_CHUNKS]
        meansq = {}
        if normed:
            meansq = dict(zip(normed, _group_meansq([chunk(c) for c in normed], g64_ref, 1.0 / HEAD_DIM)))
        for c in range(first, last):
            cols = slice(c * CHUNK, (c + 1) * CHUNK)
            acc = chunk(c)
            if c in EVEN_NORM_CHUNKS:
                acc = _norm_rope_chunk(acc, meansq[c], gain_ref[:, cols], cos, sin, HEAD_DIM // 4)
            o_ref[:, cols] = acc.astype(BF16)


def _proj_even(x_src, g, mod, layer, w, gain, g64, cos, sin, bpb, nbatch):
    d = x_src[0].shape[1]
    n_lat_blocks = bpb * nbatch
    kern = functools.partial(_proj_even_kernel, n_x=len(x_src), bpb=bpb, nbatch=nbatch)
    return pl.pallas_call(
        kern,
        grid=(n_lat_blocks + 1,),
        in_specs=(_stream_specs(x_src, d, n_lat_blocks) + [_whole(g.shape)]
                  + _mod_specs(mod, layer, (SHIFT1, SCALE1))
                  + [_whole(w.shape), _whole(gain.shape), _whole(g64.shape),
                     _rope_spec(bpb, nbatch), _rope_spec(bpb, nbatch)]),
        out_specs=pl.BlockSpec((TM, EVEN_COLS), lambda i: (i, 0)),
        out_shape=jax.ShapeDtypeStruct(((n_lat_blocks + 1) * TM, EVEN_COLS), BF16),
        compiler_params=_params("parallel"),
        name="proj_even",
    )(*x_src, g, mod, mod, w, gain, g64, cos, sin)


NA_WIDTH = NA_HEADS * HEAD_DIM
MLA_PAD = LANES
MLA_WIDTH = MLA_HEADS * MLA_PAD
ODD_W_KVA = MLA_Q_LORA
ODD_W_NQ = ODD_W_KVA + MLA_KV_LORA
ODD_W_NV = ODD_W_NQ + 2 * NA_WIDTH
ODD_W_SLOT = ODD_W_NV + NA_WIDTH
ODD_W_COLS = ODD_W_SLOT + LANES
ODD_K = MLA_WIDTH
ODD_V = ODD_K + MLA_WIDTH
ODD_NQ = ODD_V + MLA_HEADS * MLA_V
ODD_NK = ODD_NQ + NA_WIDTH
ODD_NV = ODD_NK + NA_WIDTH
ODD_COLS = ODD_NV + NA_WIDTH


def _proj_odd_kernel(x_ref, g_ref, sh_ref, sc_ref, w_ref, wq_ref, wkv_ref, qa_g_ref, kva_g_ref,
                     mq_g_ref, mk_g_ref, nq_g_ref, nk_g_ref, g64_ref, g128_ref, cos_ref, sin_ref, o_ref,
                     *, bpb, nbatch):
    h = _modulated(x_ref[...], g_ref, _mod_row(sh_ref, bpb, nbatch), _mod_row(sc_ref, bpb, nbatch)).astype(BF16)
    cos = cos_ref[...]
    sin = sin_ref[...]
    quarter = MLA_ROPE // 4

    lora = jnp.dot(h, w_ref[:, :ODD_W_NQ], preferred_element_type=F32)
    rest = jnp.dot(h, w_ref[:, ODD_W_NQ:], preferred_element_type=F32)

    def low_rank(a, gain_ref):
        ms = jnp.mean(a * a, axis=-1, keepdims=True)
        return (a * lax.rsqrt(ms + EPS) * gain_ref[...]).astype(BF16)

    n_mla = MLA_WIDTH // CHUNK
    chunks_of = lambda a, first, n: [a[:, first + c * CHUNK:first + (c + 1) * CHUNK] for c in range(n)]
    qa = low_rank(lora[:, :MLA_Q_LORA], qa_g_ref)
    qf = jnp.dot(qa, wq_ref[...], preferred_element_type=F32)
    q_chunks = chunks_of(qf, 0, n_mla)
    ms_q = _group_meansq(q_chunks, g128_ref, 1.0 / MLA_QK)
    kva = low_rank(lora[:, ODD_W_KVA:], kva_g_ref)
    kvf = jnp.dot(kva, wkv_ref[...], preferred_element_type=F32)
    kslot = rest[:, ODD_W_SLOT - ODD_W_NQ:]
    kslot2 = jnp.concatenate([kslot, kslot], axis=1)
    k_rot = _swap_quarters(kslot * mk_g_ref[:, :LANES], quarter)
    k_rot2 = jnp.concatenate([k_rot, k_rot], axis=1)
    cos2 = jnp.concatenate([cos, cos], axis=1)
    sin2 = jnp.concatenate([sin, sin], axis=1)
    k_chunks = [kc + kslot2 for kc in chunks_of(kvf, 0, n_mla)]
    ms_k = _group_meansq(k_chunks, g128_ref, 1.0 / MLA_QK)
    for c in range(n_mla):
        y = _norm_rope_chunk(q_chunks[c], ms_q[c], mq_g_ref[...], cos, sin, quarter)
        o_ref[:, c * CHUNK:(c + 1) * CHUNK] = y.astype(BF16)
    for c in range(n_mla):
        inv_rms = lax.rsqrt(ms_k[c] + EPS)
        y = (k_chunks[c] * inv_rms * mk_g_ref[...]) * cos2 + (inv_rms * k_rot2) * sin2
        o_ref[:, ODD_K + c * CHUNK:ODD_K + (c + 1) * CHUNK] = y.astype(BF16)
    o_ref[:, ODD_V:ODD_NQ] = kvf[:, MLA_WIDTH:].astype(BF16)

    n_na = NA_WIDTH // CHUNK
    n_chunks = chunks_of(rest, 0, 2 * n_na)
    ms64 = _group_meansq(n_chunks, g64_ref, 1.0 / HEAD_DIM)
    for c in range(2 * n_na):
        gain = nq_g_ref[...] if c < n_na else nk_g_ref[...]
        y = _norm_rope_chunk(n_chunks[c], ms64[c], gain, None, None, None)
        o_ref[:, ODD_NQ + c * CHUNK:ODD_NQ + (c + 1) * CHUNK] = y.astype(BF16)
    o_ref[:, ODD_NV:ODD_COLS] = rest[:, ODD_W_NV - ODD_W_NQ:ODD_W_SLOT - ODD_W_NQ].astype(BF16)


def _proj_odd(xa, g, mod, layer, w, wq, wkv, gains, g64, g128, cos, sin, bpb, nbatch):
    na, d = xa.shape
    tok_specs = _stream_specs((xa,), d, bpb * nbatch) + [_whole(g.shape)] + _mod_specs(mod, layer, (SHIFT1, SCALE1))
    consts = (w, wq, wkv) + tuple(gains) + (g64, g128)
    return pl.pallas_call(
        functools.partial(_proj_odd_kernel, bpb=bpb, nbatch=nbatch),
        grid=(na // TM,),
        in_specs=tok_specs + [_whole(a.shape) for a in consts] + [_rope_spec(bpb, nbatch)] * 2,
        out_specs=pl.BlockSpec((TM, ODD_COLS), lambda i: (i, 0)),
        out_shape=jax.ShapeDtypeStruct((na, ODD_COLS), BF16),
        compiler_params=_params("parallel"),
        name="proj_odd",
    )(xa, g, mod, mod, *consts, cos, sin)


def _attn_kernel(*refs, mode, split, main, seq, has_sink, lam_init, tq, tq_sub):
    n_sub = tq // tq_sub
    it = iter(refs)
    q_ref = next(it)
    k_ref = v_ref = bias_ref = sink_ref = lam_ref = subln_ref = None
    if main is not None:
        k_ref, v_ref = next(it), next(it)
    kc_ref, vc_ref = next(it), next(it)
    if main == "na":
        bias_ref = next(it)
    if has_sink:
        sink_ref = next(it)
    safe_ref = next(it)
    if mode == "diff":
        lam_ref, subln_ref = next(it), next(it)
    o_ref = next(it)
    vt_ref, vct_ref = (next(it), next(it)) if main == "full" else (None, None)

    pair = pl.program_id(1)
    qi = pl.program_id(2)
    q = q_ref[...]
    low = lax.broadcasted_iota(jnp.int32, (1, LANES), 1) < HALF
    if split == "mask":
        zero = jnp.zeros_like(q)
        qs = (jnp.where(low, q, zero), jnp.where(low, zero, q))
        k_of = lambda k, s: k
    else:
        qs = (q[:, :LANES], q[:, LANES:])
        k_of = lambda k, s: k[:, s * LANES:(s + 1) * LANES]
    q_of = lambda s, t: qs[s][t * tq_sub:(t + 1) * tq_sub]
    sinks = [sink_ref[2 * pair + s] * LOG2E for s in range(2)] if has_sink else None
    nt_dims = (((1,), (1,)), ((), ()))

    def finish(o_lo, o_hi):
        if mode == "diff":
            lv = lam_ref[...]
            lam = (jnp.exp(jnp.sum(lv[0:1] * lv[1:2], axis=-1, keepdims=True))
                   - jnp.exp(jnp.sum(lv[2:3] * lv[3:4], axis=-1, keepdims=True)) + lam_init)
            o = o_lo - lam * o_hi
            ms = jnp.mean(o * o, axis=-1, keepdims=True)
            o = o * lax.rsqrt(ms + EPS) * subln_ref[...] * (1.0 - lam_init)
        else:
            o = jnp.where(low, o_lo, o_hi)
        o_ref[...] = o.astype(BF16)

    window_masks = {}

    def local_keys(t):
        if main == "window":
            span = tq_sub + 2 * WINDOW
            q0 = qi * tq + t * tq_sub
            at_edge = t in (0, n_sub - 1)
            w0 = jnp.clip(q0 - WINDOW, 0, seq - span) if at_edge else q0 - WINDOW
            off = pl.multiple_of(w0, WINDOW)
            if at_edge or "inner" not in window_masks:
                key = lax.broadcasted_iota(jnp.int32, (span, tq_sub), 0)
                qry = lax.broadcasted_iota(jnp.int32, (span, tq_sub), 1)
                inside = jnp.abs(key - qry + ((w0 - q0) if at_edge else -WINDOW)) <= WINDOW
                if not at_edge:
                    window_masks["inner"] = inside
            else:
                inside = window_masks["inner"]
            logit_fn = lambda lg_t, s: jnp.where(inside, lg_t, NEG)
        else:
            rows_q = tq_sub // GRID_W
            span_rows = rows_q + NA_ROWS
            w0 = jnp.clip(qi * (tq // GRID_W) + t * rows_q - NA_ROWS // 2, 0, seq // GRID_W - span_rows)
            off = pl.multiple_of(w0 * GRID_W, GRID_W)
            span = span_rows * GRID_W
            cls = 1
            if t == 0:
                cls = jnp.where(qi == 0, 0, cls)
            if t == n_sub - 1:
                cls = jnp.where(qi == pl.num_programs(2) - 1, 2, cls)
            logit_fn = lambda lg_t, s: lg_t + bias_ref[cls, s]
        return k_ref[pl.ds(off, span), :], v_ref[pl.ds(off, span), :], logit_fn

    def online_sub(t):
        def init(s):
            if has_sink:
                m0 = jnp.full((tq_sub, 1), sinks[s], F32)
                l0 = jnp.ones((tq_sub, 1), F32)
            else:
                m0 = jnp.full((tq_sub, 1), NEG, F32)
                l0 = jnp.zeros((tq_sub, 1), F32)
            return m0, l0, jnp.zeros((tq_sub, LANES), F32)

        def segment(state, k, v, logit_fn):
            out = []
            for s in range(2):
                m, l, acc = state[s]
                logits = lax.dot_general(q_of(s, t), k_of(k, s), nt_dims, preferred_element_type=F32)
                if logit_fn is not None:
                    logits = logit_fn(logits.T, s).T
                m_new = jnp.maximum(m, jnp.max(logits, axis=-1, keepdims=True))
                alpha = jnp.exp2(m - m_new)
                p = jnp.exp2(logits - m_new)
                l = alpha * l + jnp.sum(p, axis=-1, keepdims=True)
                acc = alpha * acc + jnp.dot(p.astype(BF16), v, preferred_element_type=F32)
                out.append((m_new, l, acc))
            return tuple(out)

        state = (init(0), init(1))
        if main == "full":
            def body(c, st):
                off = pl.multiple_of(c * TK_FULL, TK_FULL)
                return segment(st, k_ref[pl.ds(off, TK_FULL), :], v_ref[pl.ds(off, TK_FULL), :], None)
            state = lax.fori_loop(0, seq // TK_FULL, body, state)
        elif main is not None:
            state = segment(state, *local_keys(t))
        (_, l_lo, acc_lo), (_, l_hi, acc_hi) = segment(state, kc_ref[...], vc_ref[...], None)
        return acc_lo / l_lo, acc_hi / l_hi

    def probs_t(k, q_rows, s, logit_fn):
        logits_t = lax.dot_general(k_of(k, s), q_rows, nt_dims, preferred_element_type=F32)
        if logit_fn is not None:
            logits_t = logit_fn(logits_t, s)
        p = jnp.exp2(logits_t)
        return p.astype(BF16), jnp.sum(p, axis=0, keepdims=True)

    def max_free():
        window_masks.clear()
        if main == "full":
            @pl.when(qi == 0)
            def _():
                for c in range(seq // TK_MAX_FREE):
                    vt_ref[c] = v_ref[c * TK_MAX_FREE:(c + 1) * TK_MAX_FREE, :].T
                vct_ref[...] = vc_ref[...].T

            acc_lat = [jnp.zeros((LANES, tq), F32)] * 2
            den_lat = [jnp.zeros((1, tq), F32)] * 2
            for c in range(seq // TK_MAX_FREE):
                k = k_ref[c * TK_MAX_FREE:(c + 1) * TK_MAX_FREE, :]
                for s in range(2):
                    p, p_sum = probs_t(k, qs[s], s, None)
                    acc_lat[s] = acc_lat[s] + jnp.dot(vt_ref[c], p, preferred_element_type=F32)
                    den_lat[s] = den_lat[s] + p_sum
            vct = vct_ref[...]
        else:
            vct = vc_ref[...].T
        p_ctx = [probs_t(kc_ref[...], qs[s], s, None) for s in range(2)]
        if main in ("window", "na"):
            windows = [local_keys(t) for t in range(n_sub)]
            p_loc = [[probs_t(k, q_of(s, t), s, fn) for s in range(2)] for t, (k, _, fn) in enumerate(windows)]
            vts = [v.T for _, v, _ in windows]
        outs = []
        for s in range(2):
            acc = jnp.dot(vct, p_ctx[s][0], preferred_element_type=F32)
            den = p_ctx[s][1]
            if main == "full":
                acc, den = acc + acc_lat[s], den + den_lat[s]
            elif main is not None:
                acc = acc + jnp.concatenate(
                    [jnp.dot(vts[t], p_loc[t][s][0], preferred_element_type=F32) for t in range(n_sub)], axis=1)
                den = den + jnp.concatenate([p_loc[t][s][1] for t in range(n_sub)], axis=1)
            if has_sink:
                den = den + jnp.exp2(jnp.full((1, tq), sinks[s], F32))
            outs.append((acc / den).T)
        finish(*outs)

    def online():
        window_masks.clear()
        parts = [online_sub(t) for t in range(n_sub)]
        finish(*[jnp.concatenate([p[s] for p in parts], axis=0) if n_sub > 1 else parts[0][s]
                 for s in range(2)])

    safe = safe_ref[0] == 1
    pl.when(safe)(max_free)
    pl.when(jnp.logical_not(safe))(online)


def _attention(p, *, name, mode, split, main, n_pairs, q_blk, k_blk, v_blk, out_cols, ctx_queries,
               nbatch, seq, ctx_len, safe, bias=None, sink=None, lam=None, subln=None, lam_init=0.0):
    n_lat = nbatch * seq
    qw = LANES if split == "mask" else 2 * LANES
    if ctx_queries:
        tq, nq = ctx_len, 1
        q_row = lambda b, j, i: n_lat // ctx_len + b
        out_rows = nbatch * ctx_len
        out_row = lambda b, j, i: b
    else:
        tq = {"full": TQ_FULL, "window": TQ_WINDOW}.get(main, TQ)
        nq = seq // tq
        q_row = lambda b, j, i: b * nq + i
        out_rows = n_lat
        out_row = q_row
    ctx_row = lambda b, j, i: n_lat // ctx_len + b

    in_specs = [pl.BlockSpec((tq, qw), lambda b, j, i: (q_row(b, j, i), q_blk(j)))]
    args = [p]
    if main is not None:
        in_specs += [pl.BlockSpec((seq, qw), lambda b, j, i: (b, k_blk(j))),
                     pl.BlockSpec((seq, LANES), lambda b, j, i: (b, v_blk(j)))]
        args += [p, p]
    in_specs += [pl.BlockSpec((ctx_len, qw), lambda b, j, i: (ctx_row(b, j, i), k_blk(j))),
                 pl.BlockSpec((ctx_len, LANES), lambda b, j, i: (ctx_row(b, j, i), v_blk(j)))]
    args += [p, p]
    tq_sub = TQ_SUB if main in ("window", "na") else tq
    n_sub = tq // tq_sub
    if main == "na":
        span = (tq_sub // GRID_W + NA_ROWS) * GRID_W
        in_specs.append(pl.BlockSpec((3, 2, span, tq_sub), lambda b, j, i: (0, j, 0, 0)))
        args.append(bias)
    if sink is not None:
        in_specs.append(pl.BlockSpec(memory_space=pltpu.SMEM))
        args.append(sink)
    in_specs.append(pl.BlockSpec(memory_space=pltpu.SMEM))
    args.append(safe)
    if mode == "diff":
        in_specs += [_whole(lam.shape), _whole(subln.shape)]
        args += [lam, subln]

    kern = functools.partial(_attn_kernel, mode=mode, split=split, main=main, seq=seq,
                             has_sink=sink is not None, lam_init=lam_init, tq=tq, tq_sub=tq_sub)
    scratch = []
    if main == "full":
        scratch = [pltpu.VMEM((seq // TK_MAX_FREE, LANES, TK_MAX_FREE), BF16), pltpu.VMEM((LANES, ctx_len), BF16)]
    return pl.pallas_call(
        kern,
        grid=(nbatch, n_pairs, nq),
        in_specs=in_specs,
        out_specs=pl.BlockSpec((tq, LANES), lambda b, j, i: (out_row(b, j, i), j)),
        out_shape=jax.ShapeDtypeStruct((out_rows, out_cols), BF16),
        scratch_shapes=scratch,
        compiler_params=_params("parallel", "parallel", "arbitrary"),
        name=name,
    )(*args)


def _na_bias_kernel(rpb_ref, o_ref, *, n_rows, rows_q):
    h = pl.program_id(0)
    n_dr = 2 * NA_ROWS - 1
    n_dc = 2 * NA_COLS - 1
    kc = lax.broadcasted_iota(jnp.int32, (GRID_W, LANES), 0)
    cq = lax.broadcasted_iota(jnp.int32, (GRID_W, LANES), 1) & (GRID_W - 1)
    dc_idx = jnp.clip(kc - cq, -(NA_COLS - 1), NA_COLS - 1) + NA_COLS - 1
    cs = jnp.clip(cq - NA_COLS // 2, 0, GRID_W - NA_COLS)
    col_ok = (kc >= cs) & (kc < cs + NA_COLS)
    tiles = [jnp.zeros((GRID_W, LANES), F32) for _ in range(n_dr)]
    for dc in range(n_dc):
        hit = dc_idx == dc
        for dr in range(n_dr):
            tiles[dr] = jnp.where(hit, rpb_ref[(h * n_dr + dr) * n_dc + dc] * LOG2E, tiles[dr])
    tiles = [jnp.where(col_ok, t, NEG) for t in tiles]
    masked = jnp.full((GRID_W, LANES), NEG, F32)
    low = lax.broadcasted_iota(jnp.int32, (GRID_W, LANES), 1) < HALF

    span_rows = rows_q + NA_ROWS
    first_q_row = (0, span_rows, n_rows - rows_q)
    for c in range(3):
        r0 = first_q_row[c]
        w0 = min(max(r0 - NA_ROWS // 2, 0), n_rows - span_rows)
        for kr in range(span_rows):
            k_abs = w0 + kr
            for m in range(rows_q // 2):
                halves = []
                for r in (r0 + 2 * m, r0 + 2 * m + 1):
                    rs = min(max(r - NA_ROWS // 2, 0), n_rows - NA_ROWS)
                    halves.append(tiles[k_abs - r + NA_ROWS - 1] if rs <= k_abs < rs + NA_ROWS else masked)
                o_ref[c, 0, kr * GRID_W:(kr + 1) * GRID_W, m * LANES:(m + 1) * LANES] = (
                    jnp.where(low, halves[0], halves[1]))


def _na_bias(rpb, seq):
    n_rows = seq // GRID_W
    rows_q = TQ_SUB // GRID_W
    span = (rows_q + NA_ROWS) * GRID_W
    kern = functools.partial(_na_bias_kernel, n_rows=n_rows, rows_q=rows_q)
    return pl.pallas_call(
        kern,
        grid=(NA_HEADS,),
        in_specs=[pl.BlockSpec(memory_space=pltpu.SMEM)],
        out_specs=pl.BlockSpec((3, 1, span, TQ_SUB), lambda h: (0, h, 0, 0)),
        out_shape=jax.ShapeDtypeStruct((3, NA_HEADS, span, TQ_SUB), F32),
        compiler_params=_params("parallel"),
        name="na_bias",
    )(rpb.reshape(-1))


def _post_kernel(*refs, n_x, n_o, bpb, nbatch):
    n_lat_blocks = bpb * nbatch
    x_refs, oa_refs, ob_refs = refs[:n_x], refs[n_x:n_x + n_o], refs[n_x + n_o:n_x + 2 * n_o]
    wo_ref, g1_ref, n2_ref, sh_ref, sc_ref, g2_ref, w1_ref, w2_ref, o_ref = refs[n_x + 2 * n_o:]
    oa = _stream_block(oa_refs, n_lat_blocks)
    ob = _stream_block(ob_refs, n_lat_blocks)
    half = oa.shape[1]
    y = (jnp.dot(oa, wo_ref[:half, :], preferred_element_type=F32)
         + jnp.dot(ob, wo_ref[half:, :], preferred_element_type=F32))
    x1 = _stream_block(x_refs, n_lat_blocks) + _mod_row(g1_ref, bpb, nbatch) * y
    h = _modulated(x1, n2_ref, _mod_row(sh_ref, bpb, nbatch), _mod_row(sc_ref, bpb, nbatch)).astype(BF16)
    a = jnp.dot(h, w1_ref[...], preferred_element_type=F32)
    a = jnp.square(jnp.maximum(a, 0.0)).astype(BF16)
    o_ref[...] = x1 + _mod_row(g2_ref, bpb, nbatch) * jnp.dot(a, w2_ref[...], preferred_element_type=F32)


def _post(x_src, oa_src, ob_src, wo, mod, n2, w1, w2, layer, n_blocks, bpb, nbatch):
    d = x_src[0].shape[1]
    half = oa_src[0].shape[1]
    n_lat_blocks = bpb * nbatch
    g1_spec, sh_spec, sc_spec, g2_spec = _mod_specs(mod, layer, (GATE1, SHIFT2, SCALE2, GATE2))
    resident = lambda a: pl.BlockSpec((None,) + a.shape[1:], lambda i: (layer, 0, 0),
                                      pipeline_mode=pl.Buffered(1))
    kern = functools.partial(_post_kernel, n_x=len(x_src), n_o=len(oa_src), bpb=bpb, nbatch=nbatch)
    return pl.pallas_call(
        kern,
        grid=(n_blocks,),
        in_specs=(_stream_specs(x_src, d, n_lat_blocks) + _stream_specs(oa_src, half, n_lat_blocks)
                  + _stream_specs(ob_src, half, n_lat_blocks)
                  + [resident(wo), g1_spec, _whole(n2.shape), sh_spec, sc_spec, g2_spec, resident(w1), resident(w2)]),
        out_specs=pl.BlockSpec((TM, d), lambda i: (i, 0)),
        out_shape=jax.ShapeDtypeStruct((n_blocks * TM, d), F32),
        compiler_params=_params("parallel"),
        name="post",
    )(*x_src, *oa_src, *ob_src, wo, mod, n2, mod, mod, mod, w1, w2)


def _block_diag_ones(group):
    idx = np.arange(CHUNK) // group
    return jnp.asarray(idx[:, None] == idx[None, :], dtype=BF16)


def _rope_tables(seq, rot_dim, lane0, pad_rows):
    t = jnp.arange(seq, dtype=jnp.int32)
    row = (t // GRID_W).astype(F32)
    col = (t % GRID_W).astype(F32)
    n_freq = rot_dim // 4
    freqs = jnp.power(ROPE_BASE, -jnp.arange(n_freq, dtype=F32) / n_freq)
    ar = row[:, None] * freqs[None, :]
    ac = col[:, None] * freqs[None, :]
    ang = jnp.concatenate([ar, ar, ac, ac], axis=-1)
    sign = jnp.asarray(np.tile(np.repeat([-1.0, 1.0], n_freq), 2), F32)
    cos, sin = jnp.cos(ang), jnp.sin(ang) * sign[None, :]
    if lane0 is None:
        reps = LANES // rot_dim
        cos, sin = jnp.tile(cos, (1, reps)), jnp.tile(sin, (1, reps))
    else:
        pad = ((0, 0), (lane0, LANES - lane0 - rot_dim))
        cos = jnp.pad(cos, pad, constant_values=1.0)
        sin = jnp.pad(sin, pad)
    cos = jnp.concatenate([cos, jnp.ones((pad_rows, LANES), F32)], axis=0)
    sin = jnp.concatenate([sin, jnp.zeros((pad_rows, LANES), F32)], axis=0)
    return cos, sin


def _logits_bounded(q_gain, k_gain, dim, bias=None, sink=None):
    bound = jnp.max(jnp.abs(q_gain)) * jnp.max(jnp.abs(k_gain)) * (dim ** 0.5) * BF16_ROUNDING_MARGIN
    if bias is not None:
        bound = bound + jnp.max(jnp.abs(bias))
    if sink is not None:
        bound = jnp.maximum(bound, jnp.max(jnp.abs(sink)))
    return (bound <= MAX_FREE_LOGIT_BOUND).astype(jnp.int32).reshape(1)


def _tile_gain(g, reps, scale=1.0):
    return (jnp.tile(g.astype(F32), reps) * scale)[None, :]


def _even_weights(w, q_g, k_g, sq_g, sk_g):
    nq = DIFF_HEADS * 2 * HEAD_DIM
    qa, ka, va = w[:, :nq], w[:, nq:2 * nq], w[:, 2 * nq:3 * nq]
    o = 3 * nq
    qb = w[:, o:o + SWA_Q_HEADS * HEAD_DIM]
    o += SWA_Q_HEADS * HEAD_DIM
    kb = [w[:, o + i * HEAD_DIM:o + (i + 1) * HEAD_DIM] for i in range(SWA_KV_HEADS)]
    o += SWA_KV_HEADS * HEAD_DIM
    vb = [w[:, o + i * HEAD_DIM:o + (i + 1) * HEAD_DIM] for i in range(SWA_KV_HEADS)]
    dup = lambda parts: [p for p in parts for _ in range(2)]
    w_new = jnp.concatenate([qa, ka, va, qb] + dup(kb) + dup(vb), axis=1).astype(BF16)
    scale = HEAD_DIM ** -0.5 * LOG2E
    ones = lambda n: jnp.ones((1, n), F32)
    gain = jnp.concatenate([
        _tile_gain(q_g, 2 * DIFF_HEADS, scale), _tile_gain(k_g, 2 * DIFF_HEADS), ones(nq),
        _tile_gain(sq_g, SWA_Q_HEADS, scale), _tile_gain(sk_g, 2 * SWA_KV_HEADS),
        ones(2 * SWA_KV_HEADS * HEAD_DIM)], axis=1)
    assert w_new.shape[1] == gain.shape[1] == EVEN_COLS
    return w_new, gain


def _odd_weights(w, wq_up, wkv_up, qa_g, kva_g, mq_g, mk_g, nq_g, nk_g):
    d = w.shape[0]
    o = 0
    parts = []
    for n in (MLA_Q_LORA, MLA_KV_LORA, MLA_ROPE, NA_HEADS * HEAD_DIM, NA_HEADS * HEAD_DIM, NA_HEADS * HEAD_DIM):
        parts.append(w[:, o:o + n])
        o += n
    q_a, kv_a, k_r, nq, nk, nv = parts
    slot = jnp.concatenate([jnp.zeros((d, MLA_NOPE), w.dtype), k_r,
                            jnp.zeros((d, LANES - MLA_QK), w.dtype)], axis=1)
    w_new = jnp.concatenate([q_a, kv_a, nq, nk, nv, slot], axis=1).astype(BF16)
    assert w_new.shape[1] == ODD_W_COLS
    wq = jnp.pad(wq_up.reshape(MLA_Q_LORA, MLA_HEADS, MLA_QK), ((0, 0), (0, 0), (0, MLA_PAD - MLA_QK)))
    wq = wq.reshape(MLA_Q_LORA, MLA_HEADS * MLA_PAD).astype(BF16)
    wkv = wkv_up.reshape(MLA_KV_LORA, MLA_HEADS, MLA_NOPE + MLA_V)
    wk = jnp.pad(wkv[..., :MLA_NOPE], ((0, 0), (0, 0), (0, MLA_PAD - MLA_NOPE))).reshape(MLA_KV_LORA, -1)
    wv = wkv[..., MLA_NOPE:].reshape(MLA_KV_LORA, MLA_HEADS * MLA_V)
    wkv_new = jnp.concatenate([wk, wv], axis=1).astype(BF16)
    pad_gain = lambda g, scale: _tile_gain(jnp.pad(g.astype(F32), (0, MLA_PAD - MLA_QK)), CHUNK // MLA_PAD, scale)
    gains = (qa_g.astype(F32)[None, :], kva_g.astype(F32)[None, :],
             pad_gain(mq_g, MLA_QK ** -0.5 * LOG2E), pad_gain(mk_g, 1.0),
             _tile_gain(nq_g, CHUNK // HEAD_DIM, HEAD_DIM ** -0.5 * LOG2E), _tile_gain(nk_g, CHUNK // HEAD_DIM))
    return w_new, wq, wkv_new, gains


def kernel(x, c, ctx, c_ctx, ada_w, ada_b, norm1_g, norm2_g, w_out, mlp_w1, mlp_w2, ev_w_in, diff_q_g,
           diff_k_g, diff_lam, diff_subln_g, swa_q_g, swa_k_g, swa_sink, od_w_in, mla_qa_g, mla_kva_g,
           mla_wq_up, mla_wkv_up, mla_q_g, mla_k_g, na_q_g, na_k_g, na_rpb):
    nbatch, seq, d = x.shape
    ctx_len = ctx.shape[1]
    depth = ada_w.shape[0]
    n_lat = nbatch * seq
    n_ctx = nbatch * ctx_len
    assert seq % TM == 0 and n_ctx == TM and seq % TQ == 0 and seq % TQ_WINDOW == 0 and seq % GRID_W == 0
    assert seq % TQ_FULL == 0 and seq % TK_MAX_FREE == 0 and seq % TK_FULL == 0
    assert seq // GRID_W >= 2 * (NA_ROWS + TQ_SUB // GRID_W) and seq >= TQ_SUB + 2 * WINDOW
    bpb = seq // TM

    mod_rows = 8
    cvec = jnp.concatenate([c, c_ctx[None, :], jnp.zeros((mod_rows - nbatch - 1, d), F32)], axis=0)
    mod = _modulation(cvec, ada_w, ada_b)

    cos64, sin64 = _rope_tables(seq, HEAD_DIM, None, n_ctx)
    cos32, sin32 = _rope_tables(seq, MLA_ROPE, MLA_NOPE, n_ctx)
    g64 = _block_diag_ones(HEAD_DIM)
    g128 = _block_diag_ones(LANES)

    x_src = (x.reshape(n_lat, d), ctx.reshape(n_ctx, d))
    wo_all, w1_all, w2_all = w_out.astype(BF16), mlp_w1.astype(BF16), mlp_w2.astype(BF16)
    common = dict(nbatch=nbatch, seq=seq, ctx_len=ctx_len)

    for layer in range(depth):
        need_ctx = layer < depth - 1
        n_blocks = bpb * nbatch + (1 if need_ctx else 0)
        n1 = norm1_g[layer][None, :]
        n2 = norm2_g[layer][None, :]
        i = layer // 2
        if layer % 2 == 0:
            lam_init = 0.8 - 0.6 * math.exp(-0.3 * layer)
            w_in, gain = _even_weights(ev_w_in[i], diff_q_g[i], diff_k_g[i], swa_q_g[i], swa_k_g[i])
            p = _proj_even(x_src, n1, mod, layer, w_in, gain, g64, cos64, sin64, bpb, nbatch)
            lam = diff_lam[i].astype(F32)
            subln = diff_subln_g[i].astype(F32)[None, :]
            sink = swa_sink[i].astype(F32)
            diff_kw = dict(mode="diff", split="mask", n_pairs=DIFF_HEADS, q_blk=lambda j: EVEN_QA // LANES + j,
                           k_blk=lambda j: EVEN_KA // LANES + j, v_blk=lambda j: EVEN_VA // LANES + j,
                           out_cols=DIFF_HEADS * LANES,
                           lam=lam, subln=subln, lam_init=lam_init,
                           safe=_logits_bounded(diff_q_g[i], diff_k_g[i], HEAD_DIM), **common)
            swa_kw = dict(mode="pair", split="mask", n_pairs=SWA_Q_HEADS // 2,
                          q_blk=lambda j: EVEN_QB // LANES + j, k_blk=lambda j: EVEN_KB // LANES + j // 2,
                          v_blk=lambda j: EVEN_VB // LANES + j // 2,
                          out_cols=SWA_Q_HEADS * HEAD_DIM, sink=sink,
                          safe=_logits_bounded(swa_q_g[i], swa_k_g[i], HEAD_DIM, sink=sink), **common)
            oa = _attention(p, name="diff_attn", main="full", ctx_queries=False, **diff_kw)
            ob = _attention(p, name="window_attn", main="window", ctx_queries=False, **swa_kw)
            if need_ctx:
                oa_c = _attention(p, name="diff_attn_ctx", main=None, ctx_queries=True, **diff_kw)
                ob_c = _attention(p, name="window_attn_ctx", main=None, ctx_queries=True, **swa_kw)
        else:
            w_in, wq, wkv, gains = _odd_weights(od_w_in[i], mla_wq_up[i], mla_wkv_up[i], mla_qa_g[i],
                                                mla_kva_g[i], mla_q_g[i], mla_k_g[i], na_q_g[i], na_k_g[i])
            p = _proj_odd(x_src[0], n1, mod, layer, w_in, wq, wkv, gains, g64, g128, cos32, sin32, bpb, nbatch)
            bias = _na_bias(na_rpb[i].astype(F32), seq)
            mla_kw = dict(mode="pair", split="slice", n_pairs=MLA_HEADS // 2, q_blk=lambda j: j,
                          k_blk=lambda j: ODD_K // CHUNK + j, v_blk=lambda j: ODD_V // LANES + j,
                          out_cols=MLA_HEADS * MLA_V,
                          safe=_logits_bounded(mla_q_g[i], mla_k_g[i], MLA_QK), **common)
            na_kw = dict(mode="pair", split="mask", n_pairs=NA_HEADS // 2,
                         q_blk=lambda j: ODD_NQ // LANES + j, k_blk=lambda j: ODD_NK // LANES + j,
                         v_blk=lambda j: ODD_NV // LANES + j, out_cols=NA_HEADS * HEAD_DIM,
                         safe=_logits_bounded(na_q_g[i], na_k_g[i], HEAD_DIM, bias=na_rpb[i]), **common)
            oa = _attention(p, name="mla_attn", main="full", ctx_queries=False, **mla_kw)
            ob = _attention(p, name="na_attn", main="na", ctx_queries=False, bias=bias, **na_kw)
            if need_ctx:
                oa_c = _attention(p, name="mla_attn_ctx", main=None, ctx_queries=True, **mla_kw)
                ob_c = _attention(p, name="na_attn_ctx", main=None, ctx_queries=True, **na_kw)
        oa_src, ob_src = ((oa, oa_c), (ob, ob_c)) if need_ctx else ((oa,), (ob,))
        x_src = (_post(x_src, oa_src, ob_src, wo_all, mod, n2, w1_all, w2_all, layer, n_blocks, bpb, nbatch),)
    return x_src[0].reshape(nbatch, seq, d)
```

```python
import functools
import math

import numpy as np
import jax
import jax.numpy as jnp
from jax import lax
from jax.experimental import pallas as pl
from jax.experimental.pallas import tpu as pltpu

F32 = jnp.float32
BF16 = jnp.bfloat16

LANES = 128
HALF = LANES // 2
CHUNK = 2 * LANES
GRID_W = 64
HEAD_DIM = 64
EPS = 1e-6
ROPE_BASE = 10000.0
DIFF_HEADS = 4
SWA_Q_HEADS = 8
SWA_KV_HEADS = 2
WINDOW = 128
MLA_HEADS = 8
MLA_Q_LORA = 512
MLA_KV_LORA = 256
MLA_NOPE = 64
MLA_ROPE = 32
MLA_QK = MLA_NOPE + MLA_ROPE
MLA_V = 64
NA_HEADS = 8
NA_ROWS = 8
NA_COLS = 16
NEG = -1e30
LOG2E = math.log2(math.e)
MAX_FREE_LOGIT_BOUND = 50.0
BF16_ROUNDING_MARGIN = 1.02

TM = 512
TQ = 2048
TQ_SUB = 256
TQ_FULL = 1024
TK_FULL = 512
TK_MAX_FREE = 1024
MOD_TN = 1536
VMEM_LIMIT = 56 * 1024 * 1024


def _params(*sem):
    return pltpu.CompilerParams(dimension_semantics=sem, vmem_limit_bytes=VMEM_LIMIT)


def _mod_kernel(c_ref, w_ref, b_ref, o_ref):
    c = c_ref[...]
    a = (c * jax.nn.sigmoid(c)).astype(BF16)
    w = w_ref[0].astype(BF16)
    o_ref[0] = jnp.dot(a, w, preferred_element_type=F32) + b_ref[0]


def _modulation(cvec, ada_w, ada_b):
    depth, d, n = ada_w.shape
    rows = cvec.shape[0]
    return pl.pallas_call(
        _mod_kernel,
        grid=(depth, n // MOD_TN),
        in_specs=[
            pl.BlockSpec((rows, d), lambda l, j: (0, 0)),
            pl.BlockSpec((1, d, MOD_TN), lambda l, j: (l, 0, j)),
            pl.BlockSpec((1, 1, MOD_TN), lambda l, j: (l, 0, j)),
        ],
        out_specs=pl.BlockSpec((1, rows, MOD_TN), lambda l, j: (l, 0, j)),
        out_shape=jax.ShapeDtypeStruct((depth, rows, n), F32),
        compiler_params=_params("parallel", "parallel"),
        name="modulation",
    )(cvec, ada_w, ada_b.reshape(depth, 1, n))


def _modulated(x, g_ref, shift, scale):
    ms = jnp.mean(x * x, axis=-1, keepdims=True)
    y = x * lax.rsqrt(ms + EPS) * g_ref[...]
    return y * (1.0 + scale) + shift


def _group_meansq(chunks, g_ref, inv_n):
    rows = chunks[0].shape[0]
    sq = jnp.concatenate([(c * c).astype(BF16) for c in chunks], axis=0)
    ss = jnp.dot(sq, g_ref[...], preferred_element_type=F32) * inv_n
    return [ss[i * rows:(i + 1) * rows] for i in range(len(chunks))]


def _swap_quarters(x, quarter):
    lane = lax.broadcasted_iota(jnp.int32, x.shape, 1)
    odd = (lane & quarter) != 0
    from_lower = pltpu.roll(x, quarter, 1)
    from_upper = pltpu.roll(x, LANES - quarter, 1)
    return jnp.where(odd, from_lower, from_upper)


def _rope(x, cos, sin_signed, quarter):
    return x * cos + _swap_quarters(x, quarter) * sin_signed


def _norm_rope_chunk(acc, meansq, gain, cos, sin, quarter):
    y = acc * lax.rsqrt(meansq + EPS) * gain
    if quarter is None:
        return y
    return jnp.concatenate([_rope(y[:, :LANES], cos, sin, quarter),
                            _rope(y[:, LANES:], cos, sin, quarter)], axis=1)


def _stream_specs(src, cols, n_lat_blocks):
    if len(src) == 1:
        return [pl.BlockSpec((TM, cols), lambda i: (i, 0))]
    return [pl.BlockSpec((TM, cols), lambda i: (jnp.minimum(i, n_lat_blocks - 1), 0)),
            pl.BlockSpec((TM, cols), lambda i: (0, 0))]


def _stream_block(refs, n_lat_blocks):
    if len(refs) == 1:
        return refs[0][...]
    return jnp.where(pl.program_id(0) < n_lat_blocks, refs[0][...], refs[1][...])


SHIFT1, SCALE1, GATE1, SHIFT2, SCALE2, GATE2 = range(6)


def _mod_specs(mod, layer, chunks):
    d = mod.shape[2] // 6
    return [pl.BlockSpec((None, mod.shape[1], d), lambda i, k=k: (layer, 0, k)) for k in chunks]


def _mod_row(ref, bpb, nbatch):
    row = jnp.minimum(pl.program_id(0) // bpb, nbatch)
    return ref[pl.ds(row, 1), :]


def _rope_spec(bpb, nbatch):
    return pl.BlockSpec((TM, LANES), lambda i: (jnp.where(i < bpb * nbatch, i % bpb, bpb), 0))


def _whole(shape):
    return pl.BlockSpec(shape, lambda *_: (0,) * len(shape))


EVEN_QA = 0
EVEN_KA = EVEN_QA + DIFF_HEADS * 2 * HEAD_DIM
EVEN_VA = EVEN_KA + DIFF_HEADS * 2 * HEAD_DIM
EVEN_QB = EVEN_VA + DIFF_HEADS * 2 * HEAD_DIM
EVEN_KB = EVEN_QB + SWA_Q_HEADS * HEAD_DIM
EVEN_VB = EVEN_KB + 2 * SWA_KV_HEADS * HEAD_DIM
EVEN_COLS = EVEN_VB + 2 * SWA_KV_HEADS * HEAD_DIM
EVEN_NORM_CHUNKS = tuple(c for c in range(EVEN_COLS // CHUNK)
                         if c * CHUNK < EVEN_VA or EVEN_QB <= c * CHUNK < EVEN_VB)
EVEN_DOT_RANGES = ((EVEN_QA // CHUNK, EVEN_VA // CHUNK), (EVEN_QB // CHUNK, EVEN_VB // CHUNK),
                   (EVEN_VA // CHUNK, EVEN_QB // CHUNK), (EVEN_VB // CHUNK, EVEN_COLS // CHUNK))


def _proj_even_kernel(*refs, n_x, bpb, nbatch):
    n_lat_blocks = bpb * nbatch
    x_refs = refs[:n_x]
    g_ref, sh_ref, sc_ref, w_ref, gain_ref, g64_ref, cos_ref, sin_ref, o_ref = refs[n_x:]
    h = _modulated(_stream_block(x_refs, n_lat_blocks), g_ref, _mod_row(sh_ref, bpb, nbatch),
                   _mod_row(sc_ref, bpb, nbatch)).astype(BF16)
    cos = cos_ref[...]
    sin = sin_ref[...]
    for first, last in EVEN_DOT_RANGES:
        full = jnp.dot(h, w_ref[:, first * CHUNK:last * CHUNK], preferred_element_type=F32)
        chunk = lambda c: full[:, (c - first) * CHUNK:(c - first + 1) * CHUNK]
        normed = [c for c in range(first, last) if c in EVEN_NORM_CHUNKS]
        meansq = {}
        if normed:
            meansq = dict(zip(normed, _group_meansq([chunk(c) for c in normed], g64_ref, 1.0 / HEAD_DIM)))
        for c in range(first, last):
            cols = slice(c * CHUNK, (c + 1) * CHUNK)
            acc = chunk(c)
            if c in EVEN_NORM_CHUNKS:
                acc = _norm_rope_chunk(acc, meansq[c], gain_ref[:, cols], cos, sin, HEAD_DIM // 4)
            o_ref[:, cols] = acc.astype(BF16)


def _proj_even(x_src, g, mod, layer, w, gain, g64, cos, sin, bpb, nbatch):
    d = x_src[0].shape[1]
    n_lat_blocks = bpb * nbatch
    kern = functools.partial(_proj_even_kernel, n_x=len(x_src), bpb=bpb, nbatch=nbatch)
    return pl.pallas_call(
        kern,
        grid=(n_lat_blocks + 1,),
        in_specs=(_stream_specs(x_src, d, n_lat_blocks) + [_whole(g.shape)]
                  + _mod_specs(mod, layer, (SHIFT1, SCALE1))
                  + [_whole(w.shape), _whole(gain.shape), _whole(g64.shape),
                     _rope_spec(bpb, nbatch), _rope_spec(bpb, nbatch)]),
        out_specs=pl.BlockSpec((TM, EVEN_COLS), lambda i: (i, 0)),
        out_shape=jax.ShapeDtypeStruct(((n_lat_blocks + 1) * TM, EVEN_COLS), BF16),
        compiler_params=_params("parallel"),
        name="proj_even",
    )(*x_src, g, mod, mod, w, gain, g64, cos, sin)


NA_WIDTH = NA_HEADS * HEAD_DIM
MLA_PAD = LANES
MLA_WIDTH = MLA_HEADS * MLA_PAD
ODD_W_KVA = MLA_Q_LORA
ODD_W_NQ = ODD_W_KVA + MLA_KV_LORA
ODD_W_NV = ODD_W_NQ + 2 * NA_WIDTH
ODD_W_SLOT = ODD_W_NV + NA_WIDTH
ODD_W_COLS = ODD_W_SLOT + LANES
ODD_K = MLA_WIDTH
ODD_V = ODD_K + MLA_WIDTH
ODD_NQ = ODD_V + MLA_HEADS * MLA_V
ODD_NK = ODD_NQ + NA_WIDTH
ODD_NV = ODD_NK + NA_WIDTH
ODD_COLS = ODD_NV + NA_WIDTH


def _proj_odd_kernel(x_ref, g_ref, sh_ref, sc_ref, w_ref, wq_ref, wkv_ref, qa_g_ref, kva_g_ref,
                     mq_g_ref, mk_g_ref, nq_g_ref, nk_g_ref, g64_ref, g128_ref, cos_ref, sin_ref, o_ref,
                     *, bpb, nbatch):
    h = _modulated(x_ref[...], g_ref, _mod_row(sh_ref, bpb, nbatch), _mod_row(sc_ref, bpb, nbatch)).astype(BF16)
    cos = cos_ref[...]
    sin = sin_ref[...]
    quarter = MLA_ROPE // 4

    lora = jnp.dot(h, w_ref[:, :ODD_W_NQ], preferred_element_type=F32)
    rest = jnp.dot(h, w_ref[:, ODD_W_NQ:], preferred_element_type=F32)

    def low_rank(a, gain_ref):
        ms = jnp.mean(a * a, axis=-1, keepdims=True)
        return (a * lax.rsqrt(ms + EPS) * gain_ref[...]).astype(BF16)

    n_mla = MLA_WIDTH // CHUNK
    chunks_of = lambda a, first, n: [a[:, first + c * CHUNK:first + (c + 1) * CHUNK] for c in range(n)]
    qa = low_rank(lora[:, :MLA_Q_LORA], qa_g_ref)
    qf = jnp.dot(qa, wq_ref[...], preferred_element_type=F32)
    q_chunks = chunks_of(qf, 0, n_mla)
    ms_q = _group_meansq(q_chunks, g128_ref, 1.0 / MLA_QK)
    kva = low_rank(lora[:, ODD_W_KVA:], kva_g_ref)
    kvf = jnp.dot(kva, wkv_ref[...], preferred_element_type=F32)
    kslot = rest[:, ODD_W_SLOT - ODD_W_NQ:]
    kslot2 = jnp.concatenate([kslot, kslot], axis=1)
    k_rot = _swap_quarters(kslot * mk_g_ref[:, :LANES], quarter)
    k_rot2 = jnp.concatenate([k_rot, k_rot], axis=1)
    cos2 = jnp.concatenate([cos, cos], axis=1)
    sin2 = jnp.concatenate([sin, sin], axis=1)
    k_chunks = [kc + kslot2 for kc in chunks_of(kvf, 0, n_mla)]
    ms_k = _group_meansq(k_chunks, g128_ref, 1.0 / MLA_QK)
    for c in range(n_mla):
        y = _norm_rope_chunk(q_chunks[c], ms_q[c], mq_g_ref[...], cos, sin, quarter)
        o_ref[:, c * CHUNK:(c + 1) * CHUNK] = y.astype(BF16)
    for c in range(n_mla):
        inv_rms = lax.rsqrt(ms_k[c] + EPS)
        y = (k_chunks[c] * inv_rms * mk_g_ref[...]) * cos2 + (inv_rms * k_rot2) * sin2
        o_ref[:, ODD_K + c * CHUNK:ODD_K + (c + 1) * CHUNK] = y.astype(BF16)
    o_ref[:, ODD_V:ODD_NQ] = kvf[:, MLA_WIDTH:].astype(BF16)

    n_na = NA_WIDTH // CHUNK
    n_chunks = chunks_of(rest, 0, 2 * n_na)
    ms64 = _group_meansq(n_chunks, g64_ref, 1.0 / HEAD_DIM)
    for c in range(2 * n_na):
        gain = nq_g_ref[...] if c < n_na else nk_g_ref[...]
        y = _norm_rope_chunk(n_chunks[c], ms64[c], gain, None, None, None)
        o_ref[:, ODD_NQ + c * CHUNK:ODD_NQ + (c + 1) * CHUNK] = y.astype(BF16)
    o_ref[:, ODD_NV:ODD_COLS] = rest[:, ODD_W_NV - ODD_W_NQ:ODD_W_SLOT - ODD_W_NQ].astype(BF16)


def _proj_odd(xa, g, mod, layer, w, wq, wkv, gains, g64, g128, cos, sin, bpb, nbatch):
    na, d = xa.shape
    tok_specs = _stream_specs((xa,), d, bpb * nbatch) + [_whole(g.shape)] + _mod_specs(mod, layer, (SHIFT1, SCALE1))
    consts = (w, wq, wkv) + tuple(gains) + (g64, g128)
    return pl.pallas_call(
        functools.partial(_proj_odd_kernel, bpb=bpb, nbatch=nbatch),
        grid=(na // TM,),
        in_specs=tok_specs + [_whole(a.shape) for a in consts] + [_rope_spec(bpb, nbatch)] * 2,
        out_specs=pl.BlockSpec((TM, ODD_COLS), lambda i: (i, 0)),
        out_shape=jax.ShapeDtypeStruct((na, ODD_COLS), BF16),
        compiler_params=_params("parallel"),
        name="proj_odd",
    )(xa, g, mod, mod, *consts, cos, sin)


def _attn_kernel(*refs, mode, split, main, seq, has_sink, lam_init, tq, tq_sub):
    n_sub = tq // tq_sub
    it = iter(refs)
    q_ref = next(it)
    k_ref = v_ref = bias_ref = sink_ref = lam_ref = subln_ref = None
    if main is not None:
        k_ref, v_ref = next(it), next(it)
    kc_ref, vc_ref = next(it), next(it)
    if main == "na":
        bias_ref = next(it)
    if has_sink:
        sink_ref = next(it)
    safe_ref = next(it)
    if mode == "diff":
        lam_ref, subln_ref = next(it), next(it)
    o_ref = next(it)
    vt_ref, vct_ref = (next(it), next(it)) if main == "full" else (None, None)

    pair = pl.program_id(1)
    qi = pl.program_id(2)
    q = q_ref[...]
    low = lax.broadcasted_iota(jnp.int32, (1, LANES), 1) < HALF
    if split == "mask":
        zero = jnp.zeros_like(q)
        qs = (jnp.where(low, q, zero), jnp.where(low, zero, q))
        k_of = lambda k, s: k
    else:
        qs = (q[:, :LANES], q[:, LANES:])
        k_of = lambda k, s: k[:, s * LANES:(s + 1) * LANES]
    q_of = lambda s, t: qs[s][t * tq_sub:(t + 1) * tq_sub]
    sinks = [sink_ref[2 * pair + s] * LOG2E for s in range(2)] if has_sink else None
    nt_dims = (((1,), (1,)), ((), ()))

    def finish(o_lo, o_hi):
        if mode == "diff":
            lv = lam_ref[...]
            lam = (jnp.exp(jnp.sum(lv[0:1] * lv[1:2], axis=-1, keepdims=True))
                   - jnp.exp(jnp.sum(lv[2:3] * lv[3:4], axis=-1, keepdims=True)) + lam_init)
            o = o_lo - lam * o_hi
            ms = jnp.mean(o * o, axis=-1, keepdims=True)
            o = o * lax.rsqrt(ms + EPS) * subln_ref[...] * (1.0 - lam_init)
        else:
            o = jnp.where(low, o_lo, o_hi)
        o_ref[...] = o.astype(BF16)

    window_masks = {}

    def local_keys(t):
        if main == "window":
            span = tq_sub + 2 * WINDOW
            q0 = qi * tq + t * tq_sub
            at_edge = t in (0, n_sub - 1)
            w0 = jnp.clip(q0 - WINDOW, 0, seq - span) if at_edge else q0 - WINDOW
            off = pl.multiple_of(w0, WINDOW)
            if at_edge or "inner" not in window_masks:
                key = lax.broadcasted_iota(jnp.int32, (span, tq_sub), 0)
                qry = lax.broadcasted_iota(jnp.int32, (span, tq_sub), 1)
                inside = jnp.abs(key - qry + ((w0 - q0) if at_edge else -WINDOW)) <= WINDOW
                if not at_edge:
                    window_masks["inner"] = inside
            else:
                inside = window_masks["inner"]
            logit_fn = lambda lg_t, s: jnp.where(inside, lg_t, NEG)
        else:
            rows_q = tq_sub // GRID_W
            span_rows = rows_q + NA_ROWS
            w0 = jnp.clip(qi * (tq // GRID_W) + t * rows_q - NA_ROWS // 2, 0, seq // GRID_W - span_rows)
            off = pl.multiple_of(w0 * GRID_W, GRID_W)
            span = span_rows * GRID_W
            cls = 1
            if t == 0:
                cls = jnp.where(qi == 0, 0, cls)
            if t == n_sub - 1:
                cls = jnp.where(qi == pl.num_programs(2) - 1, 2, cls)
            logit_fn = lambda lg_t, s: lg_t + bias_ref[cls, s]
        return k_ref[pl.ds(off, span), :], v_ref[pl.ds(off, span), :], logit_fn

    def online_sub(t):
        def init(s):
            if has_sink:
                m0 = jnp.full((tq_sub, 1), sinks[s], F32)
                l0 = jnp.ones((tq_sub, 1), F32)
            else:
                m0 = jnp.full((tq_sub, 1), NEG, F32)
                l0 = jnp.zeros((tq_sub, 1), F32)
            return m0, l0, jnp.zeros((tq_sub, LANES), F32)

        def segment(state, k, v, logit_fn):
            out = []
            for s in range(2):
                m, l, acc = state[s]
                logits = lax.dot_general(q_of(s, t), k_of(k, s), nt_dims, preferred_element_type=F32)
                if logit_fn is not None:
                    logits = logit_fn(logits.T, s).T
                m_new = jnp.maximum(m, jnp.max(logits, axis=-1, keepdims=True))
                alpha = jnp.exp2(m - m_new)
                p = jnp.exp2(logits - m_new)
                l = alpha * l + jnp.sum(p, axis=-1, keepdims=True)
                acc = alpha * acc + jnp.dot(p.astype(BF16), v, preferred_element_type=F32)
                out.append((m_new, l, acc))
            return tuple(out)

        state = (init(0), init(1))
        if main == "full":
            def body(c, st):
                off = pl.multiple_of(c * TK_FULL, TK_FULL)
                return segment(st, k_ref[pl.ds(off, TK_FULL), :], v_ref[pl.ds(off, TK_FULL), :], None)
            state = lax.fori_loop(0, seq // TK_FULL, body, state)
        elif main is not None:
            state = segment(state, *local_keys(t))
        (_, l_lo, acc_lo), (_, l_hi, acc_hi) = segment(state, kc_ref[...], vc_ref[...], None)
        return acc_lo / l_lo, acc_hi / l_hi

    def probs_t(k, q_rows, s, logit_fn):
        logits_t = lax.dot_general(k_of(k, s), q_rows, nt_dims, preferred_element_type=F32)
        if logit_fn is not None:
            logits_t = logit_fn(logits_t, s)
        p = jnp.exp2(logits_t)
        return p.astype(BF16), jnp.sum(p, axis=0, keepdims=True)

    def max_free():
        window_masks.clear()
        if main == "full":
            @pl.when(qi == 0)
            def _():
                for c in range(seq // TK_MAX_FREE):
                    vt_ref[c] = v_ref[c * TK_MAX_FREE:(c + 1) * TK_MAX_FREE, :].T
                vct_ref[...] = vc_ref[...].T

            acc_lat = [jnp.zeros((LANES, tq), F32)] * 2
            den_lat = [jnp.zeros((1, tq), F32)] * 2
            for c in range(seq // TK_MAX_FREE):
                k = k_ref[c * TK_MAX_FREE:(c + 1) * TK_MAX_FREE, :]
                for s in range(2):
                    p, p_sum = probs_t(k, qs[s], s, None)
                    acc_lat[s] = acc_lat[s] + jnp.dot(vt_ref[c], p, preferred_element_type=F32)
                    den_lat[s] = den_lat[s] + p_sum
            vct = vct_ref[...]
        else:
            vct = vc_ref[...].T
        p_ctx = [probs_t(kc_ref[...], qs[s], s, None) for s in range(2)]
        if main in ("window", "na"):
            windows = [local_keys(t) for t in range(n_sub)]
            p_loc = [[probs_t(k, q_of(s, t), s, fn) for s in range(2)] for t, (k, _, fn) in enumerate(windows)]
            vts = [v.T for _, v, _ in windows]
        outs = []
        for s in range(2):
            acc = jnp.dot(vct, p_ctx[s][0], preferred_element_type=F32)
            den = p_ctx[s][1]
            if main == "full":
                acc, den = acc + acc_lat[s], den + den_lat[s]
            elif main is not None:
                acc = acc + jnp.concatenate(
                    [jnp.dot(vts[t], p_loc[t][s][0], preferred_element_type=F32) for t in range(n_sub)], axis=1)
                den = den + jnp.concatenate([p_loc[t][s][1] for t in range(n_sub)], axis=1)
            if has_sink:
                den = den + jnp.exp2(jnp.full((1, tq), sinks[s], F32))
            outs.append((acc / den).T)
        finish(*outs)

    def online():
        window_masks.clear()
        parts = [online_sub(t) for t in range(n_sub)]
        finish(*[jnp.concatenate([p[s] for p in parts], axis=0) if n_sub > 1 else parts[0][s]
                 for s in range(2)])

    safe = safe_ref[0] == 1
    pl.when(safe)(max_free)
    pl.when(jnp.logical_not(safe))(online)


def _attention(p, *, name, mode, split, main, n_pairs, q_blk, k_blk, v_blk, out_cols, ctx_queries,
               nbatch, seq, ctx_len, safe, bias=None, sink=None, lam=None, subln=None, lam_init=0.0):
    n_lat = nbatch * seq
    qw = LANES if split == "mask" else 2 * LANES
    if ctx_queries:
        tq, nq = ctx_len, 1
        q_row = lambda b, j, i: n_lat // ctx_len + b
        out_rows = nbatch * ctx_len
        out_row = lambda b, j, i: b
    else:
        tq = TQ_FULL if main == "full" else TQ
        nq = seq // tq
        q_row = lambda b, j, i: b * nq + i
        out_rows = n_lat
        out_row = q_row
    ctx_row = lambda b, j, i: n_lat // ctx_len + b

    in_specs = [pl.BlockSpec((tq, qw), lambda b, j, i: (q_row(b, j, i), q_blk(j)))]
    args = [p]
    if main is not None:
        in_specs += [pl.BlockSpec((seq, qw), lambda b, j, i: (b, k_blk(j))),
                     pl.BlockSpec((seq, LANES), lambda b, j, i: (b, v_blk(j)))]
        args += [p, p]
    in_specs += [pl.BlockSpec((ctx_len, qw), lambda b, j, i: (ctx_row(b, j, i), k_blk(j))),
                 pl.BlockSpec((ctx_len, LANES), lambda b, j, i: (ctx_row(b, j, i), v_blk(j)))]
    args += [p, p]
    tq_sub = TQ_SUB if main in ("window", "na") else tq
    n_sub = tq // tq_sub
    if main == "na":
        span = (tq_sub // GRID_W + NA_ROWS) * GRID_W
        in_specs.append(pl.BlockSpec((3, 2, span, tq_sub), lambda b, j, i: (0, j, 0, 0)))
        args.append(bias)
    if sink is not None:
        in_specs.append(pl.BlockSpec(memory_space=pltpu.SMEM))
        args.append(sink)
    in_specs.append(pl.BlockSpec(memory_space=pltpu.SMEM))
    args.append(safe)
    if mode == "diff":
        in_specs += [_whole(lam.shape), _whole(subln.shape)]
        args += [lam, subln]

    kern = functools.partial(_attn_kernel, mode=mode, split=split, main=main, seq=seq,
                             has_sink=sink is not None, lam_init=lam_init, tq=tq, tq_sub=tq_sub)
    scratch = []
    if main == "full":
        scratch = [pltpu.VMEM((seq // TK_MAX_FREE, LANES, TK_MAX_FREE), BF16), pltpu.VMEM((LANES, ctx_len), BF16)]
    return pl.pallas_call(
        kern,
        grid=(nbatch, n_pairs, nq),
        in_specs=in_specs,
        out_specs=pl.BlockSpec((tq, LANES), lambda b, j, i: (out_row(b, j, i), j)),
        out_shape=jax.ShapeDtypeStruct((out_rows, out_cols), BF16),
        scratch_shapes=scratch,
        compiler_params=_params("parallel", "parallel", "arbitrary"),
        name=name,
    )(*args)


def _na_bias_kernel(rpb_ref, o_ref, *, n_rows, rows_q):
    h = pl.program_id(0)
    n_dr = 2 * NA_ROWS - 1
    n_dc = 2 * NA_COLS - 1
    kc = lax.broadcasted_iota(jnp.int32, (GRID_W, LANES), 0)
    cq = lax.broadcasted_iota(jnp.int32, (GRID_W, LANES), 1) & (GRID_W - 1)
    dc_idx = jnp.clip(kc - cq, -(NA_COLS - 1), NA_COLS - 1) + NA_COLS - 1
    cs = jnp.clip(cq - NA_COLS // 2, 0, GRID_W - NA_COLS)
    col_ok = (kc >= cs) & (kc < cs + NA_COLS)
    tiles = [jnp.zeros((GRID_W, LANES), F32) for _ in range(n_dr)]
    for dc in range(n_dc):
        hit = dc_idx == dc
        for dr in range(n_dr):
            tiles[dr] = jnp.where(hit, rpb_ref[(h * n_dr + dr) * n_dc + dc] * LOG2E, tiles[dr])
    tiles = [jnp.where(col_ok, t, NEG) for t in tiles]
    masked = jnp.full((GRID_W, LANES), NEG, F32)
    low = lax.broadcasted_iota(jnp.int32, (GRID_W, LANES), 1) < HALF

    span_rows = rows_q + NA_ROWS
    first_q_row = (0, span_rows, n_rows - rows_q)
    for c in range(3):
        r0 = first_q_row[c]
        w0 = min(max(r0 - NA_ROWS // 2, 0), n_rows - span_rows)
        for kr in range(span_rows):
            k_abs = w0 + kr
            for m in range(rows_q // 2):
                halves = []
                for r in (r0 + 2 * m, r0 + 2 * m + 1):
                    rs = min(max(r - NA_ROWS // 2, 0), n_rows - NA_ROWS)
                    halves.append(tiles[k_abs - r + NA_ROWS - 1] if rs <= k_abs < rs + NA_ROWS else masked)
                o_ref[c, 0, kr * GRID_W:(kr + 1) * GRID_W, m * LANES:(m + 1) * LANES] = (
                    jnp.where(low, halves[0], halves[1]))


def _na_bias(rpb, seq):
    n_rows = seq // GRID_W
    rows_q = TQ_SUB // GRID_W
    span = (rows_q + NA_ROWS) * GRID_W
    kern = functools.partial(_na_bias_kernel, n_rows=n_rows, rows_q=rows_q)
    return pl.pallas_call(
        kern,
        grid=(NA_HEADS,),
        in_specs=[pl.BlockSpec(memory_space=pltpu.SMEM)],
        out_specs=pl.BlockSpec((3, 1, span, TQ_SUB), lambda h: (0, h, 0, 0)),
        out_shape=jax.ShapeDtypeStruct((3, NA_HEADS, span, TQ_SUB), F32),
        compiler_params=_params("parallel"),
        name="na_bias",
    )(rpb.reshape(-1))


def _post_kernel(*refs, n_x, n_o, bpb, nbatch):
    n_lat_blocks = bpb * nbatch
    x_refs, oa_refs, ob_refs = refs[:n_x], refs[n_x:n_x + n_o], refs[n_x + n_o:n_x + 2 * n_o]
    wo_ref, g1_ref, n2_ref, sh_ref, sc_ref, g2_ref, w1_ref, w2_ref, o_ref = refs[n_x + 2 * n_o:]
    oa = _stream_block(oa_refs, n_lat_blocks)
    ob = _stream_block(ob_refs, n_lat_blocks)
    half = oa.shape[1]
    y = (jnp.dot(oa, wo_ref[:half, :], preferred_element_type=F32)
         + jnp.dot(ob, wo_ref[half:, :], preferred_element_type=F32))
    x1 = _stream_block(x_refs, n_lat_blocks) + _mod_row(g1_ref, bpb, nbatch) * y
    h = _modulated(x1, n2_ref, _mod_row(sh_ref, bpb, nbatch), _mod_row(sc_ref, bpb, nbatch)).astype(BF16)
    a = jnp.dot(h, w1_ref[...], preferred_element_type=F32)
    a = jnp.square(jnp.maximum(a, 0.0)).astype(BF16)
    o_ref[...] = x1 + _mod_row(g2_ref, bpb, nbatch) * jnp.dot(a, w2_ref[...], preferred_element_type=F32)


def _post(x_src, oa_src, ob_src, wo, mod, n2, w1, w2, layer, n_blocks, bpb, nbatch):
    d = x_src[0].shape[1]
    half = oa_src[0].shape[1]
    n_lat_blocks = bpb * nbatch
    g1_spec, sh_spec, sc_spec, g2_spec = _mod_specs(mod, layer, (GATE1, SHIFT2, SCALE2, GATE2))
    resident = lambda a: pl.BlockSpec((None,) + a.shape[1:], lambda i: (layer, 0, 0),
                                      pipeline_mode=pl.Buffered(1))
    kern = functools.partial(_post_kernel, n_x=len(x_src), n_o=len(oa_src), bpb=bpb, nbatch=nbatch)
    return pl.pallas_call(
        kern,
        grid=(n_blocks,),
        in_specs=(_stream_specs(x_src, d, n_lat_blocks) + _stream_specs(oa_src, half, n_lat_blocks)
                  + _stream_specs(ob_src, half, n_lat_blocks)
                  + [resident(wo), g1_spec, _whole(n2.shape), sh_spec, sc_spec, g2_spec, resident(w1), resident(w2)]),
        out_specs=pl.BlockSpec((TM, d), lambda i: (i, 0)),
        out_shape=jax.ShapeDtypeStruct((n_blocks * TM, d), F32),
        compiler_params=_params("parallel"),
        name="post",
    )(*x_src, *oa_src, *ob_src, wo, mod, n2, mod, mod, mod, w1, w2)


def _block_diag_ones(group):
    idx = np.arange(CHUNK) // group
    return jnp.asarray(idx[:, None] == idx[None, :], dtype=BF16)


def _rope_tables(seq, rot_dim, lane0, pad_rows):
    t = jnp.arange(seq, dtype=jnp.int32)
    row = (t // GRID_W).astype(F32)
    col = (t % GRID_W).astype(F32)
    n_freq = rot_dim // 4
    freqs = jnp.power(ROPE_BASE, -jnp.arange(n_freq, dtype=F32) / n_freq)
    ar = row[:, None] * freqs[None, :]
    ac = col[:, None] * freqs[None, :]
    ang = jnp.concatenate([ar, ar, ac, ac], axis=-1)
    sign = jnp.asarray(np.tile(np.repeat([-1.0, 1.0], n_freq), 2), F32)
    cos, sin = jnp.cos(ang), jnp.sin(ang) * sign[None, :]
    if lane0 is None:
        reps = LANES // rot_dim
        cos, sin = jnp.tile(cos, (1, reps)), jnp.tile(sin, (1, reps))
    else:
        pad = ((0, 0), (lane0, LANES - lane0 - rot_dim))
        cos = jnp.pad(cos, pad, constant_values=1.0)
        sin = jnp.pad(sin, pad)
    cos = jnp.concatenate([cos, jnp.ones((pad_rows, LANES), F32)], axis=0)
    sin = jnp.concatenate([sin, jnp.zeros((pad_rows, LANES), F32)], axis=0)
    return cos, sin


def _logits_bounded(q_gain, k_gain, dim, bias=None, sink=None):
    bound = jnp.max(jnp.abs(q_gain)) * jnp.max(jnp.abs(k_gain)) * (dim ** 0.5) * BF16_ROUNDING_MARGIN
    if bias is not None:
        bound = bound + jnp.max(jnp.abs(bias))
    if sink is not None:
        bound = jnp.maximum(bound, jnp.max(jnp.abs(sink)))
    return (bound <= MAX_FREE_LOGIT_BOUND).astype(jnp.int32).reshape(1)


def _tile_gain(g, reps, scale=1.0):
    return (jnp.tile(g.astype(F32), reps) * scale)[None, :]


def _even_weights(w, q_g, k_g, sq_g, sk_g):
    nq = DIFF_HEADS * 2 * HEAD_DIM
    qa, ka, va = w[:, :nq], w[:, nq:2 * nq], w[:, 2 * nq:3 * nq]
    o = 3 * nq
    qb = w[:, o:o + SWA_Q_HEADS * HEAD_DIM]
    o += SWA_Q_HEADS * HEAD_DIM
    kb = [w[:, o + i * HEAD_DIM:o + (i + 1) * HEAD_DIM] for i in range(SWA_KV_HEADS)]
    o += SWA_KV_HEADS * HEAD_DIM
    vb = [w[:, o + i * HEAD_DIM:o + (i + 1) * HEAD_DIM] for i in range(SWA_KV_HEADS)]
    dup = lambda parts: [p for p in parts for _ in range(2)]
    w_new = jnp.concatenate([qa, ka, va, qb] + dup(kb) + dup(vb), axis=1).astype(BF16)
    scale = HEAD_DIM ** -0.5 * LOG2E
    ones = lambda n: jnp.ones((1, n), F32)
    gain = jnp.concatenate([
        _tile_gain(q_g, 2 * DIFF_HEADS, scale), _tile_gain(k_g, 2 * DIFF_HEADS), ones(nq),
        _tile_gain(sq_g, SWA_Q_HEADS, scale), _tile_gain(sk_g, 2 * SWA_KV_HEADS),
        ones(2 * SWA_KV_HEADS * HEAD_DIM)], axis=1)
    assert w_new.shape[1] == gain.shape[1] == EVEN_COLS
    return w_new, gain


def _odd_weights(w, wq_up, wkv_up, qa_g, kva_g, mq_g, mk_g, nq_g, nk_g):
    d = w.shape[0]
    o = 0
    parts = []
    for n in (MLA_Q_LORA, MLA_KV_LORA, MLA_ROPE, NA_HEADS * HEAD_DIM, NA_HEADS * HEAD_DIM, NA_HEADS * HEAD_DIM):
        parts.append(w[:, o:o + n])
        o += n
    q_a, kv_a, k_r, nq, nk, nv = parts
    slot = jnp.concatenate([jnp.zeros((d, MLA_NOPE), w.dtype), k_r,
                            jnp.zeros((d, LANES - MLA_QK), w.dtype)], axis=1)
    w_new = jnp.concatenate([q_a, kv_a, nq, nk, nv, slot], axis=1).astype(BF16)
    assert w_new.shape[1] == ODD_W_COLS
    wq = jnp.pad(wq_up.reshape(MLA_Q_LORA, MLA_HEADS, MLA_QK), ((0, 0), (0, 0), (0, MLA_PAD - MLA_QK)))
    wq = wq.reshape(MLA_Q_LORA, MLA_HEADS * MLA_PAD).astype(BF16)
    wkv = wkv_up.reshape(MLA_KV_LORA, MLA_HEADS, MLA_NOPE + MLA_V)
    wk = jnp.pad(wkv[..., :MLA_NOPE], ((0, 0), (0, 0), (0, MLA_PAD - MLA_NOPE))).reshape(MLA_KV_LORA, -1)
    wv = wkv[..., MLA_NOPE:].reshape(MLA_KV_LORA, MLA_HEADS * MLA_V)
    wkv_new = jnp.concatenate([wk, wv], axis=1).astype(BF16)
    pad_gain = lambda g, scale: _tile_gain(jnp.pad(g.astype(F32), (0, MLA_PAD - MLA_QK)), CHUNK // MLA_PAD, scale)
    gains = (qa_g.astype(F32)[None, :], kva_g.astype(F32)[None, :],
             pad_gain(mq_g, MLA_QK ** -0.5 * LOG2E), pad_gain(mk_g, 1.0),
             _tile_gain(nq_g, CHUNK // HEAD_DIM, HEAD_DIM ** -0.5 * LOG2E), _tile_gain(nk_g, CHUNK // HEAD_DIM))
    return w_new, wq, wkv_new, gains


def kernel(x, c, ctx, c_ctx, ada_w, ada_b, norm1_g, norm2_g, w_out, mlp_w1, mlp_w2, ev_w_in, diff_q_g,
           diff_k_g, diff_lam, diff_subln_g, swa_q_g, swa_k_g, swa_sink, od_w_in, mla_qa_g, mla_kva_g,
           mla_wq_up, mla_wkv_up, mla_q_g, mla_k_g, na_q_g, na_k_g, na_rpb):
    nbatch, seq, d = x.shape
    ctx_len = ctx.shape[1]
    depth = ada_w.shape[0]
    n_lat = nbatch * seq
    n_ctx = nbatch * ctx_len
    assert seq % TM == 0 and n_ctx == TM and seq % TQ == 0 and seq % GRID_W == 0
    assert seq % TQ_FULL == 0 and seq % TK_MAX_FREE == 0 and seq % TK_FULL == 0
    assert seq // GRID_W >= 2 * (NA_ROWS + TQ_SUB // GRID_W) and seq >= TQ_SUB + 2 * WINDOW
    bpb = seq // TM

    mod_rows = 8
    cvec = jnp.concatenate([c, c_ctx[None, :], jnp.zeros((mod_rows - nbatch - 1, d), F32)], axis=0)
    mod = _modulation(cvec, ada_w, ada_b)

    cos64, sin64 = _rope_tables(seq, HEAD_DIM, None, n_ctx)
    cos32, sin32 = _rope_tables(seq, MLA_ROPE, MLA_NOPE, n_ctx)
    g64 = _block_diag_ones(HEAD_DIM)
    g128 = _block_diag_ones(LANES)

    x_src = (x.reshape(n_lat, d), ctx.reshape(n_ctx, d))
    wo_all, w1_all, w2_all = w_out.astype(BF16), mlp_w1.astype(BF16), mlp_w2.astype(BF16)
    common = dict(nbatch=nbatch, seq=seq, ctx_len=ctx_len)

    for layer in range(depth):
        need_ctx = layer < depth - 1
        n_blocks = bpb * nbatch + (1 if need_ctx else 0)
        n1 = norm1_g[layer][None, :]
        n2 = norm2_g[layer][None, :]
        i = layer // 2
        if layer % 2 == 0:
            lam_init = 0.8 - 0.6 * math.exp(-0.3 * layer)
            w_in, gain = _even_weights(ev_w_in[i], diff_q_g[i], diff_k_g[i], swa_q_g[i], swa_k_g[i])
            p = _proj_even(x_src, n1, mod, layer, w_in, gain, g64, cos64, sin64, bpb, nbatch)
            lam = diff_lam[i].astype(F32)
            subln = diff_subln_g[i].astype(F32)[None, :]
            sink = swa_sink[i].astype(F32)
            diff_kw = dict(mode="diff", split="mask", n_pairs=DIFF_HEADS, q_blk=lambda j: EVEN_QA // LANES + j,
                           k_blk=lambda j: EVEN_KA // LANES + j, v_blk=lambda j: EVEN_VA // LANES + j,
                           out_cols=DIFF_HEADS * LANES,
                           lam=lam, subln=subln, lam_init=lam_init,
                           safe=_logits_bounded(diff_q_g[i], diff_k_g[i], HEAD_DIM), **common)
            swa_kw = dict(mode="pair", split="mask", n_pairs=SWA_Q_HEADS // 2,
                          q_blk=lambda j: EVEN_QB // LANES + j, k_blk=lambda j: EVEN_KB // LANES + j // 2,
                          v_blk=lambda j: EVEN_VB // LANES + j // 2,
                          out_cols=SWA_Q_HEADS * HEAD_DIM, sink=sink,
                          safe=_logits_bounded(swa_q_g[i], swa_k_g[i], HEAD_DIM, sink=sink), **common)
            oa = _attention(p, name="diff_attn", main="full", ctx_queries=False, **diff_kw)
            ob = _attention(p, name="window_attn", main="window", ctx_queries=False, **swa_kw)
            if need_ctx:
                oa_c = _attention(p, name="diff_attn_ctx", main=None, ctx_queries=True, **diff_kw)
                ob_c = _attention(p, name="window_attn_ctx", main=None, ctx_queries=True, **swa_kw)
        else:
            w_in, wq, wkv, gains = _odd_weights(od_w_in[i], mla_wq_up[i], mla_wkv_up[i], mla_qa_g[i],
                                                mla_kva_g[i], mla_q_g[i], mla_k_g[i], na_q_g[i], na_k_g[i])
            p = _proj_odd(x_src[0], n1, mod, layer, w_in, wq, wkv, gains, g64, g128, cos32, sin32, bpb, nbatch)
            bias = _na_bias(na_rpb[i].astype(F32), seq)
            mla_kw = dict(mode="pair", split="slice", n_pairs=MLA_HEADS // 2, q_blk=lambda j: j,
                          k_blk=lambda j: ODD_K // CHUNK + j, v_blk=lambda j: ODD_V // LANES + j,
                          out_cols=MLA_HEADS * MLA_V,
                          safe=_logits_bounded(mla_q_g[i], mla_k_g[i], MLA_QK), **common)
            na_kw = dict(mode="pair", split="mask", n_pairs=NA_HEADS // 2,
                         q_blk=lambda j: ODD_NQ // LANES + j, k_blk=lambda j: ODD_NK // LANES + j,
                         v_blk=lambda j: ODD_NV // LANES + j, out_cols=NA_HEADS * HEAD_DIM,
                         safe=_logits_bounded(na_q_g[i], na_k_g[i], HEAD_DIM, bias=na_rpb[i]), **common)
            oa = _attention(p, name="mla_attn", main="full", ctx_queries=False, **mla_kw)
            ob = _attention(p, name="na_attn", main="na", ctx_queries=False, bias=bias, **na_kw)
            if need_ctx:
                oa_c = _attention(p, name="mla_attn_ctx", main=None, ctx_queries=True, **mla_kw)
                ob_c = _attention(p, name="na_attn_ctx", main=None, ctx_queries=True, **na_kw)
        oa_src, ob_src = ((oa, oa_c), (ob, ob_c)) if need_ctx else ((oa,), (ob,))
        x_src = (_post(x_src, oa_src, ob_src, wo_all, mod, n2, w1_all, w2_all, layer, n_blocks, bpb, nbatch),)
    return x_src[0].reshape(nbatch, seq, d)
```

```python
import functools
import math

import numpy as np
import jax
import jax.numpy as jnp
from jax import lax
from jax.experimental import pallas as pl
from jax.experimental.pallas import tpu as pltpu

F32 = jnp.float32
BF16 = jnp.bfloat16

LANES = 128
HALF = LANES // 2
CHUNK = 2 * LANES
GRID_W = 64
HEAD_DIM = 64
EPS = 1e-6
ROPE_BASE = 10000.0
DIFF_HEADS = 4
SWA_Q_HEADS = 8
SWA_KV_HEADS = 2
WINDOW = 128
MLA_HEADS = 8
MLA_Q_LORA = 512
MLA_KV_LORA = 256
MLA_NOPE = 64
MLA_ROPE = 32
MLA_QK = MLA_NOPE + MLA_ROPE
MLA_V = 64
NA_HEADS = 8
NA_ROWS = 8
NA_COLS = 16
NEG = -1e30
LOG2E = math.log2(math.e)
MAX_FREE_LOGIT_BOUND = 50.0
BF16_ROUNDING_MARGIN = 1.02

TM = 512
TQ = 2048
TQ_SUB = 256
TQ_FULL = 1024
TK_FULL = 512
TK_MAX_FREE = 2048
MOD_TN = 1536
VMEM_LIMIT = 56 * 1024 * 1024


def _params(*sem):
    return pltpu.CompilerParams(dimension_semantics=sem, vmem_limit_bytes=VMEM_LIMIT)


def _mod_kernel(c_ref, w_ref, b_ref, o_ref):
    c = c_ref[...]
    a = (c * jax.nn.sigmoid(c)).astype(BF16)
    w = w_ref[0].astype(BF16)
    o_ref[0] = jnp.dot(a, w, preferred_element_type=F32) + b_ref[0]


def _modulation(cvec, ada_w, ada_b):
    depth, d, n = ada_w.shape
    rows = cvec.shape[0]
    return pl.pallas_call(
        _mod_kernel,
        grid=(depth, n // MOD_TN),
        in_specs=[
            pl.BlockSpec((rows, d), lambda l, j: (0, 0)),
            pl.BlockSpec((1, d, MOD_TN), lambda l, j: (l, 0, j)),
            pl.BlockSpec((1, 1, MOD_TN), lambda l, j: (l, 0, j)),
        ],
        out_specs=pl.BlockSpec((1, rows, MOD_TN), lambda l, j: (l, 0, j)),
        out_shape=jax.ShapeDtypeStruct((depth, rows, n), F32),
        compiler_params=_params("parallel", "parallel"),
        name="modulation",
    )(cvec, ada_w, ada_b.reshape(depth, 1, n))


def _modulated(x, g_ref, shift, scale):
    ms = jnp.mean(x * x, axis=-1, keepdims=True)
    y = x * lax.rsqrt(ms + EPS) * g_ref[...]
    return y * (1.0 + scale) + shift


def _group_meansq(chunks, g_ref, inv_n):
    rows = chunks[0].shape[0]
    sq = jnp.concatenate([(c * c).astype(BF16) for c in chunks], axis=0)
    ss = jnp.dot(sq, g_ref[...], preferred_element_type=F32) * inv_n
    return [ss[i * rows:(i + 1) * rows] for i in range(len(chunks))]


def _swap_quarters(x, quarter):
    lane = lax.broadcasted_iota(jnp.int32, x.shape, 1)
    odd = (lane & quarter) != 0
    from_lower = pltpu.roll(x, quarter, 1)
    from_upper = pltpu.roll(x, LANES - quarter, 1)
    return jnp.where(odd, from_lower, from_upper)


def _rope(x, cos, sin_signed, quarter):
    return x * cos + _swap_quarters(x, quarter) * sin_signed


def _norm_rope_chunk(acc, meansq, gain, cos, sin, quarter):
    y = acc * lax.rsqrt(meansq + EPS) * gain
    if quarter is None:
        return y
    return jnp.concatenate([_rope(y[:, :LANES], cos, sin, quarter),
                            _rope(y[:, LANES:], cos, sin, quarter)], axis=1)


def _stream_specs(src, cols, n_lat_blocks):
    if len(src) == 1:
        return [pl.BlockSpec((TM, cols), lambda i: (i, 0))]
    return [pl.BlockSpec((TM, cols), lambda i: (jnp.minimum(i, n_lat_blocks - 1), 0)),
            pl.BlockSpec((TM, cols), lambda i: (0, 0))]


def _stream_block(refs, n_lat_blocks):
    if len(refs) == 1:
        return refs[0][...]
    return jnp.where(pl.program_id(0) < n_lat_blocks, refs[0][...], refs[1][...])


SHIFT1, SCALE1, GATE1, SHIFT2, SCALE2, GATE2 = range(6)


def _mod_specs(mod, layer, chunks):
    d = mod.shape[2] // 6
    return [pl.BlockSpec((None, mod.shape[1], d), lambda i, k=k: (layer, 0, k)) for k in chunks]


def _mod_row(ref, bpb, nbatch):
    row = jnp.minimum(pl.program_id(0) // bpb, nbatch)
    return ref[pl.ds(row, 1), :]


def _rope_spec(bpb, nbatch):
    return pl.BlockSpec((TM, LANES), lambda i: (jnp.where(i < bpb * nbatch, i % bpb, bpb), 0))


def _whole(shape):
    return pl.BlockSpec(shape, lambda *_: (0,) * len(shape))


def _resident(shape):
    return pl.BlockSpec(shape, lambda *_: (0,) * len(shape), pipeline_mode=pl.Buffered(1))


EVEN_QA = 0
EVEN_KA = EVEN_QA + DIFF_HEADS * 2 * HEAD_DIM
EVEN_VA = EVEN_KA + DIFF_HEADS * 2 * HEAD_DIM
EVEN_QB = EVEN_VA + DIFF_HEADS * 2 * HEAD_DIM
EVEN_KB = EVEN_QB + SWA_Q_HEADS * HEAD_DIM
EVEN_VB = EVEN_KB + 2 * SWA_KV_HEADS * HEAD_DIM
EVEN_COLS = EVEN_VB + 2 * SWA_KV_HEADS * HEAD_DIM
EVEN_NORM_CHUNKS = tuple(c for c in range(EVEN_COLS // CHUNK)
                         if c * CHUNK < EVEN_VA or EVEN_QB <= c * CHUNK < EVEN_VB)
EVEN_DOT_RANGES = ((EVEN_QA // CHUNK, EVEN_VA // CHUNK), (EVEN_QB // CHUNK, EVEN_VB // CHUNK),
                   (EVEN_VA // CHUNK, EVEN_QB // CHUNK), (EVEN_VB // CHUNK, EVEN_COLS // CHUNK))


def _proj_even_kernel(*refs, n_x, bpb, nbatch):
    n_lat_blocks = bpb * nbatch
    x_refs = refs[:n_x]
    g_ref, sh_ref, sc_ref, w_ref, gain_ref, g64_ref, cos_ref, sin_ref, o_ref = refs[n_x:]
    h = _modulated(_stream_block(x_refs, n_lat_blocks), g_ref, _mod_row(sh_ref, bpb, nbatch),
                   _mod_row(sc_ref, bpb, nbatch)).astype(BF16)
    cos = cos_ref[...]
    sin = sin_ref[...]
    for first, last in EVEN_DOT_RANGES:
        full = jnp.dot(h, w_ref[:, first * CHUNK:last * CHUNK], preferred_element_type=F32)
        chunk = lambda c: full[:, (c - first) * CHUNK:(c - first + 1) * CHUNK]
        normed = [c for c in range(first, last) if c in EVEN_NORM_CHUNKS]
        meansq = {}
        if normed:
            meansq = dict(zip(normed, _group_meansq([chunk(c) for c in normed], g64_ref, 1.0 / HEAD_DIM)))
        for c in range(first, last):
            cols = slice(c * CHUNK, (c + 1) * CHUNK)
            acc = chunk(c)
            if c in EVEN_NORM_CHUNKS:
                acc = _norm_rope_chunk(acc, meansq[c], gain_ref[:, cols], cos, sin, HEAD_DIM // 4)
            o_ref[:, cols] = acc.astype(BF16)


def _proj_even(x_src, g, mod, layer, w, gain, g64, cos, sin, bpb, nbatch):
    d = x_src[0].shape[1]
    n_lat_blocks = bpb * nbatch
    kern = functools.partial(_proj_even_kernel, n_x=len(x_src), bpb=bpb, nbatch=nbatch)
    return pl.pallas_call(
        kern,
        grid=(n_lat_blocks + 1,),
        in_specs=(_stream_specs(x_src, d, n_lat_blocks) + [_whole(g.shape)]
                  + _mod_specs(mod, layer, (SHIFT1, SCALE1))
                  + [_resident(w.shape), _whole(gain.shape), _whole(g64.shape),
                     _rope_spec(bpb, nbatch), _rope_spec(bpb, nbatch)]),
        out_specs=pl.BlockSpec((TM, EVEN_COLS), lambda i: (i, 0)),
        out_shape=jax.ShapeDtypeStruct(((n_lat_blocks + 1) * TM, EVEN_COLS), BF16),
        compiler_params=_params("parallel"),
        name="proj_even",
    )(*x_src, g, mod, mod, w, gain, g64, cos, sin)


NA_WIDTH = NA_HEADS * HEAD_DIM
MLA_PAD = LANES
MLA_WIDTH = MLA_HEADS * MLA_PAD
ODD_W_KVA = MLA_Q_LORA
ODD_W_NQ = ODD_W_KVA + MLA_KV_LORA
ODD_W_NV = ODD_W_NQ + 2 * NA_WIDTH
ODD_W_SLOT = ODD_W_NV + NA_WIDTH
ODD_W_COLS = ODD_W_SLOT + LANES
ODD_K = MLA_WIDTH
ODD_V = ODD_K + MLA_WIDTH
ODD_NQ = ODD_V + MLA_HEADS * MLA_V
ODD_NK = ODD_NQ + NA_WIDTH
ODD_NV = ODD_NK + NA_WIDTH
ODD_COLS = ODD_NV + NA_WIDTH


def _proj_odd_kernel(x_ref, g_ref, sh_ref, sc_ref, w_ref, wq_ref, wkv_ref, qa_g_ref, kva_g_ref,
                     mq_g_ref, mk_g_ref, nq_g_ref, nk_g_ref, g64_ref, g128_ref, cos_ref, sin_ref, o_ref,
                     *, bpb, nbatch):
    h = _modulated(x_ref[...], g_ref, _mod_row(sh_ref, bpb, nbatch), _mod_row(sc_ref, bpb, nbatch)).astype(BF16)
    cos = cos_ref[...]
    sin = sin_ref[...]
    quarter = MLA_ROPE // 4

    lora = jnp.dot(h, w_ref[:, :ODD_W_NQ], preferred_element_type=F32)
    rest = jnp.dot(h, w_ref[:, ODD_W_NQ:], preferred_element_type=F32)

    def low_rank(a, gain_ref):
        ms = jnp.mean(a * a, axis=-1, keepdims=True)
        return (a * lax.rsqrt(ms + EPS) * gain_ref[...]).astype(BF16)

    n_mla = MLA_WIDTH // CHUNK
    chunks_of = lambda a, first, n: [a[:, first + c * CHUNK:first + (c + 1) * CHUNK] for c in range(n)]
    qa = low_rank(lora[:, :MLA_Q_LORA], qa_g_ref)
    qf = jnp.dot(qa, wq_ref[...], preferred_element_type=F32)
    q_chunks = chunks_of(qf, 0, n_mla)
    ms_q = _group_meansq(q_chunks, g128_ref, 1.0 / MLA_QK)
    kva = low_rank(lora[:, ODD_W_KVA:], kva_g_ref)
    kvf = jnp.dot(kva, wkv_ref[...], preferred_element_type=F32)
    kslot = rest[:, ODD_W_SLOT - ODD_W_NQ:]
    kslot2 = jnp.concatenate([kslot, kslot], axis=1)
    k_rot = _swap_quarters(kslot * mk_g_ref[:, :LANES], quarter)
    k_rot2 = jnp.concatenate([k_rot, k_rot], axis=1)
    cos2 = jnp.concatenate([cos, cos], axis=1)
    sin2 = jnp.concatenate([sin, sin], axis=1)
    k_chunks = [kc + kslot2 for kc in chunks_of(kvf, 0, n_mla)]
    ms_k = _group_meansq(k_chunks, g128_ref, 1.0 / MLA_QK)
    for c in range(n_mla):
        y = _norm_rope_chunk(q_chunks[c], ms_q[c], mq_g_ref[...], cos, sin, quarter)
        o_ref[:, c * CHUNK:(c + 1) * CHUNK] = y.astype(BF16)
    for c in range(n_mla):
        inv_rms = lax.rsqrt(ms_k[c] + EPS)
        y = (k_chunks[c] * inv_rms * mk_g_ref[...]) * cos2 + (inv_rms * k_rot2) * sin2
        o_ref[:, ODD_K + c * CHUNK:ODD_K + (c + 1) * CHUNK] = y.astype(BF16)
    o_ref[:, ODD_V:ODD_NQ] = kvf[:, MLA_WIDTH:].astype(BF16)

    n_na = NA_WIDTH // CHUNK
    n_chunks = chunks_of(rest, 0, 2 * n_na)
    ms64 = _group_meansq(n_chunks, g64_ref, 1.0 / HEAD_DIM)
    for c in range(2 * n_na):
        gain = nq_g_ref[...] if c < n_na else nk_g_ref[...]
        y = _norm_rope_chunk(n_chunks[c], ms64[c], gain, None, None, None)
        o_ref[:, ODD_NQ + c * CHUNK:ODD_NQ + (c + 1) * CHUNK] = y.astype(BF16)
    o_ref[:, ODD_NV:ODD_COLS] = rest[:, ODD_W_NV - ODD_W_NQ:ODD_W_SLOT - ODD_W_NQ].astype(BF16)


def _proj_odd(xa, g, mod, layer, w, wq, wkv, gains, g64, g128, cos, sin, bpb, nbatch):
    na, d = xa.shape
    tok_specs = _stream_specs((xa,), d, bpb * nbatch) + [_whole(g.shape)] + _mod_specs(mod, layer, (SHIFT1, SCALE1))
    consts = (w, wq, wkv) + tuple(gains) + (g64, g128)
    return pl.pallas_call(
        functools.partial(_proj_odd_kernel, bpb=bpb, nbatch=nbatch),
        grid=(na // TM,),
        in_specs=(tok_specs + [_resident(a.shape) for a in consts[:3]] + [_whole(a.shape) for a in consts[3:]]
                  + [_rope_spec(bpb, nbatch)] * 2),
        out_specs=pl.BlockSpec((TM, ODD_COLS), lambda i: (i, 0)),
        out_shape=jax.ShapeDtypeStruct((na, ODD_COLS), BF16),
        compiler_params=_params("parallel"),
        name="proj_odd",
    )(xa, g, mod, mod, *consts, cos, sin)


def _attn_kernel(*refs, mode, split, main, seq, has_sink, lam_init, tq, tq_sub):
    n_sub = tq // tq_sub
    it = iter(refs)
    q_ref = next(it)
    k_ref = v_ref = bias_ref = sink_ref = lam_ref = subln_ref = None
    if main is not None:
        k_ref, v_ref = next(it), next(it)
    kc_ref, vc_ref = next(it), next(it)
    if main == "na":
        bias_ref = next(it)
    if has_sink:
        sink_ref = next(it)
    safe_ref = next(it)
    if mode == "diff":
        lam_ref, subln_ref = next(it), next(it)
    o_ref = next(it)
    vt_ref, vct_ref = (next(it), next(it)) if main == "full" else (None, None)

    pair = pl.program_id(1)
    qi = pl.program_id(2)
    q = q_ref[...]
    low = lax.broadcasted_iota(jnp.int32, (1, LANES), 1) < HALF
    if split == "mask":
        zero = jnp.zeros_like(q)
        qs = (jnp.where(low, q, zero), jnp.where(low, zero, q))
        k_of = lambda k, s: k
    else:
        qs = (q[:, :LANES], q[:, LANES:])
        k_of = lambda k, s: k[:, s * LANES:(s + 1) * LANES]
    q_of = lambda s, t: qs[s][t * tq_sub:(t + 1) * tq_sub]
    sinks = [sink_ref[2 * pair + s] * LOG2E for s in range(2)] if has_sink else None
    nt_dims = (((1,), (1,)), ((), ()))

    def finish(o_lo, o_hi):
        if mode == "diff":
            lv = lam_ref[...]
            lam = (jnp.exp(jnp.sum(lv[0:1] * lv[1:2], axis=-1, keepdims=True))
                   - jnp.exp(jnp.sum(lv[2:3] * lv[3:4], axis=-1, keepdims=True)) + lam_init)
            o = o_lo - lam * o_hi
            ms = jnp.mean(o * o, axis=-1, keepdims=True)
            o = o * lax.rsqrt(ms + EPS) * subln_ref[...] * (1.0 - lam_init)
        else:
            o = jnp.where(low, o_lo, o_hi)
        o_ref[...] = o.astype(BF16)

    window_masks = {}

    def local_keys(t):
        if main == "window":
            span = tq_sub + 2 * WINDOW
            q0 = qi * tq + t * tq_sub
            at_edge = t in (0, n_sub - 1)
            w0 = jnp.clip(q0 - WINDOW, 0, seq - span) if at_edge else q0 - WINDOW
            off = pl.multiple_of(w0, WINDOW)
            if at_edge or "inner" not in window_masks:
                key = lax.broadcasted_iota(jnp.int32, (span, tq_sub), 0)
                qry = lax.broadcasted_iota(jnp.int32, (span, tq_sub), 1)
                inside = jnp.abs(key - qry + ((w0 - q0) if at_edge else -WINDOW)) <= WINDOW
                if not at_edge:
                    window_masks["inner"] = inside
            else:
                inside = window_masks["inner"]
            logit_fn = lambda lg_t, s: jnp.where(inside, lg_t, NEG)
        else:
            rows_q = tq_sub // GRID_W
            span_rows = rows_q + NA_ROWS
            w0 = jnp.clip(qi * (tq // GRID_W) + t * rows_q - NA_ROWS // 2, 0, seq // GRID_W - span_rows)
            off = pl.multiple_of(w0 * GRID_W, GRID_W)
            span = span_rows * GRID_W
            cls = 1
            if t == 0:
                cls = jnp.where(qi == 0, 0, cls)
            if t == n_sub - 1:
                cls = jnp.where(qi == pl.num_programs(2) - 1, 2, cls)
            logit_fn = lambda lg_t, s: lg_t + bias_ref[cls, s]
        return k_ref[pl.ds(off, span), :], v_ref[pl.ds(off, span), :], logit_fn

    def online_sub(t):
        def init(s):
            if has_sink:
                m0 = jnp.full((tq_sub, 1), sinks[s], F32)
                l0 = jnp.ones((tq_sub, 1), F32)
            else:
                m0 = jnp.full((tq_sub, 1), NEG, F32)
                l0 = jnp.zeros((tq_sub, 1), F32)
            return m0, l0, jnp.zeros((tq_sub, LANES), F32)

        def segment(state, k, v, logit_fn):
            out = []
            for s in range(2):
                m, l, acc = state[s]
                logits = lax.dot_general(q_of(s, t), k_of(k, s), nt_dims, preferred_element_type=F32)
                if logit_fn is not None:
                    logits = logit_fn(logits.T, s).T
                m_new = jnp.maximum(m, jnp.max(logits, axis=-1, keepdims=True))
                alpha = jnp.exp2(m - m_new)
                p = jnp.exp2(logits - m_new)
                l = alpha * l + jnp.sum(p, axis=-1, keepdims=True)
                acc = alpha * acc + jnp.dot(p.astype(BF16), v, preferred_element_type=F32)
                out.append((m_new, l, acc))
            return tuple(out)

        state = (init(0), init(1))
        if main == "full":
            def body(c, st):
                off = pl.multiple_of(c * TK_FULL, TK_FULL)
                return segment(st, k_ref[pl.ds(off, TK_FULL), :], v_ref[pl.ds(off, TK_FULL), :], None)
            state = lax.fori_loop(0, seq // TK_FULL, body, state)
        elif main is not None:
            state = segment(state, *local_keys(t))
        (_, l_lo, acc_lo), (_, l_hi, acc_hi) = segment(state, kc_ref[...], vc_ref[...], None)
        return acc_lo / l_lo, acc_hi / l_hi

    def probs_t(k, q_rows, s, logit_fn):
        logits_t = lax.dot_general(k_of(k, s), q_rows, nt_dims, preferred_element_type=F32)
        if logit_fn is not None:
            logits_t = logit_fn(logits_t, s)
        p = jnp.exp2(logits_t)
        return p.astype(BF16), jnp.sum(p, axis=0, keepdims=True)

    def max_free():
        window_masks.clear()
        if main == "full":
            @pl.when(qi == 0)
            def _():
                for c in range(seq // TK_MAX_FREE):
                    vt_ref[c] = v_ref[c * TK_MAX_FREE:(c + 1) * TK_MAX_FREE, :].T
                vct_ref[...] = vc_ref[...].T

            acc_lat = [jnp.zeros((LANES, tq), F32)] * 2
            den_lat = [jnp.zeros((1, tq), F32)] * 2
            for c in range(seq // TK_MAX_FREE):
                k = k_ref[c * TK_MAX_FREE:(c + 1) * TK_MAX_FREE, :]
                for s in range(2):
                    p, p_sum = probs_t(k, qs[s], s, None)
                    acc_lat[s] = acc_lat[s] + jnp.dot(vt_ref[c], p, preferred_element_type=F32)
                    den_lat[s] = den_lat[s] + p_sum
            vct = vct_ref[...]
        else:
            vct = vc_ref[...].T
        p_ctx = [probs_t(kc_ref[...], qs[s], s, None) for s in range(2)]
        if main in ("window", "na"):
            windows = [local_keys(t) for t in range(n_sub)]
            p_loc = [[probs_t(k, q_of(s, t), s, fn) for s in range(2)] for t, (k, _, fn) in enumerate(windows)]
            vts = [v.T for _, v, _ in windows]
        outs = []
        for s in range(2):
            acc = jnp.dot(vct, p_ctx[s][0], preferred_element_type=F32)
            den = p_ctx[s][1]
            if main == "full":
                acc, den = acc + acc_lat[s], den + den_lat[s]
            elif main is not None:
                acc = acc + jnp.concatenate(
                    [jnp.dot(vts[t], p_loc[t][s][0], preferred_element_type=F32) for t in range(n_sub)], axis=1)
                den = den + jnp.concatenate([p_loc[t][s][1] for t in range(n_sub)], axis=1)
            if has_sink:
                den = den + jnp.exp2(jnp.full((1, tq), sinks[s], F32))
            outs.append((acc / den).T)
        finish(*outs)

    def online():
        window_masks.clear()
        parts = [online_sub(t) for t in range(n_sub)]
        finish(*[jnp.concatenate([p[s] for p in parts], axis=0) if n_sub > 1 else parts[0][s]
                 for s in range(2)])

    safe = safe_ref[0] == 1
    pl.when(safe)(max_free)
    pl.when(jnp.logical_not(safe))(online)


def _attention(p, *, name, mode, split, main, n_pairs, q_blk, k_blk, v_blk, out_cols, ctx_queries,
               nbatch, seq, ctx_len, safe, bias=None, sink=None, lam=None, subln=None, lam_init=0.0):
    n_lat = nbatch * seq
    qw = LANES if split == "mask" else 2 * LANES
    if ctx_queries:
        tq, nq = ctx_len, 1
        q_row = lambda b, j, i: n_lat // ctx_len + b
        out_rows = nbatch * ctx_len
        out_row = lambda b, j, i: b
    else:
        tq = TQ_FULL if main == "full" else TQ
        nq = seq // tq
        q_row = lambda b, j, i: b * nq + i
        out_rows = n_lat
        out_row = q_row
    ctx_row = lambda b, j, i: n_lat // ctx_len + b

    in_specs = [pl.BlockSpec((tq, qw), lambda b, j, i: (q_row(b, j, i), q_blk(j)))]
    args = [p]
    if main is not None:
        in_specs += [pl.BlockSpec((seq, qw), lambda b, j, i: (b, k_blk(j))),
                     pl.BlockSpec((seq, LANES), lambda b, j, i: (b, v_blk(j)))]
        args += [p, p]
    in_specs += [pl.BlockSpec((ctx_len, qw), lambda b, j, i: (ctx_row(b, j, i), k_blk(j))),
                 pl.BlockSpec((ctx_len, LANES), lambda b, j, i: (ctx_row(b, j, i), v_blk(j)))]
    args += [p, p]
    tq_sub = TQ_SUB if main in ("window", "na") else tq
    n_sub = tq // tq_sub
    if main == "na":
        span = (tq_sub // GRID_W + NA_ROWS) * GRID_W
        in_specs.append(pl.BlockSpec((3, 2, span, tq_sub), lambda b, j, i: (0, j, 0, 0)))
        args.append(bias)
    if sink is not None:
        in_specs.append(pl.BlockSpec(memory_space=pltpu.SMEM))
        args.append(sink)
    in_specs.append(pl.BlockSpec(memory_space=pltpu.SMEM))
    args.append(safe)
    if mode == "diff":
        in_specs += [_whole(lam.shape), _whole(subln.shape)]
        args += [lam, subln]

    kern = functools.partial(_attn_kernel, mode=mode, split=split, main=main, seq=seq,
                             has_sink=sink is not None, lam_init=lam_init, tq=tq, tq_sub=tq_sub)
    scratch = []
    if main == "full":
        scratch = [pltpu.VMEM((seq // TK_MAX_FREE, LANES, TK_MAX_FREE), BF16), pltpu.VMEM((LANES, ctx_len), BF16)]
    return pl.pallas_call(
        kern,
        grid=(nbatch, n_pairs, nq),
        in_specs=in_specs,
        out_specs=pl.BlockSpec((tq, LANES), lambda b, j, i: (out_row(b, j, i), j)),
        out_shape=jax.ShapeDtypeStruct((out_rows, out_cols), BF16),
        scratch_shapes=scratch,
        compiler_params=_params("parallel", "parallel", "arbitrary"),
        name=name,
    )(*args)


def _na_bias_kernel(rpb_ref, o_ref, *, n_rows, rows_q):
    h = pl.program_id(0)
    n_dr = 2 * NA_ROWS - 1
    n_dc = 2 * NA_COLS - 1
    kc = lax.broadcasted_iota(jnp.int32, (GRID_W, LANES), 0)
    cq = lax.broadcasted_iota(jnp.int32, (GRID_W, LANES), 1) & (GRID_W - 1)
    dc_idx = jnp.clip(kc - cq, -(NA_COLS - 1), NA_COLS - 1) + NA_COLS - 1
    cs = jnp.clip(cq - NA_COLS // 2, 0, GRID_W - NA_COLS)
    col_ok = (kc >= cs) & (kc < cs + NA_COLS)
    tiles = [jnp.zeros((GRID_W, LANES), F32) for _ in range(n_dr)]
    for dc in range(n_dc):
        hit = dc_idx == dc
        for dr in range(n_dr):
            tiles[dr] = jnp.where(hit, rpb_ref[(h * n_dr + dr) * n_dc + dc] * LOG2E, tiles[dr])
    tiles = [jnp.where(col_ok, t, NEG) for t in tiles]
    masked = jnp.full((GRID_W, LANES), NEG, F32)
    low = lax.broadcasted_iota(jnp.int32, (GRID_W, LANES), 1) < HALF

    span_rows = rows_q + NA_ROWS
    first_q_row = (0, span_rows, n_rows - rows_q)
    for c in range(3):
        r0 = first_q_row[c]
        w0 = min(max(r0 - NA_ROWS // 2, 0), n_rows - span_rows)
        for kr in range(span_rows):
            k_abs = w0 + kr
            for m in range(rows_q // 2):
                halves = []
                for r in (r0 + 2 * m, r0 + 2 * m + 1):
                    rs = min(max(r - NA_ROWS // 2, 0), n_rows - NA_ROWS)
                    halves.append(tiles[k_abs - r + NA_ROWS - 1] if rs <= k_abs < rs + NA_ROWS else masked)
                o_ref[c, 0, kr * GRID_W:(kr + 1) * GRID_W, m * LANES:(m + 1) * LANES] = (
                    jnp.where(low, halves[0], halves[1]))


def _na_bias(rpb, seq):
    n_rows = seq // GRID_W
    rows_q = TQ_SUB // GRID_W
    span = (rows_q + NA_ROWS) * GRID_W
    kern = functools.partial(_na_bias_kernel, n_rows=n_rows, rows_q=rows_q)
    return pl.pallas_call(
        kern,
        grid=(NA_HEADS,),
        in_specs=[pl.BlockSpec(memory_space=pltpu.SMEM)],
        out_specs=pl.BlockSpec((3, 1, span, TQ_SUB), lambda h: (0, h, 0, 0)),
        out_shape=jax.ShapeDtypeStruct((3, NA_HEADS, span, TQ_SUB), F32),
        compiler_params=_params("parallel"),
        name="na_bias",
    )(rpb.reshape(-1))


def _post_kernel(*refs, n_x, n_o, bpb, nbatch):
    n_lat_blocks = bpb * nbatch
    x_refs, oa_refs, ob_refs = refs[:n_x], refs[n_x:n_x + n_o], refs[n_x + n_o:n_x + 2 * n_o]
    wo_ref, g1_ref, n2_ref, sh_ref, sc_ref, g2_ref, w1_ref, w2_ref, o_ref = refs[n_x + 2 * n_o:]
    oa = _stream_block(oa_refs, n_lat_blocks)
    ob = _stream_block(ob_refs, n_lat_blocks)
    half = oa.shape[1]
    y = (jnp.dot(oa, wo_ref[:half, :], preferred_element_type=F32)
         + jnp.dot(ob, wo_ref[half:, :], preferred_element_type=F32))
    x1 = _stream_block(x_refs, n_lat_blocks) + _mod_row(g1_ref, bpb, nbatch) * y
    h = _modulated(x1, n2_ref, _mod_row(sh_ref, bpb, nbatch), _mod_row(sc_ref, bpb, nbatch)).astype(BF16)
    a = jnp.dot(h, w1_ref[...], preferred_element_type=F32)
    a = jnp.square(jnp.maximum(a, 0.0)).astype(BF16)
    o_ref[...] = x1 + _mod_row(g2_ref, bpb, nbatch) * jnp.dot(a, w2_ref[...], preferred_element_type=F32)


def _post(x_src, oa_src, ob_src, wo, mod, n2, w1, w2, layer, n_blocks, bpb, nbatch):
    d = x_src[0].shape[1]
    half = oa_src[0].shape[1]
    n_lat_blocks = bpb * nbatch
    g1_spec, sh_spec, sc_spec, g2_spec = _mod_specs(mod, layer, (GATE1, SHIFT2, SCALE2, GATE2))
    resident = lambda a: pl.BlockSpec((None,) + a.shape[1:], lambda i: (layer, 0, 0),
                                      pipeline_mode=pl.Buffered(1))
    kern = functools.partial(_post_kernel, n_x=len(x_src), n_o=len(oa_src), bpb=bpb, nbatch=nbatch)
    return pl.pallas_call(
        kern,
        grid=(n_blocks,),
        in_specs=(_stream_specs(x_src, d, n_lat_blocks) + _stream_specs(oa_src, half, n_lat_blocks)
                  + _stream_specs(ob_src, half, n_lat_blocks)
                  + [resident(wo), g1_spec, _whole(n2.shape), sh_spec, sc_spec, g2_spec, resident(w1), resident(w2)]),
        out_specs=pl.BlockSpec((TM, d), lambda i: (i, 0)),
        out_shape=jax.ShapeDtypeStruct((n_blocks * TM, d), F32),
        compiler_params=_params("parallel"),
        name="post",
    )(*x_src, *oa_src, *ob_src, wo, mod, n2, mod, mod, mod, w1, w2)


def _block_diag_ones(group):
    idx = np.arange(CHUNK) // group
    return jnp.asarray(idx[:, None] == idx[None, :], dtype=BF16)


def _rope_tables(seq, rot_dim, lane0, pad_rows):
    t = jnp.arange(seq, dtype=jnp.int32)
    row = (t // GRID_W).astype(F32)
    col = (t % GRID_W).astype(F32)
    n_freq = rot_dim // 4
    freqs = jnp.power(ROPE_BASE, -jnp.arange(n_freq, dtype=F32) / n_freq)
    ar = row[:, None] * freqs[None, :]
    ac = col[:, None] * freqs[None, :]
    ang = jnp.concatenate([ar, ar, ac, ac], axis=-1)
    sign = jnp.asarray(np.tile(np.repeat([-1.0, 1.0], n_freq), 2), F32)
    cos, sin = jnp.cos(ang), jnp.sin(ang) * sign[None, :]
    if lane0 is None:
        reps = LANES // rot_dim
        cos, sin = jnp.tile(cos, (1, reps)), jnp.tile(sin, (1, reps))
    else:
        pad = ((0, 0), (lane0, LANES - lane0 - rot_dim))
        cos = jnp.pad(cos, pad, constant_values=1.0)
        sin = jnp.pad(sin, pad)
    cos = jnp.concatenate([cos, jnp.ones((pad_rows, LANES), F32)], axis=0)
    sin = jnp.concatenate([sin, jnp.zeros((pad_rows, LANES), F32)], axis=0)
    return cos, sin


def _logits_bounded(q_gain, k_gain, dim, bias=None, sink=None):
    bound = jnp.max(jnp.abs(q_gain)) * jnp.max(jnp.abs(k_gain)) * (dim ** 0.5) * BF16_ROUNDING_MARGIN
    if bias is not None:
        bound = bound + jnp.max(jnp.abs(bias))
    if sink is not None:
        bound = jnp.maximum(bound, jnp.max(jnp.abs(sink)))
    return (bound <= MAX_FREE_LOGIT_BOUND).astype(jnp.int32).reshape(1)


def _tile_gain(g, reps, scale=1.0):
    return (jnp.tile(g.astype(F32), reps) * scale)[None, :]


def _even_weights(w, q_g, k_g, sq_g, sk_g):
    nq = DIFF_HEADS * 2 * HEAD_DIM
    qa, ka, va = w[:, :nq], w[:, nq:2 * nq], w[:, 2 * nq:3 * nq]
    o = 3 * nq
    qb = w[:, o:o + SWA_Q_HEADS * HEAD_DIM]
    o += SWA_Q_HEADS * HEAD_DIM
    kb = [w[:, o + i * HEAD_DIM:o + (i + 1) * HEAD_DIM] for i in range(SWA_KV_HEADS)]
    o += SWA_KV_HEADS * HEAD_DIM
    vb = [w[:, o + i * HEAD_DIM:o + (i + 1) * HEAD_DIM] for i in range(SWA_KV_HEADS)]
    dup = lambda parts: [p for p in parts for _ in range(2)]
    w_new = jnp.concatenate([qa, ka, va, qb] + dup(kb) + dup(vb), axis=1).astype(BF16)
    scale = HEAD_DIM ** -0.5 * LOG2E
    ones = lambda n: jnp.ones((1, n), F32)
    gain = jnp.concatenate([
        _tile_gain(q_g, 2 * DIFF_HEADS, scale), _tile_gain(k_g, 2 * DIFF_HEADS), ones(nq),
        _tile_gain(sq_g, SWA_Q_HEADS, scale), _tile_gain(sk_g, 2 * SWA_KV_HEADS),
        ones(2 * SWA_KV_HEADS * HEAD_DIM)], axis=1)
    assert w_new.shape[1] == gain.shape[1] == EVEN_COLS
    return w_new, gain


def _odd_weights(w, wq_up, wkv_up, qa_g, kva_g, mq_g, mk_g, nq_g, nk_g):
    d = w.shape[0]
    o = 0
    parts = []
    for n in (MLA_Q_LORA, MLA_KV_LORA, MLA_ROPE, NA_HEADS * HEAD_DIM, NA_HEADS * HEAD_DIM, NA_HEADS * HEAD_DIM):
        parts.append(w[:, o:o + n])
        o += n
    q_a, kv_a, k_r, nq, nk, nv = parts
    slot = jnp.concatenate([jnp.zeros((d, MLA_NOPE), w.dtype), k_r,
                            jnp.zeros((d, LANES - MLA_QK), w.dtype)], axis=1)
    w_new = jnp.concatenate([q_a, kv_a, nq, nk, nv, slot], axis=1).astype(BF16)
    assert w_new.shape[1] == ODD_W_COLS
    wq = jnp.pad(wq_up.reshape(MLA_Q_LORA, MLA_HEADS, MLA_QK), ((0, 0), (0, 0), (0, MLA_PAD - MLA_QK)))
    wq = wq.reshape(MLA_Q_LORA, MLA_HEADS * MLA_PAD).astype(BF16)
    wkv = wkv_up.reshape(MLA_KV_LORA, MLA_HEADS, MLA_NOPE + MLA_V)
    wk = jnp.pad(wkv[..., :MLA_NOPE], ((0, 0), (0, 0), (0, MLA_PAD - MLA_NOPE))).reshape(MLA_KV_LORA, -1)
    wv = wkv[..., MLA_NOPE:].reshape(MLA_KV_LORA, MLA_HEADS * MLA_V)
    wkv_new = jnp.concatenate([wk, wv], axis=1).astype(BF16)
    pad_gain = lambda g, scale: _tile_gain(jnp.pad(g.astype(F32), (0, MLA_PAD - MLA_QK)), CHUNK // MLA_PAD, scale)
    gains = (qa_g.astype(F32)[None, :], kva_g.astype(F32)[None, :],
             pad_gain(mq_g, MLA_QK ** -0.5 * LOG2E), pad_gain(mk_g, 1.0),
             _tile_gain(nq_g, CHUNK // HEAD_DIM, HEAD_DIM ** -0.5 * LOG2E), _tile_gain(nk_g, CHUNK // HEAD_DIM))
    return w_new, wq, wkv_new, gains


def kernel(x, c, ctx, c_ctx, ada_w, ada_b, norm1_g, norm2_g, w_out, mlp_w1, mlp_w2, ev_w_in, diff_q_g,
           diff_k_g, diff_lam, diff_subln_g, swa_q_g, swa_k_g, swa_sink, od_w_in, mla_qa_g, mla_kva_g,
           mla_wq_up, mla_wkv_up, mla_q_g, mla_k_g, na_q_g, na_k_g, na_rpb):
    nbatch, seq, d = x.shape
    ctx_len = ctx.shape[1]
    depth = ada_w.shape[0]
    n_lat = nbatch * seq
    n_ctx = nbatch * ctx_len
    assert seq % TM == 0 and n_ctx == TM and seq % TQ == 0 and seq % GRID_W == 0
    assert seq % TQ_FULL == 0 and seq % TK_MAX_FREE == 0 and seq % TK_FULL == 0
    assert seq // GRID_W >= 2 * (NA_ROWS + TQ_SUB // GRID_W) and seq >= TQ_SUB + 2 * WINDOW
    bpb = seq // TM

    mod_rows = 8
    cvec = jnp.concatenate([c, c_ctx[None, :], jnp.zeros((mod_rows - nbatch - 1, d), F32)], axis=0)
    mod = _modulation(cvec, ada_w, ada_b)

    cos64, sin64 = _rope_tables(seq, HEAD_DIM, None, n_ctx)
    cos32, sin32 = _rope_tables(seq, MLA_ROPE, MLA_NOPE, n_ctx)
    g64 = _block_diag_ones(HEAD_DIM)
    g128 = _block_diag_ones(LANES)

    x_src = (x.reshape(n_lat, d), ctx.reshape(n_ctx, d))
    wo_all, w1_all, w2_all = w_out.astype(BF16), mlp_w1.astype(BF16), mlp_w2.astype(BF16)
    common = dict(nbatch=nbatch, seq=seq, ctx_len=ctx_len)

    for layer in range(depth):
        need_ctx = layer < depth - 1
        n_blocks = bpb * nbatch + (1 if need_ctx else 0)
        n1 = norm1_g[layer][None, :]
        n2 = norm2_g[layer][None, :]
        i = layer // 2
        if layer % 2 == 0:
            lam_init = 0.8 - 0.6 * math.exp(-0.3 * layer)
            w_in, gain = _even_weights(ev_w_in[i], diff_q_g[i], diff_k_g[i], swa_q_g[i], swa_k_g[i])
            p = _proj_even(x_src, n1, mod, layer, w_in, gain, g64, cos64, sin64, bpb, nbatch)
            lam = diff_lam[i].astype(F32)
            subln = diff_subln_g[i].astype(F32)[None, :]
            sink = swa_sink[i].astype(F32)
            diff_kw = dict(mode="diff", split="mask", n_pairs=DIFF_HEADS, q_blk=lambda j: EVEN_QA // LANES + j,
                           k_blk=lambda j: EVEN_KA // LANES + j, v_blk=lambda j: EVEN_VA // LANES + j,
                           out_cols=DIFF_HEADS * LANES,
                           lam=lam, subln=subln, lam_init=lam_init,
                           safe=_logits_bounded(diff_q_g[i], diff_k_g[i], HEAD_DIM), **common)
            swa_kw = dict(mode="pair", split="mask", n_pairs=SWA_Q_HEADS // 2,
                          q_blk=lambda j: EVEN_QB // LANES + j, k_blk=lambda j: EVEN_KB // LANES + j // 2,
                          v_blk=lambda j: EVEN_VB // LANES + j // 2,
                          out_cols=SWA_Q_HEADS * HEAD_DIM, sink=sink,
                          safe=_logits_bounded(swa_q_g[i], swa_k_g[i], HEAD_DIM, sink=sink), **common)
            oa = _attention(p, name="diff_attn", main="full", ctx_queries=False, **diff_kw)
            ob = _attention(p, name="window_attn", main="window", ctx_queries=False, **swa_kw)
            if need_ctx:
                oa_c = _attention(p, name="diff_attn_ctx", main=None, ctx_queries=True, **diff_kw)
                ob_c = _attention(p, name="window_attn_ctx", main=None, ctx_queries=True, **swa_kw)
        else:
            w_in, wq, wkv, gains = _odd_weights(od_w_in[i], mla_wq_up[i], mla_wkv_up[i], mla_qa_g[i],
                                                mla_kva_g[i], mla_q_g[i], mla_k_g[i], na_q_g[i], na_k_g[i])
            p = _proj_odd(x_src[0], n1, mod, layer, w_in, wq, wkv, gains, g64, g128, cos32, sin32, bpb, nbatch)
            bias = _na_bias(na_rpb[i].astype(F32), seq)
            mla_kw = dict(mode="pair", split="slice", n_pairs=MLA_HEADS // 2, q_blk=lambda j: j,
                          k_blk=lambda j: ODD_K // CHUNK + j, v_blk=lambda j: ODD_V // LANES + j,
                          out_cols=MLA_HEADS * MLA_V,
                          safe=_logits_bounded(mla_q_g[i], mla_k_g[i], MLA_QK), **common)
            na_kw = dict(mode="pair", split="mask", n_pairs=NA_HEADS // 2,
                         q_blk=lambda j: ODD_NQ // LANES + j, k_blk=lambda j: ODD_NK // LANES + j,
                         v_blk=lambda j: ODD_NV // LANES + j, out_cols=NA_HEADS * HEAD_DIM,
                         safe=_logits_bounded(na_q_g[i], na_k_g[i], HEAD_DIM, bias=na_rpb[i]), **common)
            oa = _attention(p, name="mla_attn", main="full", ctx_queries=False, **mla_kw)
            ob = _attention(p, name="na_attn", main="na", ctx_queries=False, bias=bias, **na_kw)
            if need_ctx:
                oa_c = _attention(p, name="mla_attn_ctx", main=None, ctx_queries=True, **mla_kw)
                ob_c = _attention(p, name="na_attn_ctx", main=None, ctx_queries=True, **na_kw)
        oa_src, ob_src = ((oa, oa_c), (ob, ob_c)) if need_ctx else ((oa,), (ob,))
        x_src = (_post(x_src, oa_src, ob_src, wo_all, mod, n2, w1_all, w2_all, layer, n_blocks, bpb, nbatch),)
    return x_src[0].reshape(nbatch, seq, d)
```
